```python
import math
import jax, jax.numpy as jnp
from jax import lax
import numpy as np

D_MODEL = 2048
BATCH = 2
SEQ = 16384
DEPTH = 2
DEC_BATCH = 4
DEC_SEQ = 4096
PAST_LEN = 128

N_MEM = 256
HEAD_DIM = 128
ROPE_THETA = 500000.0
ROT_DIM = HEAD_DIM // 4
EPS = 1e-6
BLOCK = 128
WINDOW = 128
NEG_INF = -1e30
A_HEADS = 8
A_KV_HEADS = 2
A_WIDTH = A_HEADS * HEAD_DIM
B_WIDTH = 1024
HY_EMB = 33
HY_FFN = 64
HY_SHORT = 3
HY_DECAY_MIN = abs(math.log(1e-2)) / 1.5
HY_DECAY_MAX = abs(math.log(1e-2)) / 0.3
M_HEADS = 4
M_WIDTH = M_HEADS * HEAD_DIM
C_HEADS = 4
C_QK_DIM = 128
C_V_DIM = 256
C_WIDTH = C_HEADS * C_V_DIM
D_HEADS = 8
D_NOPE = 128
D_ROPE = 64
D_V = 128
D_Q_RANK = 512
D_KV_RANK = 256
D_WIDTH = D_HEADS * D_V
MIX_WIDTH = 2560
EVEN_SPLITS = (A_WIDTH, A_KV_HEADS * HEAD_DIM, A_KV_HEADS * HEAD_DIM, A_WIDTH,
               3 * B_WIDTH, B_WIDTH, M_WIDTH, M_WIDTH)
ODD_SPLITS = (C_HEADS * 2 * C_QK_DIM, C_HEADS * 2 * C_QK_DIM, C_WIDTH, C_WIDTH,
              D_Q_RANK, D_KV_RANK + D_ROPE, D_WIDTH, M_WIDTH, M_WIDTH)
EVEN_IN = sum(EVEN_SPLITS)
ODD_IN = sum(ODD_SPLITS)
N_EVEN = (DEPTH + 1) // 2
N_ODD = DEPTH // 2

kernel_name = "hybrid_bidir_encoder_two_groups"


def rms_norm(x, g):
    x32 = x.astype(jnp.float32)
    y = x32 * lax.rsqrt(jnp.mean(x32 * x32, axis=-1, keepdims=True) + EPS)
    return (y * g.astype(jnp.float32)).astype(x.dtype)


def split_cols(t, sizes):
    return jnp.split(t, [int(i) for i in np.cumsum(sizes)[:-1]], axis=-1)


def rope_tables(L, dim):
    inv = ROPE_THETA ** (-jnp.arange(0, dim, 2, dtype=jnp.float32) / dim)
    ang = jnp.arange(L, dtype=jnp.float32)[:, None] * inv[None, :]
    return jnp.cos(ang), jnp.sin(ang)


def apply_rope(x, cos, sin):
    shape = (1, x.shape[1]) + (1,) * (x.ndim - 3) + (cos.shape[-1],)
    c = cos.reshape(shape).astype(x.dtype)
    s = sin.reshape(shape).astype(x.dtype)
    x1, x2 = jnp.split(x, 2, axis=-1)
    return jnp.concatenate([x1 * c - x2 * s, x2 * c + x1 * s], axis=-1)


def partial_rope(x, cos, sin):
    return jnp.concatenate([apply_rope(x[..., :ROT_DIM], cos, sin), x[..., ROT_DIM:]], axis=-1)


def sweep_query_blocks(fn, q):
    B, L = q.shape[:2]
    nb = L // BLOCK
    qb = jnp.moveaxis(q.reshape((B, nb, BLOCK) + q.shape[2:]), 1, 0)
    ob = lax.map(fn, qb)
    return jnp.moveaxis(ob, 0, 1).reshape((B, L) + ob.shape[3:])


def window_attention(q, k, v, sink):
    B, L, HQ, dh = q.shape
    HKV = k.shape[2]
    G = HQ // HKV
    nb = L // BLOCK
    qb = q.reshape(B, nb, BLOCK, HKV, G, dh)

    def neighbours(t):
        tp = jnp.pad(t, ((0, 0), (BLOCK, BLOCK), (0, 0), (0, 0))).reshape(B, nb + 2, BLOCK, HKV, dh)
        return jnp.concatenate([tp[:, :-2], tp[:, 1:-1], tp[:, 2:]], axis=2)

    kw, vw = neighbours(k), neighbours(v)
    s = jnp.einsum("bnqhgd,bnkhd->bnhgqk", qb, kw, preferred_element_type=jnp.float32) * (dh ** -0.5)
    qi = jnp.arange(BLOCK)[:, None]
    kj = jnp.arange(3 * BLOCK)[None, :]
    in_win = jnp.abs(kj - BLOCK - qi) <= WINDOW
    kpos = (jnp.arange(nb)[:, None] - 1) * BLOCK + jnp.arange(3 * BLOCK)[None, :]
    in_seq = (kpos >= 0) & (kpos < L)
    mask = (in_win[None, :, :] & in_seq[:, None, :])[None, :, None, None]
    s = jnp.where(mask, s, NEG_INF)
    sk = sink.astype(jnp.float32).reshape(1, 1, HKV, G, 1, 1)
    m = jnp.maximum(jnp.max(s, axis=-1, keepdims=True), sk)
    pr = jnp.exp(s - m)
    pr = pr / (jnp.sum(pr, axis=-1, keepdims=True) + jnp.exp(sk - m))
    o = jnp.einsum("bnhgqk,bnkhd->bnqhgd", pr.astype(v.dtype), vw)
    return o.reshape(B, L, HQ * dh)


def hyena_filter(L, w1, b1, freq, w2, b2, w3, decay):
    f32 = jnp.float32
    t = jnp.linspace(0.0, 1.0, L, dtype=f32)[:, None]
    bands = (HY_EMB - 1) // 2
    fr = jnp.linspace(1e-4, bands - 1, bands, dtype=f32)[None, :]
    w = 2.0 * math.pi * jnp.arange(L, dtype=f32)[:, None] / L
    z = jnp.concatenate([t, jnp.cos(fr * w), -jnp.sin(fr * w)], axis=-1)
    freq = freq.astype(f32)
    h = jnp.sin(freq[0] * (z @ w1.astype(f32) + b1.astype(f32)))
    h = jnp.sin(freq[1] * (h @ w2.astype(f32) + b2.astype(f32)))
    h = (h @ w3.astype(f32)) * jnp.exp(-t * jnp.abs(decay.astype(f32)))
    hf, hb = h[:, :B_WIDTH], h[:, B_WIDTH:]
    kern = jnp.concatenate([hf, jnp.zeros((1, B_WIDTH), f32), hb[:L - 1][::-1]], axis=0)
    return kern / jnp.sum(jnp.abs(kern), axis=0, keepdims=True)


def hyena_mixer(u, short_w, short_b, bias, kern):
    up = jnp.pad(u, ((0, 0), (1, 1), (0, 0)))
    u = up[:, :-2] * short_w[0] + up[:, 1:-1] * short_w[1] + up[:, 2:] * short_w[2] + short_b
    x0, x1, v = jnp.split(u, 3, axis=-1)
    zz = (v * x1).astype(jnp.float32)
    L = zz.shape[1]
    Z = jnp.fft.rfft(zz, n=2 * L, axis=1)
    K = jnp.fft.rfft(kern, n=2 * L, axis=0)
    y = jnp.fft.irfft(Z * K[None], n=2 * L, axis=1)[:, :L] + zz * bias.astype(jnp.float32)
    return y.astype(x0.dtype) * x0


def memory_attention(q, mem_n, w_kv, q_gain, k_gain):
    B, L, _ = q.shape
    M = mem_n.shape[1]
    q = rms_norm(q.reshape(B, L, M_HEADS, HEAD_DIM), q_gain)
    k, v = jnp.split(mem_n @ w_kv, 2, axis=-1)
    k = rms_norm(k.reshape(B, M, M_HEADS, HEAD_DIM), k_gain)
    v = v.reshape(B, M, M_HEADS, HEAD_DIM)
    s = jnp.einsum("blhd,bmhd->bhlm", q, k, preferred_element_type=jnp.float32) * (HEAD_DIM ** -0.5)
    pr = jax.nn.softmax(s, axis=-1)
    return jnp.einsum("bhlm,bmhd->blhd", pr.astype(v.dtype), v).reshape(B, L, M_WIDTH)


def diff_attention(q, k, v, lam):
    scale = q.shape[-1] ** -0.5

    def block(qb):
        s = jnp.einsum("bqhcd,bkhcd->bhcqk", qb, k, preferred_element_type=jnp.float32) * scale
        pr = jax.nn.softmax(s, axis=-1)
        a = pr[:, :, 0] - lam * pr[:, :, 1]
        return jnp.einsum("bhqk,bkhd->bqhd", a.astype(v.dtype), v)

    return sweep_query_blocks(block, q)


def dense_attention(q, k, v):
    scale = q.shape[-1] ** -0.5

    def block(qb):
        s = jnp.einsum("bqhd,bkhd->bhqk", qb, k, preferred_element_type=jnp.float32) * scale
        pr = jax.nn.softmax(s, axis=-1)
        return jnp.einsum("bhqk,bkhd->bqhd", pr.astype(v.dtype), v)

    return sweep_query_blocks(block, q)


def even_layer(x, mem_n, rope_a, p, l, e):
    B, L, _ = x.shape
    cos, sin = rope_a
    h = rms_norm(x, p["norm_gain"][l])
    aq, ak, av, ag, bu, bg, mq, mg = split_cols(h @ p["w_in_even"][e], EVEN_SPLITS)
    aq = partial_rope(rms_norm(aq.reshape(B, L, A_HEADS, HEAD_DIM), p["a_q_norm"][e]), cos, sin)
    ak = partial_rope(rms_norm(ak.reshape(B, L, A_KV_HEADS, HEAD_DIM), p["a_k_norm"][e]), cos, sin)
    ya = window_attention(aq, ak, av.reshape(B, L, A_KV_HEADS, HEAD_DIM), p["a_sink"][e])
    kern = hyena_filter(L, p["hy_w1"][e], p["hy_b1"][e], p["hy_freq"][e], p["hy_w2"][e],
                        p["hy_b2"][e], p["hy_w3"][e], p["hy_decay"][e])
    yb = hyena_mixer(bu, p["hy_short_w"][e], p["hy_short_b"][e], p["hy_bias"][e], kern)
    ym = memory_attention(mq, mem_n, p["w_mem_kv"][l], p["mem_q_norm"][l], p["mem_k_norm"][l])
    y = jnp.concatenate([ya * jax.nn.silu(ag), yb * jax.nn.silu(bg), ym * jax.nn.silu(mg)], axis=-1)
    return x + y @ p["w_out_even"][e]


def odd_layer(x, mem_n, rope_a, rope_d, p, l, o):
    B, L, _ = x.shape
    cos, sin = rope_a
    cos_d, sin_d = rope_d
    h = rms_norm(x, p["norm_gain"][l])
    cq, ck, cv, cg, dqa, dkva, dg, mq, mg = split_cols(h @ p["w_in_odd"][o], ODD_SPLITS)
    cq = partial_rope(rms_norm(cq.reshape(B, L, C_HEADS, 2, C_QK_DIM), p["c_q_norm"][o]), cos, sin)
    ck = partial_rope(rms_norm(ck.reshape(B, L, C_HEADS, 2, C_QK_DIM), p["c_k_norm"][o]), cos, sin)
    lam_init = 0.8 - 0.6 * math.exp(-0.3 * l)
    lv = p["c_lambda"][o].astype(jnp.float32)
    lam = jnp.exp(jnp.sum(lv[0] * lv[1])) - jnp.exp(jnp.sum(lv[2] * lv[3])) + lam_init
    yc = diff_attention(cq, ck, cv.reshape(B, L, C_HEADS, C_V_DIM), lam)
    yc = (rms_norm(yc, p["c_out_norm"][o]) * (1.0 - lam_init)).reshape(B, L, C_WIDTH)
    qd = (rms_norm(dqa, p["d_q_a_norm"][o]) @ p["w_q_b"][o]).reshape(B, L, D_HEADS, D_NOPE + D_ROPE)
    c_kv, k_rope = jnp.split(dkva, [D_KV_RANK], axis=-1)
    kv = (rms_norm(c_kv, p["d_kv_a_norm"][o]) @ p["w_kv_b"][o]).reshape(B, L, D_HEADS, D_NOPE + D_V)
    k_nope, vd = jnp.split(kv, [D_NOPE], axis=-1)
    kd = jnp.concatenate([k_nope, jnp.broadcast_to(k_rope[:, :, None, :], (B, L, D_HEADS, D_ROPE))], axis=-1)
    qd = rms_norm(qd, p["d_q_norm"][o])
    kd = rms_norm(kd, p["d_k_norm"][o])
    qd = jnp.concatenate([qd[..., :D_NOPE], apply_rope(qd[..., D_NOPE:], cos_d, sin_d)], axis=-1)
    kd = jnp.concatenate([kd[..., :D_NOPE], apply_rope(kd[..., D_NOPE:], cos_d, sin_d)], axis=-1)
    yd = dense_attention(qd, kd, vd).reshape(B, L, D_WIDTH)
    ym = memory_attention(mq, mem_n, p["w_mem_kv"][l], p["mem_q_norm"][l], p["mem_k_norm"][l])
    y = jnp.concatenate([yc * jax.nn.silu(cg), yd * jax.nn.silu(dg), ym * jax.nn.silu(mg)], axis=-1)
    return x + y @ p["w_out_odd"][o]


def trunk(x, mem, p):
    L = x.shape[1]
    rope_a = rope_tables(L, ROT_DIM)
    rope_d = rope_tables(L, D_ROPE)
    for l in range(DEPTH):
        mem_n = rms_norm(mem, p["mem_norm_gain"][l])
        if l % 2 == 0:
            x = even_layer(x, mem_n, rope_a, p, l, l // 2)
        else:
            x = odd_layer(x, mem_n, rope_a, rope_d, p, l, l // 2)
    return x


def setup_inputs(seed: int = 0) -> dict:
    key = jax.random.key(seed)
    ks = iter(jax.random.split(key, 64))
    f32 = jnp.float32

    def nrm(shape, scale=1.0):
        return scale * jax.random.normal(next(ks), shape, f32)

    def gain(shape):
        return 1.0 + 0.02 * jax.random.normal(next(ks), shape, f32)

    NE, NO = N_EVEN, N_ODD
    decay_base = jnp.tile(jnp.linspace(HY_DECAY_MIN, HY_DECAY_MAX, B_WIDTH, dtype=f32), 2)
    return {
        "x_prompt": nrm((BATCH, SEQ, D_MODEL)),
        "x_sample": nrm((DEC_BATCH, DEC_SEQ, D_MODEL)),
        "mem_prompt": nrm((BATCH, N_MEM, D_MODEL)),
        "mem_sample": nrm((DEC_BATCH, N_MEM, D_MODEL)),
        "norm_gain": gain((DEPTH, D_MODEL)),
        "mem_norm_gain": gain((DEPTH, D_MODEL)),
        "w_mem_kv": nrm((DEPTH, D_MODEL, 2 * M_WIDTH), D_MODEL ** -0.5),
        "mem_q_norm": gain((DEPTH, HEAD_DIM)),
        "mem_k_norm": gain((DEPTH, HEAD_DIM)),
        "w_in_even": nrm((NE, D_MODEL, EVEN_IN), D_MODEL ** -0.5),
        "w_out_even": nrm((NE, MIX_WIDTH, D_MODEL), MIX_WIDTH ** -0.5),
        "a_q_norm": gain((NE, HEAD_DIM)),
        "a_k_norm": gain((NE, HEAD_DIM)),
        "a_sink": nrm((NE, A_HEADS), 0.5),
        "hy_short_w": nrm((NE, HY_SHORT, 3 * B_WIDTH), 0.5),
        "hy_short_b": nrm((NE, 3 * B_WIDTH), 0.02),
        "hy_w1": nrm((NE, HY_EMB, HY_FFN), HY_EMB ** -0.5),
        "hy_b1": nrm((NE, HY_FFN), 0.02),
        "hy_freq": gain((NE, 2, HY_FFN)),
        "hy_w2": nrm((NE, HY_FFN, HY_FFN), HY_FFN ** -0.5),
        "hy_b2": nrm((NE, HY_FFN), 0.02),
        "hy_w3": nrm((NE, HY_FFN, 2 * B_WIDTH), HY_FFN ** -0.5),
        "hy_decay": decay_base[None, :] + nrm((NE, 2 * B_WIDTH), 0.1),
        "hy_bias": nrm((NE, B_WIDTH), 1.0),
        "w_in_odd": nrm((NO, D_MODEL, ODD_IN), D_MODEL ** -0.5),
        "w_out_odd": nrm((NO, MIX_WIDTH, D_MODEL), MIX_WIDTH ** -0.5),
        "c_q_norm": gain((NO, C_QK_DIM)),
        "c_k_norm": gain((NO, C_QK_DIM)),
        "c_lambda": nrm((NO, 4, C_QK_DIM), 0.1),
        "c_out_norm": gain((NO, C_V_DIM)),
        "d_q_a_norm": gain((NO, D_Q_RANK)),
        "w_q_b": nrm((NO, D_Q_RANK, D_HEADS * (D_NOPE + D_ROPE)), D_Q_RANK ** -0.5),
        "d_kv_a_norm": gain((NO, D_KV_RANK)),
        "w_kv_b": nrm((NO, D_KV_RANK, D_HEADS * (D_NOPE + D_V)), D_KV_RANK ** -0.5),
        "d_q_norm": gain((NO, D_NOPE + D_ROPE)),
        "d_k_norm": gain((NO, D_NOPE + D_ROPE)),
    }


def reference(x_prompt, x_sample, mem_prompt, mem_sample, norm_gain, mem_norm_gain, w_mem_kv,
              mem_q_norm, mem_k_norm, w_in_even, w_out_even, a_q_norm, a_k_norm, a_sink,
              hy_short_w, hy_short_b, hy_w1, hy_b1, hy_freq, hy_w2, hy_b2, hy_w3, hy_decay, hy_bias,
              w_in_odd, w_out_odd, c_q_norm, c_k_norm, c_lambda, c_out_norm, d_q_a_norm, w_q_b,
              d_kv_a_norm, w_kv_b, d_q_norm, d_k_norm):
    p = dict(norm_gain=norm_gain, mem_norm_gain=mem_norm_gain, w_mem_kv=w_mem_kv,
             mem_q_norm=mem_q_norm, mem_k_norm=mem_k_norm, w_in_even=w_in_even,
             w_out_even=w_out_even, a_q_norm=a_q_norm, a_k_norm=a_k_norm, a_sink=a_sink,
             hy_short_w=hy_short_w, hy_short_b=hy_short_b, hy_w1=hy_w1, hy_b1=hy_b1,
             hy_freq=hy_freq, hy_w2=hy_w2, hy_b2=hy_b2, hy_w3=hy_w3, hy_decay=hy_decay,
             hy_bias=hy_bias, w_in_odd=w_in_odd, w_out_odd=w_out_odd, c_q_norm=c_q_norm,
             c_k_norm=c_k_norm, c_lambda=c_lambda, c_out_norm=c_out_norm, d_q_a_norm=d_q_a_norm,
             w_q_b=w_q_b, d_kv_a_norm=d_kv_a_norm, w_kv_b=w_kv_b, d_q_norm=d_q_norm,
             d_k_norm=d_k_norm)
    y_prompt = trunk(x_prompt, mem_prompt, p)
    y_sample = trunk(x_sample, mem_sample, p)
    return (y_prompt, y_sample)
```

```python
import functools
import math

import jax
import jax.numpy as jnp
from jax import lax
from jax.experimental import pallas as pl
from jax.experimental.pallas import tpu as pltpu

F32 = jnp.float32
BF16 = jnp.bfloat16

D_MODEL = 2048
N_MEM = 256
HEAD_DIM = 128
ROPE_THETA = 500000.0
ROT_DIM = HEAD_DIM // 4
EPS = 1e-6
BLOCK = 128
WINDOW = 128
NEG_INF = -1e30
A_HEADS = 8
A_KV_HEADS = 2
A_WIDTH = A_HEADS * HEAD_DIM
B_WIDTH = 1024
HY_EMB = 33
HY_FFN = 64
M_HEADS = 4
M_WIDTH = M_HEADS * HEAD_DIM
C_HEADS = 4
C_QK_DIM = 128
C_V_DIM = 256
C_WIDTH = C_HEADS * C_V_DIM
D_HEADS = 8
D_NOPE = 128
D_ROPE = 64
D_V = 128
D_Q_RANK = 512
D_KV_RANK = 256
D_WIDTH = D_HEADS * D_V
MIX_WIDTH = 2560
EVEN_SPLITS = (A_WIDTH, A_KV_HEADS * HEAD_DIM, A_KV_HEADS * HEAD_DIM, A_WIDTH,
               3 * B_WIDTH, B_WIDTH, M_WIDTH, M_WIDTH)
ODD_SPLITS = (C_HEADS * 2 * C_QK_DIM, C_HEADS * 2 * C_QK_DIM, C_WIDTH, C_WIDTH,
              D_Q_RANK, D_KV_RANK + D_ROPE, D_WIDTH, M_WIDTH, M_WIDTH)

LANES = 128
DFT_N2 = 128
VMEM_LIMIT = 48 * 1024 * 1024

EV_GATE, EV_MQ, EV_BU, EV_AQ, EV_AK, EV_AV = 0, 2560, 3072, 6144, 7168, 7424
OD_GATE, OD_MQ, OD_CQ, OD_CK, OD_CV, OD_DQA, OD_CKV, OD_KR = 0, 2560, 3072, 4096, 5120, 6144, 6656, 6912
OD_WIDTH = 7168


def _tile(n, pref):
    t = min(n, pref)
    assert n % t == 0, (n, t)
    return t


def _params(sem):
    return pltpu.CompilerParams(dimension_semantics=sem, vmem_limit_bytes=VMEM_LIMIT)


def _rms(x, gain):
    ms = jnp.mean(x * x, axis=-1, keepdims=True)
    return x * lax.rsqrt(ms + EPS) * gain


def _nmm_kernel(x_ref, g_ref, w_ref, o_ref, h_ref):
    @pl.when(pl.program_id(1) == 0)
    def _():
        h_ref[...] = _rms(x_ref[...].astype(F32), g_ref[...]).astype(BF16)

    o_ref[...] = jnp.dot(h_ref[...], w_ref[...], preferred_element_type=F32).astype(o_ref.dtype)


def normed_matmul(x, gain, w, *, col_block=0, tm=512, tn=512, out_dtype=BF16):
    rows = x.shape[0]
    k, n = w.shape
    tm = _tile(rows, tm)
    tn = _tile(n, tn)
    return pl.pallas_call(
        _nmm_kernel,
        grid=(rows // tm, n // tn),
        in_specs=[pl.BlockSpec((tm, k), lambda i, j: (i, col_block)),
                  pl.BlockSpec((1, k), lambda i, j: (0, 0)),
                  pl.BlockSpec((k, tn), lambda i, j: (0, j))],
        out_specs=pl.BlockSpec((tm, tn), lambda i, j: (i, j)),
        out_shape=jax.ShapeDtypeStruct((rows, n), out_dtype),
        scratch_shapes=[pltpu.VMEM((tm, k), BF16)],
        compiler_params=_params(("parallel", "arbitrary")),
        name="normed_matmul",
    )(x, gain.reshape(1, k).astype(F32), w)


def _rope(y, cf, sa, sb, half):
    return y * cf + pltpu.roll(y, LANES - half, 1) * sa + pltpu.roll(y, half, 1) * sb


def _head_prep_kernel(x_ref, g_ref, cf_ref, sa_ref, sb_ref, o_ref, *, nheads, scale):
    g = g_ref[...]
    cf, sa, sb = cf_ref[...], sa_ref[...], sb_ref[...]
    for h in range(nheads):
        cols = slice(h * LANES, (h + 1) * LANES)
        y = _rms(x_ref[:, cols].astype(F32), g)
        y = _rope(y, cf, sa, sb, ROT_DIM // 2)
        o_ref[:, cols] = (y * scale).astype(o_ref.dtype)


def head_prep(p, col, nheads, gain, tables, scale, seq, *, tm=256):
    rows = p.shape[0]
    width = nheads * LANES
    assert col % width == 0
    tm = _tile(seq, tm)
    nblk = seq // tm
    tab_spec = pl.BlockSpec((tm, LANES), lambda i: (i % nblk, 0))
    return pl.pallas_call(
        functools.partial(_head_prep_kernel, nheads=nheads, scale=scale),
        grid=(rows // tm,),
        in_specs=[pl.BlockSpec((tm, width), lambda i: (i, col // width)),
                  pl.BlockSpec((1, LANES), lambda i: (0, 0)),
                  tab_spec, tab_spec, tab_spec],
        out_specs=pl.BlockSpec((tm, width), lambda i: (i, 0)),
        out_shape=jax.ShapeDtypeStruct((rows, width), BF16),
        compiler_params=_params(("parallel",)),
        name="head_prep",
    )(p, gain.reshape(1, LANES).astype(F32), *tables)


def _window_kernel(sink_ref, q_ref, kp_ref, kc_ref, kn_ref, vp_ref, vc_ref, vn_ref, o_ref, *, nblk):
    n = pl.program_id(1)
    qi = lax.broadcasted_iota(jnp.int32, (BLOCK, 3 * BLOCK), 0)
    kj = lax.broadcasted_iota(jnp.int32, (BLOCK, 3 * BLOCK), 1)
    in_win = jnp.abs(kj - BLOCK - qi) <= WINDOW
    kpos = (n - 1) * BLOCK + kj
    mask = in_win & (kpos >= 0) & (kpos < nblk * BLOCK)
    group = A_HEADS // A_KV_HEADS
    for kv in range(A_KV_HEADS):
        cols = slice(kv * HEAD_DIM, (kv + 1) * HEAD_DIM)
        kw = jnp.concatenate([kp_ref[:, cols], kc_ref[:, cols], kn_ref[:, cols]], axis=0).astype(BF16)
        vw = jnp.concatenate([vp_ref[:, cols], vc_ref[:, cols], vn_ref[:, cols]], axis=0).astype(BF16)
        for g in range(group):
            h = kv * group + g
            q = q_ref[:, h * HEAD_DIM:(h + 1) * HEAD_DIM]
            s = lax.dot_general(q, kw, (((1,), (1,)), ((), ())), preferred_element_type=F32)
            s = jnp.where(mask, s, NEG_INF)
            sk = sink_ref[h]
            m = jnp.maximum(jnp.max(s, axis=-1, keepdims=True), sk)
            pr = jnp.exp(s - m)
            denom = jnp.sum(pr, axis=-1, keepdims=True) + jnp.exp(sk - m)
            pr = pr / denom
            o = jnp.dot(pr.astype(BF16), vw, preferred_element_type=F32)
            o_ref[:, h * HEAD_DIM:(h + 1) * HEAD_DIM] = o.astype(o_ref.dtype)


def window_attention(q, k, p, v_col, sink, batch, seq):
    nblk = seq // BLOCK
    kvw = A_KV_HEADS * HEAD_DIM
    vb = v_col // kvw

    def prev(b, n):
        return b * nblk + jnp.maximum(n - 1, 0)

    def cur(b, n):
        return b * nblk + n

    def nxt(b, n):
        return b * nblk + jnp.minimum(n + 1, nblk - 1)

    return pl.pallas_call(
        functools.partial(_window_kernel, nblk=nblk),
        grid=(batch, nblk),
        in_specs=[pl.BlockSpec(memory_space=pltpu.SMEM),
                  pl.BlockSpec((BLOCK, A_WIDTH), lambda b, n: (cur(b, n), 0)),
                  pl.BlockSpec((BLOCK, kvw), lambda b, n: (prev(b, n), 0)),
                  pl.BlockSpec((BLOCK, kvw), lambda b, n: (cur(b, n), 0)),
                  pl.BlockSpec((BLOCK, kvw), lambda b, n: (nxt(b, n), 0)),
                  pl.BlockSpec((BLOCK, kvw), lambda b, n: (prev(b, n), vb)),
                  pl.BlockSpec((BLOCK, kvw), lambda b, n: (cur(b, n), vb)),
                  pl.BlockSpec((BLOCK, kvw), lambda b, n: (nxt(b, n), vb))],
        out_specs=pl.BlockSpec((BLOCK, A_WIDTH), lambda b, n: (cur(b, n), 0)),
        out_shape=jax.ShapeDtypeStruct((batch * seq, A_WIDTH), BF16),
        compiler_params=_params(("parallel", "parallel")),
        name="window_attention",
    )(sink.astype(F32), q, k, k, k, p, p, p)


def _mem_kernel(q_ref, kv_ref, qg_ref, kg_ref, o_ref):
    qg, kg = qg_ref[...], kg_ref[...]
    scale = HEAD_DIM ** -0.5
    for h in range(M_HEADS):
        cols = slice(h * HEAD_DIM, (h + 1) * HEAD_DIM)
        q = (_rms(q_ref[:, cols].astype(F32), qg) * scale).astype(BF16)
        k = _rms(kv_ref[:, cols].astype(F32), kg).astype(BF16)
        v = kv_ref[:, M_WIDTH + h * HEAD_DIM:M_WIDTH + (h + 1) * HEAD_DIM].astype(BF16)
        s = lax.dot_general(q, k, (((1,), (1,)), ((), ())), preferred_element_type=F32)
        m = jnp.max(s, axis=-1, keepdims=True)
        pr = jnp.exp(s - m)
        pr = pr / jnp.sum(pr, axis=-1, keepdims=True)
        o = jnp.dot(pr.astype(BF16), v, preferred_element_type=F32)
        o_ref[:, cols] = o.astype(o_ref.dtype)


def memory_attention(p, q_col, kvm, q_gain, k_gain, batch, seq, *, tq=512):
    tq = _tile(seq, tq)
    nq = seq // tq
    return pl.pallas_call(
        _mem_kernel,
        grid=(batch, nq),
        in_specs=[pl.BlockSpec((tq, M_WIDTH), lambda b, i: (b * nq + i, q_col // M_WIDTH)),
                  pl.BlockSpec((N_MEM, 2 * M_WIDTH), lambda b, i: (b, 0)),
                  pl.BlockSpec((1, HEAD_DIM), lambda b, i: (0, 0)),
                  pl.BlockSpec((1, HEAD_DIM), lambda b, i: (0, 0))],
        out_specs=pl.BlockSpec((tq, M_WIDTH), lambda b, i: (b * nq + i, 0)),
        out_shape=jax.ShapeDtypeStruct((batch * seq, M_WIDTH), BF16),
        compiler_params=_params(("parallel", "parallel")),
        name="memory_attention",
    )(p, kvm, q_gain.reshape(1, HEAD_DIM).astype(F32), k_gain.reshape(1, HEAD_DIM).astype(F32))


def _flash_init(m_ref, l_ref, acc_ref):
    m_ref[...] = jnp.full(m_ref.shape, NEG_INF, F32)
    l_ref[...] = jnp.zeros(l_ref.shape, F32)
    acc_ref[...] = jnp.zeros(acc_ref.shape, F32)


def _flash_step(q_ref, k_ref, v_ref, m_ref, l_ref, acc_ref, ncomp, dqk):
    v = v_ref[...].astype(BF16)
    for c in range(ncomp):
        cols = slice(c * dqk, (c + 1) * dqk)
        s = lax.dot_general(q_ref[:, cols], k_ref[:, cols], (((1,), (1,)), ((), ())),
                            preferred_element_type=F32)
        m_prev = m_ref[c]
        m_new = jnp.maximum(m_prev, jnp.max(s, axis=-1, keepdims=True))
        alpha = jnp.exp(m_prev - m_new)
        pr = jnp.exp(s - m_new)
        l_ref[c] = alpha * l_ref[c] + jnp.sum(pr, axis=-1, keepdims=True)
        acc_ref[c] = alpha * acc_ref[c] + jnp.dot(pr.astype(BF16), v, preferred_element_type=F32)
        m_ref[c] = m_new


def _diff_flash_kernel(q_ref, k_ref, v_ref, lam_ref, g_ref, o_ref, m_ref, l_ref, acc_ref, *, nk, lam_init):
    ki = pl.program_id(3)

    @pl.when(ki == 0)
    def _():
        _flash_init(m_ref, l_ref, acc_ref)

    _flash_step(q_ref, k_ref, v_ref, m_ref, l_ref, acc_ref, 2, C_QK_DIM)

    @pl.when(ki == nk - 1)
    def _():
        lv = lam_ref[...]
        lam = (jnp.exp(jnp.sum(lv[0:1] * lv[1:2], axis=-1, keepdims=True))
               - jnp.exp(jnp.sum(lv[2:3] * lv[3:4], axis=-1, keepdims=True)) + lam_init)
        y = acc_ref[0] / l_ref[0] - lam * (acc_ref[1] / l_ref[1])
        o_ref[...] = (_rms(y, g_ref[...]) * (1.0 - lam_init)).astype(o_ref.dtype)


def _mla_flash_kernel(q_ref, k_ref, v_ref, o_ref, m_ref, l_ref, acc_ref, *, nk):
    ki = pl.program_id(3)

    @pl.when(ki == 0)
    def _():
        _flash_init(m_ref, l_ref, acc_ref)

    _flash_step(q_ref, k_ref, v_ref, m_ref, l_ref, acc_ref, 1, 2 * LANES)

    @pl.when(ki == nk - 1)
    def _():
        o_ref[...] = (acc_ref[0] / l_ref[0]).astype(o_ref.dtype)


def _flash_call(kernel, q, k, v, v_block0, extra, extra_specs, *, heads, ncomp, dv, batch, seq, tq, tk, name):
    tq = _tile(seq, tq)
    tk = _tile(seq, tk)
    nq, nk = seq // tq, seq // tk
    qk_w = 2 * LANES
    return pl.pallas_call(
        functools.partial(kernel, nk=nk),
        grid=(batch, heads, nq, nk),
        in_specs=[pl.BlockSpec((tq, qk_w), lambda b, h, i, j: (b * nq + i, h)),
                  pl.BlockSpec((tk, qk_w), lambda b, h, i, j: (b * nk + j, h)),
                  pl.BlockSpec((tk, dv), lambda b, h, i, j: (b * nk + j, v_block0 + h))] + extra_specs,
        out_specs=pl.BlockSpec((tq, dv), lambda b, h, i, j: (b * nq + i, h)),
        out_shape=jax.ShapeDtypeStruct((batch * seq, heads * dv), BF16),
        scratch_shapes=[pltpu.VMEM((ncomp, tq, 1), F32), pltpu.VMEM((ncomp, tq, 1), F32),
                        pltpu.VMEM((ncomp, tq, dv), F32)],
        compiler_params=_params(("parallel", "parallel", "parallel", "arbitrary")),
        name=name,
    )(q, k, v, *extra)


def diff_attention(q, k, p, v_col, c_lambda, out_gain, lam_init, batch, seq, *, tq=512, tk=512):
    const = lambda b, h, i, j: (0, 0)
    return _flash_call(
        functools.partial(_diff_flash_kernel, lam_init=lam_init), q, k, p, v_col // C_V_DIM,
        [c_lambda.astype(F32), out_gain.reshape(1, C_V_DIM).astype(F32)],
        [pl.BlockSpec((4, C_QK_DIM), const), pl.BlockSpec((1, C_V_DIM), const)],
        heads=C_HEADS, ncomp=2, dv=C_V_DIM, batch=batch, seq=seq, tq=tq, tk=tk, name="diff_attention")


def mla_attention(q, k, kv, batch, seq, *, tq=512, tk=512):
    return _flash_call(_mla_flash_kernel, q, k, kv, D_HEADS, [], [],
                       heads=D_HEADS, ncomp=1, dv=D_V, batch=batch, seq=seq, tq=tq, tk=tk, name="mla_attention")


def _mla_prep_kernel(n_ref, r_ref, gn_ref, gr_ref, cf_ref, sa_ref, sb_ref, o_ref, *, shared_rope, scale):
    gn, gr = gn_ref[...], gr_ref[...]
    cf, sa, sb = cf_ref[...], sa_ref[...], sb_ref[...]
    for h in range(D_HEADS):
        cols = slice(h * LANES, (h + 1) * LANES)
        xn = n_ref[:, cols].astype(F32)
        xr = (r_ref[...] if shared_rope else r_ref[:, cols]).astype(F32)
        ms = (jnp.sum(xn * xn, axis=-1, keepdims=True)
              + jnp.sum(xr * xr, axis=-1, keepdims=True)) * (1.0 / (D_NOPE + D_ROPE))
        inv = lax.rsqrt(ms + EPS)
        yn = xn * inv * gn
        yr = _rope(xr * inv * gr, cf, sa, sb, D_ROPE // 2)
        o_ref[:, 2 * h * LANES:(2 * h + 1) * LANES] = (yn * scale).astype(o_ref.dtype)
        o_ref[:, (2 * h + 1) * LANES:(2 * h + 2) * LANES] = (yr * scale).astype(o_ref.dtype)


def mla_prep(nope_arr, nope_col, rope_arr, rope_col, shared_rope, gain, tables, scale, seq, *, tm=256):
    rows = nope_arr.shape[0]
    width = D_HEADS * LANES
    tm = _tile(seq, tm)
    nblk = seq // tm
    rw = LANES if shared_rope else width
    assert nope_col % width == 0 and rope_col % rw == 0
    gn = gain[:D_NOPE].reshape(1, LANES).astype(F32)
    gr = jnp.pad(gain[D_NOPE:], (0, LANES - D_ROPE)).reshape(1, LANES).astype(F32)
    tab_spec = pl.BlockSpec((tm, LANES), lambda i: (i % nblk, 0))
    one = pl.BlockSpec((1, LANES), lambda i: (0, 0))
    return pl.pallas_call(
        functools.partial(_mla_prep_kernel, shared_rope=shared_rope, scale=scale),
        grid=(rows // tm,),
        in_specs=[pl.BlockSpec((tm, width), lambda i: (i, nope_col // width)),
                  pl.BlockSpec((tm, rw), lambda i: (i, rope_col // rw)),
                  one, one, tab_spec, tab_spec, tab_spec],
        out_specs=pl.BlockSpec((tm, 2 * width), lambda i: (i, 0)),
        out_shape=jax.ShapeDtypeStruct((rows, 2 * width), BF16),
        compiler_params=_params(("parallel",)),
        name="mla_prep",
    )(nope_arr, rope_arr, gn, gr, *tables)


HALO = 16


def _hyena_gate_kernel(u_ref, up_ref, un_ref, w_ref, b_ref, zz_ref, x0_ref, *, nblk, tm):
    li = pl.program_id(1)
    row = lax.broadcasted_iota(jnp.int32, (tm, B_WIDTH), 0)
    has_prev = (li > 0).astype(F32)
    has_next = (li < nblk - 1).astype(F32)

    def conv(part):
        cols = slice(part * B_WIDTH, (part + 1) * B_WIDTH)
        u = u_ref[:, cols].astype(F32)
        prev_row = up_ref[HALO - 1:HALO, cols].astype(F32) * has_prev
        next_row = un_ref[0:1, cols].astype(F32) * has_next
        above = jnp.where(row == 0, prev_row, pltpu.roll(u, 1, 0))
        below = jnp.where(row == tm - 1, next_row, pltpu.roll(u, tm - 1, 0))
        w = w_ref[:, cols]
        return above * w[0:1] + u * w[1:2] + below * w[2:3] + b_ref[:, cols]

    x0_ref[...] = conv(0).astype(x0_ref.dtype)
    zz_ref[...] = (conv(2) * conv(1)).astype(zz_ref.dtype)


def hyena_gate(p, u_col, short_w, short_b, batch, seq, *, tm=256):
    tm = _tile(seq, tm)
    nblk = seq // tm
    uw = 3 * B_WIDTH
    ub = u_col // uw
    hb = tm // HALO
    out = jax.ShapeDtypeStruct((batch * seq, B_WIDTH), BF16)
    return pl.pallas_call(
        functools.partial(_hyena_gate_kernel, nblk=nblk, tm=tm),
        grid=(batch, nblk),
        in_specs=[pl.BlockSpec((tm, uw), lambda b, i: (b * nblk + i, ub)),
                  pl.BlockSpec((HALO, uw), lambda b, i: (jnp.maximum((b * nblk + i) * hb - 1, 0), ub)),
                  pl.BlockSpec((HALO, uw),
                               lambda b, i: (jnp.minimum((b * nblk + i + 1) * hb, batch * nblk * hb - 1), ub)),
                  pl.BlockSpec((3, uw), lambda b, i: (0, 0)),
                  pl.BlockSpec((1, uw), lambda b, i: (0, 0))],
        out_specs=[pl.BlockSpec((tm, B_WIDTH), lambda b, i: (b * nblk + i, 0))] * 2,
        out_shape=[out, out],
        compiler_params=_params(("parallel", "parallel")),
        name="hyena_gate",
    )(p, p, p, short_w.astype(F32), short_b.reshape(1, uw).astype(F32))


def _hdot(a, b):
    return jnp.dot(a, b, preferred_element_type=F32, precision=lax.Precision.HIGHEST)


def _filter_kernel(fr_ref, w1_ref, b1_ref, fq_ref, w2_ref, b2_ref, w3_ref, dec_ref, k_ref, s_ref, *, seq, tr):
    i = pl.program_id(0)
    r = i * tr + lax.broadcasted_iota(jnp.int32, (tr, 1), 0)
    pos = jnp.where(r < seq, r, 2 * seq - 1 - r).astype(F32)
    t = pos * (1.0 / (seq - 1))
    w = pos * (2.0 * math.pi / seq)
    lane = lax.broadcasted_iota(jnp.int32, (tr, LANES), 1)
    bands = (HY_EMB - 1) // 2
    ang = w * fr_ref[...]
    z = jnp.where(lane == 0, t,
                  jnp.where(lane <= bands, jnp.cos(ang),
                            jnp.where(lane <= 2 * bands, -jnp.sin(ang), 0.0)))
    h = jnp.sin(fq_ref[0:1] * (_hdot(z, w1_ref[...]) + b1_ref[...]))
    h = jnp.sin(fq_ref[1:2] * (_hdot(h, w2_ref[...]) + b2_ref[...]))
    h = _hdot(h, w3_ref[...]) * jnp.exp(-t * jnp.abs(dec_ref[...]))
    h = jnp.where(r == seq, 0.0, h)
    k_ref[...] = h.astype(k_ref.dtype)

    @pl.when(i == 0)
    def _():
        s_ref[...] = jnp.zeros(s_ref.shape, F32)

    s_ref[...] += jnp.sum(jnp.abs(h), axis=0, keepdims=True)


def hyena_filter(seq, w1, b1, freq, w2, b2, w3, decay, *, tr=256):
    tr = _tile(seq, tr)
    nhalf = seq // tr
    bands = (HY_EMB - 1) // 2
    fr = jnp.linspace(1e-4, bands - 1, bands, dtype=F32)
    fr_lanes = jnp.concatenate([jnp.zeros((1,), F32), fr, fr, jnp.zeros((LANES - HY_EMB,), F32)]).reshape(1, LANES)
    w1p = jnp.pad(w1.astype(F32), ((0, LANES - HY_EMB), (0, 0)))
    const = lambda i: (0, 0)
    half = lambda i: (0, i // nhalf)
    return pl.pallas_call(
        functools.partial(_filter_kernel, seq=seq, tr=tr),
        grid=(2 * nhalf,),
        in_specs=[pl.BlockSpec((1, LANES), const),
                  pl.BlockSpec((LANES, HY_FFN), const),
                  pl.BlockSpec((1, HY_FFN), const),
                  pl.BlockSpec((2, HY_FFN), const),
                  pl.BlockSpec((HY_FFN, HY_FFN), const),
                  pl.BlockSpec((1, HY_FFN), const),
                  pl.BlockSpec((HY_FFN, B_WIDTH), half),
                  pl.BlockSpec((1, B_WIDTH), half)],
        out_specs=[pl.BlockSpec((tr, B_WIDTH), lambda i: (i, 0)),
                   pl.BlockSpec((1, B_WIDTH), const)],
        out_shape=[jax.ShapeDtypeStruct((2 * seq, B_WIDTH), BF16),
                   jax.ShapeDtypeStruct((1, B_WIDTH), F32)],
        compiler_params=_params(("arbitrary",)),
        name="hyena_filter",
    )(fr_lanes, w1p, b1.reshape(1, HY_FFN).astype(F32), freq.astype(F32), w2.astype(F32),
      b2.reshape(1, HY_FFN).astype(F32), w3.astype(F32), decay.reshape(1, 2 * B_WIDTH).astype(F32))


def _bdot(a, b):
    return jnp.dot(a, b, preferred_element_type=F32)


def dft_tables(n1):
    n = n1 * DFT_N2
    k1 = jnp.arange(n1, dtype=jnp.int32)[None, :, None]
    m1 = jnp.arange(n1, dtype=jnp.int32)[None, None, :]
    n2 = jnp.arange(DFT_N2, dtype=jnp.int32)[:, None, None]
    ang = ((k1 * (DFT_N2 * m1 + n2)) % n).astype(F32) * (2.0 * math.pi / n)
    gc = jnp.cos(ang).astype(BF16)
    gsn = (-jnp.sin(ang)).astype(BF16)
    hc = jnp.swapaxes(gc, 1, 2)
    hsn = jnp.swapaxes(gsn, 1, 2)
    a = jnp.arange(DFT_N2, dtype=jnp.int32)
    ang2 = ((a[:, None] * a[None, :]) % DFT_N2).astype(F32) * (2.0 * math.pi / DFT_N2)
    return gc, gsn, hc, hsn, jnp.cos(ang2).astype(BF16), jnp.sin(ang2).astype(BF16)


def _dft1_kernel(x_ref, gc_ref, gs_ref, ar_ref, ai_ref):
    x = x_ref[...].astype(BF16)
    ar_ref[...] = _bdot(gc_ref[...], x).astype(ar_ref.dtype)
    ai_ref[...] = _bdot(gs_ref[...], x).astype(ai_ref.dtype)


def dft_stage1(x, gc, gsn, n1, *, tc=1024):
    batch, k1rows, cols = x.shape
    c = cols // DFT_N2
    tc = _tile(c, tc)
    nc = c // tc
    out = jax.ShapeDtypeStruct((batch, n1, cols), BF16)
    g_spec = pl.BlockSpec((None, n1, k1rows), lambda n2, b, ci: (n2, 0, 0))
    o_spec = pl.BlockSpec((None, n1, tc), lambda n2, b, ci: (b, 0, n2 * nc + ci))
    return pl.pallas_call(
        _dft1_kernel,
        grid=(DFT_N2, batch, nc),
        in_specs=[pl.BlockSpec((None, k1rows, tc), lambda n2, b, ci: (b, 0, n2 * nc + ci)), g_spec, g_spec],
        out_specs=[o_spec, o_spec],
        out_shape=[out, out],
        compiler_params=_params(("parallel", "parallel", "parallel")),
        name="dft_stage1",
    )(x, gc, gsn)


def _filter_spectrum_kernel(ar_ref, ai_ref, s_ref, fc_ref, fs_ref, kr_ref, ki_ref, *, kb, n):
    fc, fs = fc_ref[...], fs_ref[...]
    scale = 1.0 / (s_ref[...] * n)
    for j in range(kb):
        ar, ai = ar_ref[j], ai_ref[j]
        kr_ref[j] = (_bdot(fc, ar) + _bdot(fs, ai)) * scale
        ki_ref[j] = (_bdot(fc, ai) - _bdot(fs, ar)) * scale


def filter_spectrum(ar, ai, asum, fc, fs, n1, *, kb=2, tc=512):
    c = asum.shape[1]
    kb = _tile(n1, kb)
    tc = _tile(c, tc)
    a_spec = pl.BlockSpec((kb, DFT_N2, tc), lambda k, ci: (k, 0, ci))
    f_spec = pl.BlockSpec((DFT_N2, DFT_N2), lambda k, ci: (0, 0))
    out = jax.ShapeDtypeStruct((n1, DFT_N2, c), F32)
    return pl.pallas_call(
        functools.partial(_filter_spectrum_kernel, kb=kb, n=n1 * DFT_N2),
        grid=(n1 // kb, c // tc),
        in_specs=[a_spec, a_spec, pl.BlockSpec((1, tc), lambda k, ci: (0, ci)), f_spec, f_spec],
        out_specs=[a_spec, a_spec],
        out_shape=[out, out],
        compiler_params=_params(("parallel", "parallel")),
        name="filter_spectrum",
    )(ar.reshape(n1, DFT_N2, c), ai.reshape(n1, DFT_N2, c), asum, fc, fs)


def _spectral_kernel(ar_ref, ai_ref, kr_ref, ki_ref, fc_ref, fs_ref, br_ref, bi_ref, *, kb):
    fc, fs = fc_ref[...], fs_ref[...]
    for j in range(kb):
        ar, ai = ar_ref[j], ai_ref[j]
        xr = _bdot(fc, ar) + _bdot(fs, ai)
        xi = _bdot(fc, ai) - _bdot(fs, ar)
        kr, ki = kr_ref[j], ki_ref[j]
        yr = (xr * kr - xi * ki).astype(BF16)
        yi = (xr * ki + xi * kr).astype(BF16)
        br_ref[j] = (_bdot(fc, yr) - _bdot(fs, yi)).astype(br_ref.dtype)
        bi_ref[j] = (_bdot(fc, yi) + _bdot(fs, yr)).astype(bi_ref.dtype)


def spectral_multiply(ar, ai, kr, ki, fc, fs, n1, *, kb=2, tc=512):
    batch = ar.shape[0]
    c = kr.shape[2]
    kb = _tile(n1, kb)
    tc = _tile(c, tc)
    a_spec = pl.BlockSpec((None, kb, DFT_N2, tc), lambda k, ci, b: (b, k, 0, ci))
    k_spec = pl.BlockSpec((kb, DFT_N2, tc), lambda k, ci, b: (k, 0, ci))
    f_spec = pl.BlockSpec((DFT_N2, DFT_N2), lambda k, ci, b: (0, 0))
    out = jax.ShapeDtypeStruct((batch, n1, DFT_N2, c), BF16)
    br, bi = pl.pallas_call(
        functools.partial(_spectral_kernel, kb=kb),
        grid=(n1 // kb, c // tc, batch),
        in_specs=[a_spec, a_spec, k_spec, k_spec, f_spec, f_spec],
        out_specs=[a_spec, a_spec],
        out_shape=[out, out],
        compiler_params=_params(("parallel", "parallel", "parallel")),
        name="spectral_multiply",
    )(ar.reshape(batch, n1, DFT_N2, c), ai.reshape(batch, n1, DFT_N2, c), kr, ki, fc, fs)
    return br.reshape(batch, n1, DFT_N2 * c), bi.reshape(batch, n1, DFT_N2 * c)


def _idft_out_kernel(br_ref, bi_ref, hc_ref, hs_ref, zz_ref, x0_ref, bias_ref, o_ref):
    y = _bdot(hc_ref[...], br_ref[...]) + _bdot(hs_ref[...], bi_ref[...])
    zz = zz_ref[...].astype(F32)
    o_ref[...] = ((y + zz * bias_ref[...]) * x0_ref[...].astype(F32)).astype(o_ref.dtype)


def idft_output(br, bi, hc, hsn, zz, x0, bias, n1, *, tc=1024):
    batch, _, cols = br.shape
    c = cols // DFT_N2
    tc = _tile(c, tc)
    nc = c // tc
    half = n1 // 2
    b_spec = pl.BlockSpec((None, n1, tc), lambda n2, b, ci: (b, 0, n2 * nc + ci))
    h_spec = pl.BlockSpec((None, half, n1), lambda n2, b, ci: (n2, 0, 0))
    z_spec = pl.BlockSpec((None, half, tc), lambda n2, b, ci: (b, 0, n2 * nc + ci))
    return pl.pallas_call(
        _idft_out_kernel,
        grid=(DFT_N2, batch, nc),
        in_specs=[b_spec, b_spec, h_spec, h_spec, z_spec, z_spec,
                  pl.BlockSpec((1, tc), lambda n2, b, ci: (0, ci))],
        out_specs=z_spec,
        out_shape=jax.ShapeDtypeStruct((batch, half, cols), BF16),
        compiler_params=_params(("parallel", "parallel", "parallel")),
        name="idft_output",
    )(br, bi, hc, hsn, zz, x0, bias)


def hyena_mixer(p, u_col, prm, e, batch, seq):
    n1 = 2 * seq // DFT_N2
    half = n1 // 2
    gc, gsn, hc, hsn, fc, fs = dft_tables(n1)
    kern, asum = hyena_filter(seq, prm["hy_w1"][e], prm["hy_b1"][e], prm["hy_freq"][e], prm["hy_w2"][e],
                              prm["hy_b2"][e], prm["hy_w3"][e], prm["hy_decay"][e])
    far, fai = dft_stage1(kern.reshape(1, n1, DFT_N2 * B_WIDTH), gc, gsn, n1)
    kr, ki = filter_spectrum(far[0], fai[0], asum, fc, fs, n1)
    zz, x0 = hyena_gate(p, u_col, prm["hy_short_w"][e], prm["hy_short_b"][e], batch, seq)
    zz3 = zz.reshape(batch, half, DFT_N2 * B_WIDTH)
    ar, ai = dft_stage1(zz3, gc[:, :, :half], gsn[:, :, :half], n1)
    br, bi = spectral_multiply(ar, ai, kr, ki, fc, fs, n1)
    yb = idft_output(br, bi, hc[:, :half], hsn[:, :half], zz3, x0.reshape(batch, half, DFT_N2 * B_WIDTH),
                     prm["hy_bias"][e].reshape(1, B_WIDTH).astype(F32), n1)
    return yb.reshape(batch * seq, B_WIDTH)


def _outproj_kernel(y1_ref, y2_ref, y3_ref, g_ref, x_ref, w_ref, o_ref, a_ref):
    @pl.when(pl.program_id(1) == 0)
    def _():
        off = 0
        for y_ref in (y1_ref, y2_ref, y3_ref):
            wd = y_ref.shape[1]
            g = g_ref[:, off:off + wd].astype(F32)
            a_ref[:, off:off + wd] = (y_ref[...].astype(F32) * (g * jax.nn.sigmoid(g))).astype(BF16)
            off += wd

    o_ref[...] = x_ref[...] + jnp.dot(a_ref[...], w_ref[...], preferred_element_type=F32)


def gated_outproj(y1, y2, y3, p, x, w, *, tm=512, tn=512):
    rows, d = x.shape
    tm = _tile(rows, tm)
    tn = _tile(d, tn)
    row = lambda i, j: (i, 0)
    return pl.pallas_call(
        _outproj_kernel,
        grid=(rows // tm, d // tn),
        in_specs=[pl.BlockSpec((tm, y1.shape[1]), row),
                  pl.BlockSpec((tm, y2.shape[1]), row),
                  pl.BlockSpec((tm, y3.shape[1]), row),
                  pl.BlockSpec((tm, MIX_WIDTH), row),
                  pl.BlockSpec((tm, tn), lambda i, j: (i, j)),
                  pl.BlockSpec((MIX_WIDTH, tn), lambda i, j: (0, j))],
        out_specs=pl.BlockSpec((tm, tn), lambda i, j: (i, j)),
        out_shape=jax.ShapeDtypeStruct((rows, d), F32),
        scratch_shapes=[pltpu.VMEM((tm, MIX_WIDTH), BF16)],
        compiler_params=_params(("parallel", "arbitrary")),
        name="gated_outproj",
    )(y1, y2, y3, p, x, w)


def _split_cols(w, sizes):
    out, off = [], 0
    for s in sizes:
        out.append(w[:, off:off + s])
        off += s
    return out


def _rope_tables(seq, dim, passthrough):
    half = dim // 2
    inv = ROPE_THETA ** (-jnp.arange(0, dim, 2, dtype=F32) / dim)
    ang = jnp.arange(seq, dtype=F32)[:, None] * inv[None, :]
    cos, sin = jnp.cos(ang), jnp.sin(ang)
    rest = jnp.full((seq, LANES - dim), 1.0 if passthrough else 0.0, F32)
    zeros = jnp.zeros((seq, LANES - half), F32)
    cf = jnp.concatenate([cos, cos, rest], axis=1)
    sa = jnp.concatenate([-sin, zeros], axis=1)
    sb = jnp.concatenate([jnp.zeros((seq, half), F32), sin, jnp.zeros((seq, LANES - dim), F32)], axis=1)
    return cf, sa, sb


def _memory_kv(mem, prm, l):
    batch = mem.shape[0]
    return normed_matmul(mem.reshape(batch * N_MEM, D_MODEL), prm["mem_norm_gain"][l],
                         prm["w_mem_kv"][l].astype(BF16), tm=256)


def _even_layer(x, mem, prm, l, e, batch, seq, rope_a):
    aq, ak, av, ag, bu, bg, mq, mg = _split_cols(prm["w_in_even"][e], EVEN_SPLITS)
    w_in = jnp.concatenate([ag, bg, mg, mq, bu, aq, ak, av], axis=1).astype(BF16)
    p = normed_matmul(x, prm["norm_gain"][l], w_in)
    scale = HEAD_DIM ** -0.5
    q = head_prep(p, EV_AQ, A_HEADS, prm["a_q_norm"][e], rope_a, scale, seq)
    k = head_prep(p, EV_AK, A_KV_HEADS, prm["a_k_norm"][e], rope_a, 1.0, seq)
    ya = window_attention(q, k, p, EV_AV, prm["a_sink"][e], batch, seq)
    yb = hyena_mixer(p, EV_BU, prm, e, batch, seq)
    kvm = _memory_kv(mem, prm, l)
    ym = memory_attention(p, EV_MQ, kvm, prm["mem_q_norm"][l], prm["mem_k_norm"][l], batch, seq)
    return gated_outproj(ya, yb, ym, p, x, prm["w_out_even"][e].astype(BF16))


def _odd_layer(x, mem, prm, l, o, batch, seq, rope_a, rope_d):
    cq, ck, cv, cg, dqa, dkva, dg, mq, mg = _split_cols(prm["w_in_odd"][o], ODD_SPLITS)
    pad = jnp.zeros((D_MODEL, OD_WIDTH - (OD_KR + D_ROPE)), F32)
    w_in = jnp.concatenate([cg, dg, mg, mq, cq, ck, cv, dqa, dkva, pad], axis=1).astype(BF16)
    p = normed_matmul(x, prm["norm_gain"][l], w_in)
    cscale = C_QK_DIM ** -0.5
    q = head_prep(p, OD_CQ, 2 * C_HEADS, prm["c_q_norm"][o], rope_a, cscale, seq)
    k = head_prep(p, OD_CK, 2 * C_HEADS, prm["c_k_norm"][o], rope_a, 1.0, seq)
    lam_init = 0.8 - 0.6 * math.exp(-0.3 * l)
    yc = diff_attention(q, k, p, OD_CV, prm["c_lambda"][o], prm["c_out_norm"][o], lam_init, batch, seq)
    wq = prm["w_q_b"][o].reshape(D_Q_RANK, D_HEADS, D_NOPE + D_ROPE)
    wq_rope = jnp.pad(wq[:, :, D_NOPE:], ((0, 0), (0, 0), (0, LANES - D_ROPE)))
    wq2 = jnp.concatenate([wq[:, :, :D_NOPE].reshape(D_Q_RANK, -1), wq_rope.reshape(D_Q_RANK, -1)], axis=1)
    q2 = normed_matmul(p, prm["d_q_a_norm"][o], wq2.astype(BF16), col_block=OD_DQA // D_Q_RANK)
    wkv = prm["w_kv_b"][o].reshape(D_KV_RANK, D_HEADS, D_NOPE + D_V)
    wkv2 = jnp.concatenate([wkv[:, :, :D_NOPE].reshape(D_KV_RANK, -1), wkv[:, :, D_NOPE:].reshape(D_KV_RANK, -1)],
                           axis=1)
    kv2 = normed_matmul(p, prm["d_kv_a_norm"][o], wkv2.astype(BF16), col_block=OD_CKV // D_KV_RANK)
    dscale = (D_NOPE + D_ROPE) ** -0.5
    qd = mla_prep(q2, 0, q2, D_HEADS * LANES, False, prm["d_q_norm"][o], rope_d, dscale, seq)
    kd = mla_prep(kv2, 0, p, OD_KR, True, prm["d_k_norm"][o], rope_d, 1.0, seq)
    yd = mla_attention(qd, kd, kv2, batch, seq)
    kvm = _memory_kv(mem, prm, l)
    ym = memory_attention(p, OD_MQ, kvm, prm["mem_q_norm"][l], prm["mem_k_norm"][l], batch, seq)
    return gated_outproj(yc, yd, ym, p, x, prm["w_out_odd"][o].astype(BF16))


def _trunk(x, mem, prm):
    batch, seq, d = x.shape
    depth = prm["norm_gain"].shape[0]
    rope_a = _rope_tables(seq, ROT_DIM, True)
    rope_d = _rope_tables(seq, D_ROPE, False)
    h = x.reshape(batch * seq, d)
    for l in range(depth):
        if l % 2 == 0:
            h = _even_layer(h, mem, prm, l, l // 2, batch, seq, rope_a)
        else:
            h = _odd_layer(h, mem, prm, l, l // 2, batch, seq, rope_a, rope_d)
    return h.reshape(batch, seq, d)


def kernel(x_prompt, x_sample, mem_prompt, mem_sample, norm_gain, mem_norm_gain, w_mem_kv, mem_q_norm, mem_k_norm, w_in_even, w_out_even, a_q_norm, a_k_norm, a_sink, hy_short_w, hy_short_b, hy_w1, hy_b1, hy_freq, hy_w2, hy_b2, hy_w3, hy_decay, hy_bias, w_in_odd, w_out_odd, c_q_norm, c_k_norm, c_lambda, c_out_norm, d_q_a_norm, w_q_b, d_kv_a_norm, w_kv_b, d_q_norm, d_k_norm):
    prm = dict(norm_gain=norm_gain, mem_norm_gain=mem_norm_gain, w_mem_kv=w_mem_kv,
               mem_q_norm=mem_q_norm, mem_k_norm=mem_k_norm, w_in_even=w_in_even,
               w_out_even=w_out_even, a_q_norm=a_q_norm, a_k_norm=a_k_norm, a_sink=a_sink,
               hy_short_w=hy_short_w, hy_short_b=hy_short_b, hy_w1=hy_w1, hy_b1=hy_b1,
               hy_freq=hy_freq, hy_w2=hy_w2, hy_b2=hy_b2, hy_w3=hy_w3, hy_decay=hy_decay,
               hy_bias=hy_bias, w_in_odd=w_in_odd, w_out_odd=w_out_odd, c_q_norm=c_q_norm,
               c_k_norm=c_k_norm, c_lambda=c_lambda, c_out_norm=c_out_norm, d_q_a_norm=d_q_a_norm,
               w_q_b=w_q_b, d_kv_a_norm=d_kv_a_norm, w_kv_b=w_kv_b, d_q_norm=d_q_norm,
               d_k_norm=d_k_norm)
    return (_trunk(x_prompt, mem_prompt, prm), _trunk(x_sample, mem_sample, prm))
```

```python
import functools
import math

import jax
import jax.numpy as jnp
from jax import lax
from jax.experimental import pallas as pl
from jax.experimental.pallas import tpu as pltpu

F32 = jnp.float32
BF16 = jnp.bfloat16

D_MODEL = 2048
N_MEM = 256
HEAD_DIM = 128
ROPE_THETA = 500000.0
ROT_DIM = HEAD_DIM // 4
EPS = 1e-6
BLOCK = 128
WINDOW = 128
NEG_INF = -1e30
A_HEADS = 8
A_KV_HEADS = 2
A_WIDTH = A_HEADS * HEAD_DIM
B_WIDTH = 1024
HY_EMB = 33
HY_FFN = 64
M_HEADS = 4
M_WIDTH = M_HEADS * HEAD_DIM
C_HEADS = 4
C_QK_DIM = 128
C_V_DIM = 256
C_WIDTH = C_HEADS * C_V_DIM
D_HEADS = 8
D_NOPE = 128
D_ROPE = 64
D_V = 128
D_Q_RANK = 512
D_KV_RANK = 256
D_WIDTH = D_HEADS * D_V
MIX_WIDTH = 2560
EVEN_SPLITS = (A_WIDTH, A_KV_HEADS * HEAD_DIM, A_KV_HEADS * HEAD_DIM, A_WIDTH,
               3 * B_WIDTH, B_WIDTH, M_WIDTH, M_WIDTH)
ODD_SPLITS = (C_HEADS * 2 * C_QK_DIM, C_HEADS * 2 * C_QK_DIM, C_WIDTH, C_WIDTH,
              D_Q_RANK, D_KV_RANK + D_ROPE, D_WIDTH, M_WIDTH, M_WIDTH)

LANES = 128
DFT_N2 = 128
VMEM_LIMIT = 48 * 1024 * 1024

EV_GATE, EV_MQ, EV_BU, EV_AQ, EV_AK, EV_AV = 0, 2560, 3072, 6144, 7168, 7424
OD_GATE, OD_MQ, OD_CQ, OD_CK, OD_CV, OD_DQA, OD_CKV, OD_KR = 0, 2560, 3072, 4096, 5120, 6144, 6656, 6912
OD_WIDTH = 7168


def _tile(n, pref):
    t = min(n, pref)
    assert n % t == 0, (n, t)
    return t


def _params(sem):
    return pltpu.CompilerParams(dimension_semantics=sem, vmem_limit_bytes=VMEM_LIMIT)


def _rms(x, gain):
    ms = jnp.mean(x * x, axis=-1, keepdims=True)
    return x * lax.rsqrt(ms + EPS) * gain


def _nmm_kernel(x_ref, g_ref, w_ref, o_ref, h_ref):
    @pl.when(pl.program_id(1) == 0)
    def _():
        h_ref[...] = _rms(x_ref[...].astype(F32), g_ref[...]).astype(BF16)

    o_ref[...] = jnp.dot(h_ref[...], w_ref[...], preferred_element_type=F32).astype(o_ref.dtype)


def normed_matmul(x, gain, w, *, col_block=0, tm=512, tn=512, out_dtype=BF16):
    rows = x.shape[0]
    k, n = w.shape
    tm = _tile(rows, tm)
    tn = _tile(n, tn)
    return pl.pallas_call(
        _nmm_kernel,
        grid=(rows // tm, n // tn),
        in_specs=[pl.BlockSpec((tm, k), lambda i, j: (i, col_block)),
                  pl.BlockSpec((1, k), lambda i, j: (0, 0)),
                  pl.BlockSpec((k, tn), lambda i, j: (0, j))],
        out_specs=pl.BlockSpec((tm, tn), lambda i, j: (i, j)),
        out_shape=jax.ShapeDtypeStruct((rows, n), out_dtype),
        scratch_shapes=[pltpu.VMEM((tm, k), BF16)],
        compiler_params=_params(("parallel", "arbitrary")),
        name="normed_matmul",
    )(x, gain.reshape(1, k).astype(F32), w)


def _rope(y, cf, sa, sb, half):
    return y * cf + pltpu.roll(y, LANES - half, 1) * sa + pltpu.roll(y, half, 1) * sb


def _head_prep_kernel(x_ref, g_ref, cf_ref, sa_ref, sb_ref, o_ref, *, nheads, scale):
    g = g_ref[...]
    cf, sa, sb = cf_ref[...], sa_ref[...], sb_ref[...]
    for h in range(nheads):
        cols = slice(h * LANES, (h + 1) * LANES)
        y = _rms(x_ref[:, cols].astype(F32), g)
        y = _rope(y, cf, sa, sb, ROT_DIM // 2)
        o_ref[:, cols] = (y * scale).astype(o_ref.dtype)


def head_prep(p, col, nheads, gain, tables, scale, seq, *, tm=256):
    rows = p.shape[0]
    width = nheads * LANES
    assert col % width == 0
    tm = _tile(seq, tm)
    nblk = seq // tm
    tab_spec = pl.BlockSpec((tm, LANES), lambda i: (i % nblk, 0))
    return pl.pallas_call(
        functools.partial(_head_prep_kernel, nheads=nheads, scale=scale),
        grid=(rows // tm,),
        in_specs=[pl.BlockSpec((tm, width), lambda i: (i, col // width)),
                  pl.BlockSpec((1, LANES), lambda i: (0, 0)),
                  tab_spec, tab_spec, tab_spec],
        out_specs=pl.BlockSpec((tm, width), lambda i: (i, 0)),
        out_shape=jax.ShapeDtypeStruct((rows, width), BF16),
        compiler_params=_params(("parallel",)),
        name="head_prep",
    )(p, gain.reshape(1, LANES).astype(F32), *tables)


def _window_kernel(sink_ref, q_ref, kp_ref, kc_ref, kn_ref, vp_ref, vc_ref, vn_ref, o_ref, *, nblk):
    n = pl.program_id(1)
    qi = lax.broadcasted_iota(jnp.int32, (BLOCK, 3 * BLOCK), 0)
    kj = lax.broadcasted_iota(jnp.int32, (BLOCK, 3 * BLOCK), 1)
    in_win = jnp.abs(kj - BLOCK - qi) <= WINDOW
    kpos = (n - 1) * BLOCK + kj
    mask = in_win & (kpos >= 0) & (kpos < nblk * BLOCK)
    group = A_HEADS // A_KV_HEADS
    for kv in range(A_KV_HEADS):
        cols = slice(kv * HEAD_DIM, (kv + 1) * HEAD_DIM)
        kw = jnp.concatenate([kp_ref[:, cols], kc_ref[:, cols], kn_ref[:, cols]], axis=0).astype(BF16)
        vw = jnp.concatenate([vp_ref[:, cols], vc_ref[:, cols], vn_ref[:, cols]], axis=0).astype(BF16)
        for g in range(group):
            h = kv * group + g
            q = q_ref[:, h * HEAD_DIM:(h + 1) * HEAD_DIM]
            s = lax.dot_general(q, kw, (((1,), (1,)), ((), ())), preferred_element_type=F32)
            s = jnp.where(mask, s, NEG_INF)
            sk = sink_ref[h]
            m = jnp.maximum(jnp.max(s, axis=-1, keepdims=True), sk)
            pr = jnp.exp(s - m)
            denom = jnp.sum(pr, axis=-1, keepdims=True) + jnp.exp(sk - m)
            pr = pr / denom
            o = jnp.dot(pr.astype(BF16), vw, preferred_element_type=F32)
            o_ref[:, h * HEAD_DIM:(h + 1) * HEAD_DIM] = o.astype(o_ref.dtype)


def window_attention(q, k, p, v_col, sink, batch, seq):
    nblk = seq // BLOCK
    kvw = A_KV_HEADS * HEAD_DIM
    vb = v_col // kvw

    def prev(b, n):
        return b * nblk + jnp.maximum(n - 1, 0)

    def cur(b, n):
        return b * nblk + n

    def nxt(b, n):
        return b * nblk + jnp.minimum(n + 1, nblk - 1)

    return pl.pallas_call(
        functools.partial(_window_kernel, nblk=nblk),
        grid=(batch, nblk),
        in_specs=[pl.BlockSpec(memory_space=pltpu.SMEM),
                  pl.BlockSpec((BLOCK, A_WIDTH), lambda b, n: (cur(b, n), 0)),
                  pl.BlockSpec((BLOCK, kvw), lambda b, n: (prev(b, n), 0)),
                  pl.BlockSpec((BLOCK, kvw), lambda b, n: (cur(b, n), 0)),
                  pl.BlockSpec((BLOCK, kvw), lambda b, n: (nxt(b, n), 0)),
                  pl.BlockSpec((BLOCK, kvw), lambda b, n: (prev(b, n), vb)),
                  pl.BlockSpec((BLOCK, kvw), lambda b, n: (cur(b, n), vb)),
                  pl.BlockSpec((BLOCK, kvw), lambda b, n: (nxt(b, n), vb))],
        out_specs=pl.BlockSpec((BLOCK, A_WIDTH), lambda b, n: (cur(b, n), 0)),
        out_shape=jax.ShapeDtypeStruct((batch * seq, A_WIDTH), BF16),
        compiler_params=_params(("parallel", "parallel")),
        name="window_attention",
    )(sink.astype(F32), q, k, k, k, p, p, p)


def _mem_kernel(q_ref, kv_ref, qg_ref, kg_ref, o_ref):
    qg, kg = qg_ref[...], kg_ref[...]
    scale = HEAD_DIM ** -0.5
    for h in range(M_HEADS):
        cols = slice(h * HEAD_DIM, (h + 1) * HEAD_DIM)
        q = (_rms(q_ref[:, cols].astype(F32), qg) * scale).astype(BF16)
        k = _rms(kv_ref[:, cols].astype(F32), kg).astype(BF16)
        v = kv_ref[:, M_WIDTH + h * HEAD_DIM:M_WIDTH + (h + 1) * HEAD_DIM].astype(BF16)
        s = lax.dot_general(q, k, (((1,), (1,)), ((), ())), preferred_element_type=F32)
        m = jnp.max(s, axis=-1, keepdims=True)
        pr = jnp.exp(s - m)
        pr = pr / jnp.sum(pr, axis=-1, keepdims=True)
        o = jnp.dot(pr.astype(BF16), v, preferred_element_type=F32)
        o_ref[:, cols] = o.astype(o_ref.dtype)


def memory_attention(p, q_col, kvm, q_gain, k_gain, batch, seq, *, tq=512):
    tq = _tile(seq, tq)
    nq = seq // tq
    return pl.pallas_call(
        _mem_kernel,
        grid=(batch, nq),
        in_specs=[pl.BlockSpec((tq, M_WIDTH), lambda b, i: (b * nq + i, q_col // M_WIDTH)),
                  pl.BlockSpec((N_MEM, 2 * M_WIDTH), lambda b, i: (b, 0)),
                  pl.BlockSpec((1, HEAD_DIM), lambda b, i: (0, 0)),
                  pl.BlockSpec((1, HEAD_DIM), lambda b, i: (0, 0))],
        out_specs=pl.BlockSpec((tq, M_WIDTH), lambda b, i: (b * nq + i, 0)),
        out_shape=jax.ShapeDtypeStruct((batch * seq, M_WIDTH), BF16),
        compiler_params=_params(("parallel", "parallel")),
        name="memory_attention",
    )(p, kvm, q_gain.reshape(1, HEAD_DIM).astype(F32), k_gain.reshape(1, HEAD_DIM).astype(F32))


LOG2E = math.log2(math.e)


def _chunk_rows(chunk, tk):
    return pl.ds(pl.multiple_of(chunk * tk, tk), tk)


def _pipelined_chunks(scores, consume, nk):
    scores(0, 0)

    def body(i, carry):
        scores(2 * i + 1, 1)
        consume(2 * i, 0)
        scores(jnp.minimum(2 * i + 2, nk - 1), 0)
        consume(2 * i + 1, 1)
        return carry

    lax.fori_loop(0, nk // 2, body, 0)


def _diff_flash_kernel(q_ref, k_ref, v_ref, lam_ref, g_ref, o_ref, m_ref, l_ref, acc_ref, s_ref, *, tk, nk, lam_init):
    m_ref[...] = jnp.full(m_ref.shape, NEG_INF, F32)
    l_ref[...] = jnp.zeros(l_ref.shape, F32)
    acc_ref[...] = jnp.zeros(acc_ref.shape, F32)

    def scores(chunk, slot):
        rows = _chunk_rows(chunk, tk)
        for c in range(2):
            cols = slice(c * C_QK_DIM, (c + 1) * C_QK_DIM)
            s_ref[slot, c] = lax.dot_general(q_ref[:, cols], k_ref[rows, cols], (((1,), (1,)), ((), ())),
                                             preferred_element_type=F32)

    def consume(chunk, slot):
        v = v_ref[_chunk_rows(chunk, tk), :]
        for c in range(2):
            s = s_ref[slot, c]
            m_prev = m_ref[c]
            m_new = jnp.maximum(m_prev, jnp.max(s, axis=-1, keepdims=True))
            alpha = jnp.exp2(m_prev - m_new)
            pr = jnp.exp2(s - pltpu.repeat(m_new, tk // LANES, axis=1))
            l_ref[c] = alpha * l_ref[c] + jnp.sum(pr, axis=-1, keepdims=True)
            acc_ref[c] = (pltpu.repeat(alpha, C_V_DIM // LANES, axis=1) * acc_ref[c]
                          + jnp.dot(pr.astype(BF16), v, preferred_element_type=F32))
            m_ref[c] = m_new

    _pipelined_chunks(scores, consume, nk)
    lv = lam_ref[...]
    lam = (jnp.exp(jnp.sum(lv[0:1] * lv[1:2], axis=-1, keepdims=True))
           - jnp.exp(jnp.sum(lv[2:3] * lv[3:4], axis=-1, keepdims=True)) + lam_init)
    o1 = acc_ref[0] / pltpu.repeat(l_ref[0], C_V_DIM // LANES, axis=1)
    o2 = acc_ref[1] / pltpu.repeat(l_ref[1], C_V_DIM // LANES, axis=1)
    o_ref[...] = (_rms(o1 - lam * o2, g_ref[...]) * (1.0 - lam_init)).astype(o_ref.dtype)


def _mla_flash_kernel(q_ref, k_ref, v_ref, o_ref, m_ref, acc_ref, s_ref, *, tk, nk):
    m_ref[...] = jnp.full(m_ref.shape, NEG_INF, F32)
    acc_ref[...] = jnp.zeros(acc_ref.shape, F32)
    ones = jnp.ones((tk, LANES), BF16)

    def scores(chunk, slot):
        s_ref[slot] = lax.dot_general(q_ref[...], k_ref[_chunk_rows(chunk, tk), :], (((1,), (1,)), ((), ())),
                                      preferred_element_type=F32)

    def consume(chunk, slot):
        s = s_ref[slot]
        m_prev = m_ref[...]
        m_new = jnp.maximum(m_prev, jnp.max(s, axis=-1, keepdims=True))
        alpha = jnp.exp2(m_prev - m_new)
        pr = jnp.exp2(s - pltpu.repeat(m_new, tk // LANES, axis=1))
        vv = jnp.concatenate([v_ref[_chunk_rows(chunk, tk), :], ones], axis=1)
        acc_ref[...] = (pltpu.repeat(alpha, 2, axis=1) * acc_ref[...]
                        + jnp.dot(pr.astype(BF16), vv, preferred_element_type=F32))
        m_ref[...] = m_new

    _pipelined_chunks(scores, consume, nk)
    o_ref[...] = (acc_ref[:, :D_V] / acc_ref[:, D_V:]).astype(o_ref.dtype)


def _flash_call(kernel, q, k, v, v_block0, extra, extra_specs, scratch, *, heads, dv, batch, seq, tq, tk, name):
    nq, nk = seq // tq, seq // tk
    qk_w = 2 * LANES
    return pl.pallas_call(
        functools.partial(kernel, tk=tk, nk=nk),
        grid=(batch, heads, nq),
        in_specs=[pl.BlockSpec((tq, qk_w), lambda b, h, i: (b * nq + i, h)),
                  pl.BlockSpec((seq, qk_w), lambda b, h, i: (b, h)),
                  pl.BlockSpec((seq, dv), lambda b, h, i: (b, v_block0 + h))] + extra_specs,
        out_specs=pl.BlockSpec((tq, dv), lambda b, h, i: (b * nq + i, h)),
        out_shape=jax.ShapeDtypeStruct((batch * seq, heads * dv), BF16),
        scratch_shapes=scratch,
        compiler_params=_params(("parallel", "parallel", "arbitrary")),
        name=name,
    )(q, k, v, *extra)


def diff_attention(q, k, p, v_col, c_lambda, out_gain, lam_init, batch, seq, *, tq=512, tk=512):
    tq, tk = _tile(seq, tq), _tile(seq // 2, tk)
    const = lambda b, h, i: (0, 0)
    return _flash_call(
        functools.partial(_diff_flash_kernel, lam_init=lam_init), q, k, p, v_col // C_V_DIM,
        [c_lambda.astype(F32), out_gain.reshape(1, C_V_DIM).astype(F32)],
        [pl.BlockSpec((4, C_QK_DIM), const), pl.BlockSpec((1, C_V_DIM), const)],
        [pltpu.VMEM((2, tq, LANES), F32), pltpu.VMEM((2, tq, LANES), F32), pltpu.VMEM((2, tq, C_V_DIM), F32),
         pltpu.VMEM((2, 2, tq, tk), F32)],
        heads=C_HEADS, dv=C_V_DIM, batch=batch, seq=seq, tq=tq, tk=tk, name="diff_attention")


def mla_attention(q, k, kv, batch, seq, *, tq=1024, tk=512):
    tq, tk = _tile(seq, tq), _tile(seq // 2, tk)
    return _flash_call(_mla_flash_kernel, q, k, kv, D_HEADS, [], [],
                       [pltpu.VMEM((tq, LANES), F32), pltpu.VMEM((tq, D_V + LANES), F32),
                        pltpu.VMEM((2, tq, tk), F32)],
                       heads=D_HEADS, dv=D_V, batch=batch, seq=seq, tq=tq, tk=tk, name="mla_attention")


def _mla_prep_kernel(n_ref, r_ref, gn_ref, gr_ref, cf_ref, sa_ref, sb_ref, o_ref, *, shared_rope, scale):
    gn, gr = gn_ref[...], gr_ref[...]
    cf, sa, sb = cf_ref[...], sa_ref[...], sb_ref[...]
    for h in range(D_HEADS):
        cols = slice(h * LANES, (h + 1) * LANES)
        xn = n_ref[:, cols].astype(F32)
        xr = (r_ref[...] if shared_rope else r_ref[:, cols]).astype(F32)
        ms = (jnp.sum(xn * xn, axis=-1, keepdims=True)
              + jnp.sum(xr * xr, axis=-1, keepdims=True)) * (1.0 / (D_NOPE + D_ROPE))
        inv = lax.rsqrt(ms + EPS)
        yn = xn * inv * gn
        yr = _rope(xr * inv * gr, cf, sa, sb, D_ROPE // 2)
        o_ref[:, 2 * h * LANES:(2 * h + 1) * LANES] = (yn * scale).astype(o_ref.dtype)
        o_ref[:, (2 * h + 1) * LANES:(2 * h + 2) * LANES] = (yr * scale).astype(o_ref.dtype)


def mla_prep(nope_arr, nope_col, rope_arr, rope_col, shared_rope, gain, tables, scale, seq, *, tm=256):
    rows = nope_arr.shape[0]
    width = D_HEADS * LANES
    tm = _tile(seq, tm)
    nblk = seq // tm
    rw = LANES if shared_rope else width
    assert nope_col % width == 0 and rope_col % rw == 0
    gn = gain[:D_NOPE].reshape(1, LANES).astype(F32)
    gr = jnp.pad(gain[D_NOPE:], (0, LANES - D_ROPE)).reshape(1, LANES).astype(F32)
    tab_spec = pl.BlockSpec((tm, LANES), lambda i: (i % nblk, 0))
    one = pl.BlockSpec((1, LANES), lambda i: (0, 0))
    return pl.pallas_call(
        functools.partial(_mla_prep_kernel, shared_rope=shared_rope, scale=scale),
        grid=(rows // tm,),
        in_specs=[pl.BlockSpec((tm, width), lambda i: (i, nope_col // width)),
                  pl.BlockSpec((tm, rw), lambda i: (i, rope_col // rw)),
                  one, one, tab_spec, tab_spec, tab_spec],
        out_specs=pl.BlockSpec((tm, 2 * width), lambda i: (i, 0)),
        out_shape=jax.ShapeDtypeStruct((rows, 2 * width), BF16),
        compiler_params=_params(("parallel",)),
        name="mla_prep",
    )(nope_arr, rope_arr, gn, gr, *tables)


HALO = 16


def _hyena_gate_kernel(u_ref, up_ref, un_ref, w_ref, b_ref, zz_ref, x0_ref, *, nblk, tm):
    li = pl.program_id(1)
    row = lax.broadcasted_iota(jnp.int32, (tm, B_WIDTH), 0)
    has_prev = (li > 0).astype(F32)
    has_next = (li < nblk - 1).astype(F32)

    def conv(part):
        cols = slice(part * B_WIDTH, (part + 1) * B_WIDTH)
        u = u_ref[:, cols].astype(F32)
        prev_row = up_ref[HALO - 1:HALO, cols].astype(F32) * has_prev
        next_row = un_ref[0:1, cols].astype(F32) * has_next
        above = jnp.where(row == 0, prev_row, pltpu.roll(u, 1, 0))
        below = jnp.where(row == tm - 1, next_row, pltpu.roll(u, tm - 1, 0))
        w = w_ref[:, cols]
        return above * w[0:1] + u * w[1:2] + below * w[2:3] + b_ref[:, cols]

    x0_ref[...] = conv(0).astype(x0_ref.dtype)
    zz_ref[...] = (conv(2) * conv(1)).astype(zz_ref.dtype)


def hyena_gate(p, u_col, short_w, short_b, batch, seq, *, tm=256):
    tm = _tile(seq, tm)
    nblk = seq // tm
    uw = 3 * B_WIDTH
    ub = u_col // uw
    hb = tm // HALO
    out = jax.ShapeDtypeStruct((batch * seq, B_WIDTH), BF16)
    return pl.pallas_call(
        functools.partial(_hyena_gate_kernel, nblk=nblk, tm=tm),
        grid=(batch, nblk),
        in_specs=[pl.BlockSpec((tm, uw), lambda b, i: (b * nblk + i, ub)),
                  pl.BlockSpec((HALO, uw), lambda b, i: (jnp.maximum((b * nblk + i) * hb - 1, 0), ub)),
                  pl.BlockSpec((HALO, uw),
                               lambda b, i: (jnp.minimum((b * nblk + i + 1) * hb, batch * nblk * hb - 1), ub)),
                  pl.BlockSpec((3, uw), lambda b, i: (0, 0)),
                  pl.BlockSpec((1, uw), lambda b, i: (0, 0))],
        out_specs=[pl.BlockSpec((tm, B_WIDTH), lambda b, i: (b * nblk + i, 0))] * 2,
        out_shape=[out, out],
        compiler_params=_params(("parallel", "parallel")),
        name="hyena_gate",
    )(p, p, p, short_w.astype(F32), short_b.reshape(1, uw).astype(F32))


def _hdot(a, b):
    return jnp.dot(a, b, preferred_element_type=F32, precision=lax.Precision.HIGHEST)


def _filter_kernel(fr_ref, w1_ref, b1_ref, fq_ref, w2_ref, b2_ref, w3_ref, dec_ref, k_ref, s_ref, *, seq, tr):
    i = pl.program_id(0)
    r = i * tr + lax.broadcasted_iota(jnp.int32, (tr, 1), 0)
    pos = jnp.where(r < seq, r, 2 * seq - 1 - r).astype(F32)
    t = pos * (1.0 / (seq - 1))
    w = pos * (2.0 * math.pi / seq)
    lane = lax.broadcasted_iota(jnp.int32, (tr, LANES), 1)
    bands = (HY_EMB - 1) // 2
    ang = w * fr_ref[...]
    z = jnp.where(lane == 0, t,
                  jnp.where(lane <= bands, jnp.cos(ang),
                            jnp.where(lane <= 2 * bands, -jnp.sin(ang), 0.0)))
    h = jnp.sin(fq_ref[0:1] * (_hdot(z, w1_ref[...]) + b1_ref[...]))
    h = jnp.sin(fq_ref[1:2] * (_hdot(h, w2_ref[...]) + b2_ref[...]))
    h = _hdot(h, w3_ref[...]) * jnp.exp(-t * jnp.abs(dec_ref[...]))
    h = jnp.where(r == seq, 0.0, h)
    k_ref[...] = h.astype(k_ref.dtype)

    @pl.when(i == 0)
    def _():
        s_ref[...] = jnp.zeros(s_ref.shape, F32)

    s_ref[...] += jnp.sum(jnp.abs(h), axis=0, keepdims=True)


def hyena_filter(seq, w1, b1, freq, w2, b2, w3, decay, *, tr=256):
    tr = _tile(seq, tr)
    nhalf = seq // tr
    bands = (HY_EMB - 1) // 2
    fr = jnp.linspace(1e-4, bands - 1, bands, dtype=F32)
    fr_lanes = jnp.concatenate([jnp.zeros((1,), F32), fr, fr, jnp.zeros((LANES - HY_EMB,), F32)]).reshape(1, LANES)
    w1p = jnp.pad(w1.astype(F32), ((0, LANES - HY_EMB), (0, 0)))
    const = lambda i: (0, 0)
    half = lambda i: (0, i // nhalf)
    return pl.pallas_call(
        functools.partial(_filter_kernel, seq=seq, tr=tr),
        grid=(2 * nhalf,),
        in_specs=[pl.BlockSpec((1, LANES), const),
                  pl.BlockSpec((LANES, HY_FFN), const),
                  pl.BlockSpec((1, HY_FFN), const),
                  pl.BlockSpec((2, HY_FFN), const),
                  pl.BlockSpec((HY_FFN, HY_FFN), const),
                  pl.BlockSpec((1, HY_FFN), const),
                  pl.BlockSpec((HY_FFN, B_WIDTH), half),
                  pl.BlockSpec((1, B_WIDTH), half)],
        out_specs=[pl.BlockSpec((tr, B_WIDTH), lambda i: (i, 0)),
                   pl.BlockSpec((1, B_WIDTH), const)],
        out_shape=[jax.ShapeDtypeStruct((2 * seq, B_WIDTH), BF16),
                   jax.ShapeDtypeStruct((1, B_WIDTH), F32)],
        compiler_params=_params(("arbitrary",)),
        name="hyena_filter",
    )(fr_lanes, w1p, b1.reshape(1, HY_FFN).astype(F32), freq.astype(F32), w2.astype(F32),
      b2.reshape(1, HY_FFN).astype(F32), w3.astype(F32), decay.reshape(1, 2 * B_WIDTH).astype(F32))


def _bdot(a, b):
    return jnp.dot(a, b, preferred_element_type=F32)


def dft_tables(n1):
    n = n1 * DFT_N2
    k1 = jnp.arange(n1, dtype=jnp.int32)[None, :, None]
    m1 = jnp.arange(n1, dtype=jnp.int32)[None, None, :]
    n2 = jnp.arange(DFT_N2, dtype=jnp.int32)[:, None, None]
    ang = ((k1 * (DFT_N2 * m1 + n2)) % n).astype(F32) * (2.0 * math.pi / n)
    gc = jnp.cos(ang).astype(BF16)
    gsn = (-jnp.sin(ang)).astype(BF16)
    hc = jnp.swapaxes(gc, 1, 2)
    hsn = jnp.swapaxes(gsn, 1, 2)
    a = jnp.arange(DFT_N2, dtype=jnp.int32)
    ang2 = ((a[:, None] * a[None, :]) % DFT_N2).astype(F32) * (2.0 * math.pi / DFT_N2)
    return gc, gsn, hc, hsn, jnp.cos(ang2).astype(BF16), jnp.sin(ang2).astype(BF16)


def _dft1_kernel(x_ref, gc_ref, gs_ref, ar_ref, ai_ref):
    x = x_ref[...].astype(BF16)
    ar_ref[...] = _bdot(gc_ref[...], x).astype(ar_ref.dtype)
    ai_ref[...] = _bdot(gs_ref[...], x).astype(ai_ref.dtype)


def dft_stage1(x, gc, gsn, n1, *, tc=1024):
    batch, k1rows, cols = x.shape
    c = cols // DFT_N2
    tc = _tile(c, tc)
    nc = c // tc
    out = jax.ShapeDtypeStruct((batch, n1, cols), BF16)
    g_spec = pl.BlockSpec((None, n1, k1rows), lambda n2, b, ci: (n2, 0, 0))
    o_spec = pl.BlockSpec((None, n1, tc), lambda n2, b, ci: (b, 0, n2 * nc + ci))
    return pl.pallas_call(
        _dft1_kernel,
        grid=(DFT_N2, batch, nc),
        in_specs=[pl.BlockSpec((None, k1rows, tc), lambda n2, b, ci: (b, 0, n2 * nc + ci)), g_spec, g_spec],
        out_specs=[o_spec, o_spec],
        out_shape=[out, out],
        compiler_params=_params(("parallel", "parallel", "parallel")),
        name="dft_stage1",
    )(x, gc, gsn)


def _filter_spectrum_kernel(ar_ref, ai_ref, s_ref, fc_ref, fs_ref, kr_ref, ki_ref, *, kb, n):
    fc, fs = fc_ref[...], fs_ref[...]
    scale = 1.0 / (s_ref[...] * n)
    for j in range(kb):
        ar, ai = ar_ref[j], ai_ref[j]
        kr_ref[j] = (_bdot(fc, ar) + _bdot(fs, ai)) * scale
        ki_ref[j] = (_bdot(fc, ai) - _bdot(fs, ar)) * scale


def filter_spectrum(ar, ai, asum, fc, fs, n1, *, kb=2, tc=512):
    c = asum.shape[1]
    kb = _tile(n1, kb)
    tc = _tile(c, tc)
    a_spec = pl.BlockSpec((kb, DFT_N2, tc), lambda k, ci: (k, 0, ci))
    f_spec = pl.BlockSpec((DFT_N2, DFT_N2), lambda k, ci: (0, 0))
    out = jax.ShapeDtypeStruct((n1, DFT_N2, c), F32)
    return pl.pallas_call(
        functools.partial(_filter_spectrum_kernel, kb=kb, n=n1 * DFT_N2),
        grid=(n1 // kb, c // tc),
        in_specs=[a_spec, a_spec, pl.BlockSpec((1, tc), lambda k, ci: (0, ci)), f_spec, f_spec],
        out_specs=[a_spec, a_spec],
        out_shape=[out, out],
        compiler_params=_params(("parallel", "parallel")),
        name="filter_spectrum",
    )(ar.reshape(n1, DFT_N2, c), ai.reshape(n1, DFT_N2, c), asum, fc, fs)


def _spectral_kernel(ar_ref, ai_ref, kr_ref, ki_ref, fc_ref, fs_ref, br_ref, bi_ref, *, kb):
    fc, fs = fc_ref[...], fs_ref[...]
    for j in range(kb):
        ar, ai = ar_ref[j], ai_ref[j]
        xr = _bdot(fc, ar) + _bdot(fs, ai)
        xi = _bdot(fc, ai) - _bdot(fs, ar)
        kr, ki = kr_ref[j], ki_ref[j]
        yr = (xr * kr - xi * ki).astype(BF16)
        yi = (xr * ki + xi * kr).astype(BF16)
        br_ref[j] = (_bdot(fc, yr) - _bdot(fs, yi)).astype(br_ref.dtype)
        bi_ref[j] = (_bdot(fc, yi) + _bdot(fs, yr)).astype(bi_ref.dtype)


def spectral_multiply(ar, ai, kr, ki, fc, fs, n1, *, kb=2, tc=512):
    batch = ar.shape[0]
    c = kr.shape[2]
    kb = _tile(n1, kb)
    tc = _tile(c, tc)
    a_spec = pl.BlockSpec((None, kb, DFT_N2, tc), lambda k, ci, b: (b, k, 0, ci))
    k_spec = pl.BlockSpec((kb, DFT_N2, tc), lambda k, ci, b: (k, 0, ci))
    f_spec = pl.BlockSpec((DFT_N2, DFT_N2), lambda k, ci, b: (0, 0))
    out = jax.ShapeDtypeStruct((batch, n1, DFT_N2, c), BF16)
    br, bi = pl.pallas_call(
        functools.partial(_spectral_kernel, kb=kb),
        grid=(n1 // kb, c // tc, batch),
        in_specs=[a_spec, a_spec, k_spec, k_spec, f_spec, f_spec],
        out_specs=[a_spec, a_spec],
        out_shape=[out, out],
        compiler_params=_params(("parallel", "parallel", "parallel")),
        name="spectral_multiply",
    )(ar.reshape(batch, n1, DFT_N2, c), ai.reshape(batch, n1, DFT_N2, c), kr, ki, fc, fs)
    return br.reshape(batch, n1, DFT_N2 * c), bi.reshape(batch, n1, DFT_N2 * c)


def _idft_out_kernel(br_ref, bi_ref, hc_ref, hs_ref, zz_ref, x0_ref, bias_ref, o_ref):
    y = _bdot(hc_ref[...], br_ref[...]) + _bdot(hs_ref[...], bi_ref[...])
    zz = zz_ref[...].astype(F32)
    o_ref[...] = ((y + zz * bias_ref[...]) * x0_ref[...].astype(F32)).astype(o_ref.dtype)


def idft_output(br, bi, hc, hsn, zz, x0, bias, n1, *, tc=1024):
    batch, _, cols = br.shape
    c = cols // DFT_N2
    tc = _tile(c, tc)
    nc = c // tc
    half = n1 // 2
    b_spec = pl.BlockSpec((None, n1, tc), lambda n2, b, ci: (b, 0, n2 * nc + ci))
    h_spec = pl.BlockSpec((None, half, n1), lambda n2, b, ci: (n2, 0, 0))
    z_spec = pl.BlockSpec((None, half, tc), lambda n2, b, ci: (b, 0, n2 * nc + ci))
    return pl.pallas_call(
        _idft_out_kernel,
        grid=(DFT_N2, batch, nc),
        in_specs=[b_spec, b_spec, h_spec, h_spec, z_spec, z_spec,
                  pl.BlockSpec((1, tc), lambda n2, b, ci: (0, ci))],
        out_specs=z_spec,
        out_shape=jax.ShapeDtypeStruct((batch, half, cols), BF16),
        compiler_params=_params(("parallel", "parallel", "parallel")),
        name="idft_output",
    )(br, bi, hc, hsn, zz, x0, bias)


def hyena_mixer(p, u_col, prm, e, batch, seq):
    n1 = 2 * seq // DFT_N2
    half = n1 // 2
    gc, gsn, hc, hsn, fc, fs = dft_tables(n1)
    kern, asum = hyena_filter(seq, prm["hy_w1"][e], prm["hy_b1"][e], prm["hy_freq"][e], prm["hy_w2"][e],
                              prm["hy_b2"][e], prm["hy_w3"][e], prm["hy_decay"][e])
    far, fai = dft_stage1(kern.reshape(1, n1, DFT_N2 * B_WIDTH), gc, gsn, n1)
    kr, ki = filter_spectrum(far[0], fai[0], asum, fc, fs, n1)
    zz, x0 = hyena_gate(p, u_col, prm["hy_short_w"][e], prm["hy_short_b"][e], batch, seq)
    zz3 = zz.reshape(batch, half, DFT_N2 * B_WIDTH)
    ar, ai = dft_stage1(zz3, gc[:, :, :half], gsn[:, :, :half], n1)
    br, bi = spectral_multiply(ar, ai, kr, ki, fc, fs, n1)
    yb = idft_output(br, bi, hc[:, :half], hsn[:, :half], zz3, x0.reshape(batch, half, DFT_N2 * B_WIDTH),
                     prm["hy_bias"][e].reshape(1, B_WIDTH).astype(F32), n1)
    return yb.reshape(batch * seq, B_WIDTH)


def _outproj_kernel(y1_ref, y2_ref, y3_ref, g_ref, x_ref, w_ref, o_ref, a_ref):
    @pl.when(pl.program_id(1) == 0)
    def _():
        off = 0
        for y_ref in (y1_ref, y2_ref, y3_ref):
            wd = y_ref.shape[1]
            g = g_ref[:, off:off + wd].astype(F32)
            a_ref[:, off:off + wd] = (y_ref[...].astype(F32) * (g * jax.nn.sigmoid(g))).astype(BF16)
            off += wd

    o_ref[...] = x_ref[...] + jnp.dot(a_ref[...], w_ref[...], preferred_element_type=F32)


def gated_outproj(y1, y2, y3, p, x, w, *, tm=512, tn=512):
    rows, d = x.shape
    tm = _tile(rows, tm)
    tn = _tile(d, tn)
    row = lambda i, j: (i, 0)
    return pl.pallas_call(
        _outproj_kernel,
        grid=(rows // tm, d // tn),
        in_specs=[pl.BlockSpec((tm, y1.shape[1]), row),
                  pl.BlockSpec((tm, y2.shape[1]), row),
                  pl.BlockSpec((tm, y3.shape[1]), row),
                  pl.BlockSpec((tm, MIX_WIDTH), row),
                  pl.BlockSpec((tm, tn), lambda i, j: (i, j)),
                  pl.BlockSpec((MIX_WIDTH, tn), lambda i, j: (0, j))],
        out_specs=pl.BlockSpec((tm, tn), lambda i, j: (i, j)),
        out_shape=jax.ShapeDtypeStruct((rows, d), F32),
        scratch_shapes=[pltpu.VMEM((tm, MIX_WIDTH), BF16)],
        compiler_params=_params(("parallel", "arbitrary")),
        name="gated_outproj",
    )(y1, y2, y3, p, x, w)


def _split_cols(w, sizes):
    out, off = [], 0
    for s in sizes:
        out.append(w[:, off:off + s])
        off += s
    return out


def _rope_tables(seq, dim, passthrough):
    half = dim // 2
    inv = ROPE_THETA ** (-jnp.arange(0, dim, 2, dtype=F32) / dim)
    ang = jnp.arange(seq, dtype=F32)[:, None] * inv[None, :]
    cos, sin = jnp.cos(ang), jnp.sin(ang)
    rest = jnp.full((seq, LANES - dim), 1.0 if passthrough else 0.0, F32)
    zeros = jnp.zeros((seq, LANES - half), F32)
    cf = jnp.concatenate([cos, cos, rest], axis=1)
    sa = jnp.concatenate([-sin, zeros], axis=1)
    sb = jnp.concatenate([jnp.zeros((seq, half), F32), sin, jnp.zeros((seq, LANES - dim), F32)], axis=1)
    return cf, sa, sb


def _memory_kv(mem, prm, l):
    batch = mem.shape[0]
    return normed_matmul(mem.reshape(batch * N_MEM, D_MODEL), prm["mem_norm_gain"][l],
                         prm["w_mem_kv"][l].astype(BF16), tm=256)


def _even_layer(x, mem, prm, l, e, batch, seq, rope_a):
    aq, ak, av, ag, bu, bg, mq, mg = _split_cols(prm["w_in_even"][e], EVEN_SPLITS)
    w_in = jnp.concatenate([ag, bg, mg, mq, bu, aq, ak, av], axis=1).astype(BF16)
    p = normed_matmul(x, prm["norm_gain"][l], w_in)
    scale = HEAD_DIM ** -0.5
    q = head_prep(p, EV_AQ, A_HEADS, prm["a_q_norm"][e], rope_a, scale, seq)
    k = head_prep(p, EV_AK, A_KV_HEADS, prm["a_k_norm"][e], rope_a, 1.0, seq)
    ya = window_attention(q, k, p, EV_AV, prm["a_sink"][e], batch, seq)
    yb = hyena_mixer(p, EV_BU, prm, e, batch, seq)
    kvm = _memory_kv(mem, prm, l)
    ym = memory_attention(p, EV_MQ, kvm, prm["mem_q_norm"][l], prm["mem_k_norm"][l], batch, seq)
    return gated_outproj(ya, yb, ym, p, x, prm["w_out_even"][e].astype(BF16))


def _odd_layer(x, mem, prm, l, o, batch, seq, rope_a, rope_d):
    cq, ck, cv, cg, dqa, dkva, dg, mq, mg = _split_cols(prm["w_in_odd"][o], ODD_SPLITS)
    pad = jnp.zeros((D_MODEL, OD_WIDTH - (OD_KR + D_ROPE)), F32)
    w_in = jnp.concatenate([cg, dg, mg, mq, cq, ck, cv, dqa, dkva, pad], axis=1).astype(BF16)
    p = normed_matmul(x, prm["norm_gain"][l], w_in)
    cscale = C_QK_DIM ** -0.5
    q = head_prep(p, OD_CQ, 2 * C_HEADS, prm["c_q_norm"][o], rope_a, cscale * LOG2E, seq)
    k = head_prep(p, OD_CK, 2 * C_HEADS, prm["c_k_norm"][o], rope_a, 1.0, seq)
    lam_init = 0.8 - 0.6 * math.exp(-0.3 * l)
    yc = diff_attention(q, k, p, OD_CV, prm["c_lambda"][o], prm["c_out_norm"][o], lam_init, batch, seq)
    wq = prm["w_q_b"][o].reshape(D_Q_RANK, D_HEADS, D_NOPE + D_ROPE)
    wq_rope = jnp.pad(wq[:, :, D_NOPE:], ((0, 0), (0, 0), (0, LANES - D_ROPE)))
    wq2 = jnp.concatenate([wq[:, :, :D_NOPE].reshape(D_Q_RANK, -1), wq_rope.reshape(D_Q_RANK, -1)], axis=1)
    q2 = normed_matmul(p, prm["d_q_a_norm"][o], wq2.astype(BF16), col_block=OD_DQA // D_Q_RANK)
    wkv = prm["w_kv_b"][o].reshape(D_KV_RANK, D_HEADS, D_NOPE + D_V)
    wkv2 = jnp.concatenate([wkv[:, :, :D_NOPE].reshape(D_KV_RANK, -1), wkv[:, :, D_NOPE:].reshape(D_KV_RANK, -1)],
                           axis=1)
    kv2 = normed_matmul(p, prm["d_kv_a_norm"][o], wkv2.astype(BF16), col_block=OD_CKV // D_KV_RANK)
    dscale = (D_NOPE + D_ROPE) ** -0.5
    qd = mla_prep(q2, 0, q2, D_HEADS * LANES, False, prm["d_q_norm"][o], rope_d, dscale * LOG2E, seq)
    kd = mla_prep(kv2, 0, p, OD_KR, True, prm["d_k_norm"][o], rope_d, 1.0, seq)
    yd = mla_attention(qd, kd, kv2, batch, seq)
    kvm = _memory_kv(mem, prm, l)
    ym = memory_attention(p, OD_MQ, kvm, prm["mem_q_norm"][l], prm["mem_k_norm"][l], batch, seq)
    return gated_outproj(yc, yd, ym, p, x, prm["w_out_odd"][o].astype(BF16))


def _trunk(x, mem, prm):
    batch, seq, d = x.shape
    depth = prm["norm_gain"].shape[0]
    rope_a = _rope_tables(seq, ROT_DIM, True)
    rope_d = _rope_tables(seq, D_ROPE, False)
    h = x.reshape(batch * seq, d)
    for l in range(depth):
        if l % 2 == 0:
            h = _even_layer(h, mem, prm, l, l // 2, batch, seq, rope_a)
        else:
            h = _odd_layer(h, mem, prm, l, l // 2, batch, seq, rope_a, rope_d)
    return h.reshape(batch, seq, d)


def kernel(x_prompt, x_sample, mem_prompt, mem_sample, norm_gain, mem_norm_gain, w_mem_kv, mem_q_norm, mem_k_norm, w_in_even, w_out_even, a_q_norm, a_k_norm, a_sink, hy_short_w, hy_short_b, hy_w1, hy_b1, hy_freq, hy_w2, hy_b2, hy_w3, hy_decay, hy_bias, w_in_odd, w_out_odd, c_q_norm, c_k_norm, c_lambda, c_out_norm, d_q_a_norm, w_q_b, d_kv_a_norm, w_kv_b, d_q_norm, d_k_norm):
    prm = dict(norm_gain=norm_gain, mem_norm_gain=mem_norm_gain, w_mem_kv=w_mem_kv,
               mem_q_norm=mem_q_norm, mem_k_norm=mem_k_norm, w_in_even=w_in_even,
               w_out_even=w_out_even, a_q_norm=a_q_norm, a_k_norm=a_k_norm, a_sink=a_sink,
               hy_short_w=hy_short_w, hy_short_b=hy_short_b, hy_w1=hy_w1, hy_b1=hy_b1,
               hy_freq=hy_freq, hy_w2=hy_w2, hy_b2=hy_b2, hy_w3=hy_w3, hy_decay=hy_decay,
               hy_bias=hy_bias, w_in_odd=w_in_odd, w_out_odd=w_out_odd, c_q_norm=c_q_norm,
               c_k_norm=c_k_norm, c_lambda=c_lambda, c_out_norm=c_out_norm, d_q_a_norm=d_q_a_norm,
               w_q_b=w_q_b, d_kv_a_norm=d_kv_a_norm, w_kv_b=w_kv_b, d_q_norm=d_q_norm,
               d_k_norm=d_k_norm)
    return (_trunk(x_prompt, mem_prompt, prm), _trunk(x_sample, mem_sample, prm))
```

```python
import functools
import math

import jax
import jax.numpy as jnp
from jax import lax
from jax.experimental import pallas as pl
from jax.experimental.pallas import tpu as pltpu

F32 = jnp.float32
BF16 = jnp.bfloat16

D_MODEL = 2048
N_MEM = 256
HEAD_DIM = 128
ROPE_THETA = 500000.0
ROT_DIM = HEAD_DIM // 4
EPS = 1e-6
BLOCK = 128
WINDOW = 128
NEG_INF = -1e30
A_HEADS = 8
A_KV_HEADS = 2
A_WIDTH = A_HEADS * HEAD_DIM
B_WIDTH = 1024
HY_EMB = 33
HY_FFN = 64
M_HEADS = 4
M_WIDTH = M_HEADS * HEAD_DIM
C_HEADS = 4
C_QK_DIM = 128
C_V_DIM = 256
C_WIDTH = C_HEADS * C_V_DIM
D_HEADS = 8
D_NOPE = 128
D_ROPE = 64
D_V = 128
D_Q_RANK = 512
D_KV_RANK = 256
D_WIDTH = D_HEADS * D_V
MIX_WIDTH = 2560
EVEN_SPLITS = (A_WIDTH, A_KV_HEADS * HEAD_DIM, A_KV_HEADS * HEAD_DIM, A_WIDTH,
               3 * B_WIDTH, B_WIDTH, M_WIDTH, M_WIDTH)
ODD_SPLITS = (C_HEADS * 2 * C_QK_DIM, C_HEADS * 2 * C_QK_DIM, C_WIDTH, C_WIDTH,
              D_Q_RANK, D_KV_RANK + D_ROPE, D_WIDTH, M_WIDTH, M_WIDTH)

LANES = 128
DFT_N2 = 128
VMEM_LIMIT = 48 * 1024 * 1024

EV_GATE, EV_MQ, EV_BU, EV_AQ, EV_AK, EV_AV = 0, 2560, 3072, 6144, 7168, 7424
OD_GATE, OD_MQ, OD_CQ, OD_CK, OD_CV, OD_DQA, OD_CKV, OD_KR = 0, 2560, 3072, 4096, 5120, 6144, 6656, 6912
OD_WIDTH = 7168


def _tile(n, pref):
    t = min(n, pref)
    assert n % t == 0, (n, t)
    return t


MXU_COLS = 256


def _col_tile(n, cap):
    best = None
    for t in range(MXU_COLS, min(n, cap) + 1, MXU_COLS):
        if n % t == 0:
            best = t
    assert best is not None, (n, cap)
    return best


def _params(sem):
    return pltpu.CompilerParams(dimension_semantics=sem, vmem_limit_bytes=VMEM_LIMIT)


def _rms(x, gain):
    ms = jnp.mean(x * x, axis=-1, keepdims=True)
    return x * lax.rsqrt(ms + EPS) * gain


def _nmm_kernel(x_ref, g_ref, w_ref, o_ref, h_ref):
    @pl.when(pl.program_id(1) == 0)
    def _():
        h_ref[...] = _rms(x_ref[...].astype(F32), g_ref[...]).astype(BF16)

    o_ref[...] = jnp.dot(h_ref[...], w_ref[...], preferred_element_type=F32).astype(o_ref.dtype)


def normed_matmul(x, gain, w, *, col_block=0, tm=1024, tn=1536, out_dtype=BF16):
    rows = x.shape[0]
    k, n = w.shape
    tm = _tile(rows, tm)
    tn = _col_tile(n, tn)
    return pl.pallas_call(
        _nmm_kernel,
        grid=(rows // tm, n // tn),
        in_specs=[pl.BlockSpec((tm, k), lambda i, j: (i, col_block)),
                  pl.BlockSpec((1, k), lambda i, j: (0, 0)),
                  pl.BlockSpec((k, tn), lambda i, j: (0, j))],
        out_specs=pl.BlockSpec((tm, tn), lambda i, j: (i, j)),
        out_shape=jax.ShapeDtypeStruct((rows, n), out_dtype),
        scratch_shapes=[pltpu.VMEM((tm, k), BF16)],
        compiler_params=_params(("parallel", "arbitrary")),
        name="normed_matmul",
    )(x, gain.reshape(1, k).astype(F32), w)


def _rope(y, cf, sa, sb, half):
    return y * cf + pltpu.roll(y, LANES - half, 1) * sa + pltpu.roll(y, half, 1) * sb


def _head_prep_kernel(x_ref, g_ref, cf_ref, sa_ref, sb_ref, o_ref, *, nheads, scale):
    g = g_ref[...]
    cf, sa, sb = cf_ref[...], sa_ref[...], sb_ref[...]
    for h in range(nheads):
        cols = slice(h * LANES, (h + 1) * LANES)
        y = _rms(x_ref[:, cols].astype(F32), g)
        y = _rope(y, cf, sa, sb, ROT_DIM // 2)
        o_ref[:, cols] = (y * scale).astype(o_ref.dtype)


def head_prep(p, col, nheads, gain, tables, scale, seq, *, tm=256):
    rows = p.shape[0]
    width = nheads * LANES
    assert col % width == 0
    tm = _tile(seq, tm)
    nblk = seq // tm
    tab_spec = pl.BlockSpec((tm, LANES), lambda i: (i % nblk, 0))
    return pl.pallas_call(
        functools.partial(_head_prep_kernel, nheads=nheads, scale=scale),
        grid=(rows // tm,),
        in_specs=[pl.BlockSpec((tm, width), lambda i: (i, col // width)),
                  pl.BlockSpec((1, LANES), lambda i: (0, 0)),
                  tab_spec, tab_spec, tab_spec],
        out_specs=pl.BlockSpec((tm, width), lambda i: (i, 0)),
        out_shape=jax.ShapeDtypeStruct((rows, width), BF16),
        compiler_params=_params(("parallel",)),
        name="head_prep",
    )(p, gain.reshape(1, LANES).astype(F32), *tables)


WIN_QB = 4


def _window_kernel(sink_ref, q_ref, kp_ref, kc_ref, kn_ref, vp_ref, vc_ref, vn_ref, o_ref, *, seq, qb):
    n = pl.program_id(1)
    group = A_HEADS // A_KV_HEADS
    rows = group * BLOCK
    qi = lax.broadcasted_iota(jnp.int32, (rows, 3 * BLOCK), 0) & (BLOCK - 1)
    kj = lax.broadcasted_iota(jnp.int32, (rows, 3 * BLOCK), 1)
    in_win = jnp.abs(kj - BLOCK - qi) <= WINDOW
    for kv in range(A_KV_HEADS):
        cols = slice(kv * HEAD_DIM, (kv + 1) * HEAD_DIM)
        heads = range(kv * group, (kv + 1) * group)
        kfull = jnp.concatenate([kp_ref[:, cols], kc_ref[:, cols], kn_ref[:, cols]], axis=0).astype(BF16)
        vfull = jnp.concatenate([vp_ref[:, cols], vc_ref[:, cols], vn_ref[:, cols]], axis=0).astype(BF16)
        sk = jnp.concatenate([jnp.full((BLOCK, 1), sink_ref[h], F32) for h in heads], axis=0)
        for i in range(qb):
            qrows = slice(i * BLOCK, (i + 1) * BLOCK)
            q = jnp.concatenate([q_ref[qrows, h * HEAD_DIM:(h + 1) * HEAD_DIM] for h in heads], axis=0)
            s = lax.dot_general(q, kfull[i * BLOCK:(i + 3) * BLOCK], (((1,), (1,)), ((), ())),
                                preferred_element_type=F32)
            kpos = (n * qb + i - 1) * BLOCK + kj
            s = jnp.where(in_win & (kpos >= 0) & (kpos < seq), s, NEG_INF)
            m = jnp.maximum(jnp.max(s, axis=-1, keepdims=True), sk)
            pr = jnp.exp(s - m)
            denom = jnp.sum(pr, axis=-1, keepdims=True) + jnp.exp(sk - m)
            pr = pr / denom
            o = jnp.dot(pr.astype(BF16), vfull[i * BLOCK:(i + 3) * BLOCK], preferred_element_type=F32)
            for g, h in enumerate(heads):
                o_ref[qrows, h * HEAD_DIM:(h + 1) * HEAD_DIM] = o[g * BLOCK:(g + 1) * BLOCK].astype(o_ref.dtype)


def window_attention(q, k, p, v_col, sink, batch, seq):
    nblk = seq // BLOCK
    qb = _tile(nblk, WIN_QB)
    nstep = nblk // qb
    kvw = A_KV_HEADS * HEAD_DIM
    vb = v_col // kvw

    def prev(b, n):
        return b * nblk + jnp.maximum(n * qb - 1, 0)

    def cur(b, n):
        return b * nstep + n

    def nxt(b, n):
        return b * nblk + jnp.minimum((n + 1) * qb, nblk - 1)

    return pl.pallas_call(
        functools.partial(_window_kernel, seq=seq, qb=qb),
        grid=(batch, nstep),
        in_specs=[pl.BlockSpec(memory_space=pltpu.SMEM),
                  pl.BlockSpec((qb * BLOCK, A_WIDTH), lambda b, n: (cur(b, n), 0)),
                  pl.BlockSpec((BLOCK, kvw), lambda b, n: (prev(b, n), 0)),
                  pl.BlockSpec((qb * BLOCK, kvw), lambda b, n: (cur(b, n), 0)),
                  pl.BlockSpec((BLOCK, kvw), lambda b, n: (nxt(b, n), 0)),
                  pl.BlockSpec((BLOCK, kvw), lambda b, n: (prev(b, n), vb)),
                  pl.BlockSpec((qb * BLOCK, kvw), lambda b, n: (cur(b, n), vb)),
                  pl.BlockSpec((BLOCK, kvw), lambda b, n: (nxt(b, n), vb))],
        out_specs=pl.BlockSpec((qb * BLOCK, A_WIDTH), lambda b, n: (cur(b, n), 0)),
        out_shape=jax.ShapeDtypeStruct((batch * seq, A_WIDTH), BF16),
        compiler_params=_params(("parallel", "parallel")),
        name="window_attention",
    )(sink.astype(F32), q, k, k, k, p, p, p)


def _mem_kernel(q_ref, kv_ref, qg_ref, kg_ref, o_ref):
    qg, kg = qg_ref[...], kg_ref[...]
    scale = HEAD_DIM ** -0.5
    for h in range(M_HEADS):
        cols = slice(h * HEAD_DIM, (h + 1) * HEAD_DIM)
        q = (_rms(q_ref[:, cols].astype(F32), qg) * scale).astype(BF16)
        k = _rms(kv_ref[:, cols].astype(F32), kg).astype(BF16)
        v = kv_ref[:, M_WIDTH + h * HEAD_DIM:M_WIDTH + (h + 1) * HEAD_DIM].astype(BF16)
        s = lax.dot_general(q, k, (((1,), (1,)), ((), ())), preferred_element_type=F32)
        m = jnp.max(s, axis=-1, keepdims=True)
        pr = jnp.exp(s - m)
        pr = pr / jnp.sum(pr, axis=-1, keepdims=True)
        o = jnp.dot(pr.astype(BF16), v, preferred_element_type=F32)
        o_ref[:, cols] = o.astype(o_ref.dtype)


def memory_attention(p, q_col, kvm, q_gain, k_gain, batch, seq, *, tq=512):
    tq = _tile(seq, tq)
    nq = seq // tq
    return pl.pallas_call(
        _mem_kernel,
        grid=(batch, nq),
        in_specs=[pl.BlockSpec((tq, M_WIDTH), lambda b, i: (b * nq + i, q_col // M_WIDTH)),
                  pl.BlockSpec((N_MEM, 2 * M_WIDTH), lambda b, i: (b, 0)),
                  pl.BlockSpec((1, HEAD_DIM), lambda b, i: (0, 0)),
                  pl.BlockSpec((1, HEAD_DIM), lambda b, i: (0, 0))],
        out_specs=pl.BlockSpec((tq, M_WIDTH), lambda b, i: (b * nq + i, 0)),
        out_shape=jax.ShapeDtypeStruct((batch * seq, M_WIDTH), BF16),
        compiler_params=_params(("parallel", "parallel")),
        name="memory_attention",
    )(p, kvm, q_gain.reshape(1, HEAD_DIM).astype(F32), k_gain.reshape(1, HEAD_DIM).astype(F32))


LOG2E = math.log2(math.e)


def _chunk_rows(chunk, tk):
    return pl.ds(pl.multiple_of(chunk * tk, tk), tk)


def _pipelined_chunks(scores, consume, nk):
    scores(0, 0)

    def body(i, carry):
        scores(2 * i + 1, 1)
        consume(2 * i, 0)
        scores(jnp.minimum(2 * i + 2, nk - 1), 0)
        consume(2 * i + 1, 1)
        return carry

    lax.fori_loop(0, nk // 2, body, 0)


def _diff_flash_kernel(q_ref, k_ref, v_ref, lam_ref, g_ref, o_ref, m_ref, l_ref, acc_ref, s_ref, *, tk, nk, lam_init):
    m_ref[...] = jnp.full(m_ref.shape, NEG_INF, F32)
    l_ref[...] = jnp.zeros(l_ref.shape, F32)
    acc_ref[...] = jnp.zeros(acc_ref.shape, F32)

    def scores(chunk, c):
        cols = slice(c * C_QK_DIM, (c + 1) * C_QK_DIM)
        s_ref[c] = lax.dot_general(q_ref[:, cols], k_ref[_chunk_rows(chunk, tk), cols], (((1,), (1,)), ((), ())),
                                   preferred_element_type=F32)

    def consume(chunk, c):
        s = s_ref[c]
        m_prev = m_ref[c]
        m_new = jnp.maximum(m_prev, jnp.max(s, axis=-1, keepdims=True))
        alpha = jnp.exp2(m_prev - m_new)
        pr = jnp.exp2(s - pltpu.repeat(m_new, tk // LANES, axis=1))
        l_ref[c] = alpha * l_ref[c] + jnp.sum(pr, axis=-1, keepdims=True)
        acc_ref[c] = (pltpu.repeat(alpha, C_V_DIM // LANES, axis=1) * acc_ref[c]
                      + jnp.dot(pr.astype(BF16), v_ref[_chunk_rows(chunk, tk), :], preferred_element_type=F32))
        m_ref[c] = m_new

    scores(0, 0)

    def body(i, carry):
        for j in (2 * i, 2 * i + 1):
            scores(j, 1)
            consume(j, 0)
            scores(jnp.minimum(j + 1, nk - 1), 0)
            consume(j, 1)
        return carry

    lax.fori_loop(0, nk // 2, body, 0)
    lv = lam_ref[...]
    lam = (jnp.exp(jnp.sum(lv[0:1] * lv[1:2], axis=-1, keepdims=True))
           - jnp.exp(jnp.sum(lv[2:3] * lv[3:4], axis=-1, keepdims=True)) + lam_init)
    o1 = acc_ref[0] / pltpu.repeat(l_ref[0], C_V_DIM // LANES, axis=1)
    o2 = acc_ref[1] / pltpu.repeat(l_ref[1], C_V_DIM // LANES, axis=1)
    o_ref[...] = (_rms(o1 - lam * o2, g_ref[...]) * (1.0 - lam_init)).astype(o_ref.dtype)


def _mla_flash_kernel(q_ref, k_ref, v_ref, o_ref, m_ref, acc_ref, s_ref, *, tk, nk):
    m_ref[...] = jnp.full(m_ref.shape, NEG_INF, F32)
    acc_ref[...] = jnp.zeros(acc_ref.shape, F32)
    ones = jnp.ones((tk, LANES), BF16)

    def scores(chunk, slot):
        s_ref[slot] = lax.dot_general(q_ref[...], k_ref[_chunk_rows(chunk, tk), :], (((1,), (1,)), ((), ())),
                                      preferred_element_type=F32)

    def consume(chunk, slot):
        s = s_ref[slot]
        m_prev = m_ref[...]
        m_new = jnp.maximum(m_prev, jnp.max(s, axis=-1, keepdims=True))
        alpha = jnp.exp2(m_prev - m_new)
        pr = jnp.exp2(s - pltpu.repeat(m_new, tk // LANES, axis=1))
        vv = jnp.concatenate([v_ref[_chunk_rows(chunk, tk), :], ones], axis=1)
        acc_ref[...] = (pltpu.repeat(alpha, 2, axis=1) * acc_ref[...]
                        + jnp.dot(pr.astype(BF16), vv, preferred_element_type=F32))
        m_ref[...] = m_new

    _pipelined_chunks(scores, consume, nk)
    o_ref[...] = (acc_ref[:, :D_V] / acc_ref[:, D_V:]).astype(o_ref.dtype)


def _flash_call(kernel, q, k, v, v_block0, extra, extra_specs, scratch, *, heads, dv, batch, seq, tq, tk, name):
    nq, nk = seq // tq, seq // tk
    qk_w = 2 * LANES
    return pl.pallas_call(
        functools.partial(kernel, tk=tk, nk=nk),
        grid=(batch, heads, nq),
        in_specs=[pl.BlockSpec((tq, qk_w), lambda b, h, i: (b * nq + i, h)),
                  pl.BlockSpec((seq, qk_w), lambda b, h, i: (b, h), pipeline_mode=pl.Buffered(1)),
                  pl.BlockSpec((seq, dv), lambda b, h, i: (b, v_block0 + h), pipeline_mode=pl.Buffered(1))]
        + extra_specs,
        out_specs=pl.BlockSpec((tq, dv), lambda b, h, i: (b * nq + i, h)),
        out_shape=jax.ShapeDtypeStruct((batch * seq, heads * dv), BF16),
        scratch_shapes=scratch,
        compiler_params=_params(("parallel", "parallel", "arbitrary")),
        name=name,
    )(q, k, v, *extra)


def diff_attention(q, k, p, v_col, c_lambda, out_gain, lam_init, batch, seq, *, tq=1024, tk=512):
    tq, tk = _tile(seq, tq), _tile(seq // 2, tk)
    const = lambda b, h, i: (0, 0)
    return _flash_call(
        functools.partial(_diff_flash_kernel, lam_init=lam_init), q, k, p, v_col // C_V_DIM,
        [c_lambda.astype(F32), out_gain.reshape(1, C_V_DIM).astype(F32)],
        [pl.BlockSpec((4, C_QK_DIM), const), pl.BlockSpec((1, C_V_DIM), const)],
        [pltpu.VMEM((2, tq, LANES), F32), pltpu.VMEM((2, tq, LANES), F32), pltpu.VMEM((2, tq, C_V_DIM), F32),
         pltpu.VMEM((2, tq, tk), F32)],
        heads=C_HEADS, dv=C_V_DIM, batch=batch, seq=seq, tq=tq, tk=tk, name="diff_attention")


def mla_attention(q, k, kv, batch, seq, *, tq=1024, tk=1024):
    tq, tk = _tile(seq, tq), _tile(seq // 2, tk)
    return _flash_call(_mla_flash_kernel, q, k, kv, D_HEADS, [], [],
                       [pltpu.VMEM((tq, LANES), F32), pltpu.VMEM((tq, D_V + LANES), F32),
                        pltpu.VMEM((2, tq, tk), F32)],
                       heads=D_HEADS, dv=D_V, batch=batch, seq=seq, tq=tq, tk=tk, name="mla_attention")


def _mla_prep_kernel(n_ref, r_ref, gn_ref, gr_ref, cf_ref, sa_ref, sb_ref, o_ref, *, shared_rope, scale):
    gn, gr = gn_ref[...], gr_ref[...]
    cf, sa, sb = cf_ref[...], sa_ref[...], sb_ref[...]
    for h in range(D_HEADS):
        cols = slice(h * LANES, (h + 1) * LANES)
        xn = n_ref[:, cols].astype(F32)
        xr = (r_ref[...] if shared_rope else r_ref[:, cols]).astype(F32)
        ms = (jnp.sum(xn * xn, axis=-1, keepdims=True)
              + jnp.sum(xr * xr, axis=-1, keepdims=True)) * (1.0 / (D_NOPE + D_ROPE))
        inv = lax.rsqrt(ms + EPS)
        yn = xn * inv * gn
        yr = _rope(xr * inv * gr, cf, sa, sb, D_ROPE // 2)
        o_ref[:, 2 * h * LANES:(2 * h + 1) * LANES] = (yn * scale).astype(o_ref.dtype)
        o_ref[:, (2 * h + 1) * LANES:(2 * h + 2) * LANES] = (yr * scale).astype(o_ref.dtype)


def mla_prep(nope_arr, nope_col, rope_arr, rope_col, shared_rope, gain, tables, scale, seq, *, tm=256):
    rows = nope_arr.shape[0]
    width = D_HEADS * LANES
    tm = _tile(seq, tm)
    nblk = seq // tm
    rw = LANES if shared_rope else width
    assert nope_col % width == 0 and rope_col % rw == 0
    gn = gain[:D_NOPE].reshape(1, LANES).astype(F32)
    gr = jnp.pad(gain[D_NOPE:], (0, LANES - D_ROPE)).reshape(1, LANES).astype(F32)
    tab_spec = pl.BlockSpec((tm, LANES), lambda i: (i % nblk, 0))
    one = pl.BlockSpec((1, LANES), lambda i: (0, 0))
    return pl.pallas_call(
        functools.partial(_mla_prep_kernel, shared_rope=shared_rope, scale=scale),
        grid=(rows // tm,),
        in_specs=[pl.BlockSpec((tm, width), lambda i: (i, nope_col // width)),
                  pl.BlockSpec((tm, rw), lambda i: (i, rope_col // rw)),
                  one, one, tab_spec, tab_spec, tab_spec],
        out_specs=pl.BlockSpec((tm, 2 * width), lambda i: (i, 0)),
        out_shape=jax.ShapeDtypeStruct((rows, 2 * width), BF16),
        compiler_params=_params(("parallel",)),
        name="mla_prep",
    )(nope_arr, rope_arr, gn, gr, *tables)


HALO = 16


def _hyena_gate_kernel(u_ref, up_ref, un_ref, w_ref, b_ref, zz_ref, x0_ref, *, nblk, tm):
    li = pl.program_id(1)
    row = lax.broadcasted_iota(jnp.int32, (tm, B_WIDTH), 0)
    has_prev = (li > 0).astype(F32)
    has_next = (li < nblk - 1).astype(F32)

    def conv(part):
        cols = slice(part * B_WIDTH, (part + 1) * B_WIDTH)
        u = u_ref[:, cols].astype(F32)
        prev_row = up_ref[HALO - 1:HALO, cols].astype(F32) * has_prev
        next_row = un_ref[0:1, cols].astype(F32) * has_next
        above = jnp.where(row == 0, prev_row, pltpu.roll(u, 1, 0))
        below = jnp.where(row == tm - 1, next_row, pltpu.roll(u, tm - 1, 0))
        w = w_ref[:, cols]
        return above * w[0:1] + u * w[1:2] + below * w[2:3] + b_ref[:, cols]

    x0_ref[...] = conv(0).astype(x0_ref.dtype)
    zz_ref[...] = (conv(2) * conv(1)).astype(zz_ref.dtype)


def hyena_gate(p, u_col, short_w, short_b, batch, seq, *, tm=256):
    tm = _tile(seq, tm)
    nblk = seq // tm
    uw = 3 * B_WIDTH
    ub = u_col // uw
    hb = tm // HALO
    out = jax.ShapeDtypeStruct((batch * seq, B_WIDTH), BF16)
    return pl.pallas_call(
        functools.partial(_hyena_gate_kernel, nblk=nblk, tm=tm),
        grid=(batch, nblk),
        in_specs=[pl.BlockSpec((tm, uw), lambda b, i: (b * nblk + i, ub)),
                  pl.BlockSpec((HALO, uw), lambda b, i: (jnp.maximum((b * nblk + i) * hb - 1, 0), ub)),
                  pl.BlockSpec((HALO, uw),
                               lambda b, i: (jnp.minimum((b * nblk + i + 1) * hb, batch * nblk * hb - 1), ub)),
                  pl.BlockSpec((3, uw), lambda b, i: (0, 0)),
                  pl.BlockSpec((1, uw), lambda b, i: (0, 0))],
        out_specs=[pl.BlockSpec((tm, B_WIDTH), lambda b, i: (b * nblk + i, 0))] * 2,
        out_shape=[out, out],
        compiler_params=_params(("parallel", "parallel")),
        name="hyena_gate",
    )(p, p, p, short_w.astype(F32), short_b.reshape(1, uw).astype(F32))


def _hdot(a, b):
    return jnp.dot(a, b, preferred_element_type=F32, precision=lax.Precision.HIGHEST)


def _filter_kernel(fr_ref, w1_ref, b1_ref, fq_ref, w2_ref, b2_ref, w3_ref, dec_ref, k_ref, s_ref, *, seq, tr):
    i = pl.program_id(0)
    r = i * tr + lax.broadcasted_iota(jnp.int32, (tr, 1), 0)
    pos = jnp.where(r < seq, r, 2 * seq - 1 - r).astype(F32)
    t = pos * (1.0 / (seq - 1))
    w = pos * (2.0 * math.pi / seq)
    lane = lax.broadcasted_iota(jnp.int32, (tr, LANES), 1)
    bands = (HY_EMB - 1) // 2
    ang = w * fr_ref[...]
    z = jnp.where(lane == 0, t,
                  jnp.where(lane <= bands, jnp.cos(ang),
                            jnp.where(lane <= 2 * bands, -jnp.sin(ang), 0.0)))
    h = jnp.sin(fq_ref[0:1] * (_hdot(z, w1_ref[...]) + b1_ref[...]))
    h = jnp.sin(fq_ref[1:2] * (_hdot(h, w2_ref[...]) + b2_ref[...]))
    h = _hdot(h, w3_ref[...]) * jnp.exp(-t * jnp.abs(dec_ref[...]))
    h = jnp.where(r == seq, 0.0, h)
    k_ref[...] = h.astype(k_ref.dtype)

    @pl.when(i == 0)
    def _():
        s_ref[...] = jnp.zeros(s_ref.shape, F32)

    s_ref[...] += jnp.sum(jnp.abs(h), axis=0, keepdims=True)


def hyena_filter(seq, w1, b1, freq, w2, b2, w3, decay, *, tr=256):
    tr = _tile(seq, tr)
    nhalf = seq // tr
    bands = (HY_EMB - 1) // 2
    fr = jnp.linspace(1e-4, bands - 1, bands, dtype=F32)
    fr_lanes = jnp.concatenate([jnp.zeros((1,), F32), fr, fr, jnp.zeros((LANES - HY_EMB,), F32)]).reshape(1, LANES)
    w1p = jnp.pad(w1.astype(F32), ((0, LANES - HY_EMB), (0, 0)))
    const = lambda i: (0, 0)
    half = lambda i: (0, i // nhalf)
    return pl.pallas_call(
        functools.partial(_filter_kernel, seq=seq, tr=tr),
        grid=(2 * nhalf,),
        in_specs=[pl.BlockSpec((1, LANES), const),
                  pl.BlockSpec((LANES, HY_FFN), const),
                  pl.BlockSpec((1, HY_FFN), const),
                  pl.BlockSpec((2, HY_FFN), const),
                  pl.BlockSpec((HY_FFN, HY_FFN), const),
                  pl.BlockSpec((1, HY_FFN), const),
                  pl.BlockSpec((HY_FFN, B_WIDTH), half),
                  pl.BlockSpec((1, B_WIDTH), half)],
        out_specs=[pl.BlockSpec((tr, B_WIDTH), lambda i: (i, 0)),
                   pl.BlockSpec((1, B_WIDTH), const)],
        out_shape=[jax.ShapeDtypeStruct((2 * seq, B_WIDTH), BF16),
                   jax.ShapeDtypeStruct((1, B_WIDTH), F32)],
        compiler_params=_params(("arbitrary",)),
        name="hyena_filter",
    )(fr_lanes, w1p, b1.reshape(1, HY_FFN).astype(F32), freq.astype(F32), w2.astype(F32),
      b2.reshape(1, HY_FFN).astype(F32), w3.astype(F32), decay.reshape(1, 2 * B_WIDTH).astype(F32))


def _bdot(a, b):
    return jnp.dot(a, b, preferred_element_type=F32)


def dft_tables(n1):
    n = n1 * DFT_N2
    k1 = jnp.arange(n1, dtype=jnp.int32)[None, :, None]
    m1 = jnp.arange(n1, dtype=jnp.int32)[None, None, :]
    n2 = jnp.arange(DFT_N2, dtype=jnp.int32)[:, None, None]
    ang = ((k1 * (DFT_N2 * m1 + n2)) % n).astype(F32) * (2.0 * math.pi / n)
    gc = jnp.cos(ang).astype(BF16)
    gsn = (-jnp.sin(ang)).astype(BF16)
    hc = jnp.swapaxes(gc, 1, 2)
    hsn = jnp.swapaxes(gsn, 1, 2)
    a = jnp.arange(DFT_N2, dtype=jnp.int32)
    ang2 = ((a[:, None] * a[None, :]) % DFT_N2).astype(F32) * (2.0 * math.pi / DFT_N2)
    return gc, gsn, hc, hsn, jnp.cos(ang2).astype(BF16), jnp.sin(ang2).astype(BF16)


def _dft1_kernel(x_ref, gc_ref, gs_ref, ar_ref, ai_ref):
    x = x_ref[...].astype(BF16)
    ar_ref[...] = _bdot(gc_ref[...], x).astype(ar_ref.dtype)
    ai_ref[...] = _bdot(gs_ref[...], x).astype(ai_ref.dtype)


def dft_stage1(x, gc, gsn, n1, *, tc=1024):
    batch, k1rows, cols = x.shape
    c = cols // DFT_N2
    tc = _tile(c, tc)
    nc = c // tc
    out = jax.ShapeDtypeStruct((batch, n1, cols), BF16)
    g_spec = pl.BlockSpec((None, n1, k1rows), lambda n2, b, ci: (n2, 0, 0))
    o_spec = pl.BlockSpec((None, n1, tc), lambda n2, b, ci: (b, 0, n2 * nc + ci))
    return pl.pallas_call(
        _dft1_kernel,
        grid=(DFT_N2, batch, nc),
        in_specs=[pl.BlockSpec((None, k1rows, tc), lambda n2, b, ci: (b, 0, n2 * nc + ci)), g_spec, g_spec],
        out_specs=[o_spec, o_spec],
        out_shape=[out, out],
        compiler_params=_params(("parallel", "parallel", "parallel")),
        name="dft_stage1",
    )(x, gc, gsn)


def _filter_spectrum_kernel(ar_ref, ai_ref, s_ref, fc_ref, fs_ref, kr_ref, ki_ref, *, kb, n):
    fc, fs = fc_ref[...], fs_ref[...]
    scale = 1.0 / (s_ref[...] * n)
    for j in range(kb):
        ar, ai = ar_ref[j], ai_ref[j]
        kr_ref[j] = (_bdot(fc, ar) + _bdot(fs, ai)) * scale
        ki_ref[j] = (_bdot(fc, ai) - _bdot(fs, ar)) * scale


def filter_spectrum(ar, ai, asum, fc, fs, n1, *, kb=2, tc=512):
    c = asum.shape[1]
    kb = _tile(n1, kb)
    tc = _tile(c, tc)
    a_spec = pl.BlockSpec((kb, DFT_N2, tc), lambda k, ci: (k, 0, ci))
    f_spec = pl.BlockSpec((DFT_N2, DFT_N2), lambda k, ci: (0, 0))
    out = jax.ShapeDtypeStruct((n1, DFT_N2, c), F32)
    return pl.pallas_call(
        functools.partial(_filter_spectrum_kernel, kb=kb, n=n1 * DFT_N2),
        grid=(n1 // kb, c // tc),
        in_specs=[a_spec, a_spec, pl.BlockSpec((1, tc), lambda k, ci: (0, ci)), f_spec, f_spec],
        out_specs=[a_spec, a_spec],
        out_shape=[out, out],
        compiler_params=_params(("parallel", "parallel")),
        name="filter_spectrum",
    )(ar.reshape(n1, DFT_N2, c), ai.reshape(n1, DFT_N2, c), asum, fc, fs)


def _spectral_kernel(ar_ref, ai_ref, kr_ref, ki_ref, fc_ref, fs_ref, br_ref, bi_ref, *, kb):
    fc, fs = fc_ref[...], fs_ref[...]
    for j in range(kb):
        ar, ai = ar_ref[j], ai_ref[j]
        xr = _bdot(fc, ar) + _bdot(fs, ai)
        xi = _bdot(fc, ai) - _bdot(fs, ar)
        kr, ki = kr_ref[j], ki_ref[j]
        yr = (xr * kr - xi * ki).astype(BF16)
        yi = (xr * ki + xi * kr).astype(BF16)
        br_ref[j] = (_bdot(fc, yr) - _bdot(fs, yi)).astype(br_ref.dtype)
        bi_ref[j] = (_bdot(fc, yi) + _bdot(fs, yr)).astype(bi_ref.dtype)


def spectral_multiply(ar, ai, kr, ki, fc, fs, n1, *, kb=2, tc=512):
    batch = ar.shape[0]
    c = kr.shape[2]
    kb = _tile(n1, kb)
    tc = _tile(c, tc)
    a_spec = pl.BlockSpec((None, kb, DFT_N2, tc), lambda k, ci, b: (b, k, 0, ci))
    k_spec = pl.BlockSpec((kb, DFT_N2, tc), lambda k, ci, b: (k, 0, ci))
    f_spec = pl.BlockSpec((DFT_N2, DFT_N2), lambda k, ci, b: (0, 0))
    out = jax.ShapeDtypeStruct((batch, n1, DFT_N2, c), BF16)
    br, bi = pl.pallas_call(
        functools.partial(_spectral_kernel, kb=kb),
        grid=(n1 // kb, c // tc, batch),
        in_specs=[a_spec, a_spec, k_spec, k_spec, f_spec, f_spec],
        out_specs=[a_spec, a_spec],
        out_shape=[out, out],
        compiler_params=_params(("parallel", "parallel", "parallel")),
        name="spectral_multiply",
    )(ar.reshape(batch, n1, DFT_N2, c), ai.reshape(batch, n1, DFT_N2, c), kr, ki, fc, fs)
    return br.reshape(batch, n1, DFT_N2 * c), bi.reshape(batch, n1, DFT_N2 * c)


def _idft_out_kernel(br_ref, bi_ref, hc_ref, hs_ref, zz_ref, x0_ref, bias_ref, o_ref):
    y = _bdot(hc_ref[...], br_ref[...]) + _bdot(hs_ref[...], bi_ref[...])
    zz = zz_ref[...].astype(F32)
    o_ref[...] = ((y + zz * bias_ref[...]) * x0_ref[...].astype(F32)).astype(o_ref.dtype)


def idft_output(br, bi, hc, hsn, zz, x0, bias, n1, *, tc=1024):
    batch, _, cols = br.shape
    c = cols // DFT_N2
    tc = _tile(c, tc)
    nc = c // tc
    half = n1 // 2
    b_spec = pl.BlockSpec((None, n1, tc), lambda n2, b, ci: (b, 0, n2 * nc + ci))
    h_spec = pl.BlockSpec((None, half, n1), lambda n2, b, ci: (n2, 0, 0))
    z_spec = pl.BlockSpec((None, half, tc), lambda n2, b, ci: (b, 0, n2 * nc + ci))
    return pl.pallas_call(
        _idft_out_kernel,
        grid=(DFT_N2, batch, nc),
        in_specs=[b_spec, b_spec, h_spec, h_spec, z_spec, z_spec,
                  pl.BlockSpec((1, tc), lambda n2, b, ci: (0, ci))],
        out_specs=z_spec,
        out_shape=jax.ShapeDtypeStruct((batch, half, cols), BF16),
        compiler_params=_params(("parallel", "parallel", "parallel")),
        name="idft_output",
    )(br, bi, hc, hsn, zz, x0, bias)


def hyena_mixer(p, u_col, prm, e, batch, seq):
    n1 = 2 * seq // DFT_N2
    half = n1 // 2
    gc, gsn, hc, hsn, fc, fs = dft_tables(n1)
    kern, asum = hyena_filter(seq, prm["hy_w1"][e], prm["hy_b1"][e], prm["hy_freq"][e], prm["hy_w2"][e],
                              prm["hy_b2"][e], prm["hy_w3"][e], prm["hy_decay"][e])
    far, fai = dft_stage1(kern.reshape(1, n1, DFT_N2 * B_WIDTH), gc, gsn, n1)
    kr, ki = filter_spectrum(far[0], fai[0], asum, fc, fs, n1)
    zz, x0 = hyena_gate(p, u_col, prm["hy_short_w"][e], prm["hy_short_b"][e], batch, seq)
    zz3 = zz.reshape(batch, half, DFT_N2 * B_WIDTH)
    ar, ai = dft_stage1(zz3, gc[:, :, :half], gsn[:, :, :half], n1)
    br, bi = spectral_multiply(ar, ai, kr, ki, fc, fs, n1)
    yb = idft_output(br, bi, hc[:, :half], hsn[:, :half], zz3, x0.reshape(batch, half, DFT_N2 * B_WIDTH),
                     prm["hy_bias"][e].reshape(1, B_WIDTH).astype(F32), n1)
    return yb.reshape(batch * seq, B_WIDTH)


def _outproj_kernel(y1_ref, y2_ref, y3_ref, g_ref, x_ref, w_ref, o_ref, a_ref):
    @pl.when(pl.program_id(1) == 0)
    def _():
        off = 0
        for y_ref in (y1_ref, y2_ref, y3_ref):
            wd = y_ref.shape[1]
            g = g_ref[:, off:off + wd].astype(F32)
            a_ref[:, off:off + wd] = (y_ref[...].astype(F32) * (g * jax.nn.sigmoid(g))).astype(BF16)
            off += wd

    o_ref[...] = x_ref[...] + jnp.dot(a_ref[...], w_ref[...], preferred_element_type=F32)


def gated_outproj(y1, y2, y3, p, x, w, *, tm=512, tn=1024):
    rows, d = x.shape
    tm = _tile(rows, tm)
    tn = _col_tile(d, tn)
    row = lambda i, j: (i, 0)
    return pl.pallas_call(
        _outproj_kernel,
        grid=(rows // tm, d // tn),
        in_specs=[pl.BlockSpec((tm, y1.shape[1]), row),
                  pl.BlockSpec((tm, y2.shape[1]), row),
                  pl.BlockSpec((tm, y3.shape[1]), row),
                  pl.BlockSpec((tm, MIX_WIDTH), row),
                  pl.BlockSpec((tm, tn), lambda i, j: (i, j)),
                  pl.BlockSpec((MIX_WIDTH, tn), lambda i, j: (0, j))],
        out_specs=pl.BlockSpec((tm, tn), lambda i, j: (i, j)),
        out_shape=jax.ShapeDtypeStruct((rows, d), F32),
        scratch_shapes=[pltpu.VMEM((tm, MIX_WIDTH), BF16)],
        compiler_params=_params(("parallel", "arbitrary")),
        name="gated_outproj",
    )(y1, y2, y3, p, x, w)


def _split_cols(w, sizes):
    out, off = [], 0
    for s in sizes:
        out.append(w[:, off:off + s])
        off += s
    return out


def _rope_tables(seq, dim, passthrough):
    half = dim // 2
    inv = ROPE_THETA ** (-jnp.arange(0, dim, 2, dtype=F32) / dim)
    ang = jnp.arange(seq, dtype=F32)[:, None] * inv[None, :]
    cos, sin = jnp.cos(ang), jnp.sin(ang)
    rest = jnp.full((seq, LANES - dim), 1.0 if passthrough else 0.0, F32)
    zeros = jnp.zeros((seq, LANES - half), F32)
    cf = jnp.concatenate([cos, cos, rest], axis=1)
    sa = jnp.concatenate([-sin, zeros], axis=1)
    sb = jnp.concatenate([jnp.zeros((seq, half), F32), sin, jnp.zeros((seq, LANES - dim), F32)], axis=1)
    return cf, sa, sb


def _memory_kv(mem, prm, l):
    batch = mem.shape[0]
    return normed_matmul(mem.reshape(batch * N_MEM, D_MODEL), prm["mem_norm_gain"][l],
                         prm["w_mem_kv"][l].astype(BF16))


def _even_layer(x, mem, prm, l, e, batch, seq, rope_a):
    aq, ak, av, ag, bu, bg, mq, mg = _split_cols(prm["w_in_even"][e], EVEN_SPLITS)
    w_in = jnp.concatenate([ag, bg, mg, mq, bu, aq, ak, av], axis=1).astype(BF16)
    p = normed_matmul(x, prm["norm_gain"][l], w_in)
    scale = HEAD_DIM ** -0.5
    q = head_prep(p, EV_AQ, A_HEADS, prm["a_q_norm"][e], rope_a, scale, seq)
    k = head_prep(p, EV_AK, A_KV_HEADS, prm["a_k_norm"][e], rope_a, 1.0, seq)
    ya = window_attention(q, k, p, EV_AV, prm["a_sink"][e], batch, seq)
    yb = hyena_mixer(p, EV_BU, prm, e, batch, seq)
    kvm = _memory_kv(mem, prm, l)
    ym = memory_attention(p, EV_MQ, kvm, prm["mem_q_norm"][l], prm["mem_k_norm"][l], batch, seq)
    return gated_outproj(ya, yb, ym, p, x, prm["w_out_even"][e].astype(BF16))


def _odd_layer(x, mem, prm, l, o, batch, seq, rope_a, rope_d):
    cq, ck, cv, cg, dqa, dkva, dg, mq, mg = _split_cols(prm["w_in_odd"][o], ODD_SPLITS)
    pad = jnp.zeros((D_MODEL, OD_WIDTH - (OD_KR + D_ROPE)), F32)
    w_in = jnp.concatenate([cg, dg, mg, mq, cq, ck, cv, dqa, dkva, pad], axis=1).astype(BF16)
    p = normed_matmul(x, prm["norm_gain"][l], w_in)
    cscale = C_QK_DIM ** -0.5
    q = head_prep(p, OD_CQ, 2 * C_HEADS, prm["c_q_norm"][o], rope_a, cscale * LOG2E, seq)
    k = head_prep(p, OD_CK, 2 * C_HEADS, prm["c_k_norm"][o], rope_a, 1.0, seq)
    lam_init = 0.8 - 0.6 * math.exp(-0.3 * l)
    yc = diff_attention(q, k, p, OD_CV, prm["c_lambda"][o], prm["c_out_norm"][o], lam_init, batch, seq)
    wq = prm["w_q_b"][o].reshape(D_Q_RANK, D_HEADS, D_NOPE + D_ROPE)
    wq_rope = jnp.pad(wq[:, :, D_NOPE:], ((0, 0), (0, 0), (0, LANES - D_ROPE)))
    wq2 = jnp.concatenate([wq[:, :, :D_NOPE].reshape(D_Q_RANK, -1), wq_rope.reshape(D_Q_RANK, -1)], axis=1)
    q2 = normed_matmul(p, prm["d_q_a_norm"][o], wq2.astype(BF16), col_block=OD_DQA // D_Q_RANK)
    wkv = prm["w_kv_b"][o].reshape(D_KV_RANK, D_HEADS, D_NOPE + D_V)
    wkv2 = jnp.concatenate([wkv[:, :, :D_NOPE].reshape(D_KV_RANK, -1), wkv[:, :, D_NOPE:].reshape(D_KV_RANK, -1)],
                           axis=1)
    kv2 = normed_matmul(p, prm["d_kv_a_norm"][o], wkv2.astype(BF16), col_block=OD_CKV // D_KV_RANK)
    dscale = (D_NOPE + D_ROPE) ** -0.5
    qd = mla_prep(q2, 0, q2, D_HEADS * LANES, False, prm["d_q_norm"][o], rope_d, dscale * LOG2E, seq)
    kd = mla_prep(kv2, 0, p, OD_KR, True, prm["d_k_norm"][o], rope_d, 1.0, seq)
    yd = mla_attention(qd, kd, kv2, batch, seq)
    kvm = _memory_kv(mem, prm, l)
    ym = memory_attention(p, OD_MQ, kvm, prm["mem_q_norm"][l], prm["mem_k_norm"][l], batch, seq)
    return gated_outproj(yc, yd, ym, p, x, prm["w_out_odd"][o].astype(BF16))


def _trunk(x, mem, prm):
    batch, seq, d = x.shape
    depth = prm["norm_gain"].shape[0]
    rope_a = _rope_tables(seq, ROT_DIM, True)
    rope_d = _rope_tables(seq, D_ROPE, False)
    h = x.reshape(batch * seq, d)
    for l in range(depth):
        if l % 2 == 0:
            h = _even_layer(h, mem, prm, l, l // 2, batch, seq, rope_a)
        else:
            h = _odd_layer(h, mem, prm, l, l // 2, batch, seq, rope_a, rope_d)
    return h.reshape(batch, seq, d)


def kernel(x_prompt, x_sample, mem_prompt, mem_sample, norm_gain, mem_norm_gain, w_mem_kv, mem_q_norm, mem_k_norm, w_in_even, w_out_even, a_q_norm, a_k_norm, a_sink, hy_short_w, hy_short_b, hy_w1, hy_b1, hy_freq, hy_w2, hy_b2, hy_w3, hy_decay, hy_bias, w_in_odd, w_out_odd, c_q_norm, c_k_norm, c_lambda, c_out_norm, d_q_a_norm, w_q_b, d_kv_a_norm, w_kv_b, d_q_norm, d_k_norm):
    prm = dict(norm_gain=norm_gain, mem_norm_gain=mem_norm_gain, w_mem_kv=w_mem_kv,
               mem_q_norm=mem_q_norm, mem_k_norm=mem_k_norm, w_in_even=w_in_even,
               w_out_even=w_out_even, a_q_norm=a_q_norm, a_k_norm=a_k_norm, a_sink=a_sink,
               hy_short_w=hy_short_w, hy_short_b=hy_short_b, hy_w1=hy_w1, hy_b1=hy_b1,
               hy_freq=hy_freq, hy_w2=hy_w2, hy_b2=hy_b2, hy_w3=hy_w3, hy_decay=hy_decay,
               hy_bias=hy_bias, w_in_odd=w_in_odd, w_out_odd=w_out_odd, c_q_norm=c_q_norm,
               c_k_norm=c_k_norm, c_lambda=c_lambda, c_out_norm=c_out_norm, d_q_a_norm=d_q_a_norm,
               w_q_b=w_q_b, d_kv_a_norm=d_kv_a_norm, w_kv_b=w_kv_b, d_q_norm=d_q_norm,
               d_k_norm=d_k_norm)
    return (_trunk(x_prompt, mem_prompt, prm), _trunk(x_sample, mem_sample, prm))
```

```python
import functools
import math

import jax
import jax.numpy as jnp
from jax import lax
from jax.experimental import pallas as pl
from jax.experimental.pallas import tpu as pltpu

F32 = jnp.float32
BF16 = jnp.bfloat16

D_MODEL = 2048
N_MEM = 256
HEAD_DIM = 128
ROPE_THETA = 500000.0
ROT_DIM = HEAD_DIM // 4
EPS = 1e-6
BLOCK = 128
WINDOW = 128
NEG_INF = -1e30
A_HEADS = 8
A_KV_HEADS = 2
A_WIDTH = A_HEADS * HEAD_DIM
B_WIDTH = 1024
HY_EMB = 33
HY_FFN = 64
M_HEADS = 4
M_WIDTH = M_HEADS * HEAD_DIM
C_HEADS = 4
C_QK_DIM = 128
C_V_DIM = 256
C_WIDTH = C_HEADS * C_V_DIM
D_HEADS = 8
D_NOPE = 128
D_ROPE = 64
D_V = 128
D_Q_RANK = 512
D_KV_RANK = 256
D_WIDTH = D_HEADS * D_V
MIX_WIDTH = 2560
EVEN_SPLITS = (A_WIDTH, A_KV_HEADS * HEAD_DIM, A_KV_HEADS * HEAD_DIM, A_WIDTH,
               3 * B_WIDTH, B_WIDTH, M_WIDTH, M_WIDTH)
ODD_SPLITS = (C_HEADS * 2 * C_QK_DIM, C_HEADS * 2 * C_QK_DIM, C_WIDTH, C_WIDTH,
              D_Q_RANK, D_KV_RANK + D_ROPE, D_WIDTH, M_WIDTH, M_WIDTH)

LANES = 128
MXU_COLS = 256
DFT_N2 = 128
VMEM_LIMIT = 48 * 1024 * 1024

EV_GATE, EV_MQ, EV_BU, EV_AQ, EV_AK, EV_AV = 0, 2560, 3072, 6144, 7168, 7424
OD_GATE, OD_MQ, OD_CQ, OD_CK, OD_CV, OD_DQA, OD_CKV, OD_KR = 0, 2560, 3072, 4096, 5120, 6144, 6656, 6912
OD_WIDTH = 7168


def _tile(n, pref):
    t = min(n, pref)
    assert n % t == 0, (n, t)
    return t


def _col_tile(n, cap):
    best = None
    for t in range(MXU_COLS, min(n, cap) + 1, MXU_COLS):
        if n % t == 0:
            best = t
    assert best is not None, (n, cap)
    return best


def _params(sem):
    return pltpu.CompilerParams(dimension_semantics=sem, vmem_limit_bytes=VMEM_LIMIT)


def _rms(x, gain):
    ms = jnp.mean(x * x, axis=-1, keepdims=True)
    return x * lax.rsqrt(ms + EPS) * gain


def _silu(g):
    g = g.astype(F32)
    return g * jax.nn.sigmoid(g)


def _lanes(x, k):
    return x if k == 1 else jnp.concatenate([x] * k, axis=1)


def _rope(y, cf, sn):
    return y * cf + pltpu.roll(y, LANES // 2, 1) * sn


def _nmm_kernel(x_ref, g_ref, w_ref, hg_ref, cf_ref, sn_ref, o_ref, h_ref, *, preps):
    j = pl.program_id(1)

    @pl.when(j == 0)
    def _():
        h_ref[...] = _rms(x_ref[...].astype(F32), g_ref[...]).astype(BF16)

    def product():
        return jnp.dot(h_ref[...], w_ref[...], preferred_element_type=F32)

    plain = None
    for blk in preps:
        plain = (j != blk) if plain is None else jnp.logical_and(plain, j != blk)

    if plain is None:
        o_ref[...] = product().astype(o_ref.dtype)
        return

    @pl.when(plain)
    def _():
        o_ref[...] = product().astype(o_ref.dtype)

    for blk, (gain_row0, scales) in preps.items():
        @pl.when(j == blk)
        def _(gain_row0=gain_row0, scales=scales):
            acc = product()
            cf, sn = cf_ref[...], sn_ref[...]
            for slot, scale in enumerate(scales):
                cols = slice(slot * LANES, (slot + 1) * LANES)
                y = acc[:, cols]
                if scale is not None:
                    y = _rope(_rms(y, hg_ref[gain_row0 + slot:gain_row0 + slot + 1, :]), cf, sn) * scale
                o_ref[:, cols] = y.astype(o_ref.dtype)


def normed_matmul(x, gain, w, *, seq, col_block=0, tm=1024, tn=1536, preps=None, head_gains=None, tables=None):
    rows = x.shape[0]
    k, n = w.shape
    tm = _tile(seq, tm)
    tn = _col_tile(n, tn)
    if preps is None:
        preps, head_gains = {}, jnp.ones((8, LANES), F32)
        tables = (jnp.ones((tm, LANES), F32), jnp.zeros((tm, LANES), F32))
    tab_spec = pl.BlockSpec((tm, LANES), lambda i, j: (i % (tables[0].shape[0] // tm), 0))
    return pl.pallas_call(
        functools.partial(_nmm_kernel, preps=preps),
        grid=(rows // tm, n // tn),
        in_specs=[pl.BlockSpec((tm, k), lambda i, j: (i, col_block)),
                  pl.BlockSpec((1, k), lambda i, j: (0, 0)),
                  pl.BlockSpec((k, tn), lambda i, j: (0, j)),
                  pl.BlockSpec(head_gains.shape, lambda i, j: (0, 0)),
                  tab_spec, tab_spec],
        out_specs=pl.BlockSpec((tm, tn), lambda i, j: (i, j)),
        out_shape=jax.ShapeDtypeStruct((rows, n), BF16),
        scratch_shapes=[pltpu.VMEM((tm, k), BF16)],
        compiler_params=_params(("parallel", "arbitrary")),
        name="normed_matmul",
    )(x, gain.reshape(1, k).astype(F32), w, head_gains.astype(F32), *tables)


WIN_QB = 4


def _window_kernel(sink_ref, q_ref, kp_ref, kc_ref, kn_ref, vp_ref, vc_ref, vn_ref, g_ref, o_ref, *, seq, qb):
    n = pl.program_id(1)
    group = A_HEADS // A_KV_HEADS
    rows = group * BLOCK
    qi = lax.broadcasted_iota(jnp.int32, (rows, 3 * BLOCK), 0) & (BLOCK - 1)
    kj = lax.broadcasted_iota(jnp.int32, (rows, 3 * BLOCK), 1)
    in_win = jnp.abs(kj - BLOCK - qi) <= WINDOW
    for kv in range(A_KV_HEADS):
        cols = slice(kv * HEAD_DIM, (kv + 1) * HEAD_DIM)
        heads = range(kv * group, (kv + 1) * group)
        kfull = jnp.concatenate([kp_ref[:, cols], kc_ref[:, cols], kn_ref[:, cols]], axis=0).astype(BF16)
        vfull = jnp.concatenate([vp_ref[:, cols], vc_ref[:, cols], vn_ref[:, cols]], axis=0).astype(BF16)
        sk = jnp.concatenate([jnp.full((BLOCK, 1), sink_ref[h], F32) for h in heads], axis=0)
        for i in range(qb):
            qrows = slice(i * BLOCK, (i + 1) * BLOCK)
            q = jnp.concatenate([q_ref[qrows, h * HEAD_DIM:(h + 1) * HEAD_DIM] for h in heads], axis=0)
            s = lax.dot_general(q, kfull[i * BLOCK:(i + 3) * BLOCK], (((1,), (1,)), ((), ())),
                                preferred_element_type=F32)
            kpos = (n * qb + i - 1) * BLOCK + kj
            s = jnp.where(in_win & (kpos >= 0) & (kpos < seq), s, NEG_INF)
            m = jnp.maximum(jnp.max(s, axis=-1, keepdims=True), sk)
            pr = jnp.exp(s - m)
            denom = jnp.sum(pr, axis=-1, keepdims=True) + jnp.exp(sk - m)
            pr = pr / denom
            o = jnp.dot(pr.astype(BF16), vfull[i * BLOCK:(i + 3) * BLOCK], preferred_element_type=F32)
            for g, h in enumerate(heads):
                hc = slice(h * HEAD_DIM, (h + 1) * HEAD_DIM)
                o_ref[qrows, hc] = (o[g * BLOCK:(g + 1) * BLOCK] * _silu(g_ref[qrows, hc])).astype(o_ref.dtype)


def window_attention(p, q_col, k_col, v_col, g_col, sink, batch, seq):
    nblk = seq // BLOCK
    qb = _tile(nblk, WIN_QB)
    nstep = nblk // qb
    kvw = A_KV_HEADS * HEAD_DIM
    kb, vb = k_col // kvw, v_col // kvw

    def prev(b, n):
        return b * nblk + jnp.maximum(n * qb - 1, 0)

    def cur(b, n):
        return b * nstep + n

    def nxt(b, n):
        return b * nblk + jnp.minimum((n + 1) * qb, nblk - 1)

    return pl.pallas_call(
        functools.partial(_window_kernel, seq=seq, qb=qb),
        grid=(batch, nstep),
        in_specs=[pl.BlockSpec(memory_space=pltpu.SMEM),
                  pl.BlockSpec((qb * BLOCK, A_WIDTH), lambda b, n: (cur(b, n), q_col // A_WIDTH)),
                  pl.BlockSpec((BLOCK, kvw), lambda b, n: (prev(b, n), kb)),
                  pl.BlockSpec((qb * BLOCK, kvw), lambda b, n: (cur(b, n), kb)),
                  pl.BlockSpec((BLOCK, kvw), lambda b, n: (nxt(b, n), kb)),
                  pl.BlockSpec((BLOCK, kvw), lambda b, n: (prev(b, n), vb)),
                  pl.BlockSpec((qb * BLOCK, kvw), lambda b, n: (cur(b, n), vb)),
                  pl.BlockSpec((BLOCK, kvw), lambda b, n: (nxt(b, n), vb)),
                  pl.BlockSpec((qb * BLOCK, A_WIDTH), lambda b, n: (cur(b, n), g_col // A_WIDTH))],
        out_specs=pl.BlockSpec((qb * BLOCK, A_WIDTH), lambda b, n: (cur(b, n), 0)),
        out_shape=jax.ShapeDtypeStruct((batch * seq, A_WIDTH), BF16),
        compiler_params=_params(("parallel", "parallel")),
        name="window_attention",
    )(sink.astype(F32), p, p, p, p, p, p, p, p)


def _mem_kernel(q_ref, kv_ref, qg_ref, kg_ref, g_ref, o_ref):
    qg, kg = qg_ref[...], kg_ref[...]
    scale = HEAD_DIM ** -0.5
    for h in range(M_HEADS):
        cols = slice(h * HEAD_DIM, (h + 1) * HEAD_DIM)
        q = (_rms(q_ref[:, cols].astype(F32), qg) * scale).astype(BF16)
        k = _rms(kv_ref[:, cols].astype(F32), kg).astype(BF16)
        v = kv_ref[:, M_WIDTH + h * HEAD_DIM:M_WIDTH + (h + 1) * HEAD_DIM].astype(BF16)
        s = lax.dot_general(q, k, (((1,), (1,)), ((), ())), preferred_element_type=F32)
        m = jnp.max(s, axis=-1, keepdims=True)
        pr = jnp.exp(s - m)
        pr = pr / jnp.sum(pr, axis=-1, keepdims=True)
        o = jnp.dot(pr.astype(BF16), v, preferred_element_type=F32)
        o_ref[:, cols] = (o * _silu(g_ref[:, cols])).astype(o_ref.dtype)


def memory_attention(p, q_col, g_col, kvm, q_gain, k_gain, batch, seq, *, tq=512):
    tq = _tile(seq, tq)
    nq = seq // tq
    return pl.pallas_call(
        _mem_kernel,
        grid=(batch, nq),
        in_specs=[pl.BlockSpec((tq, M_WIDTH), lambda b, i: (b * nq + i, q_col // M_WIDTH)),
                  pl.BlockSpec((N_MEM, 2 * M_WIDTH), lambda b, i: (b, 0)),
                  pl.BlockSpec((1, HEAD_DIM), lambda b, i: (0, 0)),
                  pl.BlockSpec((1, HEAD_DIM), lambda b, i: (0, 0)),
                  pl.BlockSpec((tq, M_WIDTH), lambda b, i: (b * nq + i, g_col // M_WIDTH))],
        out_specs=pl.BlockSpec((tq, M_WIDTH), lambda b, i: (b * nq + i, 0)),
        out_shape=jax.ShapeDtypeStruct((batch * seq, M_WIDTH), BF16),
        compiler_params=_params(("parallel", "parallel")),
        name="memory_attention",
    )(p, kvm, q_gain.reshape(1, HEAD_DIM).astype(F32), k_gain.reshape(1, HEAD_DIM).astype(F32), p)


LOG2E = math.log2(math.e)


def _chunk_rows(chunk, tk):
    return pl.ds(pl.multiple_of(chunk * tk, tk), tk)


def _pipelined_chunks(scores, consume, nk):
    scores(0, 0)

    def body(i, carry):
        scores(2 * i + 1, 1)
        consume(2 * i, 0)
        scores(jnp.minimum(2 * i + 2, nk - 1), 0)
        consume(2 * i + 1, 1)
        return carry

    lax.fori_loop(0, nk // 2, body, 0)


def _diff_flash_kernel(q_ref, k_ref, v_ref, lam_ref, g_ref, gate_ref, o_ref, m_ref, l_ref, acc_ref, s_ref, *,
                       tk, nk, lam_init):
    m_ref[...] = jnp.full(m_ref.shape, NEG_INF, F32)
    l_ref[...] = jnp.zeros(l_ref.shape, F32)
    acc_ref[...] = jnp.zeros(acc_ref.shape, F32)

    def scores(chunk, c):
        cols = slice(c * C_QK_DIM, (c + 1) * C_QK_DIM)
        s_ref[c] = lax.dot_general(q_ref[:, cols], k_ref[_chunk_rows(chunk, tk), cols], (((1,), (1,)), ((), ())),
                                   preferred_element_type=F32)

    def consume(chunk, c):
        s = s_ref[c]
        m_prev = m_ref[c]
        m_new = jnp.maximum(m_prev, jnp.max(s, axis=-1, keepdims=True))
        alpha = jnp.exp2(m_prev - m_new)
        pr = jnp.exp2(s - _lanes(m_new, tk // LANES))
        l_ref[c] = alpha * l_ref[c] + jnp.sum(pr, axis=-1, keepdims=True)
        acc_ref[c] = (_lanes(alpha, C_V_DIM // LANES) * acc_ref[c]
                      + jnp.dot(pr.astype(BF16), v_ref[_chunk_rows(chunk, tk), :], preferred_element_type=F32))
        m_ref[c] = m_new

    scores(0, 0)

    def body(i, carry):
        for j in (2 * i, 2 * i + 1):
            scores(j, 1)
            consume(j, 0)
            scores(jnp.minimum(j + 1, nk - 1), 0)
            consume(j, 1)
        return carry

    lax.fori_loop(0, nk // 2, body, 0)
    lv = lam_ref[...]
    lam = (jnp.exp(jnp.sum(lv[0:1] * lv[1:2], axis=-1, keepdims=True))
           - jnp.exp(jnp.sum(lv[2:3] * lv[3:4], axis=-1, keepdims=True)) + lam_init)
    o1 = acc_ref[0] / _lanes(l_ref[0], C_V_DIM // LANES)
    o2 = acc_ref[1] / _lanes(l_ref[1], C_V_DIM // LANES)
    o_ref[...] = (_rms(o1 - lam * o2, g_ref[...]) * (1.0 - lam_init) * _silu(gate_ref[...])).astype(o_ref.dtype)


def _mla_flash_kernel(q_ref, k_ref, v_ref, gate_ref, o_ref, m_ref, acc_ref, s_ref, *, tk, nk):
    m_ref[...] = jnp.full(m_ref.shape, NEG_INF, F32)
    acc_ref[...] = jnp.zeros(acc_ref.shape, F32)
    ones = jnp.ones((tk, LANES), BF16)

    def scores(chunk, slot):
        s_ref[slot] = lax.dot_general(q_ref[...], k_ref[_chunk_rows(chunk, tk), :], (((1,), (1,)), ((), ())),
                                      preferred_element_type=F32)

    def consume(chunk, slot):
        s = s_ref[slot]
        m_prev = m_ref[...]
        m_new = jnp.maximum(m_prev, jnp.max(s, axis=-1, keepdims=True))
        alpha = jnp.exp2(m_prev - m_new)
        pr = jnp.exp2(s - _lanes(m_new, tk // LANES))
        vv = jnp.concatenate([v_ref[_chunk_rows(chunk, tk), :], ones], axis=1)
        acc_ref[...] = (_lanes(alpha, 2) * acc_ref[...]
                        + jnp.dot(pr.astype(BF16), vv, preferred_element_type=F32))
        m_ref[...] = m_new

    _pipelined_chunks(scores, consume, nk)
    o_ref[...] = (acc_ref[:, :D_V] / acc_ref[:, D_V:] * _silu(gate_ref[...])).astype(o_ref.dtype)


def _flash_call(kernel, q, q_block0, k, k_block0, v, v_block0, gate, gate_block0, extra, extra_specs, scratch, *,
                heads, dv, batch, seq, tq, tk, name):
    nq, nk = seq // tq, seq // tk
    qk_w = 2 * LANES
    return pl.pallas_call(
        functools.partial(kernel, tk=tk, nk=nk),
        grid=(batch, heads, nq),
        in_specs=[pl.BlockSpec((tq, qk_w), lambda b, h, i: (b * nq + i, q_block0 + h)),
                  pl.BlockSpec((seq, qk_w), lambda b, h, i: (b, k_block0 + h), pipeline_mode=pl.Buffered(1)),
                  pl.BlockSpec((seq, dv), lambda b, h, i: (b, v_block0 + h), pipeline_mode=pl.Buffered(1))]
        + extra_specs + [pl.BlockSpec((tq, dv), lambda b, h, i: (b * nq + i, gate_block0 + h))],
        out_specs=pl.BlockSpec((tq, dv), lambda b, h, i: (b * nq + i, h)),
        out_shape=jax.ShapeDtypeStruct((batch * seq, heads * dv), BF16),
        scratch_shapes=scratch,
        compiler_params=_params(("parallel", "parallel", "arbitrary")),
        name=name,
    )(q, k, v, *extra, gate)


def diff_attention(p, q_col, k_col, v_col, g_col, c_lambda, out_gain, lam_init, batch, seq, *, tq=1024, tk=512):
    tq, tk = _tile(seq, tq), _tile(seq // 2, tk)
    const = lambda b, h, i: (0, 0)
    return _flash_call(
        functools.partial(_diff_flash_kernel, lam_init=lam_init), p, q_col // C_V_DIM, p, k_col // C_V_DIM,
        p, v_col // C_V_DIM, p, g_col // C_V_DIM,
        [c_lambda.astype(F32), out_gain.reshape(1, C_V_DIM).astype(F32)],
        [pl.BlockSpec((4, C_QK_DIM), const), pl.BlockSpec((1, C_V_DIM), const)],
        [pltpu.VMEM((2, tq, LANES), F32), pltpu.VMEM((2, tq, LANES), F32), pltpu.VMEM((2, tq, C_V_DIM), F32),
         pltpu.VMEM((2, tq, tk), F32)],
        heads=C_HEADS, dv=C_V_DIM, batch=batch, seq=seq, tq=tq, tk=tk, name="diff_attention")


def mla_attention(q, k, v, p, g_col, batch, seq, *, tq=1024, tk=1024):
    tq, tk = _tile(seq, tq), _tile(seq // 2, tk)
    return _flash_call(_mla_flash_kernel, q, 0, k, 0, v, 0, p, g_col // D_V, [], [],
                       [pltpu.VMEM((tq, LANES), F32), pltpu.VMEM((tq, D_V + LANES), F32),
                        pltpu.VMEM((2, tq, tk), F32)],
                       heads=D_HEADS, dv=D_V, batch=batch, seq=seq, tq=tq, tk=tk, name="mla_attention")


def _mla_proj_kernel(*refs, shared_rope, scale):
    if shared_rope:
        x_ref, g_ref, w_ref, kr_ref, gn_ref, gr_ref, cf_ref, sn_ref, o_ref, v_ref = refs
    else:
        x_ref, g_ref, w_ref, gn_ref, gr_ref, cf_ref, sn_ref, o_ref = refs
    h = _rms(x_ref[...].astype(F32), g_ref[...]).astype(BF16)
    acc = jnp.dot(h, w_ref[...], preferred_element_type=F32)
    gn, gr = gn_ref[...], gr_ref[...]
    cf, sn = cf_ref[...], sn_ref[...]
    width = D_HEADS * LANES
    for hd in range(D_HEADS):
        cols = slice(hd * LANES, (hd + 1) * LANES)
        xn = acc[:, cols]
        xr = kr_ref[...].astype(F32) if shared_rope else acc[:, width + hd * LANES:width + (hd + 1) * LANES]
        ms = (jnp.sum(xn * xn, axis=-1, keepdims=True)
              + jnp.sum(xr * xr, axis=-1, keepdims=True)) * (1.0 / (D_NOPE + D_ROPE))
        inv = lax.rsqrt(ms + EPS)
        o_ref[:, 2 * hd * LANES:(2 * hd + 1) * LANES] = (xn * inv * gn * scale).astype(o_ref.dtype)
        o_ref[:, (2 * hd + 1) * LANES:(2 * hd + 2) * LANES] = (_rope(xr * inv * gr, cf, sn) * scale).astype(o_ref.dtype)
    if shared_rope:
        v_ref[...] = acc[:, width:].astype(v_ref.dtype)


def mla_projection(p, col, rank, gain, w, head_gain, tables, scale, seq, *, rope_col=None, tm=512):
    rows = p.shape[0]
    width = D_HEADS * LANES
    tm = _tile(seq, tm)
    nblk = seq // tm
    shared_rope = rope_col is not None
    gn = head_gain[:D_NOPE].reshape(1, LANES).astype(F32)
    gr = _rope_slot(head_gain[D_NOPE:]).reshape(1, LANES).astype(F32)
    one = pl.BlockSpec((1, LANES), lambda i: (0, 0))
    tab_spec = pl.BlockSpec((tm, LANES), lambda i: (i % nblk, 0))
    in_specs = [pl.BlockSpec((tm, rank), lambda i: (i, col // rank)),
                pl.BlockSpec((1, rank), lambda i: (0, 0)),
                pl.BlockSpec((rank, 2 * width), lambda i: (0, 0))]
    args = [p, gain.reshape(1, rank).astype(F32), w]
    out_specs = [pl.BlockSpec((tm, 2 * width), lambda i: (i, 0))]
    out_shape = [jax.ShapeDtypeStruct((rows, 2 * width), BF16)]
    if shared_rope:
        in_specs.append(pl.BlockSpec((tm, LANES), lambda i: (i, rope_col // LANES)))
        args.append(p)
        out_specs.append(pl.BlockSpec((tm, width), lambda i: (i, 0)))
        out_shape.append(jax.ShapeDtypeStruct((rows, width), BF16))
    return pl.pallas_call(
        functools.partial(_mla_proj_kernel, shared_rope=shared_rope, scale=scale),
        grid=(rows // tm,),
        in_specs=in_specs + [one, one, tab_spec, tab_spec],
        out_specs=out_specs,
        out_shape=out_shape,
        compiler_params=_params(("parallel",)),
        name="mla_projection",
    )(*args, gn, gr, *tables)


HALO = 16


def _hyena_gate_kernel(u_ref, up_ref, un_ref, w_ref, b_ref, g_ref, zz_ref, x0_ref, *, nblk, tm):
    li = pl.program_id(1)
    row = lax.broadcasted_iota(jnp.int32, (tm, B_WIDTH), 0)
    has_prev = (li > 0).astype(F32)
    has_next = (li < nblk - 1).astype(F32)

    def conv(part):
        cols = slice(part * B_WIDTH, (part + 1) * B_WIDTH)
        u = u_ref[:, cols].astype(F32)
        prev_row = up_ref[HALO - 1:HALO, cols].astype(F32) * has_prev
        next_row = un_ref[0:1, cols].astype(F32) * has_next
        above = jnp.where(row == 0, prev_row, pltpu.roll(u, 1, 0))
        below = jnp.where(row == tm - 1, next_row, pltpu.roll(u, tm - 1, 0))
        w = w_ref[:, cols]
        return above * w[0:1] + u * w[1:2] + below * w[2:3] + b_ref[:, cols]

    x0_ref[...] = (conv(0) * _silu(g_ref[...])).astype(x0_ref.dtype)
    zz_ref[...] = (conv(2) * conv(1)).astype(zz_ref.dtype)


def hyena_gate(p, u_col, g_col, short_w, short_b, batch, seq, *, tm=256):
    tm = _tile(seq, tm)
    nblk = seq // tm
    uw = 3 * B_WIDTH
    ub = u_col // uw
    hb = tm // HALO
    out = jax.ShapeDtypeStruct((batch * seq, B_WIDTH), BF16)
    return pl.pallas_call(
        functools.partial(_hyena_gate_kernel, nblk=nblk, tm=tm),
        grid=(batch, nblk),
        in_specs=[pl.BlockSpec((tm, uw), lambda b, i: (b * nblk + i, ub)),
                  pl.BlockSpec((HALO, uw), lambda b, i: (jnp.maximum((b * nblk + i) * hb - 1, 0), ub)),
                  pl.BlockSpec((HALO, uw),
                               lambda b, i: (jnp.minimum((b * nblk + i + 1) * hb, batch * nblk * hb - 1), ub)),
                  pl.BlockSpec((3, uw), lambda b, i: (0, 0)),
                  pl.BlockSpec((1, uw), lambda b, i: (0, 0)),
                  pl.BlockSpec((tm, B_WIDTH), lambda b, i: (b * nblk + i, g_col // B_WIDTH))],
        out_specs=[pl.BlockSpec((tm, B_WIDTH), lambda b, i: (b * nblk + i, 0))] * 2,
        out_shape=[out, out],
        compiler_params=_params(("parallel", "parallel")),
        name="hyena_gate",
    )(p, p, p, short_w.astype(F32), short_b.reshape(1, uw).astype(F32), p)


def _hdot(a, b):
    return jnp.dot(a, b, preferred_element_type=F32, precision=lax.Precision.HIGHEST)


def _filter_kernel(fr_ref, w1_ref, b1_ref, fq_ref, w2_ref, b2_ref, w3_ref, dec_ref, k_ref, s_ref, *, seq, tr):
    i = pl.program_id(0)
    r = i * tr + lax.broadcasted_iota(jnp.int32, (tr, 1), 0)
    pos = jnp.where(r < seq, r, 2 * seq - 1 - r).astype(F32)
    t = pos * (1.0 / (seq - 1))
    w = pos * (2.0 * math.pi / seq)
    lane = lax.broadcasted_iota(jnp.int32, (tr, LANES), 1)
    bands = (HY_EMB - 1) // 2
    ang = w * fr_ref[...]
    z = jnp.where(lane == 0, t,
                  jnp.where(lane <= bands, jnp.cos(ang),
                            jnp.where(lane <= 2 * bands, -jnp.sin(ang), 0.0)))
    h = jnp.sin(fq_ref[0:1] * (_hdot(z, w1_ref[...]) + b1_ref[...]))
    h = jnp.sin(fq_ref[1:2] * (_hdot(h, w2_ref[...]) + b2_ref[...]))
    h = _hdot(h, w3_ref[...]) * jnp.exp(-t * jnp.abs(dec_ref[...]))
    h = jnp.where(r == seq, 0.0, h)
    k_ref[...] = h.astype(k_ref.dtype)

    @pl.when(i == 0)
    def _():
        s_ref[...] = jnp.zeros(s_ref.shape, F32)

    s_ref[...] += jnp.sum(jnp.abs(h), axis=0, keepdims=True)


def hyena_filter(seq, w1, b1, freq, w2, b2, w3, decay, *, tr=256):
    tr = _tile(seq, tr)
    nhalf = seq // tr
    bands = (HY_EMB - 1) // 2
    fr = jnp.linspace(1e-4, bands - 1, bands, dtype=F32)
    fr_lanes = jnp.concatenate([jnp.zeros((1,), F32), fr, fr, jnp.zeros((LANES - HY_EMB,), F32)]).reshape(1, LANES)
    w1p = jnp.pad(w1.astype(F32), ((0, LANES - HY_EMB), (0, 0)))
    const = lambda i: (0, 0)
    half = lambda i: (0, i // nhalf)
    return pl.pallas_call(
        functools.partial(_filter_kernel, seq=seq, tr=tr),
        grid=(2 * nhalf,),
        in_specs=[pl.BlockSpec((1, LANES), const),
                  pl.BlockSpec((LANES, HY_FFN), const),
                  pl.BlockSpec((1, HY_FFN), const),
                  pl.BlockSpec((2, HY_FFN), const),
                  pl.BlockSpec((HY_FFN, HY_FFN), const),
                  pl.BlockSpec((1, HY_FFN), const),
                  pl.BlockSpec((HY_FFN, B_WIDTH), half),
                  pl.BlockSpec((1, B_WIDTH), half)],
        out_specs=[pl.BlockSpec((tr, B_WIDTH), lambda i: (i, 0)),
                   pl.BlockSpec((1, B_WIDTH), const)],
        out_shape=[jax.ShapeDtypeStruct((2 * seq, B_WIDTH), BF16),
                   jax.ShapeDtypeStruct((1, B_WIDTH), F32)],
        compiler_params=_params(("arbitrary",)),
        name="hyena_filter",
    )(fr_lanes, w1p, b1.reshape(1, HY_FFN).astype(F32), freq.astype(F32), w2.astype(F32),
      b2.reshape(1, HY_FFN).astype(F32), w3.astype(F32), decay.reshape(1, 2 * B_WIDTH).astype(F32))


def _bdot(a, b):
    return jnp.dot(a, b, preferred_element_type=F32)


def dft_tables(n1):
    n = n1 * DFT_N2
    k1 = jnp.arange(n1, dtype=jnp.int32)[None, :, None]
    m1 = jnp.arange(n1, dtype=jnp.int32)[None, None, :]
    n2 = jnp.arange(DFT_N2, dtype=jnp.int32)[:, None, None]
    ang = ((k1 * (DFT_N2 * m1 + n2)) % n).astype(F32) * (2.0 * math.pi / n)
    gc = jnp.cos(ang).astype(BF16)
    gsn = (-jnp.sin(ang)).astype(BF16)
    hc = jnp.swapaxes(gc, 1, 2)
    hsn = jnp.swapaxes(gsn, 1, 2)
    a = jnp.arange(DFT_N2, dtype=jnp.int32)
    ang2 = ((a[:, None] * a[None, :]) % DFT_N2).astype(F32) * (2.0 * math.pi / DFT_N2)
    return gc, gsn, hc, hsn, jnp.cos(ang2).astype(BF16), jnp.sin(ang2).astype(BF16)


def _dft1_kernel(x_ref, gc_ref, gs_ref, ar_ref, ai_ref):
    x = x_ref[...].astype(BF16)
    ar_ref[...] = _bdot(gc_ref[...], x).astype(ar_ref.dtype)
    ai_ref[...] = _bdot(gs_ref[...], x).astype(ai_ref.dtype)


def dft_stage1(x, gc, gsn, n1, *, tc=1024):
    batch, k1rows, cols = x.shape
    c = cols // DFT_N2
    tc = _tile(c, tc)
    nc = c // tc
    out = jax.ShapeDtypeStruct((batch, n1, cols), BF16)
    g_spec = pl.BlockSpec((None, n1, k1rows), lambda n2, b, ci: (n2, 0, 0))
    o_spec = pl.BlockSpec((None, n1, tc), lambda n2, b, ci: (b, 0, n2 * nc + ci))
    return pl.pallas_call(
        _dft1_kernel,
        grid=(DFT_N2, batch, nc),
        in_specs=[pl.BlockSpec((None, k1rows, tc), lambda n2, b, ci: (b, 0, n2 * nc + ci)), g_spec, g_spec],
        out_specs=[o_spec, o_spec],
        out_shape=[out, out],
        compiler_params=_params(("parallel", "parallel", "parallel")),
        name="dft_stage1",
    )(x, gc, gsn)


def _filter_spectrum_kernel(ar_ref, ai_ref, s_ref, fc_ref, fs_ref, kr_ref, ki_ref, *, kb, n):
    fc, fs = fc_ref[...], fs_ref[...]
    scale = 1.0 / (s_ref[...] * n)
    for j in range(kb):
        ar, ai = ar_ref[j], ai_ref[j]
        kr_ref[j] = (_bdot(fc, ar) + _bdot(fs, ai)) * scale
        ki_ref[j] = (_bdot(fc, ai) - _bdot(fs, ar)) * scale


def filter_spectrum(ar, ai, asum, fc, fs, n1, *, kb=2, tc=512):
    c = asum.shape[1]
    kb = _tile(n1, kb)
    tc = _tile(c, tc)
    a_spec = pl.BlockSpec((kb, DFT_N2, tc), lambda k, ci: (k, 0, ci))
    f_spec = pl.BlockSpec((DFT_N2, DFT_N2), lambda k, ci: (0, 0))
    out = jax.ShapeDtypeStruct((n1, DFT_N2, c), F32)
    return pl.pallas_call(
        functools.partial(_filter_spectrum_kernel, kb=kb, n=n1 * DFT_N2),
        grid=(n1 // kb, c // tc),
        in_specs=[a_spec, a_spec, pl.BlockSpec((1, tc), lambda k, ci: (0, ci)), f_spec, f_spec],
        out_specs=[a_spec, a_spec],
        out_shape=[out, out],
        compiler_params=_params(("parallel", "parallel")),
        name="filter_spectrum",
    )(ar.reshape(n1, DFT_N2, c), ai.reshape(n1, DFT_N2, c), asum, fc, fs)


def _spectral_kernel(ar_ref, ai_ref, kr_ref, ki_ref, fc_ref, fs_ref, br_ref, bi_ref, *, kb):
    fc, fs = fc_ref[...], fs_ref[...]
    for j in range(kb):
        ar, ai = ar_ref[j], ai_ref[j]
        xr = _bdot(fc, ar) + _bdot(fs, ai)
        xi = _bdot(fc, ai) - _bdot(fs, ar)
        kr, ki = kr_ref[j], ki_ref[j]
        yr = (xr * kr - xi * ki).astype(BF16)
        yi = (xr * ki + xi * kr).astype(BF16)
        br_ref[j] = (_bdot(fc, yr) - _bdot(fs, yi)).astype(br_ref.dtype)
        bi_ref[j] = (_bdot(fc, yi) + _bdot(fs, yr)).astype(bi_ref.dtype)


def spectral_multiply(ar, ai, kr, ki, fc, fs, n1, *, kb=2, tc=512):
    batch = ar.shape[0]
    c = kr.shape[2]
    kb = _tile(n1, kb)
    tc = _tile(c, tc)
    a_spec = pl.BlockSpec((None, kb, DFT_N2, tc), lambda k, ci, b: (b, k, 0, ci))
    k_spec = pl.BlockSpec((kb, DFT_N2, tc), lambda k, ci, b: (k, 0, ci))
    f_spec = pl.BlockSpec((DFT_N2, DFT_N2), lambda k, ci, b: (0, 0))
    out = jax.ShapeDtypeStruct((batch, n1, DFT_N2, c), BF16)
    br, bi = pl.pallas_call(
        functools.partial(_spectral_kernel, kb=kb),
        grid=(n1 // kb, c // tc, batch),
        in_specs=[a_spec, a_spec, k_spec, k_spec, f_spec, f_spec],
        out_specs=[a_spec, a_spec],
        out_shape=[out, out],
        compiler_params=_params(("parallel", "parallel", "parallel")),
        name="spectral_multiply",
    )(ar.reshape(batch, n1, DFT_N2, c), ai.reshape(batch, n1, DFT_N2, c), kr, ki, fc, fs)
    return br.reshape(batch, n1, DFT_N2 * c), bi.reshape(batch, n1, DFT_N2 * c)


def _idft_out_kernel(br_ref, bi_ref, hc_ref, hs_ref, zz_ref, x0_ref, bias_ref, o_ref):
    y = _bdot(hc_ref[...], br_ref[...]) + _bdot(hs_ref[...], bi_ref[...])
    zz = zz_ref[...].astype(F32)
    o_ref[...] = ((y + zz * bias_ref[...]) * x0_ref[...].astype(F32)).astype(o_ref.dtype)


def idft_output(br, bi, hc, hsn, zz, x0, bias, n1, *, tc=1024):
    batch, _, cols = br.shape
    c = cols // DFT_N2
    tc = _tile(c, tc)
    nc = c // tc
    half = n1 // 2
    b_spec = pl.BlockSpec((None, n1, tc), lambda n2, b, ci: (b, 0, n2 * nc + ci))
    h_spec = pl.BlockSpec((None, half, n1), lambda n2, b, ci: (n2, 0, 0))
    z_spec = pl.BlockSpec((None, half, tc), lambda n2, b, ci: (b, 0, n2 * nc + ci))
    return pl.pallas_call(
        _idft_out_kernel,
        grid=(DFT_N2, batch, nc),
        in_specs=[b_spec, b_spec, h_spec, h_spec, z_spec, z_spec,
                  pl.BlockSpec((1, tc), lambda n2, b, ci: (0, ci))],
        out_specs=z_spec,
        out_shape=jax.ShapeDtypeStruct((batch, half, cols), BF16),
        compiler_params=_params(("parallel", "parallel", "parallel")),
        name="idft_output",
    )(br, bi, hc, hsn, zz, x0, bias)


def hyena_mixer(p, u_col, g_col, prm, e, batch, seq):
    n1 = 2 * seq // DFT_N2
    half = n1 // 2
    gc, gsn, hc, hsn, fc, fs = dft_tables(n1)
    kern, asum = hyena_filter(seq, prm["hy_w1"][e], prm["hy_b1"][e], prm["hy_freq"][e], prm["hy_w2"][e],
                              prm["hy_b2"][e], prm["hy_w3"][e], prm["hy_decay"][e])
    far, fai = dft_stage1(kern.reshape(1, n1, DFT_N2 * B_WIDTH), gc, gsn, n1)
    kr, ki = filter_spectrum(far[0], fai[0], asum, fc, fs, n1)
    zz, x0 = hyena_gate(p, u_col, g_col, prm["hy_short_w"][e], prm["hy_short_b"][e], batch, seq)
    zz3 = zz.reshape(batch, half, DFT_N2 * B_WIDTH)
    ar, ai = dft_stage1(zz3, gc[:, :, :half], gsn[:, :, :half], n1)
    br, bi = spectral_multiply(ar, ai, kr, ki, fc, fs, n1)
    yb = idft_output(br, bi, hc[:, :half], hsn[:, :half], zz3, x0.reshape(batch, half, DFT_N2 * B_WIDTH),
                     prm["hy_bias"][e].reshape(1, B_WIDTH).astype(F32), n1)
    return yb.reshape(batch * seq, B_WIDTH)


def _outproj_kernel(y1_ref, y2_ref, y3_ref, x_ref, w_ref, o_ref):
    acc = x_ref[...]
    off = 0
    for y_ref in (y1_ref, y2_ref, y3_ref):
        wd = y_ref.shape[1]
        acc = acc + jnp.dot(y_ref[...], w_ref[off:off + wd, :], preferred_element_type=F32)
        off += wd
    o_ref[...] = acc


def out_projection(y1, y2, y3, x, w, *, tm=512, tn=1024):
    rows, d = x.shape
    tm = _tile(rows, tm)
    tn = _col_tile(d, tn)
    row = lambda i, j: (i, 0)
    return pl.pallas_call(
        _outproj_kernel,
        grid=(rows // tm, d // tn),
        in_specs=[pl.BlockSpec((tm, y1.shape[1]), row),
                  pl.BlockSpec((tm, y2.shape[1]), row),
                  pl.BlockSpec((tm, y3.shape[1]), row),
                  pl.BlockSpec((tm, tn), lambda i, j: (i, j)),
                  pl.BlockSpec((MIX_WIDTH, tn), lambda i, j: (0, j))],
        out_specs=pl.BlockSpec((tm, tn), lambda i, j: (i, j)),
        out_shape=jax.ShapeDtypeStruct((rows, d), F32),
        compiler_params=_params(("parallel", "parallel")),
        name="out_projection",
    )(y1, y2, y3, x, w)


def _split_cols(w, sizes):
    out, off = [], 0
    for s in sizes:
        out.append(w[:, off:off + s])
        off += s
    return out


ROT_HALF = ROT_DIM // 2
HEAD_PERM = (tuple(range(ROT_HALF)) + tuple(range(ROT_DIM, LANES // 2 + ROT_HALF))
             + tuple(range(ROT_HALF, ROT_DIM)) + tuple(range(LANES // 2 + ROT_HALF, LANES)))


def _permute_heads(w, nheads):
    k = w.shape[0]
    return w.reshape(k, nheads, HEAD_DIM)[:, :, jnp.array(HEAD_PERM)].reshape(k, nheads * HEAD_DIM)


def _rope_slot(w):
    half = D_ROPE // 2
    z = jnp.zeros(w.shape[:-1] + (LANES // 2 - half,), w.dtype)
    return jnp.concatenate([w[..., :half], z, w[..., half:], z], axis=-1)


def _rope_tables(seq, dim, passthrough):
    half = dim // 2
    inv = ROPE_THETA ** (-jnp.arange(0, dim, 2, dtype=F32) / dim)
    ang = jnp.arange(seq, dtype=F32)[:, None] * inv[None, :]
    cos, sin = jnp.cos(ang), jnp.sin(ang)
    rest = jnp.full((seq, LANES // 2 - half), 1.0 if passthrough else 0.0, F32)
    zeros = jnp.zeros((seq, LANES // 2 - half), F32)
    cf = jnp.concatenate([cos, rest, cos, rest], axis=1)
    sn = jnp.concatenate([-sin, zeros, sin, zeros], axis=1)
    return cf, sn


def _memory_kv(mem, prm, l):
    batch = mem.shape[0]
    return normed_matmul(mem.reshape(batch * N_MEM, D_MODEL), prm["mem_norm_gain"][l],
                         prm["w_mem_kv"][l].astype(BF16), seq=N_MEM)


def _head_gains(*groups):
    rows = [jnp.broadcast_to(g[jnp.array(HEAD_PERM)][None, :], (n, HEAD_DIM)) for g, n in groups]
    return jnp.concatenate(rows, axis=0)


def _even_layer(x, mem, prm, l, e, batch, seq, rope_a):
    aq, ak, av, ag, bu, bg, mq, mg = _split_cols(prm["w_in_even"][e], EVEN_SPLITS)
    w_in = jnp.concatenate([ag, bg, mg, mq, bu, _permute_heads(aq, A_HEADS), _permute_heads(ak, A_KV_HEADS), av],
                           axis=1).astype(BF16)
    tn = _col_tile(w_in.shape[1], 1536)
    assert EV_AQ % tn == 0 and w_in.shape[1] - EV_AQ == tn
    slots = [HEAD_DIM ** -0.5] * A_HEADS + [1.0] * A_KV_HEADS + [None] * A_KV_HEADS
    p = normed_matmul(x, prm["norm_gain"][l], w_in, seq=seq, tn=tn, preps={EV_AQ // tn: (0, slots)},
                      head_gains=_head_gains((prm["a_q_norm"][e], A_HEADS), (prm["a_k_norm"][e], A_KV_HEADS)),
                      tables=rope_a)
    ya = window_attention(p, EV_AQ, EV_AK, EV_AV, EV_GATE, prm["a_sink"][e], batch, seq)
    yb = hyena_mixer(p, EV_BU, EV_GATE + A_WIDTH, prm, e, batch, seq)
    kvm = _memory_kv(mem, prm, l)
    ym = memory_attention(p, EV_MQ, EV_GATE + A_WIDTH + B_WIDTH, kvm, prm["mem_q_norm"][l], prm["mem_k_norm"][l],
                          batch, seq)
    return out_projection(ya, yb, ym, x, prm["w_out_even"][e].astype(BF16))


def _odd_layer(x, mem, prm, l, o, batch, seq, rope_a, rope_d):
    cq, ck, cv, cg, dqa, dkva, dg, mq, mg = _split_cols(prm["w_in_odd"][o], ODD_SPLITS)
    pad = jnp.zeros((D_MODEL, OD_WIDTH - (OD_KR + LANES)), F32)
    w_in = jnp.concatenate([cg, dg, mg, mq, _permute_heads(cq, 2 * C_HEADS), _permute_heads(ck, 2 * C_HEADS), cv, dqa,
                            dkva[:, :D_KV_RANK], _rope_slot(dkva[:, D_KV_RANK:]), pad], axis=1).astype(BF16)
    tn = _col_tile(w_in.shape[1], 1024)
    assert OD_CQ % tn == 0 and OD_CK - OD_CQ == tn and OD_CV - OD_CK == tn
    nslot = 2 * C_HEADS
    p = normed_matmul(x, prm["norm_gain"][l], w_in, seq=seq, tn=tn,
                      preps={OD_CQ // tn: (0, [C_QK_DIM ** -0.5 * LOG2E] * nslot), OD_CK // tn: (nslot, [1.0] * nslot)},
                      head_gains=_head_gains((prm["c_q_norm"][o], nslot), (prm["c_k_norm"][o], nslot)),
                      tables=rope_a)
    lam_init = 0.8 - 0.6 * math.exp(-0.3 * l)
    yc = diff_attention(p, OD_CQ, OD_CK, OD_CV, OD_GATE, prm["c_lambda"][o], prm["c_out_norm"][o], lam_init, batch, seq)
    wq = prm["w_q_b"][o].reshape(D_Q_RANK, D_HEADS, D_NOPE + D_ROPE)
    wq2 = jnp.concatenate([wq[:, :, :D_NOPE].reshape(D_Q_RANK, -1), _rope_slot(wq[:, :, D_NOPE:]).reshape(D_Q_RANK, -1)],
                          axis=1)
    wkv = prm["w_kv_b"][o].reshape(D_KV_RANK, D_HEADS, D_NOPE + D_V)
    wkv2 = jnp.concatenate([wkv[:, :, :D_NOPE].reshape(D_KV_RANK, -1), wkv[:, :, D_NOPE:].reshape(D_KV_RANK, -1)],
                           axis=1)
    dscale = (D_NOPE + D_ROPE) ** -0.5
    qd, = mla_projection(p, OD_DQA, D_Q_RANK, prm["d_q_a_norm"][o], wq2.astype(BF16), prm["d_q_norm"][o], rope_d,
                         dscale * LOG2E, seq)
    kd, vd = mla_projection(p, OD_CKV, D_KV_RANK, prm["d_kv_a_norm"][o], wkv2.astype(BF16), prm["d_k_norm"][o], rope_d,
                            1.0, seq, rope_col=OD_KR)
    yd = mla_attention(qd, kd, vd, p, OD_GATE + C_WIDTH, batch, seq)
    kvm = _memory_kv(mem, prm, l)
    ym = memory_attention(p, OD_MQ, OD_GATE + C_WIDTH + D_WIDTH, kvm, prm["mem_q_norm"][l], prm["mem_k_norm"][l],
                          batch, seq)
    return out_projection(yc, yd, ym, x, prm["w_out_odd"][o].astype(BF16))


def _trunk(x, mem, prm):
    batch, seq, d = x.shape
    depth = prm["norm_gain"].shape[0]
    rope_a = _rope_tables(seq, ROT_DIM, True)
    rope_d = _rope_tables(seq, D_ROPE, False)
    h = x.reshape(batch * seq, d)
    for l in range(depth):
        if l % 2 == 0:
            h = _even_layer(h, mem, prm, l, l // 2, batch, seq, rope_a)
        else:
            h = _odd_layer(h, mem, prm, l, l // 2, batch, seq, rope_a, rope_d)
    return h.reshape(batch, seq, d)


def kernel(x_prompt, x_sample, mem_prompt, mem_sample, norm_gain, mem_norm_gain, w_mem_kv, mem_q_norm, mem_k_norm, w_in_even, w_out_even, a_q_norm, a_k_norm, a_sink, hy_short_w, hy_short_b, hy_w1, hy_b1, hy_freq, hy_w2, hy_b2, hy_w3, hy_decay, hy_bias, w_in_odd, w_out_odd, c_q_norm, c_k_norm, c_lambda, c_out_norm, d_q_a_norm, w_q_b, d_kv_a_norm, w_kv_b, d_q_norm, d_k_norm):
    prm = dict(norm_gain=norm_gain, mem_norm_gain=mem_norm_gain, w_mem_kv=w_mem_kv,
               mem_q_norm=mem_q_norm, mem_k_norm=mem_k_norm, w_in_even=w_in_even,
               w_out_even=w_out_even, a_q_norm=a_q_norm, a_k_norm=a_k_norm, a_sink=a_sink,
               hy_short_w=hy_short_w, hy_short_b=hy_short_b, hy_w1=hy_w1, hy_b1=hy_b1,
               hy_freq=hy_freq, hy_w2=hy_w2, hy_b2=hy_b2, hy_w3=hy_w3, hy_decay=hy_decay,
               hy_bias=hy_bias, w_in_odd=w_in_odd, w_out_odd=w_out_odd, c_q_norm=c_q_norm,
               c_k_norm=c_k_norm, c_lambda=c_lambda, c_out_norm=c_out_norm, d_q_a_norm=d_q_a_norm,
               w_q_b=w_q_b, d_kv_a_norm=d_kv_a_norm, w_kv_b=w_kv_b, d_q_norm=d_q_norm,
               d_k_norm=d_k_norm)
    return (_trunk(x_prompt, mem_prompt, prm), _trunk(x_sample, mem_sample, prm))
```

```python
import functools
import math

import jax
import jax.numpy as jnp
from jax import lax
from jax.experimental import pallas as pl
from jax.experimental.pallas import tpu as pltpu

F32 = jnp.float32
BF16 = jnp.bfloat16

D_MODEL = 2048
N_MEM = 256
HEAD_DIM = 128
ROPE_THETA = 500000.0
ROT_DIM = HEAD_DIM // 4
EPS = 1e-6
BLOCK = 128
WINDOW = 128
NEG_INF = -1e30
A_HEADS = 8
A_KV_HEADS = 2
A_WIDTH = A_HEADS * HEAD_DIM
B_WIDTH = 1024
HY_EMB = 33
HY_FFN = 64
M_HEADS = 4
M_WIDTH = M_HEADS * HEAD_DIM
C_HEADS = 4
C_QK_DIM = 128
C_V_DIM = 256
C_WIDTH = C_HEADS * C_V_DIM
D_HEADS = 8
D_NOPE = 128
D_ROPE = 64
D_V = 128
D_Q_RANK = 512
D_KV_RANK = 256
D_WIDTH = D_HEADS * D_V
MIX_WIDTH = 2560
EVEN_SPLITS = (A_WIDTH, A_KV_HEADS * HEAD_DIM, A_KV_HEADS * HEAD_DIM, A_WIDTH,
               3 * B_WIDTH, B_WIDTH, M_WIDTH, M_WIDTH)
ODD_SPLITS = (C_HEADS * 2 * C_QK_DIM, C_HEADS * 2 * C_QK_DIM, C_WIDTH, C_WIDTH,
              D_Q_RANK, D_KV_RANK + D_ROPE, D_WIDTH, M_WIDTH, M_WIDTH)

LANES = 128
MXU_COLS = 256
DFT_N2 = 128
VMEM_LIMIT = 48 * 1024 * 1024

EV_GATE, EV_MQ, EV_BU, EV_AQ, EV_AK, EV_AV = 0, 2560, 3072, 6144, 7168, 7424
OD_GATE, OD_MQ, OD_CQ, OD_CK, OD_CV, OD_DQA, OD_CKV, OD_KR = 0, 2560, 3072, 4096, 5120, 6144, 6656, 6912
OD_WIDTH = 7168


def _tile(n, pref):
    t = min(n, pref)
    assert n % t == 0, (n, t)
    return t


def _col_tile(n, cap):
    best = None
    for t in range(MXU_COLS, min(n, cap) + 1, MXU_COLS):
        if n % t == 0:
            best = t
    assert best is not None, (n, cap)
    return best


def _params(sem):
    return pltpu.CompilerParams(dimension_semantics=sem, vmem_limit_bytes=VMEM_LIMIT)


def _rms(x, gain):
    ms = jnp.mean(x * x, axis=-1, keepdims=True)
    return x * lax.rsqrt(ms + EPS) * gain


def _silu(g):
    g = g.astype(F32)
    return g * jax.nn.sigmoid(g)


def _lanes(x, k):
    return x if k == 1 else jnp.concatenate([x] * k, axis=1)


def _rope(y, cf, sn):
    return y * cf + pltpu.roll(y, LANES // 2, 1) * sn


def _nmm_kernel(x_ref, g_ref, w_ref, hg_ref, cf_ref, sn_ref, o_ref, h_ref, *, preps):
    j = pl.program_id(1)

    @pl.when(j == 0)
    def _():
        h_ref[...] = _rms(x_ref[...].astype(F32), g_ref[...]).astype(BF16)

    def product():
        return jnp.dot(h_ref[...], w_ref[...], preferred_element_type=F32)

    plain = None
    for blk in preps:
        plain = (j != blk) if plain is None else jnp.logical_and(plain, j != blk)

    if plain is None:
        o_ref[...] = product().astype(o_ref.dtype)
        return

    @pl.when(plain)
    def _():
        o_ref[...] = product().astype(o_ref.dtype)

    for blk, (gain_row0, scales) in preps.items():
        @pl.when(j == blk)
        def _(gain_row0=gain_row0, scales=scales):
            acc = product()
            cf, sn = cf_ref[...], sn_ref[...]
            for slot, scale in enumerate(scales):
                cols = slice(slot * LANES, (slot + 1) * LANES)
                y = acc[:, cols]
                if scale is not None:
                    y = _rope(_rms(y, hg_ref[gain_row0 + slot:gain_row0 + slot + 1, :]), cf, sn) * scale
                o_ref[:, cols] = y.astype(o_ref.dtype)


def normed_matmul(x, gain, w, *, seq, col_block=0, tm=1024, tn=1536, preps=None, head_gains=None, tables=None):
    rows = x.shape[0]
    k, n = w.shape
    tm = _tile(seq, tm)
    tn = _col_tile(n, tn)
    if preps is None:
        preps, head_gains = {}, jnp.ones((8, LANES), F32)
        tables = (jnp.ones((tm, LANES), F32), jnp.zeros((tm, LANES), F32))
    tab_spec = pl.BlockSpec((tm, LANES), lambda i, j: (i % (tables[0].shape[0] // tm), 0))
    return pl.pallas_call(
        functools.partial(_nmm_kernel, preps=preps),
        grid=(rows // tm, n // tn),
        in_specs=[pl.BlockSpec((tm, k), lambda i, j: (i, col_block)),
                  pl.BlockSpec((1, k), lambda i, j: (0, 0)),
                  pl.BlockSpec((k, tn), lambda i, j: (0, j)),
                  pl.BlockSpec(head_gains.shape, lambda i, j: (0, 0)),
                  tab_spec, tab_spec],
        out_specs=pl.BlockSpec((tm, tn), lambda i, j: (i, j)),
        out_shape=jax.ShapeDtypeStruct((rows, n), BF16),
        scratch_shapes=[pltpu.VMEM((tm, k), BF16)],
        compiler_params=_params(("parallel", "arbitrary")),
        name="normed_matmul",
    )(x, gain.reshape(1, k).astype(F32), w, head_gains.astype(F32), *tables)


WIN_QB = 4


def _window_kernel(sink_ref, q_ref, kp_ref, kc_ref, kn_ref, vp_ref, vc_ref, vn_ref, g_ref, o_ref, *, seq, qb):
    n = pl.program_id(1)
    group = A_HEADS // A_KV_HEADS
    rows = group * BLOCK
    qi = lax.broadcasted_iota(jnp.int32, (rows, 3 * BLOCK), 0) & (BLOCK - 1)
    kj = lax.broadcasted_iota(jnp.int32, (rows, 3 * BLOCK), 1)
    in_win = jnp.abs(kj - BLOCK - qi) <= WINDOW
    for kv in range(A_KV_HEADS):
        cols = slice(kv * HEAD_DIM, (kv + 1) * HEAD_DIM)
        heads = range(kv * group, (kv + 1) * group)
        kfull = jnp.concatenate([kp_ref[:, cols], kc_ref[:, cols], kn_ref[:, cols]], axis=0).astype(BF16)
        vfull = jnp.concatenate([vp_ref[:, cols], vc_ref[:, cols], vn_ref[:, cols]], axis=0).astype(BF16)
        sk = jnp.concatenate([jnp.full((BLOCK, 1), sink_ref[h], F32) for h in heads], axis=0)
        for i in range(qb):
            qrows = slice(i * BLOCK, (i + 1) * BLOCK)
            q = jnp.concatenate([q_ref[qrows, h * HEAD_DIM:(h + 1) * HEAD_DIM] for h in heads], axis=0)
            s = lax.dot_general(q, kfull[i * BLOCK:(i + 3) * BLOCK], (((1,), (1,)), ((), ())),
                                preferred_element_type=F32)
            kpos = (n * qb + i - 1) * BLOCK + kj
            s = jnp.where(in_win & (kpos >= 0) & (kpos < seq), s, NEG_INF)
            m = jnp.maximum(jnp.max(s, axis=-1, keepdims=True), sk)
            pr = jnp.exp(s - m)
            denom = jnp.sum(pr, axis=-1, keepdims=True) + jnp.exp(sk - m)
            pr = pr / denom
            o = jnp.dot(pr.astype(BF16), vfull[i * BLOCK:(i + 3) * BLOCK], preferred_element_type=F32)
            for g, h in enumerate(heads):
                hc = slice(h * HEAD_DIM, (h + 1) * HEAD_DIM)
                o_ref[qrows, hc] = (o[g * BLOCK:(g + 1) * BLOCK] * _silu(g_ref[qrows, hc])).astype(o_ref.dtype)


def window_attention(p, q_col, k_col, v_col, g_col, sink, batch, seq):
    nblk = seq // BLOCK
    qb = _tile(nblk, WIN_QB)
    nstep = nblk // qb
    kvw = A_KV_HEADS * HEAD_DIM
    kb, vb = k_col // kvw, v_col // kvw

    def prev(b, n):
        return b * nblk + jnp.maximum(n * qb - 1, 0)

    def cur(b, n):
        return b * nstep + n

    def nxt(b, n):
        return b * nblk + jnp.minimum((n + 1) * qb, nblk - 1)

    return pl.pallas_call(
        functools.partial(_window_kernel, seq=seq, qb=qb),
        grid=(batch, nstep),
        in_specs=[pl.BlockSpec(memory_space=pltpu.SMEM),
                  pl.BlockSpec((qb * BLOCK, A_WIDTH), lambda b, n: (cur(b, n), q_col // A_WIDTH)),
                  pl.BlockSpec((BLOCK, kvw), lambda b, n: (prev(b, n), kb)),
                  pl.BlockSpec((qb * BLOCK, kvw), lambda b, n: (cur(b, n), kb)),
                  pl.BlockSpec((BLOCK, kvw), lambda b, n: (nxt(b, n), kb)),
                  pl.BlockSpec((BLOCK, kvw), lambda b, n: (prev(b, n), vb)),
                  pl.BlockSpec((qb * BLOCK, kvw), lambda b, n: (cur(b, n), vb)),
                  pl.BlockSpec((BLOCK, kvw), lambda b, n: (nxt(b, n), vb)),
                  pl.BlockSpec((qb * BLOCK, A_WIDTH), lambda b, n: (cur(b, n), g_col // A_WIDTH))],
        out_specs=pl.BlockSpec((qb * BLOCK, A_WIDTH), lambda b, n: (cur(b, n), 0)),
        out_shape=jax.ShapeDtypeStruct((batch * seq, A_WIDTH), BF16),
        compiler_params=_params(("parallel", "parallel")),
        name="window_attention",
    )(sink.astype(F32), p, p, p, p, p, p, p, p)


def _mem_kernel(q_ref, kv_ref, qg_ref, kg_ref, g_ref, o_ref):
    qg, kg = qg_ref[...], kg_ref[...]
    scale = HEAD_DIM ** -0.5
    for h in range(M_HEADS):
        cols = slice(h * HEAD_DIM, (h + 1) * HEAD_DIM)
        q = (_rms(q_ref[:, cols].astype(F32), qg) * scale).astype(BF16)
        k = _rms(kv_ref[:, cols].astype(F32), kg).astype(BF16)
        v = kv_ref[:, M_WIDTH + h * HEAD_DIM:M_WIDTH + (h + 1) * HEAD_DIM].astype(BF16)
        s = lax.dot_general(q, k, (((1,), (1,)), ((), ())), preferred_element_type=F32)
        m = jnp.max(s, axis=-1, keepdims=True)
        pr = jnp.exp(s - m)
        pr = pr / jnp.sum(pr, axis=-1, keepdims=True)
        o = jnp.dot(pr.astype(BF16), v, preferred_element_type=F32)
        o_ref[:, cols] = (o * _silu(g_ref[:, cols])).astype(o_ref.dtype)


def memory_attention(p, q_col, g_col, kvm, q_gain, k_gain, batch, seq, *, tq=512):
    tq = _tile(seq, tq)
    nq = seq // tq
    return pl.pallas_call(
        _mem_kernel,
        grid=(batch, nq),
        in_specs=[pl.BlockSpec((tq, M_WIDTH), lambda b, i: (b * nq + i, q_col // M_WIDTH)),
                  pl.BlockSpec((N_MEM, 2 * M_WIDTH), lambda b, i: (b, 0)),
                  pl.BlockSpec((1, HEAD_DIM), lambda b, i: (0, 0)),
                  pl.BlockSpec((1, HEAD_DIM), lambda b, i: (0, 0)),
                  pl.BlockSpec((tq, M_WIDTH), lambda b, i: (b * nq + i, g_col // M_WIDTH))],
        out_specs=pl.BlockSpec((tq, M_WIDTH), lambda b, i: (b * nq + i, 0)),
        out_shape=jax.ShapeDtypeStruct((batch * seq, M_WIDTH), BF16),
        compiler_params=_params(("parallel", "parallel")),
        name="memory_attention",
    )(p, kvm, q_gain.reshape(1, HEAD_DIM).astype(F32), k_gain.reshape(1, HEAD_DIM).astype(F32), p)


LOG2E = math.log2(math.e)


def _chunk_rows(chunk, tk):
    return pl.ds(pl.multiple_of(chunk * tk, tk), tk)


def _pipelined_chunks(scores, consume, nk):
    scores(0, 0)

    def body(i, carry):
        scores(2 * i + 1, 1)
        consume(2 * i, 0)
        scores(jnp.minimum(2 * i + 2, nk - 1), 0)
        consume(2 * i + 1, 1)
        return carry

    lax.fori_loop(0, nk // 2, body, 0)


def _diff_flash_kernel(q_ref, k_ref, v_ref, lam_ref, g_ref, gate_ref, o_ref, m_ref, l_ref, acc_ref, s_ref, *,
                       tk, nk, lam_init):
    m_ref[...] = jnp.full(m_ref.shape, NEG_INF, F32)
    l_ref[...] = jnp.zeros(l_ref.shape, F32)
    acc_ref[...] = jnp.zeros(acc_ref.shape, F32)

    def scores(chunk, c):
        cols = slice(c * C_QK_DIM, (c + 1) * C_QK_DIM)
        s_ref[c] = lax.dot_general(q_ref[:, cols], k_ref[_chunk_rows(chunk, tk), cols], (((1,), (1,)), ((), ())),
                                   preferred_element_type=F32)

    def consume(chunk, c):
        s = s_ref[c]
        m_prev = m_ref[c]
        m_new = jnp.maximum(m_prev, jnp.max(s, axis=-1, keepdims=True))
        alpha = jnp.exp2(m_prev - m_new)
        pr = jnp.exp2(s - _lanes(m_new, tk // LANES))
        l_ref[c] = alpha * l_ref[c] + jnp.sum(pr, axis=-1, keepdims=True)
        acc_ref[c] = (_lanes(alpha, C_V_DIM // LANES) * acc_ref[c]
                      + jnp.dot(pr.astype(BF16), v_ref[_chunk_rows(chunk, tk), :], preferred_element_type=F32))
        m_ref[c] = m_new

    scores(0, 0)

    def body(i, carry):
        for j in (2 * i, 2 * i + 1):
            scores(j, 1)
            consume(j, 0)
            scores(jnp.minimum(j + 1, nk - 1), 0)
            consume(j, 1)
        return carry

    lax.fori_loop(0, nk // 2, body, 0)
    lv = lam_ref[...]
    lam = (jnp.exp(jnp.sum(lv[0:1] * lv[1:2], axis=-1, keepdims=True))
           - jnp.exp(jnp.sum(lv[2:3] * lv[3:4], axis=-1, keepdims=True)) + lam_init)
    o1 = acc_ref[0] / _lanes(l_ref[0], C_V_DIM // LANES)
    o2 = acc_ref[1] / _lanes(l_ref[1], C_V_DIM // LANES)
    o_ref[...] = (_rms(o1 - lam * o2, g_ref[...]) * (1.0 - lam_init) * _silu(gate_ref[...])).astype(o_ref.dtype)


def _mla_flash_kernel(q_ref, k_ref, v_ref, gate_ref, o_ref, m_ref, acc_ref, s_ref, *, tk, nk):
    m_ref[...] = jnp.full(m_ref.shape, NEG_INF, F32)
    acc_ref[...] = jnp.zeros(acc_ref.shape, F32)
    ones = jnp.ones((tk, LANES), BF16)

    def scores(chunk, slot):
        s_ref[slot] = lax.dot_general(q_ref[...], k_ref[_chunk_rows(chunk, tk), :], (((1,), (1,)), ((), ())),
                                      preferred_element_type=F32)

    def consume(chunk, slot):
        s = s_ref[slot]
        m_prev = m_ref[...]
        m_new = jnp.maximum(m_prev, jnp.max(s, axis=-1, keepdims=True))
        alpha = jnp.exp2(m_prev - m_new)
        pr = jnp.exp2(s - _lanes(m_new, tk // LANES))
        vv = jnp.concatenate([v_ref[_chunk_rows(chunk, tk), :], ones], axis=1)
        acc_ref[...] = (_lanes(alpha, 2) * acc_ref[...]
                        + jnp.dot(pr.astype(BF16), vv, preferred_element_type=F32))
        m_ref[...] = m_new

    _pipelined_chunks(scores, consume, nk)
    o_ref[...] = (acc_ref[:, :D_V] / acc_ref[:, D_V:] * _silu(gate_ref[...])).astype(o_ref.dtype)


def _flash_call(kernel, q, q_block0, k, k_block0, v, v_block0, gate, gate_block0, extra, extra_specs, scratch, *,
                heads, dv, batch, seq, tq, tk, name):
    nq, nk = seq // tq, seq // tk
    qk_w = 2 * LANES
    return pl.pallas_call(
        functools.partial(kernel, tk=tk, nk=nk),
        grid=(batch, heads, nq),
        in_specs=[pl.BlockSpec((tq, qk_w), lambda b, h, i: (b * nq + i, q_block0 + h)),
                  pl.BlockSpec((seq, qk_w), lambda b, h, i: (b, k_block0 + h), pipeline_mode=pl.Buffered(1)),
                  pl.BlockSpec((seq, dv), lambda b, h, i: (b, v_block0 + h), pipeline_mode=pl.Buffered(1))]
        + extra_specs + [pl.BlockSpec((tq, dv), lambda b, h, i: (b * nq + i, gate_block0 + h))],
        out_specs=pl.BlockSpec((tq, dv), lambda b, h, i: (b * nq + i, h)),
        out_shape=jax.ShapeDtypeStruct((batch * seq, heads * dv), BF16),
        scratch_shapes=scratch,
        compiler_params=_params(("parallel", "parallel", "arbitrary")),
        name=name,
    )(q, k, v, *extra, gate)


def diff_attention(p, q_col, k_col, v_col, g_col, c_lambda, out_gain, lam_init, batch, seq, *, tq=1024, tk=512):
    tq, tk = _tile(seq, tq), _tile(seq // 2, tk)
    const = lambda b, h, i: (0, 0)
    return _flash_call(
        functools.partial(_diff_flash_kernel, lam_init=lam_init), p, q_col // C_V_DIM, p, k_col // C_V_DIM,
        p, v_col // C_V_DIM, p, g_col // C_V_DIM,
        [c_lambda.astype(F32), out_gain.reshape(1, C_V_DIM).astype(F32)],
        [pl.BlockSpec((4, C_QK_DIM), const), pl.BlockSpec((1, C_V_DIM), const)],
        [pltpu.VMEM((2, tq, LANES), F32), pltpu.VMEM((2, tq, LANES), F32), pltpu.VMEM((2, tq, C_V_DIM), F32),
         pltpu.VMEM((2, tq, tk), F32)],
        heads=C_HEADS, dv=C_V_DIM, batch=batch, seq=seq, tq=tq, tk=tk, name="diff_attention")


def mla_attention(q, k, v, p, g_col, batch, seq, *, tq=1024, tk=1024):
    tq, tk = _tile(seq, tq), _tile(seq // 2, tk)
    return _flash_call(_mla_flash_kernel, q, 0, k, 0, v, 0, p, g_col // D_V, [], [],
                       [pltpu.VMEM((tq, LANES), F32), pltpu.VMEM((tq, D_V + LANES), F32),
                        pltpu.VMEM((2, tq, tk), F32)],
                       heads=D_HEADS, dv=D_V, batch=batch, seq=seq, tq=tq, tk=tk, name="mla_attention")


def _mla_proj_kernel(*refs, shared_rope, scale):
    if shared_rope:
        x_ref, g_ref, w_ref, kr_ref, gn_ref, gr_ref, cf_ref, sn_ref, o_ref, v_ref = refs
    else:
        x_ref, g_ref, w_ref, gn_ref, gr_ref, cf_ref, sn_ref, o_ref = refs
    h = _rms(x_ref[...].astype(F32), g_ref[...]).astype(BF16)
    acc = jnp.dot(h, w_ref[...], preferred_element_type=F32)
    gn, gr = gn_ref[...], gr_ref[...]
    cf, sn = cf_ref[...], sn_ref[...]
    width = D_HEADS * LANES
    for hd in range(D_HEADS):
        cols = slice(hd * LANES, (hd + 1) * LANES)
        xn = acc[:, cols]
        xr = kr_ref[...].astype(F32) if shared_rope else acc[:, width + hd * LANES:width + (hd + 1) * LANES]
        ms = (jnp.sum(xn * xn, axis=-1, keepdims=True)
              + jnp.sum(xr * xr, axis=-1, keepdims=True)) * (1.0 / (D_NOPE + D_ROPE))
        inv = lax.rsqrt(ms + EPS)
        o_ref[:, 2 * hd * LANES:(2 * hd + 1) * LANES] = (xn * inv * gn * scale).astype(o_ref.dtype)
        o_ref[:, (2 * hd + 1) * LANES:(2 * hd + 2) * LANES] = (_rope(xr * inv * gr, cf, sn) * scale).astype(o_ref.dtype)
    if shared_rope:
        v_ref[...] = acc[:, width:].astype(v_ref.dtype)


def mla_projection(p, col, rank, gain, w, head_gain, tables, scale, seq, *, rope_col=None, tm=512):
    rows = p.shape[0]
    width = D_HEADS * LANES
    tm = _tile(seq, tm)
    nblk = seq // tm
    shared_rope = rope_col is not None
    gn = head_gain[:D_NOPE].reshape(1, LANES).astype(F32)
    gr = _rope_slot(head_gain[D_NOPE:]).reshape(1, LANES).astype(F32)
    one = pl.BlockSpec((1, LANES), lambda i: (0, 0))
    tab_spec = pl.BlockSpec((tm, LANES), lambda i: (i % nblk, 0))
    in_specs = [pl.BlockSpec((tm, rank), lambda i: (i, col // rank)),
                pl.BlockSpec((1, rank), lambda i: (0, 0)),
                pl.BlockSpec((rank, 2 * width), lambda i: (0, 0))]
    args = [p, gain.reshape(1, rank).astype(F32), w]
    out_specs = [pl.BlockSpec((tm, 2 * width), lambda i: (i, 0))]
    out_shape = [jax.ShapeDtypeStruct((rows, 2 * width), BF16)]
    if shared_rope:
        in_specs.append(pl.BlockSpec((tm, LANES), lambda i: (i, rope_col // LANES)))
        args.append(p)
        out_specs.append(pl.BlockSpec((tm, width), lambda i: (i, 0)))
        out_shape.append(jax.ShapeDtypeStruct((rows, width), BF16))
    return pl.pallas_call(
        functools.partial(_mla_proj_kernel, shared_rope=shared_rope, scale=scale),
        grid=(rows // tm,),
        in_specs=in_specs + [one, one, tab_spec, tab_spec],
        out_specs=out_specs,
        out_shape=out_shape,
        compiler_params=_params(("parallel",)),
        name="mla_projection",
    )(*args, gn, gr, *tables)


SUB = 8
FILT_ROWS = SUB * DFT_N2
HALO = 16


def _half_rows(n1):
    return n1 // 2 + SUB


def _hyena_gate_kernel(*refs, nblk, tm):
    u_refs, up_refs, un_refs = refs[0:3], refs[3:6], refs[6:9]
    w_ref, b_ref, g_ref, zz_ref, x0_ref = refs[9:]
    li = pl.program_id(1)
    tc = zz_ref.shape[-1]
    row = lax.broadcasted_iota(jnp.int32, (tm, tc), 0)
    has_prev = (li > 0).astype(F32)
    has_next = (li < nblk - 1).astype(F32)

    def conv(part):
        u = u_refs[part][...].astype(F32)
        prev_row = up_refs[part][HALO - 1:HALO, :].astype(F32) * has_prev
        next_row = un_refs[part][0:1, :].astype(F32) * has_next
        above = jnp.where(row == 0, prev_row, pltpu.roll(u, 1, 0))
        below = jnp.where(row == tm - 1, next_row, pltpu.roll(u, tm - 1, 0))
        w = w_ref[part]
        return above * w[0:1] + u * w[1:2] + below * w[2:3] + b_ref[part]

    x0 = conv(0) * _silu(g_ref[...])
    zz = conv(2) * conv(1)
    for a in range(tm // DFT_N2):
        rows = slice(a * DFT_N2, (a + 1) * DFT_N2)
        zz_ref[:, a, :] = zz[rows]
        x0_ref[:, a, :] = x0[rows]


def hyena_gate(p, u_col, g_col, short_w, short_b, batch, seq, *, tc=512):
    tm = FILT_ROWS
    nblk = seq // tm
    half = seq // DFT_N2
    tc = _tile(B_WIDTH, tc)
    nc = B_WIDTH // tc
    hb = tm // HALO
    last_halo = batch * nblk * hb - 1

    def part_specs(part):
        cb = (u_col + part * B_WIDTH) // tc
        return (pl.BlockSpec((tm, tc), lambda b, i, c: (b * nblk + i, cb + c)),
                pl.BlockSpec((HALO, tc), lambda b, i, c: (jnp.maximum((b * nblk + i) * hb - 1, 0), cb + c)),
                pl.BlockSpec((HALO, tc), lambda b, i, c: (jnp.minimum((b * nblk + i + 1) * hb, last_halo), cb + c)))

    specs = [part_specs(part) for part in range(3)]
    out = jax.ShapeDtypeStruct((batch, DFT_N2, half, B_WIDTH), F32)
    o_spec = pl.BlockSpec((None, DFT_N2, tm // DFT_N2, tc), lambda b, i, c: (b, 0, i, c))
    w3 = short_w.astype(F32).reshape(3, 3, B_WIDTH).transpose(1, 0, 2)
    b3 = short_b.astype(F32).reshape(3, 1, B_WIDTH)
    return pl.pallas_call(
        functools.partial(_hyena_gate_kernel, nblk=nblk, tm=tm),
        grid=(batch, nblk, nc),
        in_specs=[sp[0] for sp in specs] + [sp[1] for sp in specs] + [sp[2] for sp in specs]
        + [pl.BlockSpec((3, 3, tc), lambda b, i, c: (0, 0, c)),
           pl.BlockSpec((3, 1, tc), lambda b, i, c: (0, 0, c)),
           pl.BlockSpec((tm, tc), lambda b, i, c: (b * nblk + i, g_col // tc + c))],
        out_specs=[o_spec, o_spec],
        out_shape=[out, out],
        compiler_params=_params(("parallel", "parallel", "parallel")),
        name="hyena_gate",
    )(*([p] * 9), w3, b3, p)


def _hdot(a, b):
    return jnp.dot(a, b, preferred_element_type=F32, precision=lax.Precision.HIGHEST)


def _filter_kernel(fr_ref, w1_ref, b1_ref, fq_ref, w2_ref, b2_ref, w3_ref, dec_ref, k_ref, s_ref, t_ref, h_ref, *,
                   seq, tr):
    i, c = pl.program_id(0), pl.program_id(1)
    r = i * tr + lax.broadcasted_iota(jnp.int32, (tr, 1), 0)

    @pl.when(c == 0)
    def _():
        pos = jnp.where(r < seq, r, 2 * seq - 1 - r).astype(F32)
        t = pos * (1.0 / (seq - 1))
        w = pos * (2.0 * math.pi / seq)
        lane = lax.broadcasted_iota(jnp.int32, (tr, LANES), 1)
        bands = (HY_EMB - 1) // 2
        ang = w * fr_ref[...]
        z = jnp.where(lane == 0, t,
                      jnp.where(lane <= bands, jnp.cos(ang),
                                jnp.where(lane <= 2 * bands, -jnp.sin(ang), 0.0)))
        h = jnp.sin(fq_ref[0:1] * (_hdot(z, w1_ref[...]) + b1_ref[...]))
        h_ref[...] = jnp.sin(fq_ref[1:2] * (_hdot(h, w2_ref[...]) + b2_ref[...]))
        t_ref[...] = jnp.broadcast_to(t, t_ref.shape)

    h = _hdot(h_ref[...], w3_ref[...]) * jnp.exp(-t_ref[:, 0:1] * jnp.abs(dec_ref[...]))
    h = jnp.where(r == seq, 0.0, h)
    for a in range(tr // DFT_N2):
        k_ref[:, a, :] = h[a * DFT_N2:(a + 1) * DFT_N2]

    @pl.when(jnp.logical_and(i == 0, c == 0))
    def _():
        s_ref[...] = jnp.zeros(s_ref.shape, F32)

    tc = h.shape[1]
    col_sum = jnp.sum(jnp.abs(h), axis=0, keepdims=True)
    for cc in range(s_ref.shape[1] // tc):
        @pl.when(c == cc)
        def _(cc=cc):
            s_ref[:, cc * tc:(cc + 1) * tc] += col_sum


def hyena_filter(seq, w1, b1, freq, w2, b2, w3, decay, *, tc=512):
    tr = FILT_ROWS
    assert seq % tr == 0
    nhalf = seq // tr
    n1 = 2 * seq // DFT_N2
    tc = _tile(B_WIDTH, tc)
    nc = B_WIDTH // tc
    bands = (HY_EMB - 1) // 2
    fr = jnp.linspace(1e-4, bands - 1, bands, dtype=F32)
    fr_lanes = jnp.concatenate([jnp.zeros((1,), F32), fr, fr, jnp.zeros((LANES - HY_EMB,), F32)]).reshape(1, LANES)
    w1p = jnp.pad(w1.astype(F32), ((0, LANES - HY_EMB), (0, 0)))
    const = lambda i, c: (0, 0)
    half = lambda i, c: (0, (i // nhalf) * nc + c)
    return pl.pallas_call(
        functools.partial(_filter_kernel, seq=seq, tr=tr),
        grid=(2 * nhalf, nc),
        in_specs=[pl.BlockSpec((1, LANES), const),
                  pl.BlockSpec((LANES, HY_FFN), const),
                  pl.BlockSpec((1, HY_FFN), const),
                  pl.BlockSpec((2, HY_FFN), const),
                  pl.BlockSpec((HY_FFN, HY_FFN), const),
                  pl.BlockSpec((1, HY_FFN), const),
                  pl.BlockSpec((HY_FFN, tc), half),
                  pl.BlockSpec((1, tc), half)],
        out_specs=[pl.BlockSpec((DFT_N2, tr // DFT_N2, tc), lambda i, c: (0, i, c)),
                   pl.BlockSpec((1, B_WIDTH), const)],
        out_shape=[jax.ShapeDtypeStruct((DFT_N2, n1, B_WIDTH), F32),
                   jax.ShapeDtypeStruct((1, B_WIDTH), F32)],
        scratch_shapes=[pltpu.VMEM((tr, LANES), F32), pltpu.VMEM((tr, HY_FFN), F32)],
        compiler_params=_params(("arbitrary", "arbitrary")),
        name="hyena_filter",
    )(fr_lanes, w1p, b1.reshape(1, HY_FFN).astype(F32), freq.astype(F32), w2.astype(F32),
      b2.reshape(1, HY_FFN).astype(F32), w3.astype(F32), decay.reshape(1, 2 * B_WIDTH).astype(F32))


def _bdot(a, b):
    return jnp.dot(a, b, preferred_element_type=F32)


def dft_tables(n1):
    n = n1 * DFT_N2
    rows = _half_rows(n1)
    kk = jnp.arange(rows, dtype=jnp.int32)
    k1 = kk[None, :, None]
    m1 = jnp.arange(n1, dtype=jnp.int32)[None, None, :]
    n2 = jnp.arange(DFT_N2, dtype=jnp.int32)[:, None, None]
    ang = ((k1 * (DFT_N2 * m1 + n2)) % n).astype(F32) * (2.0 * math.pi / n)
    live = (kk <= n1 // 2).astype(F32)[None, :, None]
    weight = jnp.where((kk == 0) | (kk == n1 // 2), 1.0, 2.0)[None, :, None] * live
    cos, sin = jnp.cos(ang), jnp.sin(ang)
    gc = (cos * live).astype(BF16)
    gsn = (-sin * live).astype(BF16)
    hc = jnp.swapaxes(cos * weight, 1, 2)[:, :n1 // 2].astype(BF16)
    hsn = jnp.swapaxes(-sin * weight, 1, 2)[:, :n1 // 2].astype(BF16)
    a = jnp.arange(DFT_N2, dtype=jnp.int32)
    ang2 = ((a[:, None] * a[None, :]) % DFT_N2).astype(F32) * (2.0 * math.pi / DFT_N2)
    ff = jnp.concatenate([jnp.cos(ang2), jnp.sin(ang2)], axis=0).astype(BF16)
    return gc, gsn, hc, hsn, ff


def _dft1_kernel(x_ref, gc_ref, gs_ref, ar_ref, ai_ref):
    for j in range(SUB):
        x = x_ref[j].astype(BF16)
        ar_ref[:, j, :] = _bdot(gc_ref[j], x)
        ai_ref[:, j, :] = _bdot(gs_ref[j], x)


def dft_stage1(x, gc, gsn, *, tc=512):
    batch, _, k1rows, c = x.shape
    rows = gc.shape[1]
    tc = _tile(c, tc)
    out = jax.ShapeDtypeStruct((batch, rows, DFT_N2, c), F32)
    g_spec = pl.BlockSpec((SUB, rows, k1rows), lambda g, b, ci: (g, 0, 0))
    o_spec = pl.BlockSpec((None, rows, SUB, tc), lambda g, b, ci: (b, 0, g, ci))
    return pl.pallas_call(
        _dft1_kernel,
        grid=(DFT_N2 // SUB, batch, c // tc),
        in_specs=[pl.BlockSpec((None, SUB, k1rows, tc), lambda g, b, ci: (b, g, 0, ci)), g_spec, g_spec],
        out_specs=[o_spec, o_spec],
        out_shape=[out, out],
        compiler_params=_params(("parallel", "parallel", "parallel")),
        name="dft_stage1",
    )(x, gc, gsn)


def _stage2(ff, ar, ai):
    p = _bdot(ff, ar)
    q = _bdot(ff, ai)
    return p[:DFT_N2] + q[DFT_N2:], q[:DFT_N2] - p[DFT_N2:]


def _filter_spectrum_kernel(ar_ref, ai_ref, s_ref, ff_ref, kr_ref, ki_ref, *, n):
    ff = ff_ref[...]
    scale = 1.0 / (s_ref[...] * n)
    for j in range(SUB):
        xr, xi = _stage2(ff, ar_ref[j].astype(BF16), ai_ref[j].astype(BF16))
        kr_ref[j] = xr * scale
        ki_ref[j] = xi * scale


def filter_spectrum(ar, ai, asum, ff, n, *, tc=512):
    rows, _, c = ar.shape
    tc = _tile(c, tc)
    a_spec = pl.BlockSpec((SUB, DFT_N2, tc), lambda k, ci: (k, 0, ci))
    out = jax.ShapeDtypeStruct((rows, DFT_N2, c), F32)
    return pl.pallas_call(
        functools.partial(_filter_spectrum_kernel, n=n),
        grid=(rows // SUB, c // tc),
        in_specs=[a_spec, a_spec, pl.BlockSpec((1, tc), lambda k, ci: (0, ci)),
                  pl.BlockSpec((2 * DFT_N2, DFT_N2), lambda k, ci: (0, 0))],
        out_specs=[a_spec, a_spec],
        out_shape=[out, out],
        compiler_params=_params(("parallel", "parallel")),
        name="filter_spectrum",
    )(ar, ai, asum, ff)


def _spectral_kernel(ar_ref, ai_ref, kr_ref, ki_ref, ff_ref, br_ref, bi_ref):
    ff = ff_ref[...]
    for j in range(SUB):
        xr, xi = _stage2(ff, ar_ref[j].astype(BF16), ai_ref[j].astype(BF16))
        kr, ki = kr_ref[j], ki_ref[j]
        yr = (xr * kr - xi * ki).astype(BF16)
        yi = (xr * ki + xi * kr).astype(BF16)
        r = _bdot(ff, yr)
        s = _bdot(ff, yi)
        br_ref[:, j, :] = r[:DFT_N2] - s[DFT_N2:]
        bi_ref[:, j, :] = s[:DFT_N2] + r[DFT_N2:]


def spectral_multiply(ar, ai, kr, ki, ff, *, tc=512):
    batch, rows, _, c = ar.shape
    tc = _tile(c, tc)
    a_spec = pl.BlockSpec((None, SUB, DFT_N2, tc), lambda k, ci, b: (b, k, 0, ci))
    k_spec = pl.BlockSpec((SUB, DFT_N2, tc), lambda k, ci, b: (k, 0, ci))
    o_spec = pl.BlockSpec((None, DFT_N2, SUB, tc), lambda k, ci, b: (b, 0, k, ci))
    out = jax.ShapeDtypeStruct((batch, DFT_N2, rows, c), F32)
    return pl.pallas_call(
        _spectral_kernel,
        grid=(rows // SUB, c // tc, batch),
        in_specs=[a_spec, a_spec, k_spec, k_spec, pl.BlockSpec((2 * DFT_N2, DFT_N2), lambda k, ci, b: (0, 0))],
        out_specs=[o_spec, o_spec],
        out_shape=[out, out],
        compiler_params=_params(("parallel", "parallel", "parallel")),
        name="spectral_multiply",
    )(ar, ai, kr, ki, ff)


def _idft_out_kernel(br_ref, bi_ref, hc_ref, hs_ref, zz_ref, x0_ref, bias_ref, o_ref):
    bias = bias_ref[...]
    for j in range(SUB):
        y = _bdot(hc_ref[j], br_ref[j].astype(BF16)) + _bdot(hs_ref[j], bi_ref[j].astype(BF16))
        o_ref[:, j, :] = (y + zz_ref[j] * bias) * x0_ref[j]


def idft_output(br, bi, hc, hsn, zz, x0, bias, *, tc=512):
    batch, _, rows, c = br.shape
    half = hc.shape[1]
    tc = _tile(c, tc)
    b_spec = pl.BlockSpec((None, SUB, rows, tc), lambda g, b, ci: (b, g, 0, ci))
    h_spec = pl.BlockSpec((SUB, half, rows), lambda g, b, ci: (g, 0, 0))
    z_spec = pl.BlockSpec((None, SUB, half, tc), lambda g, b, ci: (b, g, 0, ci))
    return pl.pallas_call(
        _idft_out_kernel,
        grid=(DFT_N2 // SUB, batch, c // tc),
        in_specs=[b_spec, b_spec, h_spec, h_spec, z_spec, z_spec, pl.BlockSpec((1, tc), lambda g, b, ci: (0, ci))],
        out_specs=pl.BlockSpec((half, SUB, tc), lambda g, b, ci: (b, g, ci)),
        out_shape=jax.ShapeDtypeStruct((batch * half, DFT_N2, c), F32),
        compiler_params=_params(("parallel", "parallel", "parallel")),
        name="idft_output",
    )(br, bi, hc, hsn, zz, x0, bias)


def hyena_mixer(p, u_col, g_col, prm, e, batch, seq):
    n1 = 2 * seq // DFT_N2
    gc, gsn, hc, hsn, ff = dft_tables(n1)
    kern, asum = hyena_filter(seq, prm["hy_w1"][e], prm["hy_b1"][e], prm["hy_freq"][e], prm["hy_w2"][e],
                              prm["hy_b2"][e], prm["hy_w3"][e], prm["hy_decay"][e])
    far, fai = dft_stage1(kern[None], gc, gsn)
    kr, ki = filter_spectrum(far[0], fai[0], asum, ff, n1 * DFT_N2)
    zz, x0 = hyena_gate(p, u_col, g_col, prm["hy_short_w"][e], prm["hy_short_b"][e], batch, seq)
    ar, ai = dft_stage1(zz, gc[:, :, :n1 // 2], gsn[:, :, :n1 // 2])
    br, bi = spectral_multiply(ar, ai, kr, ki, ff)
    yb = idft_output(br, bi, hc, hsn, zz, x0, prm["hy_bias"][e].reshape(1, B_WIDTH).astype(F32))
    return yb.reshape(batch * seq, B_WIDTH)


def _outproj_kernel(y1_ref, y2_ref, y3_ref, x_ref, w_ref, o_ref):
    acc = x_ref[...]
    off = 0
    for y_ref in (y1_ref, y2_ref, y3_ref):
        wd = y_ref.shape[1]
        acc = acc + jnp.dot(y_ref[...].astype(BF16), w_ref[off:off + wd, :], preferred_element_type=F32)
        off += wd
    o_ref[...] = acc


def out_projection(y1, y2, y3, x, w, *, tm=512, tn=1024):
    rows, d = x.shape
    tm = _tile(rows, tm)
    tn = _col_tile(d, tn)
    row = lambda i, j: (i, 0)
    return pl.pallas_call(
        _outproj_kernel,
        grid=(rows // tm, d // tn),
        in_specs=[pl.BlockSpec((tm, y1.shape[1]), row),
                  pl.BlockSpec((tm, y2.shape[1]), row),
                  pl.BlockSpec((tm, y3.shape[1]), row),
                  pl.BlockSpec((tm, tn), lambda i, j: (i, j)),
                  pl.BlockSpec((MIX_WIDTH, tn), lambda i, j: (0, j))],
        out_specs=pl.BlockSpec((tm, tn), lambda i, j: (i, j)),
        out_shape=jax.ShapeDtypeStruct((rows, d), F32),
        compiler_params=_params(("parallel", "parallel")),
        name="out_projection",
    )(y1, y2, y3, x, w)


def _split_cols(w, sizes):
    out, off = [], 0
    for s in sizes:
        out.append(w[:, off:off + s])
        off += s
    return out


ROT_HALF = ROT_DIM // 2
HEAD_PERM = (tuple(range(ROT_HALF)) + tuple(range(ROT_DIM, LANES // 2 + ROT_HALF))
             + tuple(range(ROT_HALF, ROT_DIM)) + tuple(range(LANES // 2 + ROT_HALF, LANES)))


def _permute_heads(w, nheads):
    k = w.shape[0]
    return w.reshape(k, nheads, HEAD_DIM)[:, :, jnp.array(HEAD_PERM)].reshape(k, nheads * HEAD_DIM)


def _rope_slot(w):
    half = D_ROPE // 2
    z = jnp.zeros(w.shape[:-1] + (LANES // 2 - half,), w.dtype)
    return jnp.concatenate([w[..., :half], z, w[..., half:], z], axis=-1)


def _rope_tables(seq, dim, passthrough):
    half = dim // 2
    inv = ROPE_THETA ** (-jnp.arange(0, dim, 2, dtype=F32) / dim)
    ang = jnp.arange(seq, dtype=F32)[:, None] * inv[None, :]
    cos, sin = jnp.cos(ang), jnp.sin(ang)
    rest = jnp.full((seq, LANES // 2 - half), 1.0 if passthrough else 0.0, F32)
    zeros = jnp.zeros((seq, LANES // 2 - half), F32)
    cf = jnp.concatenate([cos, rest, cos, rest], axis=1)
    sn = jnp.concatenate([-sin, zeros, sin, zeros], axis=1)
    return cf, sn


def _memory_kv(mem, prm, l):
    batch = mem.shape[0]
    return normed_matmul(mem.reshape(batch * N_MEM, D_MODEL), prm["mem_norm_gain"][l],
                         prm["w_mem_kv"][l].astype(BF16), seq=N_MEM)


def _head_gains(*groups):
    rows = [jnp.broadcast_to(g[jnp.array(HEAD_PERM)][None, :], (n, HEAD_DIM)) for g, n in groups]
    return jnp.concatenate(rows, axis=0)


def _even_layer(x, mem, prm, l, e, batch, seq, rope_a):
    aq, ak, av, ag, bu, bg, mq, mg = _split_cols(prm["w_in_even"][e], EVEN_SPLITS)
    w_in = jnp.concatenate([ag, bg, mg, mq, bu, _permute_heads(aq, A_HEADS), _permute_heads(ak, A_KV_HEADS), av],
                           axis=1).astype(BF16)
    tn = _col_tile(w_in.shape[1], 1536)
    assert EV_AQ % tn == 0 and w_in.shape[1] - EV_AQ == tn
    slots = [HEAD_DIM ** -0.5] * A_HEADS + [1.0] * A_KV_HEADS + [None] * A_KV_HEADS
    p = normed_matmul(x, prm["norm_gain"][l], w_in, seq=seq, tn=tn, preps={EV_AQ // tn: (0, slots)},
                      head_gains=_head_gains((prm["a_q_norm"][e], A_HEADS), (prm["a_k_norm"][e], A_KV_HEADS)),
                      tables=rope_a)
    ya = window_attention(p, EV_AQ, EV_AK, EV_AV, EV_GATE, prm["a_sink"][e], batch, seq)
    yb = hyena_mixer(p, EV_BU, EV_GATE + A_WIDTH, prm, e, batch, seq)
    kvm = _memory_kv(mem, prm, l)
    ym = memory_attention(p, EV_MQ, EV_GATE + A_WIDTH + B_WIDTH, kvm, prm["mem_q_norm"][l], prm["mem_k_norm"][l],
                          batch, seq)
    return out_projection(ya, yb, ym, x, prm["w_out_even"][e].astype(BF16))


def _odd_layer(x, mem, prm, l, o, batch, seq, rope_a, rope_d):
    cq, ck, cv, cg, dqa, dkva, dg, mq, mg = _split_cols(prm["w_in_odd"][o], ODD_SPLITS)
    pad = jnp.zeros((D_MODEL, OD_WIDTH - (OD_KR + LANES)), F32)
    w_in = jnp.concatenate([cg, dg, mg, mq, _permute_heads(cq, 2 * C_HEADS), _permute_heads(ck, 2 * C_HEADS), cv, dqa,
                            dkva[:, :D_KV_RANK], _rope_slot(dkva[:, D_KV_RANK:]), pad], axis=1).astype(BF16)
    tn = _col_tile(w_in.shape[1], 1024)
    assert OD_CQ % tn == 0 and OD_CK - OD_CQ == tn and OD_CV - OD_CK == tn
    nslot = 2 * C_HEADS
    p = normed_matmul(x, prm["norm_gain"][l], w_in, seq=seq, tn=tn,
                      preps={OD_CQ // tn: (0, [C_QK_DIM ** -0.5 * LOG2E] * nslot), OD_CK // tn: (nslot, [1.0] * nslot)},
                      head_gains=_head_gains((prm["c_q_norm"][o], nslot), (prm["c_k_norm"][o], nslot)),
                      tables=rope_a)
    lam_init = 0.8 - 0.6 * math.exp(-0.3 * l)
    yc = diff_attention(p, OD_CQ, OD_CK, OD_CV, OD_GATE, prm["c_lambda"][o], prm["c_out_norm"][o], lam_init, batch, seq)
    wq = prm["w_q_b"][o].reshape(D_Q_RANK, D_HEADS, D_NOPE + D_ROPE)
    wq2 = jnp.concatenate([wq[:, :, :D_NOPE].reshape(D_Q_RANK, -1), _rope_slot(wq[:, :, D_NOPE:]).reshape(D_Q_RANK, -1)],
                          axis=1)
    wkv = prm["w_kv_b"][o].reshape(D_KV_RANK, D_HEADS, D_NOPE + D_V)
    wkv2 = jnp.concatenate([wkv[:, :, :D_NOPE].reshape(D_KV_RANK, -1), wkv[:, :, D_NOPE:].reshape(D_KV_RANK, -1)],
                           axis=1)
    dscale = (D_NOPE + D_ROPE) ** -0.5
    qd, = mla_projection(p, OD_DQA, D_Q_RANK, prm["d_q_a_norm"][o], wq2.astype(BF16), prm["d_q_norm"][o], rope_d,
                         dscale * LOG2E, seq)
    kd, vd = mla_projection(p, OD_CKV, D_KV_RANK, prm["d_kv_a_norm"][o], wkv2.astype(BF16), prm["d_k_norm"][o], rope_d,
                            1.0, seq, rope_col=OD_KR)
    yd = mla_attention(qd, kd, vd, p, OD_GATE + C_WIDTH, batch, seq)
    kvm = _memory_kv(mem, prm, l)
    ym = memory_attention(p, OD_MQ, OD_GATE + C_WIDTH + D_WIDTH, kvm, prm["mem_q_norm"][l], prm["mem_k_norm"][l],
                          batch, seq)
    return out_projection(yc, yd, ym, x, prm["w_out_odd"][o].astype(BF16))


def _trunk(x, mem, prm):
    batch, seq, d = x.shape
    depth = prm["norm_gain"].shape[0]
    rope_a = _rope_tables(seq, ROT_DIM, True)
    rope_d = _rope_tables(seq, D_ROPE, False)
    h = x.reshape(batch * seq, d)
    for l in range(depth):
        if l % 2 == 0:
            h = _even_layer(h, mem, prm, l, l // 2, batch, seq, rope_a)
        else:
            h = _odd_layer(h, mem, prm, l, l // 2, batch, seq, rope_a, rope_d)
    return h.reshape(batch, seq, d)


def kernel(x_prompt, x_sample, mem_prompt, mem_sample, norm_gain, mem_norm_gain, w_mem_kv, mem_q_norm, mem_k_norm, w_in_even, w_out_even, a_q_norm, a_k_norm, a_sink, hy_short_w, hy_short_b, hy_w1, hy_b1, hy_freq, hy_w2, hy_b2, hy_w3, hy_decay, hy_bias, w_in_odd, w_out_odd, c_q_norm, c_k_norm, c_lambda, c_out_norm, d_q_a_norm, w_q_b, d_kv_a_norm, w_kv_b, d_q_norm, d_k_norm):
    prm = dict(norm_gain=norm_gain, mem_norm_gain=mem_norm_gain, w_mem_kv=w_mem_kv,
               mem_q_norm=mem_q_norm, mem_k_norm=mem_k_norm, w_in_even=w_in_even,
               w_out_even=w_out_even, a_q_norm=a_q_norm, a_k_norm=a_k_norm, a_sink=a_sink,
               hy_short_w=hy_short_w, hy_short_b=hy_short_b, hy_w1=hy_w1, hy_b1=hy_b1,
               hy_freq=hy_freq, hy_w2=hy_w2, hy_b2=hy_b2, hy_w3=hy_w3, hy_decay=hy_decay,
               hy_bias=hy_bias, w_in_odd=w_in_odd, w_out_odd=w_out_odd, c_q_norm=c_q_norm,
               c_k_norm=c_k_norm, c_lambda=c_lambda, c_out_norm=c_out_norm, d_q_a_norm=d_q_a_norm,
               w_q_b=w_q_b, d_kv_a_norm=d_kv_a_norm, w_kv_b=w_kv_b, d_q_norm=d_q_norm,
               d_k_norm=d_k_norm)
    return (_trunk(x_prompt, mem_prompt, prm), _trunk(x_sample, mem_sample, prm))
```

```python
import functools
import math

import jax
import jax.numpy as jnp
from jax import lax
from jax.experimental import pallas as pl
from jax.experimental.pallas import tpu as pltpu

F32 = jnp.float32
BF16 = jnp.bfloat16

D_MODEL = 2048
N_MEM = 256
HEAD_DIM = 128
ROPE_THETA = 500000.0
ROT_DIM = HEAD_DIM // 4
EPS = 1e-6
BLOCK = 128
WINDOW = 128
NEG_INF = -1e30
A_HEADS = 8
A_KV_HEADS = 2
A_WIDTH = A_HEADS * HEAD_DIM
B_WIDTH = 1024
HY_EMB = 33
HY_FFN = 64
M_HEADS = 4
M_WIDTH = M_HEADS * HEAD_DIM
C_HEADS = 4
C_QK_DIM = 128
C_V_DIM = 256
C_WIDTH = C_HEADS * C_V_DIM
D_HEADS = 8
D_NOPE = 128
D_ROPE = 64
D_V = 128
D_Q_RANK = 512
D_KV_RANK = 256
D_WIDTH = D_HEADS * D_V
MIX_WIDTH = 2560
EVEN_SPLITS = (A_WIDTH, A_KV_HEADS * HEAD_DIM, A_KV_HEADS * HEAD_DIM, A_WIDTH,
               3 * B_WIDTH, B_WIDTH, M_WIDTH, M_WIDTH)
ODD_SPLITS = (C_HEADS * 2 * C_QK_DIM, C_HEADS * 2 * C_QK_DIM, C_WIDTH, C_WIDTH,
              D_Q_RANK, D_KV_RANK + D_ROPE, D_WIDTH, M_WIDTH, M_WIDTH)

LANES = 128
MXU_COLS = 256
DFT_N2 = 128
VMEM_LIMIT = 48 * 1024 * 1024

EV_GATE, EV_MQ, EV_BU, EV_AQ, EV_AK, EV_AV = 0, 2560, 3072, 6144, 7168, 7424
OD_GATE, OD_MQ, OD_CQ, OD_CK, OD_CV, OD_DQA, OD_CKV, OD_KR = 0, 2560, 3072, 4096, 5120, 6144, 6656, 6912
OD_WIDTH = 7168


def _tile(n, pref):
    t = min(n, pref)
    assert n % t == 0, (n, t)
    return t


def _col_tile(n, cap):
    best = None
    for t in range(MXU_COLS, min(n, cap) + 1, MXU_COLS):
        if n % t == 0:
            best = t
    assert best is not None, (n, cap)
    return best


def _params(sem):
    return pltpu.CompilerParams(dimension_semantics=sem, vmem_limit_bytes=VMEM_LIMIT)


def _rms(x, gain):
    ms = jnp.mean(x * x, axis=-1, keepdims=True)
    return x * lax.rsqrt(ms + EPS) * gain


def _silu(g):
    g = g.astype(F32)
    return g * jax.nn.sigmoid(g)


def _lanes(x, k):
    return x if k == 1 else jnp.concatenate([x] * k, axis=1)


def _rope(y, cf, sn):
    return y * cf + pltpu.roll(y, LANES // 2, 1) * sn


def _nmm_kernel(x_ref, g_ref, w_ref, hg_ref, cf_ref, sn_ref, o_ref, h_ref, *, preps):
    j = pl.program_id(1)

    @pl.when(j == 0)
    def _():
        h_ref[...] = _rms(x_ref[...].astype(F32), g_ref[...]).astype(BF16)

    def product():
        return jnp.dot(h_ref[...], w_ref[...], preferred_element_type=F32)

    plain = None
    for blk in preps:
        plain = (j != blk) if plain is None else jnp.logical_and(plain, j != blk)

    if plain is None:
        o_ref[...] = product().astype(o_ref.dtype)
        return

    @pl.when(plain)
    def _():
        o_ref[...] = product().astype(o_ref.dtype)

    for blk, (gain_row0, scales) in preps.items():
        @pl.when(j == blk)
        def _(gain_row0=gain_row0, scales=scales):
            cf, sn = cf_ref[...], sn_ref[...]
            per_tile = MXU_COLS // LANES
            for tile in range(len(scales) // per_tile):
                acc = jnp.dot(h_ref[...], w_ref[:, tile * MXU_COLS:(tile + 1) * MXU_COLS],
                              preferred_element_type=F32)
                for sub in range(per_tile):
                    slot = tile * per_tile + sub
                    y = acc[:, sub * LANES:(sub + 1) * LANES]
                    if scales[slot] is not None:
                        gain = hg_ref[gain_row0 + slot:gain_row0 + slot + 1, :]
                        y = _rope(_rms(y, gain), cf, sn) * scales[slot]
                    o_ref[:, slot * LANES:(slot + 1) * LANES] = y.astype(o_ref.dtype)


def normed_matmul(x, gain, w, *, seq, col_block=0, tm=1024, tn=1536, preps=None, head_gains=None, tables=None):
    rows = x.shape[0]
    k, n = w.shape
    tm = _tile(seq, tm)
    tn = _col_tile(n, tn)
    if preps is None:
        preps, head_gains = {}, jnp.ones((8, LANES), F32)
        tables = (jnp.ones((tm, LANES), F32), jnp.zeros((tm, LANES), F32))
    tab_spec = pl.BlockSpec((tm, LANES), lambda i, j: (i % (tables[0].shape[0] // tm), 0))
    return pl.pallas_call(
        functools.partial(_nmm_kernel, preps=preps),
        grid=(rows // tm, n // tn),
        in_specs=[pl.BlockSpec((tm, k), lambda i, j: (i, col_block)),
                  pl.BlockSpec((1, k), lambda i, j: (0, 0)),
                  pl.BlockSpec((k, tn), lambda i, j: (0, j)),
                  pl.BlockSpec(head_gains.shape, lambda i, j: (0, 0)),
                  tab_spec, tab_spec],
        out_specs=pl.BlockSpec((tm, tn), lambda i, j: (i, j)),
        out_shape=jax.ShapeDtypeStruct((rows, n), BF16),
        scratch_shapes=[pltpu.VMEM((tm, k), BF16)],
        compiler_params=_params(("parallel", "arbitrary")),
        name="normed_matmul",
    )(x, gain.reshape(1, k).astype(F32), w, head_gains.astype(F32), *tables)


WIN_QB = 4


def _window_kernel(sink_ref, q_ref, kp_ref, kc_ref, kn_ref, vp_ref, vc_ref, vn_ref, g_ref, o_ref, s_ref, p_ref, *,
                   seq, qb):
    n = pl.program_id(1)
    group = A_HEADS // A_KV_HEADS
    rows = group * BLOCK
    qi = lax.broadcasted_iota(jnp.int32, (rows, 3 * BLOCK), 0) & (BLOCK - 1)
    kj = lax.broadcasted_iota(jnp.int32, (rows, 3 * BLOCK), 1)
    in_win = jnp.abs(kj - BLOCK - qi) <= WINDOW
    pairs = [(kv, i) for kv in range(A_KV_HEADS) for i in range(qb)]

    def kv_rows(refs, kv):
        cols = slice(kv * HEAD_DIM, (kv + 1) * HEAD_DIM)
        return jnp.concatenate([r[:, cols] for r in refs], axis=0).astype(BF16)

    kfull = [kv_rows((kp_ref, kc_ref, kn_ref), kv) for kv in range(A_KV_HEADS)]
    for idx, (kv, i) in enumerate(pairs):
        qrows = slice(i * BLOCK, (i + 1) * BLOCK)
        q = jnp.concatenate([q_ref[qrows, h * HEAD_DIM:(h + 1) * HEAD_DIM]
                             for h in range(kv * group, (kv + 1) * group)], axis=0)
        s_ref[idx] = lax.dot_general(q, kfull[kv][i * BLOCK:(i + 3) * BLOCK], (((1,), (1,)), ((), ())),
                                     preferred_element_type=F32)

    for idx, (kv, i) in enumerate(pairs):
        sk = jnp.concatenate([jnp.full((BLOCK, 1), sink_ref[h], F32)
                              for h in range(kv * group, (kv + 1) * group)], axis=0)
        kpos = (n * qb + i - 1) * BLOCK + kj
        s = jnp.where(in_win & (kpos >= 0) & (kpos < seq), s_ref[idx], NEG_INF)
        m = jnp.maximum(jnp.max(s, axis=-1, keepdims=True), sk)
        pr = jnp.exp(s - m)
        denom = jnp.sum(pr, axis=-1, keepdims=True) + jnp.exp(sk - m)
        p_ref[idx] = (pr / denom).astype(BF16)

    vfull = [kv_rows((vp_ref, vc_ref, vn_ref), kv) for kv in range(A_KV_HEADS)]
    for idx, (kv, i) in enumerate(pairs):
        qrows = slice(i * BLOCK, (i + 1) * BLOCK)
        o = jnp.dot(p_ref[idx], vfull[kv][i * BLOCK:(i + 3) * BLOCK], preferred_element_type=F32)
        for g, h in enumerate(range(kv * group, (kv + 1) * group)):
            hc = slice(h * HEAD_DIM, (h + 1) * HEAD_DIM)
            o_ref[qrows, hc] = (o[g * BLOCK:(g + 1) * BLOCK] * _silu(g_ref[qrows, hc])).astype(o_ref.dtype)


def window_attention(p, q_col, k_col, v_col, g_col, sink, batch, seq):
    nblk = seq // BLOCK
    qb = _tile(nblk, WIN_QB)
    nstep = nblk // qb
    kvw = A_KV_HEADS * HEAD_DIM
    kb, vb = k_col // kvw, v_col // kvw

    def prev(b, n):
        return b * nblk + jnp.maximum(n * qb - 1, 0)

    def cur(b, n):
        return b * nstep + n

    def nxt(b, n):
        return b * nblk + jnp.minimum((n + 1) * qb, nblk - 1)

    return pl.pallas_call(
        functools.partial(_window_kernel, seq=seq, qb=qb),
        grid=(batch, nstep),
        in_specs=[pl.BlockSpec(memory_space=pltpu.SMEM),
                  pl.BlockSpec((qb * BLOCK, A_WIDTH), lambda b, n: (cur(b, n), q_col // A_WIDTH)),
                  pl.BlockSpec((BLOCK, kvw), lambda b, n: (prev(b, n), kb)),
                  pl.BlockSpec((qb * BLOCK, kvw), lambda b, n: (cur(b, n), kb)),
                  pl.BlockSpec((BLOCK, kvw), lambda b, n: (nxt(b, n), kb)),
                  pl.BlockSpec((BLOCK, kvw), lambda b, n: (prev(b, n), vb)),
                  pl.BlockSpec((qb * BLOCK, kvw), lambda b, n: (cur(b, n), vb)),
                  pl.BlockSpec((BLOCK, kvw), lambda b, n: (nxt(b, n), vb)),
                  pl.BlockSpec((qb * BLOCK, A_WIDTH), lambda b, n: (cur(b, n), g_col // A_WIDTH))],
        out_specs=pl.BlockSpec((qb * BLOCK, A_WIDTH), lambda b, n: (cur(b, n), 0)),
        out_shape=jax.ShapeDtypeStruct((batch * seq, A_WIDTH), BF16),
        scratch_shapes=[pltpu.VMEM((A_KV_HEADS * qb, A_HEADS // A_KV_HEADS * BLOCK, 3 * BLOCK), F32),
                        pltpu.VMEM((A_KV_HEADS * qb, A_HEADS // A_KV_HEADS * BLOCK, 3 * BLOCK), BF16)],
        compiler_params=_params(("parallel", "parallel")),
        name="window_attention",
    )(sink.astype(F32), p, p, p, p, p, p, p, p)


def _mem_kernel(q_ref, kv_ref, qg_ref, kg_ref, g_ref, o_ref):
    qg, kg = qg_ref[...], kg_ref[...]
    scale = HEAD_DIM ** -0.5
    for h in range(M_HEADS):
        cols = slice(h * HEAD_DIM, (h + 1) * HEAD_DIM)
        q = (_rms(q_ref[:, cols].astype(F32), qg) * scale).astype(BF16)
        k = _rms(kv_ref[:, cols].astype(F32), kg).astype(BF16)
        v = kv_ref[:, M_WIDTH + h * HEAD_DIM:M_WIDTH + (h + 1) * HEAD_DIM].astype(BF16)
        s = lax.dot_general(q, k, (((1,), (1,)), ((), ())), preferred_element_type=F32)
        m = jnp.max(s, axis=-1, keepdims=True)
        pr = jnp.exp(s - m)
        pr = pr / jnp.sum(pr, axis=-1, keepdims=True)
        o = jnp.dot(pr.astype(BF16), v, preferred_element_type=F32)
        o_ref[:, cols] = (o * _silu(g_ref[:, cols])).astype(o_ref.dtype)


def memory_attention(p, q_col, g_col, kvm, q_gain, k_gain, batch, seq, *, tq=512):
    tq = _tile(seq, tq)
    nq = seq // tq
    return pl.pallas_call(
        _mem_kernel,
        grid=(batch, nq),
        in_specs=[pl.BlockSpec((tq, M_WIDTH), lambda b, i: (b * nq + i, q_col // M_WIDTH)),
                  pl.BlockSpec((N_MEM, 2 * M_WIDTH), lambda b, i: (b, 0)),
                  pl.BlockSpec((1, HEAD_DIM), lambda b, i: (0, 0)),
                  pl.BlockSpec((1, HEAD_DIM), lambda b, i: (0, 0)),
                  pl.BlockSpec((tq, M_WIDTH), lambda b, i: (b * nq + i, g_col // M_WIDTH))],
        out_specs=pl.BlockSpec((tq, M_WIDTH), lambda b, i: (b * nq + i, 0)),
        out_shape=jax.ShapeDtypeStruct((batch * seq, M_WIDTH), BF16),
        compiler_params=_params(("parallel", "parallel")),
        name="memory_attention",
    )(p, kvm, q_gain.reshape(1, HEAD_DIM).astype(F32), k_gain.reshape(1, HEAD_DIM).astype(F32), p)


LOG2E = math.log2(math.e)


def _chunk_rows(chunk, tk):
    return pl.ds(pl.multiple_of(chunk * tk, tk), tk)


def _pipelined_chunks(scores, consume, nk):
    scores(0, 0)

    def body(i, carry):
        scores(2 * i + 1, 1)
        consume(2 * i, 0)
        scores(jnp.minimum(2 * i + 2, nk - 1), 0)
        consume(2 * i + 1, 1)
        return carry

    lax.fori_loop(0, nk // 2, body, 0)


def _diff_flash_kernel(q_ref, k_ref, v_ref, lam_ref, g_ref, gate_ref, o_ref, m_ref, l_ref, acc_ref, s_ref, *,
                       tk, nk, lam_init):
    m_ref[...] = jnp.full(m_ref.shape, NEG_INF, F32)
    l_ref[...] = jnp.zeros(l_ref.shape, F32)
    acc_ref[...] = jnp.zeros(acc_ref.shape, F32)

    def scores(chunk, c):
        cols = slice(c * C_QK_DIM, (c + 1) * C_QK_DIM)
        s_ref[c] = lax.dot_general(q_ref[:, cols], k_ref[_chunk_rows(chunk, tk), cols], (((1,), (1,)), ((), ())),
                                   preferred_element_type=F32)

    def consume(chunk, c):
        s = s_ref[c]
        m_prev = m_ref[c]
        m_new = jnp.maximum(m_prev, jnp.max(s, axis=-1, keepdims=True))
        alpha = jnp.exp2(m_prev - m_new)
        pr = jnp.exp2(s - _lanes(m_new, tk // LANES))
        l_ref[c] = alpha * l_ref[c] + jnp.sum(pr, axis=-1, keepdims=True)
        acc_ref[c] = (_lanes(alpha, C_V_DIM // LANES) * acc_ref[c]
                      + jnp.dot(pr.astype(BF16), v_ref[_chunk_rows(chunk, tk), :], preferred_element_type=F32))
        m_ref[c] = m_new

    scores(0, 0)

    def body(i, carry):
        for j in (2 * i, 2 * i + 1):
            scores(j, 1)
            consume(j, 0)
            scores(jnp.minimum(j + 1, nk - 1), 0)
            consume(j, 1)
        return carry

    lax.fori_loop(0, nk // 2, body, 0)
    lv = lam_ref[...]
    lam = (jnp.exp(jnp.sum(lv[0:1] * lv[1:2], axis=-1, keepdims=True))
           - jnp.exp(jnp.sum(lv[2:3] * lv[3:4], axis=-1, keepdims=True)) + lam_init)
    o1 = acc_ref[0] / _lanes(l_ref[0], C_V_DIM // LANES)
    o2 = acc_ref[1] / _lanes(l_ref[1], C_V_DIM // LANES)
    o_ref[...] = (_rms(o1 - lam * o2, g_ref[...]) * (1.0 - lam_init) * _silu(gate_ref[...])).astype(o_ref.dtype)


def _mla_flash_kernel(q_ref, k_ref, v_ref, gate_ref, o_ref, m_ref, acc_ref, s_ref, *, tk, nk):
    m_ref[...] = jnp.full(m_ref.shape, NEG_INF, F32)
    acc_ref[...] = jnp.zeros(acc_ref.shape, F32)
    ones = jnp.ones((tk, LANES), BF16)

    def scores(chunk, slot):
        s_ref[slot] = lax.dot_general(q_ref[...], k_ref[_chunk_rows(chunk, tk), :], (((1,), (1,)), ((), ())),
                                      preferred_element_type=F32)

    def consume(chunk, slot):
        s = s_ref[slot]
        m_prev = m_ref[...]
        m_new = jnp.maximum(m_prev, jnp.max(s, axis=-1, keepdims=True))
        alpha = jnp.exp2(m_prev - m_new)
        pr = jnp.exp2(s - _lanes(m_new, tk // LANES))
        vv = jnp.concatenate([v_ref[_chunk_rows(chunk, tk), :], ones], axis=1)
        acc_ref[...] = (_lanes(alpha, 2) * acc_ref[...]
                        + jnp.dot(pr.astype(BF16), vv, preferred_element_type=F32))
        m_ref[...] = m_new

    _pipelined_chunks(scores, consume, nk)
    o_ref[...] = (acc_ref[:, :D_V] / acc_ref[:, D_V:] * _silu(gate_ref[...])).astype(o_ref.dtype)


def _flash_call(kernel, q, q_block0, k, k_block0, v, v_block0, gate, gate_block0, extra, extra_specs, scratch, *,
                heads, dv, batch, seq, tq, tk, name):
    nq, nk = seq // tq, seq // tk
    qk_w = 2 * LANES
    return pl.pallas_call(
        functools.partial(kernel, tk=tk, nk=nk),
        grid=(batch, heads, nq),
        in_specs=[pl.BlockSpec((tq, qk_w), lambda b, h, i: (b * nq + i, q_block0 + h)),
                  pl.BlockSpec((seq, qk_w), lambda b, h, i: (b, k_block0 + h), pipeline_mode=pl.Buffered(1)),
                  pl.BlockSpec((seq, dv), lambda b, h, i: (b, v_block0 + h), pipeline_mode=pl.Buffered(1))]
        + extra_specs + [pl.BlockSpec((tq, dv), lambda b, h, i: (b * nq + i, gate_block0 + h))],
        out_specs=pl.BlockSpec((tq, dv), lambda b, h, i: (b * nq + i, h)),
        out_shape=jax.ShapeDtypeStruct((batch * seq, heads * dv), BF16),
        scratch_shapes=scratch,
        compiler_params=_params(("parallel", "parallel", "arbitrary")),
        name=name,
    )(q, k, v, *extra, gate)


def diff_attention(p, q_col, k_col, v_col, g_col, c_lambda, out_gain, lam_init, batch, seq, *, tq=1024, tk=1024):
    tq, tk = _tile(seq, tq), _tile(seq // 2, tk)
    const = lambda b, h, i: (0, 0)
    return _flash_call(
        functools.partial(_diff_flash_kernel, lam_init=lam_init), p, q_col // C_V_DIM, p, k_col // C_V_DIM,
        p, v_col // C_V_DIM, p, g_col // C_V_DIM,
        [c_lambda.astype(F32), out_gain.reshape(1, C_V_DIM).astype(F32)],
        [pl.BlockSpec((4, C_QK_DIM), const), pl.BlockSpec((1, C_V_DIM), const)],
        [pltpu.VMEM((2, tq, LANES), F32), pltpu.VMEM((2, tq, LANES), F32), pltpu.VMEM((2, tq, C_V_DIM), F32),
         pltpu.VMEM((2, tq, tk), F32)],
        heads=C_HEADS, dv=C_V_DIM, batch=batch, seq=seq, tq=tq, tk=tk, name="diff_attention")


def mla_attention(q, k, v, p, g_col, batch, seq, *, tq=2048, tk=512):
    tq, tk = _tile(seq, tq), _tile(seq // 2, tk)
    return _flash_call(_mla_flash_kernel, q, 0, k, 0, v, 0, p, g_col // D_V, [], [],
                       [pltpu.VMEM((tq, LANES), F32), pltpu.VMEM((tq, D_V + LANES), F32),
                        pltpu.VMEM((2, tq, tk), F32)],
                       heads=D_HEADS, dv=D_V, batch=batch, seq=seq, tq=tq, tk=tk, name="mla_attention")


def _mla_proj_kernel(*refs, shared_rope, scale):
    if shared_rope:
        x_ref, g_ref, w_ref, kr_ref, gn_ref, gr_ref, cf_ref, sn_ref, o_ref, v_ref = refs
    else:
        x_ref, g_ref, w_ref, gn_ref, gr_ref, cf_ref, sn_ref, o_ref = refs
    h = _rms(x_ref[...].astype(F32), g_ref[...]).astype(BF16)
    acc = jnp.dot(h, w_ref[...], preferred_element_type=F32)
    gn, gr = gn_ref[...], gr_ref[...]
    cf, sn = cf_ref[...], sn_ref[...]
    width = D_HEADS * LANES
    for hd in range(D_HEADS):
        cols = slice(hd * LANES, (hd + 1) * LANES)
        xn = acc[:, cols]
        xr = kr_ref[...].astype(F32) if shared_rope else acc[:, width + hd * LANES:width + (hd + 1) * LANES]
        ms = (jnp.sum(xn * xn, axis=-1, keepdims=True)
              + jnp.sum(xr * xr, axis=-1, keepdims=True)) * (1.0 / (D_NOPE + D_ROPE))
        inv = lax.rsqrt(ms + EPS)
        o_ref[:, 2 * hd * LANES:(2 * hd + 1) * LANES] = (xn * inv * gn * scale).astype(o_ref.dtype)
        o_ref[:, (2 * hd + 1) * LANES:(2 * hd + 2) * LANES] = (_rope(xr * inv * gr, cf, sn) * scale).astype(o_ref.dtype)
    if shared_rope:
        v_ref[...] = acc[:, width:].astype(v_ref.dtype)


def mla_projection(p, col, rank, gain, w, head_gain, tables, scale, seq, *, rope_col=None, tm=512):
    rows = p.shape[0]
    width = D_HEADS * LANES
    tm = _tile(seq, tm)
    nblk = seq // tm
    shared_rope = rope_col is not None
    gn = head_gain[:D_NOPE].reshape(1, LANES).astype(F32)
    gr = _rope_slot(head_gain[D_NOPE:]).reshape(1, LANES).astype(F32)
    one = pl.BlockSpec((1, LANES), lambda i: (0, 0))
    tab_spec = pl.BlockSpec((tm, LANES), lambda i: (i % nblk, 0))
    in_specs = [pl.BlockSpec((tm, rank), lambda i: (i, col // rank)),
                pl.BlockSpec((1, rank), lambda i: (0, 0)),
                pl.BlockSpec((rank, 2 * width), lambda i: (0, 0))]
    args = [p, gain.reshape(1, rank).astype(F32), w]
    out_specs = [pl.BlockSpec((tm, 2 * width), lambda i: (i, 0))]
    out_shape = [jax.ShapeDtypeStruct((rows, 2 * width), BF16)]
    if shared_rope:
        in_specs.append(pl.BlockSpec((tm, LANES), lambda i: (i, rope_col // LANES)))
        args.append(p)
        out_specs.append(pl.BlockSpec((tm, width), lambda i: (i, 0)))
        out_shape.append(jax.ShapeDtypeStruct((rows, width), BF16))
    return pl.pallas_call(
        functools.partial(_mla_proj_kernel, shared_rope=shared_rope, scale=scale),
        grid=(rows // tm,),
        in_specs=in_specs + [one, one, tab_spec, tab_spec],
        out_specs=out_specs,
        out_shape=out_shape,
        compiler_params=_params(("parallel",)),
        name="mla_projection",
    )(*args, gn, gr, *tables)


SUB = 8
FILT_ROWS = SUB * DFT_N2
HALO = 16


def _half_rows(n1):
    return n1 // 2 + SUB


def _hyena_gate_kernel(*refs, nblk, tm):
    u_refs, up_refs, un_refs = refs[0:3], refs[3:6], refs[6:9]
    w_ref, b_ref, g_ref, zz_ref, x0_ref = refs[9:]
    li = pl.program_id(1)
    tc = zz_ref.shape[-1]
    row = lax.broadcasted_iota(jnp.int32, (tm, tc), 0)
    has_prev = (li > 0).astype(F32)
    has_next = (li < nblk - 1).astype(F32)

    def conv(part):
        u = u_refs[part][...].astype(F32)
        prev_row = up_refs[part][HALO - 1:HALO, :].astype(F32) * has_prev
        next_row = un_refs[part][0:1, :].astype(F32) * has_next
        above = jnp.where(row == 0, prev_row, pltpu.roll(u, 1, 0))
        below = jnp.where(row == tm - 1, next_row, pltpu.roll(u, tm - 1, 0))
        w = w_ref[part]
        return above * w[0:1] + u * w[1:2] + below * w[2:3] + b_ref[part]

    x0 = conv(0) * _silu(g_ref[...])
    zz = conv(2) * conv(1)
    for a in range(tm // DFT_N2):
        rows = slice(a * DFT_N2, (a + 1) * DFT_N2)
        zz_ref[:, a, :] = zz[rows]
        x0_ref[:, a, :] = x0[rows]


def hyena_gate(p, u_col, g_col, short_w, short_b, batch, seq, *, tc=512):
    tm = FILT_ROWS
    nblk = seq // tm
    half = seq // DFT_N2
    tc = _tile(B_WIDTH, tc)
    nc = B_WIDTH // tc
    hb = tm // HALO
    last_halo = batch * nblk * hb - 1

    def part_specs(part):
        cb = (u_col + part * B_WIDTH) // tc
        return (pl.BlockSpec((tm, tc), lambda b, i, c: (b * nblk + i, cb + c)),
                pl.BlockSpec((HALO, tc), lambda b, i, c: (jnp.maximum((b * nblk + i) * hb - 1, 0), cb + c)),
                pl.BlockSpec((HALO, tc), lambda b, i, c: (jnp.minimum((b * nblk + i + 1) * hb, last_halo), cb + c)))

    specs = [part_specs(part) for part in range(3)]
    out = jax.ShapeDtypeStruct((batch, DFT_N2, half, B_WIDTH), F32)
    o_spec = pl.BlockSpec((None, DFT_N2, tm // DFT_N2, tc), lambda b, i, c: (b, 0, i, c))
    w3 = short_w.astype(F32).reshape(3, 3, B_WIDTH).transpose(1, 0, 2)
    b3 = short_b.astype(F32).reshape(3, 1, B_WIDTH)
    return pl.pallas_call(
        functools.partial(_hyena_gate_kernel, nblk=nblk, tm=tm),
        grid=(batch, nblk, nc),
        in_specs=[sp[0] for sp in specs] + [sp[1] for sp in specs] + [sp[2] for sp in specs]
        + [pl.BlockSpec((3, 3, tc), lambda b, i, c: (0, 0, c)),
           pl.BlockSpec((3, 1, tc), lambda b, i, c: (0, 0, c)),
           pl.BlockSpec((tm, tc), lambda b, i, c: (b * nblk + i, g_col // tc + c))],
        out_specs=[o_spec, o_spec],
        out_shape=[out, out],
        compiler_params=_params(("parallel", "parallel", "parallel")),
        name="hyena_gate",
    )(*([p] * 9), w3, b3, p)


def _hdot(a, b):
    return jnp.dot(a, b, preferred_element_type=F32, precision=lax.Precision.HIGHEST)


def _filter_kernel(fr_ref, w1_ref, b1_ref, fq_ref, w2_ref, b2_ref, w3_ref, dec_ref, k_ref, s_ref, t_ref, h_ref, *,
                   seq, tr):
    i, c = pl.program_id(0), pl.program_id(1)
    r = i * tr + lax.broadcasted_iota(jnp.int32, (tr, 1), 0)

    @pl.when(c == 0)
    def _():
        pos = jnp.where(r < seq, r, 2 * seq - 1 - r).astype(F32)
        t = pos * (1.0 / (seq - 1))
        w = pos * (2.0 * math.pi / seq)
        lane = lax.broadcasted_iota(jnp.int32, (tr, LANES), 1)
        bands = (HY_EMB - 1) // 2
        ang = w * fr_ref[...]
        z = jnp.where(lane == 0, t,
                      jnp.where(lane <= bands, jnp.cos(ang),
                                jnp.where(lane <= 2 * bands, -jnp.sin(ang), 0.0)))
        h = jnp.sin(fq_ref[0:1] * (_hdot(z, w1_ref[...]) + b1_ref[...]))
        h_ref[...] = jnp.sin(fq_ref[1:2] * (_hdot(h, w2_ref[...]) + b2_ref[...]))
        t_ref[...] = jnp.broadcast_to(t, t_ref.shape)

    h = _hdot(h_ref[...], w3_ref[...]) * jnp.exp(-t_ref[:, 0:1] * jnp.abs(dec_ref[...]))
    h = jnp.where(r == seq, 0.0, h)
    for a in range(tr // DFT_N2):
        k_ref[:, a, :] = h[a * DFT_N2:(a + 1) * DFT_N2]

    @pl.when(jnp.logical_and(i == 0, c == 0))
    def _():
        s_ref[...] = jnp.zeros(s_ref.shape, F32)

    tc = h.shape[1]
    col_sum = jnp.sum(jnp.abs(h), axis=0, keepdims=True)
    for cc in range(s_ref.shape[1] // tc):
        @pl.when(c == cc)
        def _(cc=cc):
            s_ref[:, cc * tc:(cc + 1) * tc] += col_sum


def hyena_filter(seq, w1, b1, freq, w2, b2, w3, decay, *, tc=512):
    tr = FILT_ROWS
    assert seq % tr == 0
    nhalf = seq // tr
    n1 = 2 * seq // DFT_N2
    tc = _tile(B_WIDTH, tc)
    nc = B_WIDTH // tc
    bands = (HY_EMB - 1) // 2
    fr = jnp.linspace(1e-4, bands - 1, bands, dtype=F32)
    fr_lanes = jnp.concatenate([jnp.zeros((1,), F32), fr, fr, jnp.zeros((LANES - HY_EMB,), F32)]).reshape(1, LANES)
    w1p = jnp.pad(w1.astype(F32), ((0, LANES - HY_EMB), (0, 0)))
    const = lambda i, c: (0, 0)
    half = lambda i, c: (0, (i // nhalf) * nc + c)
    return pl.pallas_call(
        functools.partial(_filter_kernel, seq=seq, tr=tr),
        grid=(2 * nhalf, nc),
        in_specs=[pl.BlockSpec((1, LANES), const),
                  pl.BlockSpec((LANES, HY_FFN), const),
                  pl.BlockSpec((1, HY_FFN), const),
                  pl.BlockSpec((2, HY_FFN), const),
                  pl.BlockSpec((HY_FFN, HY_FFN), const),
                  pl.BlockSpec((1, HY_FFN), const),
                  pl.BlockSpec((HY_FFN, tc), half),
                  pl.BlockSpec((1, tc), half)],
        out_specs=[pl.BlockSpec((DFT_N2, tr // DFT_N2, tc), lambda i, c: (0, i, c)),
                   pl.BlockSpec((1, B_WIDTH), const)],
        out_shape=[jax.ShapeDtypeStruct((DFT_N2, n1, B_WIDTH), F32),
                   jax.ShapeDtypeStruct((1, B_WIDTH), F32)],
        scratch_shapes=[pltpu.VMEM((tr, LANES), F32), pltpu.VMEM((tr, HY_FFN), F32)],
        compiler_params=_params(("arbitrary", "arbitrary")),
        name="hyena_filter",
    )(fr_lanes, w1p, b1.reshape(1, HY_FFN).astype(F32), freq.astype(F32), w2.astype(F32),
      b2.reshape(1, HY_FFN).astype(F32), w3.astype(F32), decay.reshape(1, 2 * B_WIDTH).astype(F32))


def _bdot(a, b):
    return jnp.dot(a, b, preferred_element_type=F32)


def dft_tables(n1):
    n = n1 * DFT_N2
    rows = _half_rows(n1)
    kk = jnp.arange(rows, dtype=jnp.int32)
    k1 = kk[None, :, None]
    m1 = jnp.arange(n1, dtype=jnp.int32)[None, None, :]
    n2 = jnp.arange(DFT_N2, dtype=jnp.int32)[:, None, None]
    ang = ((k1 * (DFT_N2 * m1 + n2)) % n).astype(F32) * (2.0 * math.pi / n)
    live = (kk <= n1 // 2).astype(F32)[None, :, None]
    weight = jnp.where((kk == 0) | (kk == n1 // 2), 1.0, 2.0)[None, :, None] * live
    cos, sin = jnp.cos(ang), jnp.sin(ang)
    gc = (cos * live).astype(BF16)
    gsn = (-sin * live).astype(BF16)
    hc = jnp.swapaxes(cos * weight, 1, 2)[:, :n1 // 2].astype(BF16)
    hsn = jnp.swapaxes(-sin * weight, 1, 2)[:, :n1 // 2].astype(BF16)
    a = jnp.arange(DFT_N2, dtype=jnp.int32)
    ang2 = ((a[:, None] * a[None, :]) % DFT_N2).astype(F32) * (2.0 * math.pi / DFT_N2)
    ff = jnp.concatenate([jnp.cos(ang2), jnp.sin(ang2)], axis=0).astype(BF16)
    return gc, gsn, hc, hsn, ff


def _dft1_kernel(x_ref, gc_ref, gs_ref, ar_ref, ai_ref):
    for j in range(SUB):
        x = x_ref[j].astype(BF16)
        ar_ref[:, j, :] = _bdot(gc_ref[j], x)
        ai_ref[:, j, :] = _bdot(gs_ref[j], x)


def dft_stage1(x, gc, gsn, *, tc=512):
    batch, _, k1rows, c = x.shape
    rows = gc.shape[1]
    tc = _tile(c, tc)
    out = jax.ShapeDtypeStruct((batch, rows, DFT_N2, c), F32)
    g_spec = pl.BlockSpec((SUB, rows, k1rows), lambda g, b, ci: (g, 0, 0))
    o_spec = pl.BlockSpec((None, rows, SUB, tc), lambda g, b, ci: (b, 0, g, ci))
    return pl.pallas_call(
        _dft1_kernel,
        grid=(DFT_N2 // SUB, batch, c // tc),
        in_specs=[pl.BlockSpec((None, SUB, k1rows, tc), lambda g, b, ci: (b, g, 0, ci)), g_spec, g_spec],
        out_specs=[o_spec, o_spec],
        out_shape=[out, out],
        compiler_params=_params(("parallel", "parallel", "parallel")),
        name="dft_stage1",
    )(x, gc, gsn)


def _stage2(ff, ar, ai):
    p = _bdot(ff, ar)
    q = _bdot(ff, ai)
    return p[:DFT_N2] + q[DFT_N2:], q[:DFT_N2] - p[DFT_N2:]


def _filter_spectrum_kernel(ar_ref, ai_ref, s_ref, ff_ref, kr_ref, ki_ref, *, n):
    ff = ff_ref[...]
    scale = 1.0 / (s_ref[...] * n)
    for j in range(SUB):
        xr, xi = _stage2(ff, ar_ref[j].astype(BF16), ai_ref[j].astype(BF16))
        kr_ref[j] = xr * scale
        ki_ref[j] = xi * scale


def filter_spectrum(ar, ai, asum, ff, n, *, tc=512):
    rows, _, c = ar.shape
    tc = _tile(c, tc)
    a_spec = pl.BlockSpec((SUB, DFT_N2, tc), lambda k, ci: (k, 0, ci))
    out = jax.ShapeDtypeStruct((rows, DFT_N2, c), F32)
    return pl.pallas_call(
        functools.partial(_filter_spectrum_kernel, n=n),
        grid=(rows // SUB, c // tc),
        in_specs=[a_spec, a_spec, pl.BlockSpec((1, tc), lambda k, ci: (0, ci)),
                  pl.BlockSpec((2 * DFT_N2, DFT_N2), lambda k, ci: (0, 0))],
        out_specs=[a_spec, a_spec],
        out_shape=[out, out],
        compiler_params=_params(("parallel", "parallel")),
        name="filter_spectrum",
    )(ar, ai, asum, ff)


def _spectral_kernel(ar_ref, ai_ref, kr_ref, ki_ref, ff_ref, br_ref, bi_ref):
    ff = ff_ref[...]
    for j in range(SUB):
        xr, xi = _stage2(ff, ar_ref[j].astype(BF16), ai_ref[j].astype(BF16))
        kr, ki = kr_ref[j], ki_ref[j]
        yr = (xr * kr - xi * ki).astype(BF16)
        yi = (xr * ki + xi * kr).astype(BF16)
        r = _bdot(ff, yr)
        s = _bdot(ff, yi)
        br_ref[:, j, :] = r[:DFT_N2] - s[DFT_N2:]
        bi_ref[:, j, :] = s[:DFT_N2] + r[DFT_N2:]


def spectral_multiply(ar, ai, kr, ki, ff, *, tc=512):
    batch, rows, _, c = ar.shape
    tc = _tile(c, tc)
    a_spec = pl.BlockSpec((None, SUB, DFT_N2, tc), lambda k, ci, b: (b, k, 0, ci))
    k_spec = pl.BlockSpec((SUB, DFT_N2, tc), lambda k, ci, b: (k, 0, ci))
    o_spec = pl.BlockSpec((None, DFT_N2, SUB, tc), lambda k, ci, b: (b, 0, k, ci))
    out = jax.ShapeDtypeStruct((batch, DFT_N2, rows, c), F32)
    return pl.pallas_call(
        _spectral_kernel,
        grid=(rows // SUB, c // tc, batch),
        in_specs=[a_spec, a_spec, k_spec, k_spec, pl.BlockSpec((2 * DFT_N2, DFT_N2), lambda k, ci, b: (0, 0))],
        out_specs=[o_spec, o_spec],
        out_shape=[out, out],
        compiler_params=_params(("parallel", "parallel", "parallel")),
        name="spectral_multiply",
    )(ar, ai, kr, ki, ff)


def _idft_out_kernel(br_ref, bi_ref, hc_ref, hs_ref, zz_ref, x0_ref, bias_ref, o_ref):
    bias = bias_ref[...]
    for j in range(SUB):
        y = _bdot(hc_ref[j], br_ref[j].astype(BF16)) + _bdot(hs_ref[j], bi_ref[j].astype(BF16))
        o_ref[:, j, :] = (y + zz_ref[j] * bias) * x0_ref[j]


def idft_output(br, bi, hc, hsn, zz, x0, bias, *, tc=512):
    batch, _, rows, c = br.shape
    half = hc.shape[1]
    tc = _tile(c, tc)
    b_spec = pl.BlockSpec((None, SUB, rows, tc), lambda g, b, ci: (b, g, 0, ci))
    h_spec = pl.BlockSpec((SUB, half, rows), lambda g, b, ci: (g, 0, 0))
    z_spec = pl.BlockSpec((None, SUB, half, tc), lambda g, b, ci: (b, g, 0, ci))
    return pl.pallas_call(
        _idft_out_kernel,
        grid=(DFT_N2 // SUB, batch, c // tc),
        in_specs=[b_spec, b_spec, h_spec, h_spec, z_spec, z_spec, pl.BlockSpec((1, tc), lambda g, b, ci: (0, ci))],
        out_specs=pl.BlockSpec((half, SUB, tc), lambda g, b, ci: (b, g, ci)),
        out_shape=jax.ShapeDtypeStruct((batch * half, DFT_N2, c), F32),
        compiler_params=_params(("parallel", "parallel", "parallel")),
        name="idft_output",
    )(br, bi, hc, hsn, zz, x0, bias)


def hyena_mixer(p, u_col, g_col, prm, e, batch, seq):
    n1 = 2 * seq // DFT_N2
    gc, gsn, hc, hsn, ff = dft_tables(n1)
    kern, asum = hyena_filter(seq, prm["hy_w1"][e], prm["hy_b1"][e], prm["hy_freq"][e], prm["hy_w2"][e],
                              prm["hy_b2"][e], prm["hy_w3"][e], prm["hy_decay"][e])
    far, fai = dft_stage1(kern[None], gc, gsn)
    kr, ki = filter_spectrum(far[0], fai[0], asum, ff, n1 * DFT_N2)
    zz, x0 = hyena_gate(p, u_col, g_col, prm["hy_short_w"][e], prm["hy_short_b"][e], batch, seq)
    ar, ai = dft_stage1(zz, gc[:, :, :n1 // 2], gsn[:, :, :n1 // 2])
    br, bi = spectral_multiply(ar, ai, kr, ki, ff)
    yb = idft_output(br, bi, hc, hsn, zz, x0, prm["hy_bias"][e].reshape(1, B_WIDTH).astype(F32))
    return yb.reshape(batch * seq, B_WIDTH)


def _outproj_kernel(y1_ref, y2_ref, y3_ref, x_ref, w_ref, o_ref):
    acc = x_ref[...]
    off = 0
    for y_ref in (y1_ref, y2_ref, y3_ref):
        wd = y_ref.shape[1]
        acc = acc + jnp.dot(y_ref[...].astype(BF16), w_ref[off:off + wd, :], preferred_element_type=F32)
        off += wd
    o_ref[...] = acc


def out_projection(y1, y2, y3, x, w, *, tm=512):
    rows, d = x.shape
    tm = _tile(rows, tm)
    row = lambda i: (i, 0)
    return pl.pallas_call(
        _outproj_kernel,
        grid=(rows // tm,),
        in_specs=[pl.BlockSpec((tm, y1.shape[1]), row),
                  pl.BlockSpec((tm, y2.shape[1]), row),
                  pl.BlockSpec((tm, y3.shape[1]), row),
                  pl.BlockSpec((tm, d), row),
                  pl.BlockSpec((MIX_WIDTH, d), lambda i: (0, 0), pipeline_mode=pl.Buffered(1))],
        out_specs=pl.BlockSpec((tm, d), row),
        out_shape=jax.ShapeDtypeStruct((rows, d), F32),
        compiler_params=_params(("parallel",)),
        name="out_projection",
    )(y1, y2, y3, x, w)


def _split_cols(w, sizes):
    out, off = [], 0
    for s in sizes:
        out.append(w[:, off:off + s])
        off += s
    return out


ROT_HALF = ROT_DIM // 2
HEAD_PERM = (tuple(range(ROT_HALF)) + tuple(range(ROT_DIM, LANES // 2 + ROT_HALF))
             + tuple(range(ROT_HALF, ROT_DIM)) + tuple(range(LANES // 2 + ROT_HALF, LANES)))


def _permute_heads(w, nheads):
    k = w.shape[0]
    return w.reshape(k, nheads, HEAD_DIM)[:, :, jnp.array(HEAD_PERM)].reshape(k, nheads * HEAD_DIM)


def _rope_slot(w):
    half = D_ROPE // 2
    z = jnp.zeros(w.shape[:-1] + (LANES // 2 - half,), w.dtype)
    return jnp.concatenate([w[..., :half], z, w[..., half:], z], axis=-1)


def _rope_tables(seq, dim, passthrough):
    half = dim // 2
    inv = ROPE_THETA ** (-jnp.arange(0, dim, 2, dtype=F32) / dim)
    ang = jnp.arange(seq, dtype=F32)[:, None] * inv[None, :]
    cos, sin = jnp.cos(ang), jnp.sin(ang)
    rest = jnp.full((seq, LANES // 2 - half), 1.0 if passthrough else 0.0, F32)
    zeros = jnp.zeros((seq, LANES // 2 - half), F32)
    cf = jnp.concatenate([cos, rest, cos, rest], axis=1)
    sn = jnp.concatenate([-sin, zeros, sin, zeros], axis=1)
    return cf, sn


def _memory_kv(mem, prm, l):
    batch = mem.shape[0]
    return normed_matmul(mem.reshape(batch * N_MEM, D_MODEL), prm["mem_norm_gain"][l],
                         prm["w_mem_kv"][l].astype(BF16), seq=N_MEM)


def _head_gains(*groups):
    rows = [jnp.broadcast_to(g[jnp.array(HEAD_PERM)][None, :], (n, HEAD_DIM)) for g, n in groups]
    return jnp.concatenate(rows, axis=0)


def _even_layer(x, mem, prm, l, e, batch, seq, rope_a):
    aq, ak, av, ag, bu, bg, mq, mg = _split_cols(prm["w_in_even"][e], EVEN_SPLITS)
    w_in = jnp.concatenate([ag, bg, mg, mq, bu, _permute_heads(aq, A_HEADS), _permute_heads(ak, A_KV_HEADS), av],
                           axis=1).astype(BF16)
    tn = _col_tile(w_in.shape[1], 1536)
    assert EV_AQ % tn == 0 and w_in.shape[1] - EV_AQ == tn
    slots = [HEAD_DIM ** -0.5] * A_HEADS + [1.0] * A_KV_HEADS + [None] * A_KV_HEADS
    p = normed_matmul(x, prm["norm_gain"][l], w_in, seq=seq, tn=tn, preps={EV_AQ // tn: (0, slots)},
                      head_gains=_head_gains((prm["a_q_norm"][e], A_HEADS), (prm["a_k_norm"][e], A_KV_HEADS)),
                      tables=rope_a)
    ya = window_attention(p, EV_AQ, EV_AK, EV_AV, EV_GATE, prm["a_sink"][e], batch, seq)
    yb = hyena_mixer(p, EV_BU, EV_GATE + A_WIDTH, prm, e, batch, seq)
    kvm = _memory_kv(mem, prm, l)
    ym = memory_attention(p, EV_MQ, EV_GATE + A_WIDTH + B_WIDTH, kvm, prm["mem_q_norm"][l], prm["mem_k_norm"][l],
                          batch, seq)
    return out_projection(ya, yb, ym, x, prm["w_out_even"][e].astype(BF16))


def _odd_layer(x, mem, prm, l, o, batch, seq, rope_a, rope_d):
    cq, ck, cv, cg, dqa, dkva, dg, mq, mg = _split_cols(prm["w_in_odd"][o], ODD_SPLITS)
    pad = jnp.zeros((D_MODEL, OD_WIDTH - (OD_KR + LANES)), F32)
    w_in = jnp.concatenate([cg, dg, mg, mq, _permute_heads(cq, 2 * C_HEADS), _permute_heads(ck, 2 * C_HEADS), cv, dqa,
                            dkva[:, :D_KV_RANK], _rope_slot(dkva[:, D_KV_RANK:]), pad], axis=1).astype(BF16)
    tn = _col_tile(w_in.shape[1], 1024)
    assert OD_CQ % tn == 0 and OD_CK - OD_CQ == tn and OD_CV - OD_CK == tn
    nslot = 2 * C_HEADS
    p = normed_matmul(x, prm["norm_gain"][l], w_in, seq=seq, tn=tn,
                      preps={OD_CQ // tn: (0, [C_QK_DIM ** -0.5 * LOG2E] * nslot), OD_CK // tn: (nslot, [1.0] * nslot)},
                      head_gains=_head_gains((prm["c_q_norm"][o], nslot), (prm["c_k_norm"][o], nslot)),
                      tables=rope_a)
    lam_init = 0.8 - 0.6 * math.exp(-0.3 * l)
    yc = diff_attention(p, OD_CQ, OD_CK, OD_CV, OD_GATE, prm["c_lambda"][o], prm["c_out_norm"][o], lam_init, batch, seq)
    wq = prm["w_q_b"][o].reshape(D_Q_RANK, D_HEADS, D_NOPE + D_ROPE)
    wq2 = jnp.concatenate([wq[:, :, :D_NOPE].reshape(D_Q_RANK, -1), _rope_slot(wq[:, :, D_NOPE:]).reshape(D_Q_RANK, -1)],
                          axis=1)
    wkv = prm["w_kv_b"][o].reshape(D_KV_RANK, D_HEADS, D_NOPE + D_V)
    wkv2 = jnp.concatenate([wkv[:, :, :D_NOPE].reshape(D_KV_RANK, -1), wkv[:, :, D_NOPE:].reshape(D_KV_RANK, -1)],
                           axis=1)
    dscale = (D_NOPE + D_ROPE) ** -0.5
    qd, = mla_projection(p, OD_DQA, D_Q_RANK, prm["d_q_a_norm"][o], wq2.astype(BF16), prm["d_q_norm"][o], rope_d,
                         dscale * LOG2E, seq)
    kd, vd = mla_projection(p, OD_CKV, D_KV_RANK, prm["d_kv_a_norm"][o], wkv2.astype(BF16), prm["d_k_norm"][o], rope_d,
                            1.0, seq, rope_col=OD_KR)
    yd = mla_attention(qd, kd, vd, p, OD_GATE + C_WIDTH, batch, seq)
    kvm = _memory_kv(mem, prm, l)
    ym = memory_attention(p, OD_MQ, OD_GATE + C_WIDTH + D_WIDTH, kvm, prm["mem_q_norm"][l], prm["mem_k_norm"][l],
                          batch, seq)
    return out_projection(yc, yd, ym, x, prm["w_out_odd"][o].astype(BF16))


def _trunk(x, mem, prm):
    batch, seq, d = x.shape
    depth = prm["norm_gain"].shape[0]
    rope_a = _rope_tables(seq, ROT_DIM, True)
    rope_d = _rope_tables(seq, D_ROPE, False)
    h = x.reshape(batch * seq, d)
    for l in range(depth):
        if l % 2 == 0:
            h = _even_layer(h, mem, prm, l, l // 2, batch, seq, rope_a)
        else:
            h = _odd_layer(h, mem, prm, l, l // 2, batch, seq, rope_a, rope_d)
    return h.reshape(batch, seq, d)


def kernel(x_prompt, x_sample, mem_prompt, mem_sample, norm_gain, mem_norm_gain, w_mem_kv, mem_q_norm, mem_k_norm, w_in_even, w_out_even, a_q_norm, a_k_norm, a_sink, hy_short_w, hy_short_b, hy_w1, hy_b1, hy_freq, hy_w2, hy_b2, hy_w3, hy_decay, hy_bias, w_in_odd, w_out_odd, c_q_norm, c_k_norm, c_lambda, c_out_norm, d_q_a_norm, w_q_b, d_kv_a_norm, w_kv_b, d_q_norm, d_k_norm):
    prm = dict(norm_gain=norm_gain, mem_norm_gain=mem_norm_gain, w_mem_kv=w_mem_kv,
               mem_q_norm=mem_q_norm, mem_k_norm=mem_k_norm, w_in_even=w_in_even,
               w_out_even=w_out_even, a_q_norm=a_q_norm, a_k_norm=a_k_norm, a_sink=a_sink,
               hy_short_w=hy_short_w, hy_short_b=hy_short_b, hy_w1=hy_w1, hy_b1=hy_b1,
               hy_freq=hy_freq, hy_w2=hy_w2, hy_b2=hy_b2, hy_w3=hy_w3, hy_decay=hy_decay,
               hy_bias=hy_bias, w_in_odd=w_in_odd, w_out_odd=w_out_odd, c_q_norm=c_q_norm,
               c_k_norm=c_k_norm, c_lambda=c_lambda, c_out_norm=c_out_norm, d_q_a_norm=d_q_a_norm,
               w_q_b=w_q_b, d_kv_a_norm=d_kv_a_norm, w_kv_b=w_kv_b, d_q_norm=d_q_norm,
               d_k_norm=d_k_norm)
    return (_trunk(x_prompt, mem_prompt, prm), _trunk(x_sample, mem_sample, prm))
```

```python
import functools
import math

import jax
import jax.numpy as jnp
from jax import lax
from jax.experimental import pallas as pl
from jax.experimental.pallas import tpu as pltpu

F32 = jnp.float32
BF16 = jnp.bfloat16

D_MODEL = 2048
N_MEM = 256
HEAD_DIM = 128
ROPE_THETA = 500000.0
ROT_DIM = HEAD_DIM // 4
EPS = 1e-6
BLOCK = 128
WINDOW = 128
NEG_INF = -1e30
A_HEADS = 8
A_KV_HEADS = 2
A_WIDTH = A_HEADS * HEAD_DIM
B_WIDTH = 1024
HY_EMB = 33
HY_FFN = 64
M_HEADS = 4
M_WIDTH = M_HEADS * HEAD_DIM
C_HEADS = 4
C_QK_DIM = 128
C_V_DIM = 256
C_WIDTH = C_HEADS * C_V_DIM
D_HEADS = 8
D_NOPE = 128
D_ROPE = 64
D_V = 128
D_Q_RANK = 512
D_KV_RANK = 256
D_WIDTH = D_HEADS * D_V
MIX_WIDTH = 2560
EVEN_SPLITS = (A_WIDTH, A_KV_HEADS * HEAD_DIM, A_KV_HEADS * HEAD_DIM, A_WIDTH,
               3 * B_WIDTH, B_WIDTH, M_WIDTH, M_WIDTH)
ODD_SPLITS = (C_HEADS * 2 * C_QK_DIM, C_HEADS * 2 * C_QK_DIM, C_WIDTH, C_WIDTH,
              D_Q_RANK, D_KV_RANK + D_ROPE, D_WIDTH, M_WIDTH, M_WIDTH)

LANES = 128
SUBLANES = 8
MXU_COLS = 256
DFT_N2 = 128
VMEM_LIMIT = 56 * 1024 * 1024

EV_GATE, EV_MQ, EV_BU, EV_AQ, EV_AK, EV_AV = 0, 2560, 3072, 6144, 7168, 7424
OD_GATE, OD_MQ, OD_CQ, OD_CK, OD_CV, OD_DQA, OD_CKV, OD_KR = 0, 2560, 3072, 4096, 5120, 6144, 6656, 6912
OD_WIDTH = 7168


def _tile(n, pref):
    t = min(n, pref)
    assert n % t == 0, (n, t)
    return t


def _col_tile(n, cap):
    best = None
    for t in range(MXU_COLS, min(n, cap) + 1, MXU_COLS):
        if n % t == 0:
            best = t
    assert best is not None, (n, cap)
    return best


def _params(sem):
    return pltpu.CompilerParams(dimension_semantics=sem, vmem_limit_bytes=VMEM_LIMIT)


def _rms(x, gain):
    ms = jnp.mean(x * x, axis=-1, keepdims=True)
    return x * lax.rsqrt(ms + EPS) * gain


def _silu(g):
    g = g.astype(F32)
    return g * jax.nn.sigmoid(g)


def _lanes(x, k):
    return x if k == 1 else jnp.concatenate([x] * k, axis=1)


def _rope(y, cf, sn):
    return y * cf + pltpu.roll(y, LANES // 2, 1) * sn


def _nmm_kernel(x_ref, g_ref, w_ref, hg_ref, cf_ref, sn_ref, o_ref, h_ref, acc_ref, *, preps):
    j = pl.program_id(1)

    @pl.when(j == 0)
    def _():
        h_ref[...] = _rms(x_ref[...].astype(F32), g_ref[...]).astype(BF16)

    def product():
        return jnp.dot(h_ref[...], w_ref[...], preferred_element_type=F32)

    plain = None
    for blk in preps:
        plain = (j != blk) if plain is None else jnp.logical_and(plain, j != blk)

    if plain is None:
        o_ref[...] = product().astype(o_ref.dtype)
        return

    @pl.when(plain)
    def _():
        o_ref[...] = product().astype(o_ref.dtype)

    for blk, (gain_row0, scales) in preps.items():
        @pl.when(j == blk)
        def _(gain_row0=gain_row0, scales=scales):
            acc_ref[...] = product()

            @pl.when(pl.program_id(0) >= 0)
            def _():
                cf, sn = cf_ref[...], sn_ref[...]
                for slot, scale in enumerate(scales):
                    cols = slice(slot * LANES, (slot + 1) * LANES)
                    y = acc_ref[:, cols]
                    if scale is not None:
                        y = _rope(_rms(y, hg_ref[gain_row0 + slot:gain_row0 + slot + 1, :]), cf, sn) * scale
                    o_ref[:, cols] = y.astype(o_ref.dtype)


def normed_matmul(x, gain, w, *, seq, col_block=0, tm=1024, tn=1536, preps=None, head_gains=None, tables=None):
    rows = x.shape[0]
    k, n = w.shape
    tm = _tile(seq, tm)
    tn = _col_tile(n, tn)
    if preps is None:
        preps, head_gains = {}, jnp.ones((8, LANES), F32)
        tables = (jnp.ones((tm, LANES), F32), jnp.zeros((tm, LANES), F32))
    tab_spec = pl.BlockSpec((tm, LANES), lambda i, j: (i % (tables[0].shape[0] // tm), 0))
    return pl.pallas_call(
        functools.partial(_nmm_kernel, preps=preps),
        grid=(rows // tm, n // tn),
        in_specs=[pl.BlockSpec((tm, k), lambda i, j: (i, col_block)),
                  pl.BlockSpec((1, k), lambda i, j: (0, 0)),
                  pl.BlockSpec((k, tn), lambda i, j: (0, j)),
                  pl.BlockSpec(head_gains.shape, lambda i, j: (0, 0)),
                  tab_spec, tab_spec],
        out_specs=pl.BlockSpec((tm, tn), lambda i, j: (i, j)),
        out_shape=jax.ShapeDtypeStruct((rows, n), BF16),
        scratch_shapes=[pltpu.VMEM((tm, k), BF16), pltpu.VMEM((tm, tn) if preps else (SUBLANES, LANES), F32)],
        compiler_params=_params(("parallel", "arbitrary")),
        name="normed_matmul",
    )(x, gain.reshape(1, k).astype(F32), w, head_gains.astype(F32), *tables)


WIN_QB = 4


def _window_kernel(sink_ref, q_ref, kp_ref, kc_ref, kn_ref, vp_ref, vc_ref, vn_ref, g_ref, o_ref, s_ref, p_ref, *,
                   seq, qb):
    n = pl.program_id(1)
    group = A_HEADS // A_KV_HEADS
    rows = group * BLOCK
    qi = lax.broadcasted_iota(jnp.int32, (rows, 3 * BLOCK), 0) & (BLOCK - 1)
    kj = lax.broadcasted_iota(jnp.int32, (rows, 3 * BLOCK), 1)
    in_win = jnp.abs(kj - BLOCK - qi) <= WINDOW
    pairs = [(kv, i) for kv in range(A_KV_HEADS) for i in range(qb)]

    def kv_rows(refs, kv):
        cols = slice(kv * HEAD_DIM, (kv + 1) * HEAD_DIM)
        return jnp.concatenate([r[:, cols] for r in refs], axis=0).astype(BF16)

    kfull = [kv_rows((kp_ref, kc_ref, kn_ref), kv) for kv in range(A_KV_HEADS)]
    for idx, (kv, i) in enumerate(pairs):
        qrows = slice(i * BLOCK, (i + 1) * BLOCK)
        q = jnp.concatenate([q_ref[qrows, h * HEAD_DIM:(h + 1) * HEAD_DIM]
                             for h in range(kv * group, (kv + 1) * group)], axis=0)
        s_ref[idx] = lax.dot_general(q, kfull[kv][i * BLOCK:(i + 3) * BLOCK], (((1,), (1,)), ((), ())),
                                     preferred_element_type=F32)

    for idx, (kv, i) in enumerate(pairs):
        sk = jnp.concatenate([jnp.full((BLOCK, 1), sink_ref[h], F32)
                              for h in range(kv * group, (kv + 1) * group)], axis=0)
        kpos = (n * qb + i - 1) * BLOCK + kj
        s = jnp.where(in_win & (kpos >= 0) & (kpos < seq), s_ref[idx], NEG_INF)
        m = jnp.maximum(jnp.max(s, axis=-1, keepdims=True), sk)
        pr = jnp.exp(s - m)
        denom = jnp.sum(pr, axis=-1, keepdims=True) + jnp.exp(sk - m)
        p_ref[idx] = (pr / denom).astype(BF16)

    vfull = [kv_rows((vp_ref, vc_ref, vn_ref), kv) for kv in range(A_KV_HEADS)]
    for idx, (kv, i) in enumerate(pairs):
        qrows = slice(i * BLOCK, (i + 1) * BLOCK)
        o = jnp.dot(p_ref[idx], vfull[kv][i * BLOCK:(i + 3) * BLOCK], preferred_element_type=F32)
        for g, h in enumerate(range(kv * group, (kv + 1) * group)):
            hc = slice(h * HEAD_DIM, (h + 1) * HEAD_DIM)
            o_ref[qrows, hc] = (o[g * BLOCK:(g + 1) * BLOCK] * _silu(g_ref[qrows, hc])).astype(o_ref.dtype)


def window_attention(p, q_col, k_col, v_col, g_col, sink, batch, seq):
    nblk = seq // BLOCK
    qb = _tile(nblk, WIN_QB)
    nstep = nblk // qb
    kvw = A_KV_HEADS * HEAD_DIM
    kb, vb = k_col // kvw, v_col // kvw

    def prev(b, n):
        return b * nblk + jnp.maximum(n * qb - 1, 0)

    def cur(b, n):
        return b * nstep + n

    def nxt(b, n):
        return b * nblk + jnp.minimum((n + 1) * qb, nblk - 1)

    return pl.pallas_call(
        functools.partial(_window_kernel, seq=seq, qb=qb),
        grid=(batch, nstep),
        in_specs=[pl.BlockSpec(memory_space=pltpu.SMEM),
                  pl.BlockSpec((qb * BLOCK, A_WIDTH), lambda b, n: (cur(b, n), q_col // A_WIDTH)),
                  pl.BlockSpec((BLOCK, kvw), lambda b, n: (prev(b, n), kb)),
                  pl.BlockSpec((qb * BLOCK, kvw), lambda b, n: (cur(b, n), kb)),
                  pl.BlockSpec((BLOCK, kvw), lambda b, n: (nxt(b, n), kb)),
                  pl.BlockSpec((BLOCK, kvw), lambda b, n: (prev(b, n), vb)),
                  pl.BlockSpec((qb * BLOCK, kvw), lambda b, n: (cur(b, n), vb)),
                  pl.BlockSpec((BLOCK, kvw), lambda b, n: (nxt(b, n), vb)),
                  pl.BlockSpec((qb * BLOCK, A_WIDTH), lambda b, n: (cur(b, n), g_col // A_WIDTH))],
        out_specs=pl.BlockSpec((qb * BLOCK, A_WIDTH), lambda b, n: (cur(b, n), 0)),
        out_shape=jax.ShapeDtypeStruct((batch * seq, A_WIDTH), BF16),
        scratch_shapes=[pltpu.VMEM((A_KV_HEADS * qb, A_HEADS // A_KV_HEADS * BLOCK, 3 * BLOCK), F32),
                        pltpu.VMEM((A_KV_HEADS * qb, A_HEADS // A_KV_HEADS * BLOCK, 3 * BLOCK), BF16)],
        compiler_params=_params(("parallel", "parallel")),
        name="window_attention",
    )(sink.astype(F32), p, p, p, p, p, p, p, p)


def _mem_kernel(q_ref, kv_ref, qg_ref, kg_ref, g_ref, o_ref):
    qg, kg = qg_ref[...], kg_ref[...]
    scale = HEAD_DIM ** -0.5
    for h in range(M_HEADS):
        cols = slice(h * HEAD_DIM, (h + 1) * HEAD_DIM)
        q = (_rms(q_ref[:, cols].astype(F32), qg) * scale).astype(BF16)
        k = _rms(kv_ref[:, cols].astype(F32), kg).astype(BF16)
        v = kv_ref[:, M_WIDTH + h * HEAD_DIM:M_WIDTH + (h + 1) * HEAD_DIM].astype(BF16)
        s = lax.dot_general(q, k, (((1,), (1,)), ((), ())), preferred_element_type=F32)
        m = jnp.max(s, axis=-1, keepdims=True)
        pr = jnp.exp(s - m)
        pr = pr / jnp.sum(pr, axis=-1, keepdims=True)
        o = jnp.dot(pr.astype(BF16), v, preferred_element_type=F32)
        o_ref[:, cols] = (o * _silu(g_ref[:, cols])).astype(o_ref.dtype)


def memory_attention(p, q_col, g_col, kvm, q_gain, k_gain, batch, seq, *, tq=512):
    tq = _tile(seq, tq)
    nq = seq // tq
    return pl.pallas_call(
        _mem_kernel,
        grid=(batch, nq),
        in_specs=[pl.BlockSpec((tq, M_WIDTH), lambda b, i: (b * nq + i, q_col // M_WIDTH)),
                  pl.BlockSpec((N_MEM, 2 * M_WIDTH), lambda b, i: (b, 0)),
                  pl.BlockSpec((1, HEAD_DIM), lambda b, i: (0, 0)),
                  pl.BlockSpec((1, HEAD_DIM), lambda b, i: (0, 0)),
                  pl.BlockSpec((tq, M_WIDTH), lambda b, i: (b * nq + i, g_col // M_WIDTH))],
        out_specs=pl.BlockSpec((tq, M_WIDTH), lambda b, i: (b * nq + i, 0)),
        out_shape=jax.ShapeDtypeStruct((batch * seq, M_WIDTH), BF16),
        compiler_params=_params(("parallel", "parallel")),
        name="memory_attention",
    )(p, kvm, q_gain.reshape(1, HEAD_DIM).astype(F32), k_gain.reshape(1, HEAD_DIM).astype(F32), p)


LOG2E = math.log2(math.e)


def _chunk_rows(chunk, tk):
    return pl.ds(pl.multiple_of(chunk * tk, tk), tk)


def _pipelined_chunks(scores, consume, nk):
    scores(0, 0)

    def body(i, carry):
        scores(2 * i + 1, 1)
        consume(2 * i, 0)
        scores(jnp.minimum(2 * i + 2, nk - 1), 0)
        consume(2 * i + 1, 1)
        return carry

    lax.fori_loop(0, nk // 2, body, 0)


def _diff_flash_kernel(q_ref, k_ref, v_ref, lam_ref, g_ref, gate_ref, o_ref, m_ref, l_ref, acc_ref, s_ref, *,
                       tk, nk, lam_init):
    m_ref[...] = jnp.full(m_ref.shape, NEG_INF, F32)
    l_ref[...] = jnp.zeros(l_ref.shape, F32)
    acc_ref[...] = jnp.zeros(acc_ref.shape, F32)

    def scores(chunk, c):
        cols = slice(c * C_QK_DIM, (c + 1) * C_QK_DIM)
        s_ref[c] = lax.dot_general(q_ref[:, cols], k_ref[_chunk_rows(chunk, tk), cols], (((1,), (1,)), ((), ())),
                                   preferred_element_type=F32)

    def consume(chunk, c):
        s = s_ref[c]
        m_prev = m_ref[c]
        m_new = jnp.maximum(m_prev, jnp.max(s, axis=-1, keepdims=True))
        alpha = jnp.exp2(m_prev - m_new)
        pr = jnp.exp2(s - _lanes(m_new, tk // LANES))
        l_ref[c] = alpha * l_ref[c] + jnp.sum(pr, axis=-1, keepdims=True)
        acc_ref[c] = (_lanes(alpha, C_V_DIM // LANES) * acc_ref[c]
                      + jnp.dot(pr.astype(BF16), v_ref[_chunk_rows(chunk, tk), :], preferred_element_type=F32))
        m_ref[c] = m_new

    scores(0, 0)

    def body(i, carry):
        for j in (2 * i, 2 * i + 1):
            scores(j, 1)
            consume(j, 0)
            scores(jnp.minimum(j + 1, nk - 1), 0)
            consume(j, 1)
        return carry

    lax.fori_loop(0, nk // 2, body, 0)
    lv = lam_ref[...]
    lam = (jnp.exp(jnp.sum(lv[0:1] * lv[1:2], axis=-1, keepdims=True))
           - jnp.exp(jnp.sum(lv[2:3] * lv[3:4], axis=-1, keepdims=True)) + lam_init)
    o1 = acc_ref[0] / _lanes(l_ref[0], C_V_DIM // LANES)
    o2 = acc_ref[1] / _lanes(l_ref[1], C_V_DIM // LANES)
    o_ref[...] = (_rms(o1 - lam * o2, g_ref[...]) * (1.0 - lam_init) * _silu(gate_ref[...])).astype(o_ref.dtype)


def _mla_flash_kernel(q_ref, k_ref, v_ref, gate_ref, o_ref, m_ref, acc_ref, s_ref, *, tk, nk):
    m_ref[...] = jnp.full(m_ref.shape, NEG_INF, F32)
    acc_ref[...] = jnp.zeros(acc_ref.shape, F32)
    ones = jnp.ones((tk, LANES), BF16)

    def scores(chunk, slot):
        s_ref[slot] = lax.dot_general(q_ref[...], k_ref[_chunk_rows(chunk, tk), :], (((1,), (1,)), ((), ())),
                                      preferred_element_type=F32)

    def consume(chunk, slot):
        s = s_ref[slot]
        m_prev = m_ref[...]
        m_new = jnp.maximum(m_prev, jnp.max(s, axis=-1, keepdims=True))
        alpha = jnp.exp2(m_prev - m_new)
        pr = jnp.exp2(s - _lanes(m_new, tk // LANES))
        vv = jnp.concatenate([v_ref[_chunk_rows(chunk, tk), :], ones], axis=1)
        acc_ref[...] = (_lanes(alpha, 2) * acc_ref[...]
                        + jnp.dot(pr.astype(BF16), vv, preferred_element_type=F32))
        m_ref[...] = m_new

    _pipelined_chunks(scores, consume, nk)
    o_ref[...] = (acc_ref[:, :D_V] / acc_ref[:, D_V:] * _silu(gate_ref[...])).astype(o_ref.dtype)


def _flash_call(kernel, q, q_block0, k, k_block0, v, v_block0, gate, gate_block0, extra, extra_specs, scratch, *,
                heads, dv, batch, seq, tq, tk, kv_buffers, name):
    nq, nk = seq // tq, seq // tk
    qk_w = 2 * LANES
    resident = pl.Buffered(kv_buffers)
    return pl.pallas_call(
        functools.partial(kernel, tk=tk, nk=nk),
        grid=(batch, heads, nq),
        in_specs=[pl.BlockSpec((tq, qk_w), lambda b, h, i: (b * nq + i, q_block0 + h)),
                  pl.BlockSpec((seq, qk_w), lambda b, h, i: (b, k_block0 + h), pipeline_mode=resident),
                  pl.BlockSpec((seq, dv), lambda b, h, i: (b, v_block0 + h), pipeline_mode=resident)]
        + extra_specs + [pl.BlockSpec((tq, dv), lambda b, h, i: (b * nq + i, gate_block0 + h))],
        out_specs=pl.BlockSpec((tq, dv), lambda b, h, i: (b * nq + i, h)),
        out_shape=jax.ShapeDtypeStruct((batch * seq, heads * dv), BF16),
        scratch_shapes=scratch,
        compiler_params=_params(("parallel", "parallel", "arbitrary")),
        name=name,
    )(q, k, v, *extra, gate)


def diff_attention(p, q_col, k_col, v_col, g_col, c_lambda, out_gain, lam_init, batch, seq, *, tq=1024, tk=1024):
    tq, tk = _tile(seq, tq), _tile(seq // 2, tk)
    const = lambda b, h, i: (0, 0)
    return _flash_call(
        functools.partial(_diff_flash_kernel, lam_init=lam_init), p, q_col // C_V_DIM, p, k_col // C_V_DIM,
        p, v_col // C_V_DIM, p, g_col // C_V_DIM,
        [c_lambda.astype(F32), out_gain.reshape(1, C_V_DIM).astype(F32)],
        [pl.BlockSpec((4, C_QK_DIM), const), pl.BlockSpec((1, C_V_DIM), const)],
        [pltpu.VMEM((2, tq, LANES), F32), pltpu.VMEM((2, tq, LANES), F32), pltpu.VMEM((2, tq, C_V_DIM), F32),
         pltpu.VMEM((2, tq, tk), F32)],
        heads=C_HEADS, dv=C_V_DIM, batch=batch, seq=seq, tq=tq, tk=tk, kv_buffers=1, name="diff_attention")


def mla_attention(q, k, v, p, g_col, batch, seq, *, tq=2048, tk=512):
    tq, tk = _tile(seq, tq), _tile(seq // 2, tk)
    return _flash_call(_mla_flash_kernel, q, 0, k, 0, v, 0, p, g_col // D_V, [], [],
                       [pltpu.VMEM((tq, LANES), F32), pltpu.VMEM((tq, D_V + LANES), F32),
                        pltpu.VMEM((2, tq, tk), F32)],
                       heads=D_HEADS, dv=D_V, batch=batch, seq=seq, tq=tq, tk=tk, kv_buffers=2,
                       name="mla_attention")


def _mla_proj_kernel(*refs, shared_rope, scale):
    if shared_rope:
        x_ref, g_ref, w_ref, kr_ref, gn_ref, gr_ref, cf_ref, sn_ref, o_ref, v_ref, acc_ref = refs
    else:
        x_ref, g_ref, w_ref, gn_ref, gr_ref, cf_ref, sn_ref, o_ref, acc_ref = refs
    h = _rms(x_ref[...].astype(F32), g_ref[...]).astype(BF16)
    acc_ref[...] = jnp.dot(h, w_ref[...], preferred_element_type=F32)
    width = D_HEADS * LANES

    @pl.when(pl.program_id(0) >= 0)
    def _():
        gn, gr = gn_ref[...], gr_ref[...]
        cf, sn = cf_ref[...], sn_ref[...]
        for hd in range(D_HEADS):
            cols = slice(hd * LANES, (hd + 1) * LANES)
            xn = acc_ref[:, cols]
            xr = (kr_ref[...].astype(F32) if shared_rope
                  else acc_ref[:, width + hd * LANES:width + (hd + 1) * LANES])
            ms = (jnp.sum(xn * xn, axis=-1, keepdims=True)
                  + jnp.sum(xr * xr, axis=-1, keepdims=True)) * (1.0 / (D_NOPE + D_ROPE))
            inv = lax.rsqrt(ms + EPS)
            o_ref[:, 2 * hd * LANES:(2 * hd + 1) * LANES] = (xn * inv * gn * scale).astype(o_ref.dtype)
            o_ref[:, (2 * hd + 1) * LANES:(2 * hd + 2) * LANES] = (_rope(xr * inv * gr, cf, sn)
                                                                   * scale).astype(o_ref.dtype)
        if shared_rope:
            v_ref[...] = acc_ref[:, width:].astype(v_ref.dtype)


def mla_projection(p, col, rank, gain, w, head_gain, tables, scale, seq, *, rope_col=None, tm=512):
    rows = p.shape[0]
    width = D_HEADS * LANES
    tm = _tile(seq, tm)
    nblk = seq // tm
    shared_rope = rope_col is not None
    gn = head_gain[:D_NOPE].reshape(1, LANES).astype(F32)
    gr = _rope_slot(head_gain[D_NOPE:]).reshape(1, LANES).astype(F32)
    one = pl.BlockSpec((1, LANES), lambda i: (0, 0))
    tab_spec = pl.BlockSpec((tm, LANES), lambda i: (i % nblk, 0))
    in_specs = [pl.BlockSpec((tm, rank), lambda i: (i, col // rank)),
                pl.BlockSpec((1, rank), lambda i: (0, 0)),
                pl.BlockSpec((rank, 2 * width), lambda i: (0, 0))]
    args = [p, gain.reshape(1, rank).astype(F32), w]
    out_specs = [pl.BlockSpec((tm, 2 * width), lambda i: (i, 0))]
    out_shape = [jax.ShapeDtypeStruct((rows, 2 * width), BF16)]
    if shared_rope:
        in_specs.append(pl.BlockSpec((tm, LANES), lambda i: (i, rope_col // LANES)))
        args.append(p)
        out_specs.append(pl.BlockSpec((tm, width), lambda i: (i, 0)))
        out_shape.append(jax.ShapeDtypeStruct((rows, width), BF16))
    return pl.pallas_call(
        functools.partial(_mla_proj_kernel, shared_rope=shared_rope, scale=scale),
        grid=(rows // tm,),
        in_specs=in_specs + [one, one, tab_spec, tab_spec],
        out_specs=out_specs,
        out_shape=out_shape,
        scratch_shapes=[pltpu.VMEM((tm, 2 * width), F32)],
        compiler_params=_params(("parallel",)),
        name="mla_projection",
    )(*args, gn, gr, *tables)


SUB = 8
FILT_ROWS = SUB * DFT_N2
HALO = 16


def _half_rows(n1):
    return n1 // 2 + SUB


def _hyena_gate_kernel(*refs, nblk, tm):
    u_refs, up_refs, un_refs = refs[0:3], refs[3:6], refs[6:9]
    w_ref, b_ref, g_ref, zz_ref, x0_ref = refs[9:]
    li = pl.program_id(1)
    tc = zz_ref.shape[-1]
    row = lax.broadcasted_iota(jnp.int32, (tm, tc), 0)
    has_prev = (li > 0).astype(F32)
    has_next = (li < nblk - 1).astype(F32)

    def conv(part):
        u = u_refs[part][...].astype(F32)
        prev_row = up_refs[part][HALO - 1:HALO, :].astype(F32) * has_prev
        next_row = un_refs[part][0:1, :].astype(F32) * has_next
        above = jnp.where(row == 0, prev_row, pltpu.roll(u, 1, 0))
        below = jnp.where(row == tm - 1, next_row, pltpu.roll(u, tm - 1, 0))
        w = w_ref[part]
        return above * w[0:1] + u * w[1:2] + below * w[2:3] + b_ref[part]

    x0 = conv(0) * _silu(g_ref[...])
    zz = conv(2) * conv(1)
    for a in range(tm // DFT_N2):
        rows = slice(a * DFT_N2, (a + 1) * DFT_N2)
        zz_ref[:, a, :] = zz[rows]
        x0_ref[:, a, :] = x0[rows]


def hyena_gate(p, u_col, g_col, short_w, short_b, batch, seq, *, tc=512):
    tm = FILT_ROWS
    nblk = seq // tm
    half = seq // DFT_N2
    tc = _tile(B_WIDTH, tc)
    nc = B_WIDTH // tc
    hb = tm // HALO
    last_halo = batch * nblk * hb - 1

    def part_specs(part):
        cb = (u_col + part * B_WIDTH) // tc
        return (pl.BlockSpec((tm, tc), lambda b, i, c: (b * nblk + i, cb + c)),
                pl.BlockSpec((HALO, tc), lambda b, i, c: (jnp.maximum((b * nblk + i) * hb - 1, 0), cb + c)),
                pl.BlockSpec((HALO, tc), lambda b, i, c: (jnp.minimum((b * nblk + i + 1) * hb, last_halo), cb + c)))

    specs = [part_specs(part) for part in range(3)]
    out = jax.ShapeDtypeStruct((batch, DFT_N2, half, B_WIDTH), F32)
    o_spec = pl.BlockSpec((None, DFT_N2, tm // DFT_N2, tc), lambda b, i, c: (b, 0, i, c))
    w3 = short_w.astype(F32).reshape(3, 3, B_WIDTH).transpose(1, 0, 2)
    b3 = short_b.astype(F32).reshape(3, 1, B_WIDTH)
    return pl.pallas_call(
        functools.partial(_hyena_gate_kernel, nblk=nblk, tm=tm),
        grid=(batch, nblk, nc),
        in_specs=[sp[0] for sp in specs] + [sp[1] for sp in specs] + [sp[2] for sp in specs]
        + [pl.BlockSpec((3, 3, tc), lambda b, i, c: (0, 0, c)),
           pl.BlockSpec((3, 1, tc), lambda b, i, c: (0, 0, c)),
           pl.BlockSpec((tm, tc), lambda b, i, c: (b * nblk + i, g_col // tc + c))],
        out_specs=[o_spec, o_spec],
        out_shape=[out, out],
        compiler_params=_params(("parallel", "parallel", "parallel")),
        name="hyena_gate",
    )(*([p] * 9), w3, b3, p)


def _hdot(a, b):
    return jnp.dot(a, b, preferred_element_type=F32, precision=lax.Precision.HIGHEST)


def _filter_kernel(fr_ref, w1_ref, b1_ref, fq_ref, w2_ref, b2_ref, w3_ref, dec_ref, k_ref, s_ref, t_ref, h_ref, *,
                   seq, tr):
    i, c = pl.program_id(0), pl.program_id(1)
    r = i * tr + lax.broadcasted_iota(jnp.int32, (tr, 1), 0)

    @pl.when(c == 0)
    def _():
        pos = jnp.where(r < seq, r, 2 * seq - 1 - r).astype(F32)
        t = pos * (1.0 / (seq - 1))
        w = pos * (2.0 * math.pi / seq)
        lane = lax.broadcasted_iota(jnp.int32, (tr, LANES), 1)
        bands = (HY_EMB - 1) // 2
        ang = w * fr_ref[...]
        z = jnp.where(lane == 0, t,
                      jnp.where(lane <= bands, jnp.cos(ang),
                                jnp.where(lane <= 2 * bands, -jnp.sin(ang), 0.0)))
        h = jnp.sin(fq_ref[0:1] * (_hdot(z, w1_ref[...]) + b1_ref[...]))
        h_ref[...] = jnp.sin(fq_ref[1:2] * (_hdot(h, w2_ref[...]) + b2_ref[...]))
        t_ref[...] = jnp.broadcast_to(t, t_ref.shape)

    h = _hdot(h_ref[...], w3_ref[...]) * jnp.exp(-t_ref[:, 0:1] * jnp.abs(dec_ref[...]))
    h = jnp.where(r == seq, 0.0, h)
    for a in range(tr // DFT_N2):
        k_ref[:, a, :] = h[a * DFT_N2:(a + 1) * DFT_N2]

    @pl.when(jnp.logical_and(i == 0, c == 0))
    def _():
        s_ref[...] = jnp.zeros(s_ref.shape, F32)

    tc = h.shape[1]
    col_sum = jnp.sum(jnp.abs(h), axis=0, keepdims=True)
    for cc in range(s_ref.shape[1] // tc):
        @pl.when(c == cc)
        def _(cc=cc):
            s_ref[:, cc * tc:(cc + 1) * tc] += col_sum


def hyena_filter(seq, w1, b1, freq, w2, b2, w3, decay, *, tc=512):
    tr = FILT_ROWS
    assert seq % tr == 0
    nhalf = seq // tr
    n1 = 2 * seq // DFT_N2
    tc = _tile(B_WIDTH, tc)
    nc = B_WIDTH // tc
    bands = (HY_EMB - 1) // 2
    fr = jnp.linspace(1e-4, bands - 1, bands, dtype=F32)
    fr_lanes = jnp.concatenate([jnp.zeros((1,), F32), fr, fr, jnp.zeros((LANES - HY_EMB,), F32)]).reshape(1, LANES)
    w1p = jnp.pad(w1.astype(F32), ((0, LANES - HY_EMB), (0, 0)))
    const = lambda i, c: (0, 0)
    half = lambda i, c: (0, (i // nhalf) * nc + c)
    return pl.pallas_call(
        functools.partial(_filter_kernel, seq=seq, tr=tr),
        grid=(2 * nhalf, nc),
        in_specs=[pl.BlockSpec((1, LANES), const),
                  pl.BlockSpec((LANES, HY_FFN), const),
                  pl.BlockSpec((1, HY_FFN), const),
                  pl.BlockSpec((2, HY_FFN), const),
                  pl.BlockSpec((HY_FFN, HY_FFN), const),
                  pl.BlockSpec((1, HY_FFN), const),
                  pl.BlockSpec((HY_FFN, tc), half),
                  pl.BlockSpec((1, tc), half)],
        out_specs=[pl.BlockSpec((DFT_N2, tr // DFT_N2, tc), lambda i, c: (0, i, c)),
                   pl.BlockSpec((1, B_WIDTH), const)],
        out_shape=[jax.ShapeDtypeStruct((DFT_N2, n1, B_WIDTH), F32),
                   jax.ShapeDtypeStruct((1, B_WIDTH), F32)],
        scratch_shapes=[pltpu.VMEM((tr, LANES), F32), pltpu.VMEM((tr, HY_FFN), F32)],
        compiler_params=_params(("arbitrary", "arbitrary")),
        name="hyena_filter",
    )(fr_lanes, w1p, b1.reshape(1, HY_FFN).astype(F32), freq.astype(F32), w2.astype(F32),
      b2.reshape(1, HY_FFN).astype(F32), w3.astype(F32), decay.reshape(1, 2 * B_WIDTH).astype(F32))


def _bdot(a, b):
    return jnp.dot(a, b, preferred_element_type=F32)


def dft_tables(n1):
    n = n1 * DFT_N2
    rows = _half_rows(n1)
    kk = jnp.arange(rows, dtype=jnp.int32)
    k1 = kk[None, :, None]
    m1 = jnp.arange(n1, dtype=jnp.int32)[None, None, :]
    n2 = jnp.arange(DFT_N2, dtype=jnp.int32)[:, None, None]
    ang = ((k1 * (DFT_N2 * m1 + n2)) % n).astype(F32) * (2.0 * math.pi / n)
    live = (kk <= n1 // 2).astype(F32)[None, :, None]
    weight = jnp.where((kk == 0) | (kk == n1 // 2), 1.0, 2.0)[None, :, None] * live
    cos, sin = jnp.cos(ang), jnp.sin(ang)
    gc = (cos * live).astype(BF16)
    gsn = (-sin * live).astype(BF16)
    hc = jnp.swapaxes(cos * weight, 1, 2)[:, :n1 // 2].astype(BF16)
    hsn = jnp.swapaxes(-sin * weight, 1, 2)[:, :n1 // 2].astype(BF16)
    a = jnp.arange(DFT_N2, dtype=jnp.int32)
    ang2 = ((a[:, None] * a[None, :]) % DFT_N2).astype(F32) * (2.0 * math.pi / DFT_N2)
    ff = jnp.concatenate([jnp.cos(ang2), jnp.sin(ang2)], axis=0).astype(BF16)
    return gc, gsn, hc, hsn, ff


def _dft1_kernel(x_ref, gc_ref, gs_ref, ar_ref, ai_ref):
    for j in range(SUB):
        x = x_ref[j].astype(BF16)
        ar_ref[:, j, :] = _bdot(gc_ref[j], x)
        ai_ref[:, j, :] = _bdot(gs_ref[j], x)


def dft_stage1(x, gc, gsn, *, tc=512):
    batch, _, k1rows, c = x.shape
    rows = gc.shape[1]
    tc = _tile(c, tc)
    out = jax.ShapeDtypeStruct((batch, rows, DFT_N2, c), F32)
    g_spec = pl.BlockSpec((SUB, rows, k1rows), lambda g, b, ci: (g, 0, 0))
    o_spec = pl.BlockSpec((None, rows, SUB, tc), lambda g, b, ci: (b, 0, g, ci))
    return pl.pallas_call(
        _dft1_kernel,
        grid=(DFT_N2 // SUB, batch, c // tc),
        in_specs=[pl.BlockSpec((None, SUB, k1rows, tc), lambda g, b, ci: (b, g, 0, ci)), g_spec, g_spec],
        out_specs=[o_spec, o_spec],
        out_shape=[out, out],
        compiler_params=_params(("parallel", "parallel", "parallel")),
        name="dft_stage1",
    )(x, gc, gsn)


def _stage2(ff, ar, ai):
    p = _bdot(ff, ar)
    q = _bdot(ff, ai)
    return p[:DFT_N2] + q[DFT_N2:], q[:DFT_N2] - p[DFT_N2:]


def _filter_spectrum_kernel(ar_ref, ai_ref, s_ref, ff_ref, kr_ref, ki_ref, *, n):
    ff = ff_ref[...]
    scale = 1.0 / (s_ref[...] * n)
    for j in range(SUB):
        xr, xi = _stage2(ff, ar_ref[j].astype(BF16), ai_ref[j].astype(BF16))
        kr_ref[j] = xr * scale
        ki_ref[j] = xi * scale


def filter_spectrum(ar, ai, asum, ff, n, *, tc=512):
    rows, _, c = ar.shape
    tc = _tile(c, tc)
    a_spec = pl.BlockSpec((SUB, DFT_N2, tc), lambda k, ci: (k, 0, ci))
    out = jax.ShapeDtypeStruct((rows, DFT_N2, c), F32)
    return pl.pallas_call(
        functools.partial(_filter_spectrum_kernel, n=n),
        grid=(rows // SUB, c // tc),
        in_specs=[a_spec, a_spec, pl.BlockSpec((1, tc), lambda k, ci: (0, ci)),
                  pl.BlockSpec((2 * DFT_N2, DFT_N2), lambda k, ci: (0, 0))],
        out_specs=[a_spec, a_spec],
        out_shape=[out, out],
        compiler_params=_params(("parallel", "parallel")),
        name="filter_spectrum",
    )(ar, ai, asum, ff)


def _spectral_kernel(ar_ref, ai_ref, kr_ref, ki_ref, ff_ref, br_ref, bi_ref):
    ff = ff_ref[...]
    for j in range(SUB):
        xr, xi = _stage2(ff, ar_ref[j].astype(BF16), ai_ref[j].astype(BF16))
        kr, ki = kr_ref[j], ki_ref[j]
        yr = (xr * kr - xi * ki).astype(BF16)
        yi = (xr * ki + xi * kr).astype(BF16)
        r = _bdot(ff, yr)
        s = _bdot(ff, yi)
        br_ref[:, j, :] = r[:DFT_N2] - s[DFT_N2:]
        bi_ref[:, j, :] = s[:DFT_N2] + r[DFT_N2:]


def spectral_multiply(ar, ai, kr, ki, ff, *, tc=512):
    batch, rows, _, c = ar.shape
    tc = _tile(c, tc)
    a_spec = pl.BlockSpec((None, SUB, DFT_N2, tc), lambda k, ci, b: (b, k, 0, ci))
    k_spec = pl.BlockSpec((SUB, DFT_N2, tc), lambda k, ci, b: (k, 0, ci))
    o_spec = pl.BlockSpec((None, DFT_N2, SUB, tc), lambda k, ci, b: (b, 0, k, ci))
    out = jax.ShapeDtypeStruct((batch, DFT_N2, rows, c), F32)
    return pl.pallas_call(
        _spectral_kernel,
        grid=(rows // SUB, c // tc, batch),
        in_specs=[a_spec, a_spec, k_spec, k_spec, pl.BlockSpec((2 * DFT_N2, DFT_N2), lambda k, ci, b: (0, 0))],
        out_specs=[o_spec, o_spec],
        out_shape=[out, out],
        compiler_params=_params(("parallel", "parallel", "parallel")),
        name="spectral_multiply",
    )(ar, ai, kr, ki, ff)


def _idft_out_kernel(br_ref, bi_ref, hc_ref, hs_ref, zz_ref, x0_ref, bias_ref, o_ref):
    bias = bias_ref[...]
    for j in range(SUB):
        y = _bdot(hc_ref[j], br_ref[j].astype(BF16)) + _bdot(hs_ref[j], bi_ref[j].astype(BF16))
        o_ref[:, j, :] = (y + zz_ref[j] * bias) * x0_ref[j]


def idft_output(br, bi, hc, hsn, zz, x0, bias, *, tc=512):
    batch, _, rows, c = br.shape
    half = hc.shape[1]
    tc = _tile(c, tc)
    b_spec = pl.BlockSpec((None, SUB, rows, tc), lambda g, b, ci: (b, g, 0, ci))
    h_spec = pl.BlockSpec((SUB, half, rows), lambda g, b, ci: (g, 0, 0))
    z_spec = pl.BlockSpec((None, SUB, half, tc), lambda g, b, ci: (b, g, 0, ci))
    return pl.pallas_call(
        _idft_out_kernel,
        grid=(DFT_N2 // SUB, batch, c // tc),
        in_specs=[b_spec, b_spec, h_spec, h_spec, z_spec, z_spec, pl.BlockSpec((1, tc), lambda g, b, ci: (0, ci))],
        out_specs=pl.BlockSpec((half, SUB, tc), lambda g, b, ci: (b, g, ci)),
        out_shape=jax.ShapeDtypeStruct((batch * half, DFT_N2, c), F32),
        compiler_params=_params(("parallel", "parallel", "parallel")),
        name="idft_output",
    )(br, bi, hc, hsn, zz, x0, bias)


def hyena_mixer(p, u_col, g_col, prm, e, batch, seq):
    n1 = 2 * seq // DFT_N2
    gc, gsn, hc, hsn, ff = dft_tables(n1)
    kern, asum = hyena_filter(seq, prm["hy_w1"][e], prm["hy_b1"][e], prm["hy_freq"][e], prm["hy_w2"][e],
                              prm["hy_b2"][e], prm["hy_w3"][e], prm["hy_decay"][e])
    far, fai = dft_stage1(kern[None], gc, gsn)
    kr, ki = filter_spectrum(far[0], fai[0], asum, ff, n1 * DFT_N2)
    zz, x0 = hyena_gate(p, u_col, g_col, prm["hy_short_w"][e], prm["hy_short_b"][e], batch, seq)
    ar, ai = dft_stage1(zz, gc[:, :, :n1 // 2], gsn[:, :, :n1 // 2])
    br, bi = spectral_multiply(ar, ai, kr, ki, ff)
    yb = idft_output(br, bi, hc, hsn, zz, x0, prm["hy_bias"][e].reshape(1, B_WIDTH).astype(F32))
    return yb.reshape(batch * seq, B_WIDTH)


def _outproj_kernel(y1_ref, y2_ref, y3_ref, x_ref, w_ref, o_ref):
    acc = x_ref[...]
    off = 0
    for y_ref in (y1_ref, y2_ref, y3_ref):
        wd = y_ref.shape[1]
        acc = acc + jnp.dot(y_ref[...].astype(BF16), w_ref[off:off + wd, :], preferred_element_type=F32)
        off += wd
    o_ref[...] = acc


def out_projection(y1, y2, y3, x, w, *, tm=512):
    rows, d = x.shape
    tm = _tile(rows, tm)
    row = lambda i: (i, 0)
    return pl.pallas_call(
        _outproj_kernel,
        grid=(rows // tm,),
        in_specs=[pl.BlockSpec((tm, y1.shape[1]), row),
                  pl.BlockSpec((tm, y2.shape[1]), row),
                  pl.BlockSpec((tm, y3.shape[1]), row),
                  pl.BlockSpec((tm, d), row),
                  pl.BlockSpec((MIX_WIDTH, d), lambda i: (0, 0), pipeline_mode=pl.Buffered(1))],
        out_specs=pl.BlockSpec((tm, d), row),
        out_shape=jax.ShapeDtypeStruct((rows, d), F32),
        compiler_params=_params(("parallel",)),
        name="out_projection",
    )(y1, y2, y3, x, w)


def _split_cols(w, sizes):
    out, off = [], 0
    for s in sizes:
        out.append(w[:, off:off + s])
        off += s
    return out


ROT_HALF = ROT_DIM // 2
HEAD_PERM = (tuple(range(ROT_HALF)) + tuple(range(ROT_DIM, LANES // 2 + ROT_HALF))
             + tuple(range(ROT_HALF, ROT_DIM)) + tuple(range(LANES // 2 + ROT_HALF, LANES)))


def _permute_heads(w, nheads):
    k = w.shape[0]
    return w.reshape(k, nheads, HEAD_DIM)[:, :, jnp.array(HEAD_PERM)].reshape(k, nheads * HEAD_DIM)


def _rope_slot(w):
    half = D_ROPE // 2
    z = jnp.zeros(w.shape[:-1] + (LANES // 2 - half,), w.dtype)
    return jnp.concatenate([w[..., :half], z, w[..., half:], z], axis=-1)


def _rope_tables(seq, dim, passthrough):
    half = dim // 2
    inv = ROPE_THETA ** (-jnp.arange(0, dim, 2, dtype=F32) / dim)
    ang = jnp.arange(seq, dtype=F32)[:, None] * inv[None, :]
    cos, sin = jnp.cos(ang), jnp.sin(ang)
    rest = jnp.full((seq, LANES // 2 - half), 1.0 if passthrough else 0.0, F32)
    zeros = jnp.zeros((seq, LANES // 2 - half), F32)
    cf = jnp.concatenate([cos, rest, cos, rest], axis=1)
    sn = jnp.concatenate([-sin, zeros, sin, zeros], axis=1)
    return cf, sn


def _memory_kv(mem, prm, l):
    batch = mem.shape[0]
    return normed_matmul(mem.reshape(batch * N_MEM, D_MODEL), prm["mem_norm_gain"][l],
                         prm["w_mem_kv"][l].astype(BF16), seq=N_MEM)


def _head_gains(*groups):
    rows = [jnp.broadcast_to(g[jnp.array(HEAD_PERM)][None, :], (n, HEAD_DIM)) for g, n in groups]
    return jnp.concatenate(rows, axis=0)


def _even_layer(x, mem, prm, l, e, batch, seq, rope_a):
    aq, ak, av, ag, bu, bg, mq, mg = _split_cols(prm["w_in_even"][e], EVEN_SPLITS)
    w_in = jnp.concatenate([ag, bg, mg, mq, bu, _permute_heads(aq, A_HEADS), _permute_heads(ak, A_KV_HEADS), av],
                           axis=1).astype(BF16)
    tn = _col_tile(w_in.shape[1], 1536)
    assert EV_AQ % tn == 0 and w_in.shape[1] - EV_AQ == tn
    slots = [HEAD_DIM ** -0.5] * A_HEADS + [1.0] * A_KV_HEADS + [None] * A_KV_HEADS
    p = normed_matmul(x, prm["norm_gain"][l], w_in, seq=seq, tn=tn, preps={EV_AQ // tn: (0, slots)},
                      head_gains=_head_gains((prm["a_q_norm"][e], A_HEADS), (prm["a_k_norm"][e], A_KV_HEADS)),
                      tables=rope_a)
    ya = window_attention(p, EV_AQ, EV_AK, EV_AV, EV_GATE, prm["a_sink"][e], batch, seq)
    yb = hyena_mixer(p, EV_BU, EV_GATE + A_WIDTH, prm, e, batch, seq)
    kvm = _memory_kv(mem, prm, l)
    ym = memory_attention(p, EV_MQ, EV_GATE + A_WIDTH + B_WIDTH, kvm, prm["mem_q_norm"][l], prm["mem_k_norm"][l],
                          batch, seq)
    return out_projection(ya, yb, ym, x, prm["w_out_even"][e].astype(BF16))


def _odd_layer(x, mem, prm, l, o, batch, seq, rope_a, rope_d):
    cq, ck, cv, cg, dqa, dkva, dg, mq, mg = _split_cols(prm["w_in_odd"][o], ODD_SPLITS)
    pad = jnp.zeros((D_MODEL, OD_WIDTH - (OD_KR + LANES)), F32)
    w_in = jnp.concatenate([cg, dg, mg, mq, _permute_heads(cq, 2 * C_HEADS), _permute_heads(ck, 2 * C_HEADS), cv, dqa,
                            dkva[:, :D_KV_RANK], _rope_slot(dkva[:, D_KV_RANK:]), pad], axis=1).astype(BF16)
    tn = _col_tile(w_in.shape[1], 1024)
    assert OD_CQ % tn == 0 and OD_CK - OD_CQ == tn and OD_CV - OD_CK == tn
    nslot = 2 * C_HEADS
    p = normed_matmul(x, prm["norm_gain"][l], w_in, seq=seq, tn=tn,
                      preps={OD_CQ // tn: (0, [C_QK_DIM ** -0.5 * LOG2E] * nslot), OD_CK // tn: (nslot, [1.0] * nslot)},
                      head_gains=_head_gains((prm["c_q_norm"][o], nslot), (prm["c_k_norm"][o], nslot)),
                      tables=rope_a)
    lam_init = 0.8 - 0.6 * math.exp(-0.3 * l)
    yc = diff_attention(p, OD_CQ, OD_CK, OD_CV, OD_GATE, prm["c_lambda"][o], prm["c_out_norm"][o], lam_init, batch, seq)
    wq = prm["w_q_b"][o].reshape(D_Q_RANK, D_HEADS, D_NOPE + D_ROPE)
    wq2 = jnp.concatenate([wq[:, :, :D_NOPE].reshape(D_Q_RANK, -1), _rope_slot(wq[:, :, D_NOPE:]).reshape(D_Q_RANK, -1)],
                          axis=1)
    wkv = prm["w_kv_b"][o].reshape(D_KV_RANK, D_HEADS, D_NOPE + D_V)
    wkv2 = jnp.concatenate([wkv[:, :, :D_NOPE].reshape(D_KV_RANK, -1), wkv[:, :, D_NOPE:].reshape(D_KV_RANK, -1)],
                           axis=1)
    dscale = (D_NOPE + D_ROPE) ** -0.5
    qd, = mla_projection(p, OD_DQA, D_Q_RANK, prm["d_q_a_norm"][o], wq2.astype(BF16), prm["d_q_norm"][o], rope_d,
                         dscale * LOG2E, seq)
    kd, vd = mla_projection(p, OD_CKV, D_KV_RANK, prm["d_kv_a_norm"][o], wkv2.astype(BF16), prm["d_k_norm"][o], rope_d,
                            1.0, seq, rope_col=OD_KR)
    yd = mla_attention(qd, kd, vd, p, OD_GATE + C_WIDTH, batch, seq)
    kvm = _memory_kv(mem, prm, l)
    ym = memory_attention(p, OD_MQ, OD_GATE + C_WIDTH + D_WIDTH, kvm, prm["mem_q_norm"][l], prm["mem_k_norm"][l],
                          batch, seq)
    return out_projection(yc, yd, ym, x, prm["w_out_odd"][o].astype(BF16))


def _trunk(x, mem, prm):
    batch, seq, d = x.shape
    depth = prm["norm_gain"].shape[0]
    rope_a = _rope_tables(seq, ROT_DIM, True)
    rope_d = _rope_tables(seq, D_ROPE, False)
    h = x.reshape(batch * seq, d)
    for l in range(depth):
        if l % 2 == 0:
            h = _even_layer(h, mem, prm, l, l // 2, batch, seq, rope_a)
        else:
            h = _odd_layer(h, mem, prm, l, l // 2, batch, seq, rope_a, rope_d)
    return h.reshape(batch, seq, d)


def kernel(x_prompt, x_sample, mem_prompt, mem_sample, norm_gain, mem_norm_gain, w_mem_kv, mem_q_norm, mem_k_norm, w_in_even, w_out_even, a_q_norm, a_k_norm, a_sink, hy_short_w, hy_short_b, hy_w1, hy_b1, hy_freq, hy_w2, hy_b2, hy_w3, hy_decay, hy_bias, w_in_odd, w_out_odd, c_q_norm, c_k_norm, c_lambda, c_out_norm, d_q_a_norm, w_q_b, d_kv_a_norm, w_kv_b, d_q_norm, d_k_norm):
    prm = dict(norm_gain=norm_gain, mem_norm_gain=mem_norm_gain, w_mem_kv=w_mem_kv,
               mem_q_norm=mem_q_norm, mem_k_norm=mem_k_norm, w_in_even=w_in_even,
               w_out_even=w_out_even, a_q_norm=a_q_norm, a_k_norm=a_k_norm, a_sink=a_sink,
               hy_short_w=hy_short_w, hy_short_b=hy_short_b, hy_w1=hy_w1, hy_b1=hy_b1,
               hy_freq=hy_freq, hy_w2=hy_w2, hy_b2=hy_b2, hy_w3=hy_w3, hy_decay=hy_decay,
               hy_bias=hy_bias, w_in_odd=w_in_odd, w_out_odd=w_out_odd, c_q_norm=c_q_norm,
               c_k_norm=c_k_norm, c_lambda=c_lambda, c_out_norm=c_out_norm, d_q_a_norm=d_q_a_norm,
               w_q_b=w_q_b, d_kv_a_norm=d_kv_a_norm, w_kv_b=w_kv_b, d_q_norm=d_q_norm,
               d_k_norm=d_k_norm)
    return (_trunk(x_prompt, mem_prompt, prm), _trunk(x_sample, mem_sample, prm))
```

```python
import functools
import math

import jax
import jax.numpy as jnp
from jax import lax
from jax.experimental import pallas as pl
from jax.experimental.pallas import tpu as pltpu

F32 = jnp.float32
BF16 = jnp.bfloat16

D_MODEL = 2048
N_MEM = 256
HEAD_DIM = 128
ROPE_THETA = 500000.0
ROT_DIM = HEAD_DIM // 4
EPS = 1e-6
BLOCK = 128
WINDOW = 128
NEG_INF = -1e30
A_HEADS = 8
A_KV_HEADS = 2
A_WIDTH = A_HEADS * HEAD_DIM
B_WIDTH = 1024
HY_EMB = 33
HY_FFN = 64
M_HEADS = 4
M_WIDTH = M_HEADS * HEAD_DIM
C_HEADS = 4
C_QK_DIM = 128
C_V_DIM = 256
C_WIDTH = C_HEADS * C_V_DIM
D_HEADS = 8
D_NOPE = 128
D_ROPE = 64
D_V = 128
D_Q_RANK = 512
D_KV_RANK = 256
D_WIDTH = D_HEADS * D_V
MIX_WIDTH = 2560
EVEN_SPLITS = (A_WIDTH, A_KV_HEADS * HEAD_DIM, A_KV_HEADS * HEAD_DIM, A_WIDTH,
               3 * B_WIDTH, B_WIDTH, M_WIDTH, M_WIDTH)
ODD_SPLITS = (C_HEADS * 2 * C_QK_DIM, C_HEADS * 2 * C_QK_DIM, C_WIDTH, C_WIDTH,
              D_Q_RANK, D_KV_RANK + D_ROPE, D_WIDTH, M_WIDTH, M_WIDTH)

LANES = 128
SUBLANES = 8
MXU_COLS = 256
DFT_N2 = 128
VMEM_LIMIT = 56 * 1024 * 1024

EV_GATE, EV_MQ, EV_BU, EV_AQ, EV_AK, EV_AV = 0, 2560, 3072, 6144, 7168, 7424
OD_GATE, OD_MQ, OD_CQ, OD_CK, OD_CV, OD_DQA, OD_CKV, OD_KR = 0, 2560, 3072, 4096, 5120, 6144, 6656, 6912
OD_WIDTH = 7168


def _tile(n, pref):
    t = min(n, pref)
    assert n % t == 0, (n, t)
    return t


def _col_tile(n, cap):
    best = None
    for t in range(MXU_COLS, min(n, cap) + 1, MXU_COLS):
        if n % t == 0:
            best = t
    assert best is not None, (n, cap)
    return best


def _params(sem):
    return pltpu.CompilerParams(dimension_semantics=sem, vmem_limit_bytes=VMEM_LIMIT)


def _rms(x, gain):
    ms = jnp.mean(x * x, axis=-1, keepdims=True)
    return x * lax.rsqrt(ms + EPS) * gain


def _silu(g):
    g = g.astype(F32)
    return g * jax.nn.sigmoid(g)


def _lanes(x, k):
    return x if k == 1 else jnp.concatenate([x] * k, axis=1)


def _rope(y, cf, sn):
    return y * cf + pltpu.roll(y, LANES // 2, 1) * sn


def _nmm_kernel(x_ref, g_ref, w_ref, hg_ref, cf_ref, sn_ref, o_ref, h_ref, acc_ref, *, preps):
    j = pl.program_id(1)

    @pl.when(j == 0)
    def _():
        h_ref[...] = _rms(x_ref[...].astype(F32), g_ref[...]).astype(BF16)

    def product():
        return jnp.dot(h_ref[...], w_ref[...], preferred_element_type=F32)

    plain = None
    for blk in preps:
        plain = (j != blk) if plain is None else jnp.logical_and(plain, j != blk)

    if plain is None:
        o_ref[...] = product().astype(o_ref.dtype)
        return

    @pl.when(plain)
    def _():
        o_ref[...] = product().astype(o_ref.dtype)

    for blk, (gain_row0, scales) in preps.items():
        @pl.when(j == blk)
        def _(gain_row0=gain_row0, scales=scales):
            acc_ref[...] = product()

            @pl.when(pl.program_id(0) >= 0)
            def _():
                cf, sn = cf_ref[...], sn_ref[...]
                for slot, scale in enumerate(scales):
                    cols = slice(slot * LANES, (slot + 1) * LANES)
                    y = acc_ref[:, cols]
                    if scale is not None:
                        y = _rope(_rms(y, hg_ref[gain_row0 + slot:gain_row0 + slot + 1, :]), cf, sn) * scale
                    o_ref[:, cols] = y.astype(o_ref.dtype)


def normed_matmul(x, gain, w, *, seq, col_block=0, tm=1024, tn=1536, preps=None, head_gains=None, tables=None):
    rows = x.shape[0]
    k, n = w.shape
    tm = _tile(seq, tm)
    tn = _col_tile(n, tn)
    if preps is None:
        preps, head_gains = {}, jnp.ones((8, LANES), F32)
        tables = (jnp.ones((tm, LANES), F32), jnp.zeros((tm, LANES), F32))
    tab_spec = pl.BlockSpec((tm, LANES), lambda i, j: (i % (tables[0].shape[0] // tm), 0))
    return pl.pallas_call(
        functools.partial(_nmm_kernel, preps=preps),
        grid=(rows // tm, n // tn),
        in_specs=[pl.BlockSpec((tm, k), lambda i, j: (i, col_block)),
                  pl.BlockSpec((1, k), lambda i, j: (0, 0)),
                  pl.BlockSpec((k, tn), lambda i, j: (0, j)),
                  pl.BlockSpec(head_gains.shape, lambda i, j: (0, 0)),
                  tab_spec, tab_spec],
        out_specs=pl.BlockSpec((tm, tn), lambda i, j: (i, j)),
        out_shape=jax.ShapeDtypeStruct((rows, n), BF16),
        scratch_shapes=[pltpu.VMEM((tm, k), BF16), pltpu.VMEM((tm, tn) if preps else (SUBLANES, LANES), F32)],
        compiler_params=_params(("parallel", "arbitrary")),
        name="normed_matmul",
    )(x, gain.reshape(1, k).astype(F32), w, head_gains.astype(F32), *tables)


WIN_QB = 4


def _window_kernel(sink_ref, q_ref, kp_ref, kc_ref, kn_ref, vp_ref, vc_ref, vn_ref, g_ref, o_ref, s_ref, p_ref, *,
                   seq, qb):
    n = pl.program_id(1)
    group = A_HEADS // A_KV_HEADS
    rows = group * BLOCK
    qi = lax.broadcasted_iota(jnp.int32, (rows, 3 * BLOCK), 0) & (BLOCK - 1)
    kj = lax.broadcasted_iota(jnp.int32, (rows, 3 * BLOCK), 1)
    in_win = jnp.abs(kj - BLOCK - qi) <= WINDOW
    pairs = [(kv, i) for kv in range(A_KV_HEADS) for i in range(qb)]

    def kv_rows(refs, kv):
        cols = slice(kv * HEAD_DIM, (kv + 1) * HEAD_DIM)
        return jnp.concatenate([r[:, cols] for r in refs], axis=0).astype(BF16)

    kfull = [kv_rows((kp_ref, kc_ref, kn_ref), kv) for kv in range(A_KV_HEADS)]
    for idx, (kv, i) in enumerate(pairs):
        qrows = slice(i * BLOCK, (i + 1) * BLOCK)
        q = jnp.concatenate([q_ref[qrows, h * HEAD_DIM:(h + 1) * HEAD_DIM]
                             for h in range(kv * group, (kv + 1) * group)], axis=0)
        s_ref[idx] = lax.dot_general(q, kfull[kv][i * BLOCK:(i + 3) * BLOCK], (((1,), (1,)), ((), ())),
                                     preferred_element_type=F32)

    for idx, (kv, i) in enumerate(pairs):
        sk = jnp.concatenate([jnp.full((BLOCK, 1), sink_ref[h], F32)
                              for h in range(kv * group, (kv + 1) * group)], axis=0)
        kpos = (n * qb + i - 1) * BLOCK + kj
        s = jnp.where(in_win & (kpos >= 0) & (kpos < seq), s_ref[idx], NEG_INF)
        m = jnp.maximum(jnp.max(s, axis=-1, keepdims=True), sk)
        pr = jnp.exp(s - m)
        denom = jnp.sum(pr, axis=-1, keepdims=True) + jnp.exp(sk - m)
        p_ref[idx] = (pr / denom).astype(BF16)

    vfull = [kv_rows((vp_ref, vc_ref, vn_ref), kv) for kv in range(A_KV_HEADS)]
    for idx, (kv, i) in enumerate(pairs):
        qrows = slice(i * BLOCK, (i + 1) * BLOCK)
        o = jnp.dot(p_ref[idx], vfull[kv][i * BLOCK:(i + 3) * BLOCK], preferred_element_type=F32)
        for g, h in enumerate(range(kv * group, (kv + 1) * group)):
            hc = slice(h * HEAD_DIM, (h + 1) * HEAD_DIM)
            o_ref[qrows, hc] = (o[g * BLOCK:(g + 1) * BLOCK] * _silu(g_ref[qrows, hc])).astype(o_ref.dtype)


def window_attention(p, q_col, k_col, v_col, g_col, sink, batch, seq):
    nblk = seq // BLOCK
    qb = _tile(nblk, WIN_QB)
    nstep = nblk // qb
    kvw = A_KV_HEADS * HEAD_DIM
    kb, vb = k_col // kvw, v_col // kvw

    def prev(b, n):
        return b * nblk + jnp.maximum(n * qb - 1, 0)

    def cur(b, n):
        return b * nstep + n

    def nxt(b, n):
        return b * nblk + jnp.minimum((n + 1) * qb, nblk - 1)

    return pl.pallas_call(
        functools.partial(_window_kernel, seq=seq, qb=qb),
        grid=(batch, nstep),
        in_specs=[pl.BlockSpec(memory_space=pltpu.SMEM),
                  pl.BlockSpec((qb * BLOCK, A_WIDTH), lambda b, n: (cur(b, n), q_col // A_WIDTH)),
                  pl.BlockSpec((BLOCK, kvw), lambda b, n: (prev(b, n), kb)),
                  pl.BlockSpec((qb * BLOCK, kvw), lambda b, n: (cur(b, n), kb)),
                  pl.BlockSpec((BLOCK, kvw), lambda b, n: (nxt(b, n), kb)),
                  pl.BlockSpec((BLOCK, kvw), lambda b, n: (prev(b, n), vb)),
                  pl.BlockSpec((qb * BLOCK, kvw), lambda b, n: (cur(b, n), vb)),
                  pl.BlockSpec((BLOCK, kvw), lambda b, n: (nxt(b, n), vb)),
                  pl.BlockSpec((qb * BLOCK, A_WIDTH), lambda b, n: (cur(b, n), g_col // A_WIDTH))],
        out_specs=pl.BlockSpec((qb * BLOCK, A_WIDTH), lambda b, n: (cur(b, n), 0)),
        out_shape=jax.ShapeDtypeStruct((batch * seq, A_WIDTH), BF16),
        scratch_shapes=[pltpu.VMEM((A_KV_HEADS * qb, A_HEADS // A_KV_HEADS * BLOCK, 3 * BLOCK), F32),
                        pltpu.VMEM((A_KV_HEADS * qb, A_HEADS // A_KV_HEADS * BLOCK, 3 * BLOCK), BF16)],
        compiler_params=_params(("parallel", "parallel")),
        name="window_attention",
    )(sink.astype(F32), p, p, p, p, p, p, p, p)


def _mem_kernel(q_ref, kv_ref, qg_ref, kg_ref, g_ref, o_ref):
    qg, kg = qg_ref[...], kg_ref[...]
    scale = HEAD_DIM ** -0.5
    for h in range(M_HEADS):
        cols = slice(h * HEAD_DIM, (h + 1) * HEAD_DIM)
        q = (_rms(q_ref[:, cols].astype(F32), qg) * scale).astype(BF16)
        k = _rms(kv_ref[:, cols].astype(F32), kg).astype(BF16)
        v = kv_ref[:, M_WIDTH + h * HEAD_DIM:M_WIDTH + (h + 1) * HEAD_DIM].astype(BF16)
        s = lax.dot_general(q, k, (((1,), (1,)), ((), ())), preferred_element_type=F32)
        m = jnp.max(s, axis=-1, keepdims=True)
        pr = jnp.exp(s - m)
        pr = pr / jnp.sum(pr, axis=-1, keepdims=True)
        o = jnp.dot(pr.astype(BF16), v, preferred_element_type=F32)
        o_ref[:, cols] = (o * _silu(g_ref[:, cols])).astype(o_ref.dtype)


def memory_attention(p, q_col, g_col, kvm, q_gain, k_gain, batch, seq, *, tq=512):
    tq = _tile(seq, tq)
    nq = seq // tq
    return pl.pallas_call(
        _mem_kernel,
        grid=(batch, nq),
        in_specs=[pl.BlockSpec((tq, M_WIDTH), lambda b, i: (b * nq + i, q_col // M_WIDTH)),
                  pl.BlockSpec((N_MEM, 2 * M_WIDTH), lambda b, i: (b, 0)),
                  pl.BlockSpec((1, HEAD_DIM), lambda b, i: (0, 0)),
                  pl.BlockSpec((1, HEAD_DIM), lambda b, i: (0, 0)),
                  pl.BlockSpec((tq, M_WIDTH), lambda b, i: (b * nq + i, g_col // M_WIDTH))],
        out_specs=pl.BlockSpec((tq, M_WIDTH), lambda b, i: (b * nq + i, 0)),
        out_shape=jax.ShapeDtypeStruct((batch * seq, M_WIDTH), BF16),
        compiler_params=_params(("parallel", "parallel")),
        name="memory_attention",
    )(p, kvm, q_gain.reshape(1, HEAD_DIM).astype(F32), k_gain.reshape(1, HEAD_DIM).astype(F32), p)


LOG2E = math.log2(math.e)


def _chunk_rows(chunk, tk):
    return pl.ds(pl.multiple_of(chunk * tk, tk), tk)


def _pipelined_chunks(scores, consume, nk):
    scores(0, 0)

    def body(i, carry):
        scores(2 * i + 1, 1)
        consume(2 * i, 0)
        scores(jnp.minimum(2 * i + 2, nk - 1), 0)
        consume(2 * i + 1, 1)
        return carry

    lax.fori_loop(0, nk // 2, body, 0)


def _diff_flash_kernel(q_ref, k_ref, v_ref, lam_ref, g_ref, gate_ref, o_ref, m_ref, l_ref, acc_ref, s_ref, *,
                       tk, nk, lam_init):
    m_ref[...] = jnp.full(m_ref.shape, NEG_INF, F32)
    l_ref[...] = jnp.zeros(l_ref.shape, F32)
    acc_ref[...] = jnp.zeros(acc_ref.shape, F32)

    def scores(chunk, c):
        cols = slice(c * C_QK_DIM, (c + 1) * C_QK_DIM)
        s_ref[c] = lax.dot_general(q_ref[:, cols], k_ref[_chunk_rows(chunk, tk), cols], (((1,), (1,)), ((), ())),
                                   preferred_element_type=F32)

    def consume(chunk, c):
        s = s_ref[c]
        m_prev = m_ref[c]
        m_new = jnp.maximum(m_prev, jnp.max(s, axis=-1, keepdims=True))
        alpha = jnp.exp2(m_prev - m_new)
        pr = jnp.exp2(s - _lanes(m_new, tk // LANES))
        l_ref[c] = alpha * l_ref[c] + jnp.sum(pr, axis=-1, keepdims=True)
        acc_ref[c] = (_lanes(alpha, C_V_DIM // LANES) * acc_ref[c]
                      + jnp.dot(pr.astype(BF16), v_ref[_chunk_rows(chunk, tk), :], preferred_element_type=F32))
        m_ref[c] = m_new

    scores(0, 0)

    def body(i, carry):
        for j in (2 * i, 2 * i + 1):
            scores(j, 1)
            consume(j, 0)
            scores(jnp.minimum(j + 1, nk - 1), 0)
            consume(j, 1)
        return carry

    lax.fori_loop(0, nk // 2, body, 0)
    lv = lam_ref[...]
    lam = (jnp.exp(jnp.sum(lv[0:1] * lv[1:2], axis=-1, keepdims=True))
           - jnp.exp(jnp.sum(lv[2:3] * lv[3:4], axis=-1, keepdims=True)) + lam_init)
    o1 = acc_ref[0] / _lanes(l_ref[0], C_V_DIM // LANES)
    o2 = acc_ref[1] / _lanes(l_ref[1], C_V_DIM // LANES)
    o_ref[...] = (_rms(o1 - lam * o2, g_ref[...]) * (1.0 - lam_init) * _silu(gate_ref[...])).astype(o_ref.dtype)


def _mla_flash_kernel(q_ref, k_ref, v_ref, gate_ref, o_ref, m_ref, acc_ref, s_ref, *, tk, nk):
    m_ref[...] = jnp.full(m_ref.shape, NEG_INF, F32)
    acc_ref[...] = jnp.zeros(acc_ref.shape, F32)
    ones = jnp.ones((tk, LANES), BF16)

    def scores(chunk, slot):
        s_ref[slot] = lax.dot_general(q_ref[...], k_ref[_chunk_rows(chunk, tk), :], (((1,), (1,)), ((), ())),
                                      preferred_element_type=F32)

    def consume(chunk, slot):
        s = s_ref[slot]
        m_prev = m_ref[...]
        m_new = jnp.maximum(m_prev, jnp.max(s, axis=-1, keepdims=True))
        alpha = jnp.exp2(m_prev - m_new)
        pr = jnp.exp2(s - _lanes(m_new, tk // LANES))
        vv = jnp.concatenate([v_ref[_chunk_rows(chunk, tk), :], ones], axis=1)
        acc_ref[...] = (_lanes(alpha, 2) * acc_ref[...]
                        + jnp.dot(pr.astype(BF16), vv, preferred_element_type=F32))
        m_ref[...] = m_new

    _pipelined_chunks(scores, consume, nk)
    o_ref[...] = (acc_ref[:, :D_V] / acc_ref[:, D_V:] * _silu(gate_ref[...])).astype(o_ref.dtype)


def _flash_call(kernel, q, q_block0, k, k_block0, v, v_block0, gate, gate_block0, extra, extra_specs, scratch, *,
                heads, dv, batch, seq, tq, tk, kv_buffers, name):
    nq, nk = seq // tq, seq // tk
    qk_w = 2 * LANES
    resident = pl.Buffered(kv_buffers)
    return pl.pallas_call(
        functools.partial(kernel, tk=tk, nk=nk),
        grid=(batch, heads, nq),
        in_specs=[pl.BlockSpec((tq, qk_w), lambda b, h, i: (b * nq + i, q_block0 + h)),
                  pl.BlockSpec((seq, qk_w), lambda b, h, i: (b, k_block0 + h), pipeline_mode=resident),
                  pl.BlockSpec((seq, dv), lambda b, h, i: (b, v_block0 + h), pipeline_mode=resident)]
        + extra_specs + [pl.BlockSpec((tq, dv), lambda b, h, i: (b * nq + i, gate_block0 + h))],
        out_specs=pl.BlockSpec((tq, dv), lambda b, h, i: (b * nq + i, h)),
        out_shape=jax.ShapeDtypeStruct((batch * seq, heads * dv), BF16),
        scratch_shapes=scratch,
        compiler_params=_params(("parallel", "parallel", "arbitrary")),
        name=name,
    )(q, k, v, *extra, gate)


def diff_attention(p, q_col, k_col, v_col, g_col, c_lambda, out_gain, lam_init, batch, seq, *, tq=1024, tk=1024):
    tq, tk = _tile(seq, tq), _tile(seq // 2, tk)
    const = lambda b, h, i: (0, 0)
    return _flash_call(
        functools.partial(_diff_flash_kernel, lam_init=lam_init), p, q_col // C_V_DIM, p, k_col // C_V_DIM,
        p, v_col // C_V_DIM, p, g_col // C_V_DIM,
        [c_lambda.astype(F32), out_gain.reshape(1, C_V_DIM).astype(F32)],
        [pl.BlockSpec((4, C_QK_DIM), const), pl.BlockSpec((1, C_V_DIM), const)],
        [pltpu.VMEM((2, tq, LANES), F32), pltpu.VMEM((2, tq, LANES), F32), pltpu.VMEM((2, tq, C_V_DIM), F32),
         pltpu.VMEM((2, tq, tk), F32)],
        heads=C_HEADS, dv=C_V_DIM, batch=batch, seq=seq, tq=tq, tk=tk, kv_buffers=1, name="diff_attention")


def mla_attention(q, k, v, p, g_col, batch, seq, *, tq=2048, tk=512):
    tq, tk = _tile(seq, tq), _tile(seq // 2, tk)
    return _flash_call(_mla_flash_kernel, q, 0, k, 0, v, 0, p, g_col // D_V, [], [],
                       [pltpu.VMEM((tq, LANES), F32), pltpu.VMEM((tq, D_V + LANES), F32),
                        pltpu.VMEM((2, tq, tk), F32)],
                       heads=D_HEADS, dv=D_V, batch=batch, seq=seq, tq=tq, tk=tk, kv_buffers=2,
                       name="mla_attention")


def _mla_proj_kernel(*refs, shared_rope, scale):
    if shared_rope:
        x_ref, g_ref, w_ref, kr_ref, gn_ref, gr_ref, cf_ref, sn_ref, o_ref, v_ref = refs
    else:
        x_ref, g_ref, w_ref, gn_ref, gr_ref, cf_ref, sn_ref, o_ref = refs
    h = _rms(x_ref[...].astype(F32), g_ref[...]).astype(BF16)
    acc = jnp.dot(h, w_ref[...], preferred_element_type=F32)
    gn, gr = gn_ref[...], gr_ref[...]
    cf, sn = cf_ref[...], sn_ref[...]
    width = D_HEADS * LANES
    for hd in range(D_HEADS):
        cols = slice(hd * LANES, (hd + 1) * LANES)
        xn = acc[:, cols]
        xr = kr_ref[...].astype(F32) if shared_rope else acc[:, width + hd * LANES:width + (hd + 1) * LANES]
        ms = (jnp.sum(xn * xn, axis=-1, keepdims=True)
              + jnp.sum(xr * xr, axis=-1, keepdims=True)) * (1.0 / (D_NOPE + D_ROPE))
        inv = lax.rsqrt(ms + EPS)
        o_ref[:, 2 * hd * LANES:(2 * hd + 1) * LANES] = (xn * inv * gn * scale).astype(o_ref.dtype)
        o_ref[:, (2 * hd + 1) * LANES:(2 * hd + 2) * LANES] = (_rope(xr * inv * gr, cf, sn) * scale).astype(o_ref.dtype)
    if shared_rope:
        v_ref[...] = acc[:, width:].astype(v_ref.dtype)


def mla_projection(p, col, rank, gain, w, head_gain, tables, scale, seq, *, rope_col=None, tm=512):
    rows = p.shape[0]
    width = D_HEADS * LANES
    tm = _tile(seq, tm)
    nblk = seq // tm
    shared_rope = rope_col is not None
    gn = head_gain[:D_NOPE].reshape(1, LANES).astype(F32)
    gr = _rope_slot(head_gain[D_NOPE:]).reshape(1, LANES).astype(F32)
    one = pl.BlockSpec((1, LANES), lambda i: (0, 0))
    tab_spec = pl.BlockSpec((tm, LANES), lambda i: (i % nblk, 0))
    in_specs = [pl.BlockSpec((tm, rank), lambda i: (i, col // rank)),
                pl.BlockSpec((1, rank), lambda i: (0, 0)),
                pl.BlockSpec((rank, 2 * width), lambda i: (0, 0))]
    args = [p, gain.reshape(1, rank).astype(F32), w]
    out_specs = [pl.BlockSpec((tm, 2 * width), lambda i: (i, 0))]
    out_shape = [jax.ShapeDtypeStruct((rows, 2 * width), BF16)]
    if shared_rope:
        in_specs.append(pl.BlockSpec((tm, LANES), lambda i: (i, rope_col // LANES)))
        args.append(p)
        out_specs.append(pl.BlockSpec((tm, width), lambda i: (i, 0)))
        out_shape.append(jax.ShapeDtypeStruct((rows, width), BF16))
    return pl.pallas_call(
        functools.partial(_mla_proj_kernel, shared_rope=shared_rope, scale=scale),
        grid=(rows // tm,),
        in_specs=in_specs + [one, one, tab_spec, tab_spec],
        out_specs=out_specs,
        out_shape=out_shape,
        compiler_params=_params(("parallel",)),
        name="mla_projection",
    )(*args, gn, gr, *tables)


SUB = SUBLANES
FILT_ROWS = SUB * DFT_N2
HALO = 16


def _half_rows(n1):
    return n1 // 2 + SUB


def _hyena_gate_kernel(*refs, nblk, tm):
    u_refs, up_refs, un_refs = refs[0:3], refs[3:6], refs[6:9]
    w_ref, b_ref, g_ref, zz_ref, x0_ref = refs[9:]
    li = pl.program_id(1)
    tc = zz_ref.shape[-1]
    row = lax.broadcasted_iota(jnp.int32, (tm, tc), 0)
    has_prev = (li > 0).astype(F32)
    has_next = (li < nblk - 1).astype(F32)

    def conv(part):
        u = u_refs[part][...].astype(F32)
        prev_row = up_refs[part][HALO - 1:HALO, :].astype(F32) * has_prev
        next_row = un_refs[part][0:1, :].astype(F32) * has_next
        above = jnp.where(row == 0, prev_row, pltpu.roll(u, 1, 0))
        below = jnp.where(row == tm - 1, next_row, pltpu.roll(u, tm - 1, 0))
        w = w_ref[part]
        return above * w[0:1] + u * w[1:2] + below * w[2:3] + b_ref[part]

    x0 = conv(0) * _silu(g_ref[...])
    zz = conv(2) * conv(1)
    for a in range(tm // DFT_N2):
        rows = slice(a * DFT_N2, (a + 1) * DFT_N2)
        zz_ref[:, a, :] = zz[rows]
        x0_ref[:, a, :] = x0[rows]


def hyena_gate(p, u_col, g_col, short_w, short_b, batch, seq, *, tc=512):
    tm = FILT_ROWS
    nblk = seq // tm
    half = seq // DFT_N2
    tc = _tile(B_WIDTH, tc)
    nc = B_WIDTH // tc
    hb = tm // HALO
    last_halo = batch * nblk * hb - 1

    def part_specs(part):
        cb = (u_col + part * B_WIDTH) // tc
        return (pl.BlockSpec((tm, tc), lambda b, i, c: (b * nblk + i, cb + c)),
                pl.BlockSpec((HALO, tc), lambda b, i, c: (jnp.maximum((b * nblk + i) * hb - 1, 0), cb + c)),
                pl.BlockSpec((HALO, tc), lambda b, i, c: (jnp.minimum((b * nblk + i + 1) * hb, last_halo), cb + c)))

    specs = [part_specs(part) for part in range(3)]
    out = jax.ShapeDtypeStruct((batch, DFT_N2, half, B_WIDTH), F32)
    o_spec = pl.BlockSpec((None, DFT_N2, tm // DFT_N2, tc), lambda b, i, c: (b, 0, i, c))
    w3 = short_w.astype(F32).reshape(3, 3, B_WIDTH).transpose(1, 0, 2)
    b3 = short_b.astype(F32).reshape(3, 1, B_WIDTH)
    return pl.pallas_call(
        functools.partial(_hyena_gate_kernel, nblk=nblk, tm=tm),
        grid=(batch, nblk, nc),
        in_specs=[sp[0] for sp in specs] + [sp[1] for sp in specs] + [sp[2] for sp in specs]
        + [pl.BlockSpec((3, 3, tc), lambda b, i, c: (0, 0, c)),
           pl.BlockSpec((3, 1, tc), lambda b, i, c: (0, 0, c)),
           pl.BlockSpec((tm, tc), lambda b, i, c: (b * nblk + i, g_col // tc + c))],
        out_specs=[o_spec, o_spec],
        out_shape=[out, out],
        compiler_params=_params(("parallel", "parallel", "parallel")),
        name="hyena_gate",
    )(*([p] * 9), w3, b3, p)


def _hdot(a, b):
    return jnp.dot(a, b, preferred_element_type=F32, precision=lax.Precision.HIGHEST)


def _filter_kernel(fr_ref, w1_ref, b1_ref, fq_ref, w2_ref, b2_ref, w3_ref, dec_ref, k_ref, s_ref, t_ref, h_ref, *,
                   seq, tr):
    i, c = pl.program_id(0), pl.program_id(1)
    r = i * tr + lax.broadcasted_iota(jnp.int32, (tr, 1), 0)

    @pl.when(c == 0)
    def _():
        pos = jnp.where(r < seq, r, 2 * seq - 1 - r).astype(F32)
        t = pos * (1.0 / (seq - 1))
        w = pos * (2.0 * math.pi / seq)
        lane = lax.broadcasted_iota(jnp.int32, (tr, LANES), 1)
        bands = (HY_EMB - 1) // 2
        z = jnp.where(lane == 0, t, jnp.where(lane <= 2 * bands, jnp.cos(w * fr_ref[0:1] + fr_ref[1:2]), 0.0))
        h = jnp.sin(fq_ref[0:1] * (_hdot(z, w1_ref[...]) + b1_ref[...]))
        h_ref[...] = jnp.sin(fq_ref[1:2] * (_hdot(h, w2_ref[...]) + b2_ref[...]))
        t_ref[...] = jnp.broadcast_to(t, t_ref.shape)

    h = _hdot(h_ref[...], w3_ref[...]) * jnp.exp(-t_ref[:, 0:1] * jnp.abs(dec_ref[...]))
    h = jnp.where(r == seq, 0.0, h)
    for a in range(tr // DFT_N2):
        k_ref[:, a, :] = h[a * DFT_N2:(a + 1) * DFT_N2]

    @pl.when(jnp.logical_and(i == 0, c == 0))
    def _():
        s_ref[...] = jnp.zeros(s_ref.shape, F32)

    tc = h.shape[1]
    col_sum = jnp.sum(jnp.abs(h), axis=0, keepdims=True)
    for cc in range(s_ref.shape[1] // tc):
        @pl.when(c == cc)
        def _(cc=cc):
            s_ref[:, cc * tc:(cc + 1) * tc] += col_sum


def hyena_filter(seq, w1, b1, freq, w2, b2, w3, decay, *, tc=512):
    tr = FILT_ROWS
    assert seq % tr == 0
    nhalf = seq // tr
    n1 = 2 * seq // DFT_N2
    tc = _tile(B_WIDTH, tc)
    nc = B_WIDTH // tc
    bands = (HY_EMB - 1) // 2
    fr = jnp.linspace(1e-4, bands - 1, bands, dtype=F32)
    pad = jnp.zeros((LANES - HY_EMB,), F32)
    fr_lanes = jnp.stack([jnp.concatenate([jnp.zeros((1,), F32), fr, fr, pad]),
                          jnp.concatenate([jnp.zeros((1 + bands,), F32), jnp.full((bands,), 0.5 * math.pi, F32), pad])])
    w1p = jnp.pad(w1.astype(F32), ((0, LANES - HY_EMB), (0, 0)))
    const = lambda i, c: (0, 0)
    half = lambda i, c: (0, (i // nhalf) * nc + c)
    return pl.pallas_call(
        functools.partial(_filter_kernel, seq=seq, tr=tr),
        grid=(2 * nhalf, nc),
        in_specs=[pl.BlockSpec((2, LANES), const),
                  pl.BlockSpec((LANES, HY_FFN), const),
                  pl.BlockSpec((1, HY_FFN), const),
                  pl.BlockSpec((2, HY_FFN), const),
                  pl.BlockSpec((HY_FFN, HY_FFN), const),
                  pl.BlockSpec((1, HY_FFN), const),
                  pl.BlockSpec((HY_FFN, tc), half),
                  pl.BlockSpec((1, tc), half)],
        out_specs=[pl.BlockSpec((DFT_N2, tr // DFT_N2, tc), lambda i, c: (0, i, c)),
                   pl.BlockSpec((1, B_WIDTH), const)],
        out_shape=[jax.ShapeDtypeStruct((DFT_N2, n1, B_WIDTH), F32),
                   jax.ShapeDtypeStruct((1, B_WIDTH), F32)],
        scratch_shapes=[pltpu.VMEM((tr, LANES), F32), pltpu.VMEM((tr, HY_FFN), F32)],
        compiler_params=_params(("arbitrary", "arbitrary")),
        name="hyena_filter",
    )(fr_lanes, w1p, b1.reshape(1, HY_FFN).astype(F32), freq.astype(F32), w2.astype(F32),
      b2.reshape(1, HY_FFN).astype(F32), w3.astype(F32), decay.reshape(1, 2 * B_WIDTH).astype(F32))


def _bdot(a, b):
    return jnp.dot(a, b, preferred_element_type=F32)


def dft_tables(n1):
    n = n1 * DFT_N2
    rows = _half_rows(n1)
    kk = jnp.arange(rows, dtype=jnp.int32)
    k1 = kk[None, :, None]
    m1 = jnp.arange(n1, dtype=jnp.int32)[None, None, :]
    n2 = jnp.arange(DFT_N2, dtype=jnp.int32)[:, None, None]
    ang = ((k1 * (DFT_N2 * m1 + n2)) % n).astype(F32) * (2.0 * math.pi / n)
    live = (kk <= n1 // 2).astype(F32)[None, :, None]
    weight = jnp.where((kk == 0) | (kk == n1 // 2), 1.0, 2.0)[None, :, None] * live
    cos, sin = jnp.cos(ang), jnp.sin(ang)
    g = jnp.concatenate([cos * live, -sin * live], axis=1).astype(BF16)
    hc = jnp.swapaxes(cos * weight, 1, 2)[:, :n1 // 2].astype(BF16)
    hsn = jnp.swapaxes(-sin * weight, 1, 2)[:, :n1 // 2].astype(BF16)
    a = jnp.arange(DFT_N2, dtype=jnp.int32)
    ang2 = ((a[:, None] * a[None, :]) % DFT_N2).astype(F32) * (2.0 * math.pi / DFT_N2)
    ff = jnp.concatenate([jnp.cos(ang2), jnp.sin(ang2)], axis=0).astype(BF16)
    return g, hc, hsn, ff


def _dft1_kernel(x_ref, g_ref, ar_ref, ai_ref):
    rows = ar_ref.shape[0]
    for j in range(SUB):
        a = _bdot(g_ref[j], x_ref[j].astype(BF16))
        ar_ref[:, j, :] = a[:rows]
        ai_ref[:, j, :] = a[rows:]


def dft_stage1(x, g, *, tc=512):
    batch, _, k1rows, c = x.shape
    rows = g.shape[1] // 2
    tc = _tile(c, tc)
    out = jax.ShapeDtypeStruct((batch, rows, DFT_N2, c), F32)
    g_spec = pl.BlockSpec((SUB, 2 * rows, k1rows), lambda g, b, ci: (g, 0, 0))
    o_spec = pl.BlockSpec((None, rows, SUB, tc), lambda g, b, ci: (b, 0, g, ci))
    return pl.pallas_call(
        _dft1_kernel,
        grid=(DFT_N2 // SUB, batch, c // tc),
        in_specs=[pl.BlockSpec((None, SUB, k1rows, tc), lambda g, b, ci: (b, g, 0, ci)), g_spec],
        out_specs=[o_spec, o_spec],
        out_shape=[out, out],
        compiler_params=_params(("parallel", "parallel", "parallel")),
        name="dft_stage1",
    )(x, g)


def _stage2(ff, ar, ai):
    p = _bdot(ff, ar)
    q = _bdot(ff, ai)
    return p[:DFT_N2] + q[DFT_N2:], q[:DFT_N2] - p[DFT_N2:]


def _filter_spectrum_kernel(ar_ref, ai_ref, s_ref, ff_ref, kr_ref, ki_ref, *, n):
    ff = ff_ref[...]
    scale = 1.0 / (s_ref[...] * n)
    for j in range(SUB):
        xr, xi = _stage2(ff, ar_ref[j].astype(BF16), ai_ref[j].astype(BF16))
        kr_ref[j] = xr * scale
        ki_ref[j] = xi * scale


def filter_spectrum(ar, ai, asum, ff, n, *, tc=512):
    rows, _, c = ar.shape
    tc = _tile(c, tc)
    a_spec = pl.BlockSpec((SUB, DFT_N2, tc), lambda k, ci: (k, 0, ci))
    out = jax.ShapeDtypeStruct((rows, DFT_N2, c), F32)
    return pl.pallas_call(
        functools.partial(_filter_spectrum_kernel, n=n),
        grid=(rows // SUB, c // tc),
        in_specs=[a_spec, a_spec, pl.BlockSpec((1, tc), lambda k, ci: (0, ci)),
                  pl.BlockSpec((2 * DFT_N2, DFT_N2), lambda k, ci: (0, 0))],
        out_specs=[a_spec, a_spec],
        out_shape=[out, out],
        compiler_params=_params(("parallel", "parallel")),
        name="filter_spectrum",
    )(ar, ai, asum, ff)


def _spectral_kernel(ar_ref, ai_ref, kr_ref, ki_ref, ff_ref, br_ref, bi_ref):
    ff = ff_ref[...]
    for j in range(SUB):
        xr, xi = _stage2(ff, ar_ref[j].astype(BF16), ai_ref[j].astype(BF16))
        kr, ki = kr_ref[j], ki_ref[j]
        yr = (xr * kr - xi * ki).astype(BF16)
        yi = (xr * ki + xi * kr).astype(BF16)
        r = _bdot(ff, yr)
        s = _bdot(ff, yi)
        br_ref[:, j, :] = r[:DFT_N2] - s[DFT_N2:]
        bi_ref[:, j, :] = s[:DFT_N2] + r[DFT_N2:]


def spectral_multiply(ar, ai, kr, ki, ff, *, tc=512):
    batch, rows, _, c = ar.shape
    tc = _tile(c, tc)
    a_spec = pl.BlockSpec((None, SUB, DFT_N2, tc), lambda k, ci, b: (b, k, 0, ci))
    k_spec = pl.BlockSpec((SUB, DFT_N2, tc), lambda k, ci, b: (k, 0, ci))
    o_spec = pl.BlockSpec((None, DFT_N2, SUB, tc), lambda k, ci, b: (b, 0, k, ci))
    out = jax.ShapeDtypeStruct((batch, DFT_N2, rows, c), F32)
    return pl.pallas_call(
        _spectral_kernel,
        grid=(rows // SUB, c // tc, batch),
        in_specs=[a_spec, a_spec, k_spec, k_spec, pl.BlockSpec((2 * DFT_N2, DFT_N2), lambda k, ci, b: (0, 0))],
        out_specs=[o_spec, o_spec],
        out_shape=[out, out],
        compiler_params=_params(("parallel", "parallel", "parallel")),
        name="spectral_multiply",
    )(ar, ai, kr, ki, ff)


def _idft_out_kernel(br_ref, bi_ref, hc_ref, hs_ref, zz_ref, x0_ref, bias_ref, o_ref):
    bias = bias_ref[...]
    for j in range(SUB):
        y = _bdot(hc_ref[j], br_ref[j].astype(BF16)) + _bdot(hs_ref[j], bi_ref[j].astype(BF16))
        o_ref[:, j, :] = (y + zz_ref[j] * bias) * x0_ref[j]


def idft_output(br, bi, hc, hsn, zz, x0, bias, *, tc=512):
    batch, _, rows, c = br.shape
    half = hc.shape[1]
    tc = _tile(c, tc)
    b_spec = pl.BlockSpec((None, SUB, rows, tc), lambda g, b, ci: (b, g, 0, ci))
    h_spec = pl.BlockSpec((SUB, half, rows), lambda g, b, ci: (g, 0, 0))
    z_spec = pl.BlockSpec((None, SUB, half, tc), lambda g, b, ci: (b, g, 0, ci))
    return pl.pallas_call(
        _idft_out_kernel,
        grid=(DFT_N2 // SUB, batch, c // tc),
        in_specs=[b_spec, b_spec, h_spec, h_spec, z_spec, z_spec, pl.BlockSpec((1, tc), lambda g, b, ci: (0, ci))],
        out_specs=pl.BlockSpec((half, SUB, tc), lambda g, b, ci: (b, g, ci)),
        out_shape=jax.ShapeDtypeStruct((batch * half, DFT_N2, c), F32),
        compiler_params=_params(("parallel", "parallel", "parallel")),
        name="idft_output",
    )(br, bi, hc, hsn, zz, x0, bias)


def hyena_mixer(p, u_col, g_col, prm, e, batch, seq):
    n1 = 2 * seq // DFT_N2
    g, hc, hsn, ff = dft_tables(n1)
    kern, asum = hyena_filter(seq, prm["hy_w1"][e], prm["hy_b1"][e], prm["hy_freq"][e], prm["hy_w2"][e],
                              prm["hy_b2"][e], prm["hy_w3"][e], prm["hy_decay"][e])
    far, fai = dft_stage1(kern[None], g)
    kr, ki = filter_spectrum(far[0], fai[0], asum, ff, n1 * DFT_N2)
    zz, x0 = hyena_gate(p, u_col, g_col, prm["hy_short_w"][e], prm["hy_short_b"][e], batch, seq)
    ar, ai = dft_stage1(zz, g[:, :, :n1 // 2])
    br, bi = spectral_multiply(ar, ai, kr, ki, ff)
    yb = idft_output(br, bi, hc, hsn, zz, x0, prm["hy_bias"][e].reshape(1, B_WIDTH).astype(F32))
    return yb.reshape(batch * seq, B_WIDTH)


def _outproj_kernel(y1_ref, y2_ref, y3_ref, x_ref, w_ref, o_ref):
    acc = x_ref[...]
    off = 0
    for y_ref in (y1_ref, y2_ref, y3_ref):
        wd = y_ref.shape[1]
        acc = acc + jnp.dot(y_ref[...].astype(BF16), w_ref[off:off + wd, :], preferred_element_type=F32)
        off += wd
    o_ref[...] = acc


def out_projection(y1, y2, y3, x, w, *, tm=512):
    rows, d = x.shape
    tm = _tile(rows, tm)
    row = lambda i: (i, 0)
    return pl.pallas_call(
        _outproj_kernel,
        grid=(rows // tm,),
        in_specs=[pl.BlockSpec((tm, y1.shape[1]), row),
                  pl.BlockSpec((tm, y2.shape[1]), row),
                  pl.BlockSpec((tm, y3.shape[1]), row),
                  pl.BlockSpec((tm, d), row),
                  pl.BlockSpec((MIX_WIDTH, d), lambda i: (0, 0), pipeline_mode=pl.Buffered(1))],
        out_specs=pl.BlockSpec((tm, d), row),
        out_shape=jax.ShapeDtypeStruct((rows, d), F32),
        compiler_params=_params(("parallel",)),
        name="out_projection",
    )(y1, y2, y3, x, w)


def _split_cols(w, sizes):
    out, off = [], 0
    for s in sizes:
        out.append(w[:, off:off + s])
        off += s
    return out


ROT_HALF = ROT_DIM // 2
HEAD_PERM = (tuple(range(ROT_HALF)) + tuple(range(ROT_DIM, LANES // 2 + ROT_HALF))
             + tuple(range(ROT_HALF, ROT_DIM)) + tuple(range(LANES // 2 + ROT_HALF, LANES)))


def _permute_heads(w, nheads):
    k = w.shape[0]
    return w.reshape(k, nheads, HEAD_DIM)[:, :, jnp.array(HEAD_PERM)].reshape(k, nheads * HEAD_DIM)


def _rope_slot(w):
    half = D_ROPE // 2
    z = jnp.zeros(w.shape[:-1] + (LANES // 2 - half,), w.dtype)
    return jnp.concatenate([w[..., :half], z, w[..., half:], z], axis=-1)


def _rope_tables(seq, dim, passthrough):
    half = dim // 2
    inv = ROPE_THETA ** (-jnp.arange(0, dim, 2, dtype=F32) / dim)
    ang = jnp.arange(seq, dtype=F32)[:, None] * inv[None, :]
    cos, sin = jnp.cos(ang), jnp.sin(ang)
    rest = jnp.full((seq, LANES // 2 - half), 1.0 if passthrough else 0.0, F32)
    zeros = jnp.zeros((seq, LANES // 2 - half), F32)
    cf = jnp.concatenate([cos, rest, cos, rest], axis=1)
    sn = jnp.concatenate([-sin, zeros, sin, zeros], axis=1)
    return cf, sn


def _memory_kv(mem, prm, l):
    batch = mem.shape[0]
    return normed_matmul(mem.reshape(batch * N_MEM, D_MODEL), prm["mem_norm_gain"][l],
                         prm["w_mem_kv"][l].astype(BF16), seq=N_MEM)


def _head_gains(*groups):
    rows = [jnp.broadcast_to(g[jnp.array(HEAD_PERM)][None, :], (n, HEAD_DIM)) for g, n in groups]
    return jnp.concatenate(rows, axis=0)


def _even_layer(x, mem, prm, l, e, batch, seq, rope_a):
    aq, ak, av, ag, bu, bg, mq, mg = _split_cols(prm["w_in_even"][e], EVEN_SPLITS)
    w_in = jnp.concatenate([ag, bg, mg, mq, bu, _permute_heads(aq, A_HEADS), _permute_heads(ak, A_KV_HEADS), av],
                           axis=1).astype(BF16)
    tn = _col_tile(w_in.shape[1], 1536)
    assert EV_AQ % tn == 0 and w_in.shape[1] - EV_AQ == tn
    slots = [HEAD_DIM ** -0.5] * A_HEADS + [1.0] * A_KV_HEADS + [None] * A_KV_HEADS
    p = normed_matmul(x, prm["norm_gain"][l], w_in, seq=seq, tn=tn, preps={EV_AQ // tn: (0, slots)},
                      head_gains=_head_gains((prm["a_q_norm"][e], A_HEADS), (prm["a_k_norm"][e], A_KV_HEADS)),
                      tables=rope_a)
    ya = window_attention(p, EV_AQ, EV_AK, EV_AV, EV_GATE, prm["a_sink"][e], batch, seq)
    yb = hyena_mixer(p, EV_BU, EV_GATE + A_WIDTH, prm, e, batch, seq)
    kvm = _memory_kv(mem, prm, l)
    ym = memory_attention(p, EV_MQ, EV_GATE + A_WIDTH + B_WIDTH, kvm, prm["mem_q_norm"][l], prm["mem_k_norm"][l],
                          batch, seq)
    return out_projection(ya, yb, ym, x, prm["w_out_even"][e].astype(BF16))


def _odd_layer(x, mem, prm, l, o, batch, seq, rope_a, rope_d):
    cq, ck, cv, cg, dqa, dkva, dg, mq, mg = _split_cols(prm["w_in_odd"][o], ODD_SPLITS)
    pad = jnp.zeros((D_MODEL, OD_WIDTH - (OD_KR + LANES)), F32)
    w_in = jnp.concatenate([cg, dg, mg, mq, _permute_heads(cq, 2 * C_HEADS), _permute_heads(ck, 2 * C_HEADS), cv, dqa,
                            dkva[:, :D_KV_RANK], _rope_slot(dkva[:, D_KV_RANK:]), pad], axis=1).astype(BF16)
    tn = _col_tile(w_in.shape[1], 1024)
    assert OD_CQ % tn == 0 and OD_CK - OD_CQ == tn and OD_CV - OD_CK == tn
    nslot = 2 * C_HEADS
    p = normed_matmul(x, prm["norm_gain"][l], w_in, seq=seq, tn=tn,
                      preps={OD_CQ // tn: (0, [C_QK_DIM ** -0.5 * LOG2E] * nslot), OD_CK // tn: (nslot, [1.0] * nslot)},
                      head_gains=_head_gains((prm["c_q_norm"][o], nslot), (prm["c_k_norm"][o], nslot)),
                      tables=rope_a)
    lam_init = 0.8 - 0.6 * math.exp(-0.3 * l)
    yc = diff_attention(p, OD_CQ, OD_CK, OD_CV, OD_GATE, prm["c_lambda"][o], prm["c_out_norm"][o], lam_init, batch, seq)
    wq = prm["w_q_b"][o].reshape(D_Q_RANK, D_HEADS, D_NOPE + D_ROPE)
    wq2 = jnp.concatenate([wq[:, :, :D_NOPE].reshape(D_Q_RANK, -1), _rope_slot(wq[:, :, D_NOPE:]).reshape(D_Q_RANK, -1)],
                          axis=1)
    wkv = prm["w_kv_b"][o].reshape(D_KV_RANK, D_HEADS, D_NOPE + D_V)
    wkv2 = jnp.concatenate([wkv[:, :, :D_NOPE].reshape(D_KV_RANK, -1), wkv[:, :, D_NOPE:].reshape(D_KV_RANK, -1)],
                           axis=1)
    dscale = (D_NOPE + D_ROPE) ** -0.5
    qd, = mla_projection(p, OD_DQA, D_Q_RANK, prm["d_q_a_norm"][o], wq2.astype(BF16), prm["d_q_norm"][o], rope_d,
                         dscale * LOG2E, seq)
    kd, vd = mla_projection(p, OD_CKV, D_KV_RANK, prm["d_kv_a_norm"][o], wkv2.astype(BF16), prm["d_k_norm"][o], rope_d,
                            1.0, seq, rope_col=OD_KR)
    yd = mla_attention(qd, kd, vd, p, OD_GATE + C_WIDTH, batch, seq)
    kvm = _memory_kv(mem, prm, l)
    ym = memory_attention(p, OD_MQ, OD_GATE + C_WIDTH + D_WIDTH, kvm, prm["mem_q_norm"][l], prm["mem_k_norm"][l],
                          batch, seq)
    return out_projection(yc, yd, ym, x, prm["w_out_odd"][o].astype(BF16))


def _trunk(x, mem, prm):
    batch, seq, d = x.shape
    depth = prm["norm_gain"].shape[0]
    rope_a = _rope_tables(seq, ROT_DIM, True)
    rope_d = _rope_tables(seq, D_ROPE, False)
    h = x.reshape(batch * seq, d)
    for l in range(depth):
        if l % 2 == 0:
            h = _even_layer(h, mem, prm, l, l // 2, batch, seq, rope_a)
        else:
            h = _odd_layer(h, mem, prm, l, l // 2, batch, seq, rope_a, rope_d)
    return h.reshape(batch, seq, d)


def kernel(x_prompt, x_sample, mem_prompt, mem_sample, norm_gain, mem_norm_gain, w_mem_kv, mem_q_norm, mem_k_norm, w_in_even, w_out_even, a_q_norm, a_k_norm, a_sink, hy_short_w, hy_short_b, hy_w1, hy_b1, hy_freq, hy_w2, hy_b2, hy_w3, hy_decay, hy_bias, w_in_odd, w_out_odd, c_q_norm, c_k_norm, c_lambda, c_out_norm, d_q_a_norm, w_q_b, d_kv_a_norm, w_kv_b, d_q_norm, d_k_norm):
    prm = dict(norm_gain=norm_gain, mem_norm_gain=mem_norm_gain, w_mem_kv=w_mem_kv,
               mem_q_norm=mem_q_norm, mem_k_norm=mem_k_norm, w_in_even=w_in_even,
               w_out_even=w_out_even, a_q_norm=a_q_norm, a_k_norm=a_k_norm, a_sink=a_sink,
               hy_short_w=hy_short_w, hy_short_b=hy_short_b, hy_w1=hy_w1, hy_b1=hy_b1,
               hy_freq=hy_freq, hy_w2=hy_w2, hy_b2=hy_b2, hy_w3=hy_w3, hy_decay=hy_decay,
               hy_bias=hy_bias, w_in_odd=w_in_odd, w_out_odd=w_out_odd, c_q_norm=c_q_norm,
               c_k_norm=c_k_norm, c_lambda=c_lambda, c_out_norm=c_out_norm, d_q_a_norm=d_q_a_norm,
               w_q_b=w_q_b, d_kv_a_norm=d_kv_a_norm, w_kv_b=w_kv_b, d_q_norm=d_q_norm,
               d_k_norm=d_k_norm)
    return (_trunk(x_prompt, mem_prompt, prm), _trunk(x_sample, mem_sample, prm))
```

```python
import functools
import math

import jax
import jax.numpy as jnp
from jax import lax
from jax.experimental import pallas as pl
from jax.experimental.pallas import tpu as pltpu

F32 = jnp.float32
BF16 = jnp.bfloat16

D_MODEL = 2048
N_MEM = 256
HEAD_DIM = 128
ROPE_THETA = 500000.0
ROT_DIM = HEAD_DIM // 4
EPS = 1e-6
BLOCK = 128
WINDOW = 128
NEG_INF = -1e30
A_HEADS = 8
A_KV_HEADS = 2
A_WIDTH = A_HEADS * HEAD_DIM
B_WIDTH = 1024
HY_EMB = 33
HY_FFN = 64
M_HEADS = 4
M_WIDTH = M_HEADS * HEAD_DIM
C_HEADS = 4
C_QK_DIM = 128
C_V_DIM = 256
C_WIDTH = C_HEADS * C_V_DIM
D_HEADS = 8
D_NOPE = 128
D_ROPE = 64
D_V = 128
D_Q_RANK = 512
D_KV_RANK = 256
D_WIDTH = D_HEADS * D_V
MIX_WIDTH = 2560
EVEN_SPLITS = (A_WIDTH, A_KV_HEADS * HEAD_DIM, A_KV_HEADS * HEAD_DIM, A_WIDTH,
               3 * B_WIDTH, B_WIDTH, M_WIDTH, M_WIDTH)
ODD_SPLITS = (C_HEADS * 2 * C_QK_DIM, C_HEADS * 2 * C_QK_DIM, C_WIDTH, C_WIDTH,
              D_Q_RANK, D_KV_RANK + D_ROPE, D_WIDTH, M_WIDTH, M_WIDTH)

LANES = 128
SUBLANES = 8
MXU_COLS = 256
DFT_N2 = 128
VMEM_LIMIT = 56 * 1024 * 1024

EV_GATE, EV_MQ, EV_BU, EV_AQ, EV_AK, EV_AV = 0, 2560, 3072, 6144, 7168, 7424
OD_GATE, OD_MQ, OD_CQ, OD_CK, OD_CV, OD_DQA, OD_CKV, OD_KR = 0, 2560, 3072, 4096, 5120, 6144, 6656, 6912
OD_WIDTH = 7168


def _tile(n, pref):
    t = min(n, pref)
    assert n % t == 0, (n, t)
    return t


def _col_tile(n, cap):
    best = None
    for t in range(MXU_COLS, min(n, cap) + 1, MXU_COLS):
        if n % t == 0:
            best = t
    assert best is not None, (n, cap)
    return best


def _params(sem):
    return pltpu.CompilerParams(dimension_semantics=sem, vmem_limit_bytes=VMEM_LIMIT)


def _rms(x, gain):
    ms = jnp.mean(x * x, axis=-1, keepdims=True)
    return x * lax.rsqrt(ms + EPS) * gain


def _silu(g):
    g = g.astype(F32)
    return g * jax.nn.sigmoid(g)


def _lanes(x, k):
    return x if k == 1 else jnp.concatenate([x] * k, axis=1)


def _rope(y, cf, sn):
    return y * cf + pltpu.roll(y, LANES // 2, 1) * sn


def _nmm_kernel(x_ref, g_ref, w_ref, hg_ref, cf_ref, sn_ref, o_ref, h_ref, acc_ref, *, preps):
    j = pl.program_id(1)

    @pl.when(j == 0)
    def _():
        h_ref[...] = _rms(x_ref[...].astype(F32), g_ref[...]).astype(BF16)

    def product():
        return jnp.dot(h_ref[...], w_ref[...], preferred_element_type=F32)

    plain = None
    for blk in preps:
        plain = (j != blk) if plain is None else jnp.logical_and(plain, j != blk)

    if plain is None:
        o_ref[...] = product().astype(o_ref.dtype)
        return

    @pl.when(plain)
    def _():
        o_ref[...] = product().astype(o_ref.dtype)

    for blk, (gain_row0, scales) in preps.items():
        @pl.when(j == blk)
        def _(gain_row0=gain_row0, scales=scales):
            acc_ref[...] = product()

            @pl.when(pl.program_id(0) >= 0)
            def _():
                cf, sn = cf_ref[...], sn_ref[...]
                for slot, scale in enumerate(scales):
                    cols = slice(slot * LANES, (slot + 1) * LANES)
                    y = acc_ref[:, cols]
                    if scale is not None:
                        y = _rope(_rms(y, hg_ref[gain_row0 + slot:gain_row0 + slot + 1, :]), cf, sn) * scale
                    o_ref[:, cols] = y.astype(o_ref.dtype)


def normed_matmul(x, gain, w, *, seq, col_block=0, tm=1024, tn=1536, preps=None, head_gains=None, tables=None):
    rows = x.shape[0]
    k, n = w.shape
    tm = _tile(seq, tm)
    tn = _col_tile(n, tn)
    if preps is None:
        preps, head_gains = {}, jnp.ones((8, LANES), F32)
        tables = (jnp.ones((tm, LANES), F32), jnp.zeros((tm, LANES), F32))
    tab_spec = pl.BlockSpec((tm, LANES), lambda i, j: (i % (tables[0].shape[0] // tm), 0))
    return pl.pallas_call(
        functools.partial(_nmm_kernel, preps=preps),
        grid=(rows // tm, n // tn),
        in_specs=[pl.BlockSpec((tm, k), lambda i, j: (i, col_block)),
                  pl.BlockSpec((1, k), lambda i, j: (0, 0)),
                  pl.BlockSpec((k, tn), lambda i, j: (0, j)),
                  pl.BlockSpec(head_gains.shape, lambda i, j: (0, 0)),
                  tab_spec, tab_spec],
        out_specs=pl.BlockSpec((tm, tn), lambda i, j: (i, j)),
        out_shape=jax.ShapeDtypeStruct((rows, n), BF16),
        scratch_shapes=[pltpu.VMEM((tm, k), BF16), pltpu.VMEM((tm, tn) if preps else (SUBLANES, LANES), F32)],
        compiler_params=_params(("parallel", "arbitrary")),
        name="normed_matmul",
    )(x, gain.reshape(1, k).astype(F32), w, head_gains.astype(F32), *tables)


WIN_QB = 4


def _window_kernel(sink_ref, q_ref, kp_ref, kc_ref, kn_ref, vp_ref, vc_ref, vn_ref, g_ref, o_ref, s_ref, p_ref, *,
                   seq, qb):
    n = pl.program_id(1)
    group = A_HEADS // A_KV_HEADS
    rows = group * BLOCK
    qi = lax.broadcasted_iota(jnp.int32, (rows, 3 * BLOCK), 0) & (BLOCK - 1)
    kj = lax.broadcasted_iota(jnp.int32, (rows, 3 * BLOCK), 1)
    in_win = jnp.abs(kj - BLOCK - qi) <= WINDOW
    pairs = [(kv, i) for kv in range(A_KV_HEADS) for i in range(qb)]

    def kv_rows(refs, kv):
        cols = slice(kv * HEAD_DIM, (kv + 1) * HEAD_DIM)
        return jnp.concatenate([r[:, cols] for r in refs], axis=0).astype(BF16)

    kfull = [kv_rows((kp_ref, kc_ref, kn_ref), kv) for kv in range(A_KV_HEADS)]
    for idx, (kv, i) in enumerate(pairs):
        qrows = slice(i * BLOCK, (i + 1) * BLOCK)
        q = jnp.concatenate([q_ref[qrows, h * HEAD_DIM:(h + 1) * HEAD_DIM]
                             for h in range(kv * group, (kv + 1) * group)], axis=0)
        s_ref[idx] = lax.dot_general(q, kfull[kv][i * BLOCK:(i + 3) * BLOCK], (((1,), (1,)), ((), ())),
                                     preferred_element_type=F32)

    for idx, (kv, i) in enumerate(pairs):
        sk = jnp.concatenate([jnp.full((BLOCK, 1), sink_ref[h], F32)
                              for h in range(kv * group, (kv + 1) * group)], axis=0)
        kpos = (n * qb + i - 1) * BLOCK + kj
        s = jnp.where(in_win & (kpos >= 0) & (kpos < seq), s_ref[idx], NEG_INF)
        m = jnp.maximum(jnp.max(s, axis=-1, keepdims=True), sk)
        pr = jnp.exp(s - m)
        denom = jnp.sum(pr, axis=-1, keepdims=True) + jnp.exp(sk - m)
        p_ref[idx] = (pr / denom).astype(BF16)

    vfull = [kv_rows((vp_ref, vc_ref, vn_ref), kv) for kv in range(A_KV_HEADS)]
    for idx, (kv, i) in enumerate(pairs):
        qrows = slice(i * BLOCK, (i + 1) * BLOCK)
        o = jnp.dot(p_ref[idx], vfull[kv][i * BLOCK:(i + 3) * BLOCK], preferred_element_type=F32)
        for g, h in enumerate(range(kv * group, (kv + 1) * group)):
            hc = slice(h * HEAD_DIM, (h + 1) * HEAD_DIM)
            o_ref[qrows, hc] = (o[g * BLOCK:(g + 1) * BLOCK] * _silu(g_ref[qrows, hc])).astype(o_ref.dtype)


def window_attention(p, q_col, k_col, v_col, g_col, sink, batch, seq):
    nblk = seq // BLOCK
    qb = _tile(nblk, WIN_QB)
    nstep = nblk // qb
    kvw = A_KV_HEADS * HEAD_DIM
    kb, vb = k_col // kvw, v_col // kvw

    def prev(b, n):
        return b * nblk + jnp.maximum(n * qb - 1, 0)

    def cur(b, n):
        return b * nstep + n

    def nxt(b, n):
        return b * nblk + jnp.minimum((n + 1) * qb, nblk - 1)

    return pl.pallas_call(
        functools.partial(_window_kernel, seq=seq, qb=qb),
        grid=(batch, nstep),
        in_specs=[pl.BlockSpec(memory_space=pltpu.SMEM),
                  pl.BlockSpec((qb * BLOCK, A_WIDTH), lambda b, n: (cur(b, n), q_col // A_WIDTH)),
                  pl.BlockSpec((BLOCK, kvw), lambda b, n: (prev(b, n), kb)),
                  pl.BlockSpec((qb * BLOCK, kvw), lambda b, n: (cur(b, n), kb)),
                  pl.BlockSpec((BLOCK, kvw), lambda b, n: (nxt(b, n), kb)),
                  pl.BlockSpec((BLOCK, kvw), lambda b, n: (prev(b, n), vb)),
                  pl.BlockSpec((qb * BLOCK, kvw), lambda b, n: (cur(b, n), vb)),
                  pl.BlockSpec((BLOCK, kvw), lambda b, n: (nxt(b, n), vb)),
                  pl.BlockSpec((qb * BLOCK, A_WIDTH), lambda b, n: (cur(b, n), g_col // A_WIDTH))],
        out_specs=pl.BlockSpec((qb * BLOCK, A_WIDTH), lambda b, n: (cur(b, n), 0)),
        out_shape=jax.ShapeDtypeStruct((batch * seq, A_WIDTH), BF16),
        scratch_shapes=[pltpu.VMEM((A_KV_HEADS * qb, A_HEADS // A_KV_HEADS * BLOCK, 3 * BLOCK), F32),
                        pltpu.VMEM((A_KV_HEADS * qb, A_HEADS // A_KV_HEADS * BLOCK, 3 * BLOCK), BF16)],
        compiler_params=_params(("parallel", "parallel")),
        name="window_attention",
    )(sink.astype(F32), p, p, p, p, p, p, p, p)


def _mem_kernel(q_ref, kv_ref, qg_ref, kg_ref, g_ref, o_ref):
    qg, kg = qg_ref[...], kg_ref[...]
    scale = HEAD_DIM ** -0.5
    for h in range(M_HEADS):
        cols = slice(h * HEAD_DIM, (h + 1) * HEAD_DIM)
        q = (_rms(q_ref[:, cols].astype(F32), qg) * scale).astype(BF16)
        k = _rms(kv_ref[:, cols].astype(F32), kg).astype(BF16)
        v = kv_ref[:, M_WIDTH + h * HEAD_DIM:M_WIDTH + (h + 1) * HEAD_DIM].astype(BF16)
        s = lax.dot_general(q, k, (((1,), (1,)), ((), ())), preferred_element_type=F32)
        m = jnp.max(s, axis=-1, keepdims=True)
        pr = jnp.exp(s - m)
        pr = pr / jnp.sum(pr, axis=-1, keepdims=True)
        o = jnp.dot(pr.astype(BF16), v, preferred_element_type=F32)
        o_ref[:, cols] = (o * _silu(g_ref[:, cols])).astype(o_ref.dtype)


def memory_attention(p, q_col, g_col, kvm, q_gain, k_gain, batch, seq, *, tq=512):
    tq = _tile(seq, tq)
    nq = seq // tq
    return pl.pallas_call(
        _mem_kernel,
        grid=(batch, nq),
        in_specs=[pl.BlockSpec((tq, M_WIDTH), lambda b, i: (b * nq + i, q_col // M_WIDTH)),
                  pl.BlockSpec((N_MEM, 2 * M_WIDTH), lambda b, i: (b, 0)),
                  pl.BlockSpec((1, HEAD_DIM), lambda b, i: (0, 0)),
                  pl.BlockSpec((1, HEAD_DIM), lambda b, i: (0, 0)),
                  pl.BlockSpec((tq, M_WIDTH), lambda b, i: (b * nq + i, g_col // M_WIDTH))],
        out_specs=pl.BlockSpec((tq, M_WIDTH), lambda b, i: (b * nq + i, 0)),
        out_shape=jax.ShapeDtypeStruct((batch * seq, M_WIDTH), BF16),
        compiler_params=_params(("parallel", "parallel")),
        name="memory_attention",
    )(p, kvm, q_gain.reshape(1, HEAD_DIM).astype(F32), k_gain.reshape(1, HEAD_DIM).astype(F32), p)


LOG2E = math.log2(math.e)


def _chunk_rows(chunk, tk):
    return pl.ds(pl.multiple_of(chunk * tk, tk), tk)


MLA_TRIP_CHUNKS = 4
DIFF_TRIP_CHUNKS = 2


def _pipelined_chunks(scores, consume, nk, per_trip):
    scores(0, 0)

    def body(i, carry):
        for j in range(per_trip):
            chunk = per_trip * i + j
            scores(jnp.minimum(chunk + 1, nk - 1), (j + 1) % 2)
            consume(chunk, j % 2)
        return carry

    lax.fori_loop(0, nk // per_trip, body, 0)


def _diff_flash_kernel(q_ref, k_ref, v_ref, lam_ref, g_ref, gate_ref, o_ref, m_ref, l_ref, acc_ref, s_ref, *,
                       tk, nk, lam_init):
    m_ref[...] = jnp.full(m_ref.shape, NEG_INF, F32)
    l_ref[...] = jnp.zeros(l_ref.shape, F32)
    acc_ref[...] = jnp.zeros(acc_ref.shape, F32)

    def scores(chunk, c):
        cols = slice(c * C_QK_DIM, (c + 1) * C_QK_DIM)
        s_ref[c] = lax.dot_general(q_ref[:, cols], k_ref[_chunk_rows(chunk, tk), cols], (((1,), (1,)), ((), ())),
                                   preferred_element_type=F32)

    def consume(chunk, c):
        s = s_ref[c]
        m_prev = m_ref[c]
        m_new = jnp.maximum(m_prev, jnp.max(s, axis=-1, keepdims=True))
        alpha = jnp.exp2(m_prev - m_new)
        pr = jnp.exp2(s - _lanes(m_new, tk // LANES))
        l_ref[c] = alpha * l_ref[c] + jnp.sum(pr, axis=-1, keepdims=True)
        acc_ref[c] = (_lanes(alpha, C_V_DIM // LANES) * acc_ref[c]
                      + jnp.dot(pr.astype(BF16), v_ref[_chunk_rows(chunk, tk), :], preferred_element_type=F32))
        m_ref[c] = m_new

    scores(0, 0)

    def body(i, carry):
        for step in range(DIFF_TRIP_CHUNKS):
            j = DIFF_TRIP_CHUNKS * i + step
            scores(j, 1)
            consume(j, 0)
            scores(jnp.minimum(j + 1, nk - 1), 0)
            consume(j, 1)
        return carry

    lax.fori_loop(0, nk // DIFF_TRIP_CHUNKS, body, 0)
    lv = lam_ref[...]
    lam = (jnp.exp(jnp.sum(lv[0:1] * lv[1:2], axis=-1, keepdims=True))
           - jnp.exp(jnp.sum(lv[2:3] * lv[3:4], axis=-1, keepdims=True)) + lam_init)
    o1 = acc_ref[0] / _lanes(l_ref[0], C_V_DIM // LANES)
    o2 = acc_ref[1] / _lanes(l_ref[1], C_V_DIM // LANES)
    o_ref[...] = (_rms(o1 - lam * o2, g_ref[...]) * (1.0 - lam_init) * _silu(gate_ref[...])).astype(o_ref.dtype)


def _mla_flash_kernel(q_ref, k_ref, v_ref, gate_ref, o_ref, m_ref, acc_ref, s_ref, *, tk, nk):
    m_ref[...] = jnp.full(m_ref.shape, NEG_INF, F32)
    acc_ref[...] = jnp.zeros(acc_ref.shape, F32)
    ones = jnp.ones((tk, LANES), BF16)

    def scores(chunk, slot):
        s_ref[slot] = lax.dot_general(q_ref[...], k_ref[_chunk_rows(chunk, tk), :], (((1,), (1,)), ((), ())),
                                      preferred_element_type=F32)

    def consume(chunk, slot):
        s = s_ref[slot]
        m_prev = m_ref[...]
        m_new = jnp.maximum(m_prev, jnp.max(s, axis=-1, keepdims=True))
        alpha = jnp.exp2(m_prev - m_new)
        pr = jnp.exp2(s - _lanes(m_new, tk // LANES))
        vv = jnp.concatenate([v_ref[_chunk_rows(chunk, tk), :], ones], axis=1)
        acc_ref[...] = (_lanes(alpha, 2) * acc_ref[...]
                        + jnp.dot(pr.astype(BF16), vv, preferred_element_type=F32))
        m_ref[...] = m_new

    _pipelined_chunks(scores, consume, nk, MLA_TRIP_CHUNKS)
    o_ref[...] = (acc_ref[:, :D_V] / acc_ref[:, D_V:] * _silu(gate_ref[...])).astype(o_ref.dtype)


def _flash_call(kernel, q, q_block0, k, k_block0, v, v_block0, gate, gate_block0, extra, extra_specs, scratch, *,
                heads, dv, batch, seq, tq, tk, kv_buffers, name):
    nq, nk = seq // tq, seq // tk
    qk_w = 2 * LANES
    resident = pl.Buffered(kv_buffers)
    return pl.pallas_call(
        functools.partial(kernel, tk=tk, nk=nk),
        grid=(batch, heads, nq),
        in_specs=[pl.BlockSpec((tq, qk_w), lambda b, h, i: (b * nq + i, q_block0 + h)),
                  pl.BlockSpec((seq, qk_w), lambda b, h, i: (b, k_block0 + h), pipeline_mode=resident),
                  pl.BlockSpec((seq, dv), lambda b, h, i: (b, v_block0 + h), pipeline_mode=resident)]
        + extra_specs + [pl.BlockSpec((tq, dv), lambda b, h, i: (b * nq + i, gate_block0 + h))],
        out_specs=pl.BlockSpec((tq, dv), lambda b, h, i: (b * nq + i, h)),
        out_shape=jax.ShapeDtypeStruct((batch * seq, heads * dv), BF16),
        scratch_shapes=scratch,
        compiler_params=_params(("parallel", "parallel", "arbitrary")),
        name=name,
    )(q, k, v, *extra, gate)


def diff_attention(p, q_col, k_col, v_col, g_col, c_lambda, out_gain, lam_init, batch, seq, *, tq=1024, tk=1024):
    tq, tk = _tile(seq, tq), _tile(seq // DIFF_TRIP_CHUNKS, tk)
    const = lambda b, h, i: (0, 0)
    return _flash_call(
        functools.partial(_diff_flash_kernel, lam_init=lam_init), p, q_col // C_V_DIM, p, k_col // C_V_DIM,
        p, v_col // C_V_DIM, p, g_col // C_V_DIM,
        [c_lambda.astype(F32), out_gain.reshape(1, C_V_DIM).astype(F32)],
        [pl.BlockSpec((4, C_QK_DIM), const), pl.BlockSpec((1, C_V_DIM), const)],
        [pltpu.VMEM((2, tq, LANES), F32), pltpu.VMEM((2, tq, LANES), F32), pltpu.VMEM((2, tq, C_V_DIM), F32),
         pltpu.VMEM((2, tq, tk), F32)],
        heads=C_HEADS, dv=C_V_DIM, batch=batch, seq=seq, tq=tq, tk=tk, kv_buffers=1, name="diff_attention")


def mla_attention(q, k, v, p, g_col, batch, seq, *, tq=2048, tk=512):
    tq, tk = _tile(seq, tq), _tile(seq // MLA_TRIP_CHUNKS, tk)
    return _flash_call(_mla_flash_kernel, q, 0, k, 0, v, 0, p, g_col // D_V, [], [],
                       [pltpu.VMEM((tq, LANES), F32), pltpu.VMEM((tq, D_V + LANES), F32),
                        pltpu.VMEM((2, tq, tk), F32)],
                       heads=D_HEADS, dv=D_V, batch=batch, seq=seq, tq=tq, tk=tk, kv_buffers=2,
                       name="mla_attention")


def _mla_proj_kernel(*refs, shared_rope, scale):
    if shared_rope:
        x_ref, g_ref, w_ref, kr_ref, gn_ref, gr_ref, cf_ref, sn_ref, o_ref, v_ref = refs
    else:
        x_ref, g_ref, w_ref, gn_ref, gr_ref, cf_ref, sn_ref, o_ref = refs
    h = _rms(x_ref[...].astype(F32), g_ref[...]).astype(BF16)
    acc = jnp.dot(h, w_ref[...], preferred_element_type=F32)
    gn, gr = gn_ref[...], gr_ref[...]
    cf, sn = cf_ref[...], sn_ref[...]
    width = D_HEADS * LANES
    for hd in range(D_HEADS):
        cols = slice(hd * LANES, (hd + 1) * LANES)
        xn = acc[:, cols]
        xr = kr_ref[...].astype(F32) if shared_rope else acc[:, width + hd * LANES:width + (hd + 1) * LANES]
        ms = (jnp.sum(xn * xn, axis=-1, keepdims=True)
              + jnp.sum(xr * xr, axis=-1, keepdims=True)) * (1.0 / (D_NOPE + D_ROPE))
        inv = lax.rsqrt(ms + EPS)
        o_ref[:, 2 * hd * LANES:(2 * hd + 1) * LANES] = (xn * inv * gn * scale).astype(o_ref.dtype)
        o_ref[:, (2 * hd + 1) * LANES:(2 * hd + 2) * LANES] = (_rope(xr * inv * gr, cf, sn) * scale).astype(o_ref.dtype)
    if shared_rope:
        v_ref[...] = acc[:, width:].astype(v_ref.dtype)


def mla_projection(p, col, rank, gain, w, head_gain, tables, scale, seq, *, rope_col=None, tm=512):
    rows = p.shape[0]
    width = D_HEADS * LANES
    tm = _tile(seq, tm)
    nblk = seq // tm
    shared_rope = rope_col is not None
    gn = head_gain[:D_NOPE].reshape(1, LANES).astype(F32)
    gr = _rope_slot(head_gain[D_NOPE:]).reshape(1, LANES).astype(F32)
    one = pl.BlockSpec((1, LANES), lambda i: (0, 0))
    tab_spec = pl.BlockSpec((tm, LANES), lambda i: (i % nblk, 0))
    in_specs = [pl.BlockSpec((tm, rank), lambda i: (i, col // rank)),
                pl.BlockSpec((1, rank), lambda i: (0, 0)),
                pl.BlockSpec((rank, 2 * width), lambda i: (0, 0))]
    args = [p, gain.reshape(1, rank).astype(F32), w]
    out_specs = [pl.BlockSpec((tm, 2 * width), lambda i: (i, 0))]
    out_shape = [jax.ShapeDtypeStruct((rows, 2 * width), BF16)]
    if shared_rope:
        in_specs.append(pl.BlockSpec((tm, LANES), lambda i: (i, rope_col // LANES)))
        args.append(p)
        out_specs.append(pl.BlockSpec((tm, width), lambda i: (i, 0)))
        out_shape.append(jax.ShapeDtypeStruct((rows, width), BF16))
    return pl.pallas_call(
        functools.partial(_mla_proj_kernel, shared_rope=shared_rope, scale=scale),
        grid=(rows // tm,),
        in_specs=in_specs + [one, one, tab_spec, tab_spec],
        out_specs=out_specs,
        out_shape=out_shape,
        compiler_params=_params(("parallel",)),
        name="mla_projection",
    )(*args, gn, gr, *tables)


SUB = SUBLANES
FILT_ROWS = SUB * DFT_N2
HALO = 16


def _half_rows(n1):
    return n1 // 2 + SUB


def _hyena_gate_kernel(*refs, nblk, tm):
    u_refs, up_refs, un_refs = refs[0:3], refs[3:6], refs[6:9]
    w_ref, b_ref, g_ref, zz_ref, x0_ref = refs[9:]
    li = pl.program_id(1)
    tc = zz_ref.shape[-1]
    row = lax.broadcasted_iota(jnp.int32, (tm, tc), 0)
    has_prev = (li > 0).astype(F32)
    has_next = (li < nblk - 1).astype(F32)

    def conv(part):
        u = u_refs[part][...].astype(F32)
        prev_row = up_refs[part][HALO - 1:HALO, :].astype(F32) * has_prev
        next_row = un_refs[part][0:1, :].astype(F32) * has_next
        above = jnp.where(row == 0, prev_row, pltpu.roll(u, 1, 0))
        below = jnp.where(row == tm - 1, next_row, pltpu.roll(u, tm - 1, 0))
        w = w_ref[part]
        return above * w[0:1] + u * w[1:2] + below * w[2:3] + b_ref[part]

    x0 = conv(0) * _silu(g_ref[...])
    zz = conv(2) * conv(1)
    for a in range(tm // DFT_N2):
        rows = slice(a * DFT_N2, (a + 1) * DFT_N2)
        zz_ref[:, a, :] = zz[rows]
        x0_ref[:, a, :] = x0[rows]


def hyena_gate(p, u_col, g_col, short_w, short_b, batch, seq, *, tc=512):
    tm = FILT_ROWS
    nblk = seq // tm
    half = seq // DFT_N2
    tc = _tile(B_WIDTH, tc)
    nc = B_WIDTH // tc
    hb = tm // HALO
    last_halo = batch * nblk * hb - 1

    def part_specs(part):
        cb = (u_col + part * B_WIDTH) // tc
        return (pl.BlockSpec((tm, tc), lambda b, i, c: (b * nblk + i, cb + c)),
                pl.BlockSpec((HALO, tc), lambda b, i, c: (jnp.maximum((b * nblk + i) * hb - 1, 0), cb + c)),
                pl.BlockSpec((HALO, tc), lambda b, i, c: (jnp.minimum((b * nblk + i + 1) * hb, last_halo), cb + c)))

    specs = [part_specs(part) for part in range(3)]
    out = jax.ShapeDtypeStruct((batch, DFT_N2, half, B_WIDTH), F32)
    o_spec = pl.BlockSpec((None, DFT_N2, tm // DFT_N2, tc), lambda b, i, c: (b, 0, i, c))
    w3 = short_w.astype(F32).reshape(3, 3, B_WIDTH).transpose(1, 0, 2)
    b3 = short_b.astype(F32).reshape(3, 1, B_WIDTH)
    return pl.pallas_call(
        functools.partial(_hyena_gate_kernel, nblk=nblk, tm=tm),
        grid=(batch, nblk, nc),
        in_specs=[sp[0] for sp in specs] + [sp[1] for sp in specs] + [sp[2] for sp in specs]
        + [pl.BlockSpec((3, 3, tc), lambda b, i, c: (0, 0, c)),
           pl.BlockSpec((3, 1, tc), lambda b, i, c: (0, 0, c)),
           pl.BlockSpec((tm, tc), lambda b, i, c: (b * nblk + i, g_col // tc + c))],
        out_specs=[o_spec, o_spec],
        out_shape=[out, out],
        compiler_params=_params(("parallel", "parallel", "parallel")),
        name="hyena_gate",
    )(*([p] * 9), w3, b3, p)


def _hdot(a, b):
    return jnp.dot(a, b, preferred_element_type=F32, precision=lax.Precision.HIGHEST)


def _filter_kernel(fr_ref, w1_ref, b1_ref, fq_ref, w2_ref, b2_ref, w3_ref, dec_ref, k_ref, s_ref, t_ref, h_ref, *,
                   seq, tr):
    i, c = pl.program_id(0), pl.program_id(1)
    r = i * tr + lax.broadcasted_iota(jnp.int32, (tr, 1), 0)

    @pl.when(c == 0)
    def _():
        pos = jnp.where(r < seq, r, 2 * seq - 1 - r).astype(F32)
        t = pos * (1.0 / (seq - 1))
        w = pos * (2.0 * math.pi / seq)
        lane = lax.broadcasted_iota(jnp.int32, (tr, LANES), 1)
        bands = (HY_EMB - 1) // 2
        z = jnp.where(lane == 0, t, jnp.where(lane <= 2 * bands, jnp.cos(w * fr_ref[0:1] + fr_ref[1:2]), 0.0))
        h = jnp.sin(fq_ref[0:1] * (_hdot(z, w1_ref[...]) + b1_ref[...]))
        h_ref[...] = jnp.sin(fq_ref[1:2] * (_hdot(h, w2_ref[...]) + b2_ref[...]))
        t_ref[...] = jnp.broadcast_to(t, t_ref.shape)

    h = _hdot(h_ref[...], w3_ref[...]) * jnp.exp(-t_ref[:, 0:1] * jnp.abs(dec_ref[...]))
    h = jnp.where(r == seq, 0.0, h)
    for a in range(tr // DFT_N2):
        k_ref[:, a, :] = h[a * DFT_N2:(a + 1) * DFT_N2]

    @pl.when(jnp.logical_and(i == 0, c == 0))
    def _():
        s_ref[...] = jnp.zeros(s_ref.shape, F32)

    tc = h.shape[1]
    col_sum = jnp.sum(jnp.abs(h), axis=0, keepdims=True)
    for cc in range(s_ref.shape[1] // tc):
        @pl.when(c == cc)
        def _(cc=cc):
            s_ref[:, cc * tc:(cc + 1) * tc] += col_sum


def hyena_filter(seq, w1, b1, freq, w2, b2, w3, decay, *, tc=512):
    tr = FILT_ROWS
    assert seq % tr == 0
    nhalf = seq // tr
    n1 = 2 * seq // DFT_N2
    tc = _tile(B_WIDTH, tc)
    nc = B_WIDTH // tc
    bands = (HY_EMB - 1) // 2
    fr = jnp.linspace(1e-4, bands - 1, bands, dtype=F32)
    pad = jnp.zeros((LANES - HY_EMB,), F32)
    fr_lanes = jnp.stack([jnp.concatenate([jnp.zeros((1,), F32), fr, fr, pad]),
                          jnp.concatenate([jnp.zeros((1 + bands,), F32), jnp.full((bands,), 0.5 * math.pi, F32), pad])])
    w1p = jnp.pad(w1.astype(F32), ((0, LANES - HY_EMB), (0, 0)))
    const = lambda i, c: (0, 0)
    half = lambda i, c: (0, (i // nhalf) * nc + c)
    return pl.pallas_call(
        functools.partial(_filter_kernel, seq=seq, tr=tr),
        grid=(2 * nhalf, nc),
        in_specs=[pl.BlockSpec((2, LANES), const),
                  pl.BlockSpec((LANES, HY_FFN), const),
                  pl.BlockSpec((1, HY_FFN), const),
                  pl.BlockSpec((2, HY_FFN), const),
                  pl.BlockSpec((HY_FFN, HY_FFN), const),
                  pl.BlockSpec((1, HY_FFN), const),
                  pl.BlockSpec((HY_FFN, tc), half),
                  pl.BlockSpec((1, tc), half)],
        out_specs=[pl.BlockSpec((DFT_N2, tr // DFT_N2, tc), lambda i, c: (0, i, c)),
                   pl.BlockSpec((1, B_WIDTH), const)],
        out_shape=[jax.ShapeDtypeStruct((DFT_N2, n1, B_WIDTH), F32),
                   jax.ShapeDtypeStruct((1, B_WIDTH), F32)],
        scratch_shapes=[pltpu.VMEM((tr, LANES), F32), pltpu.VMEM((tr, HY_FFN), F32)],
        compiler_params=_params(("arbitrary", "arbitrary")),
        name="hyena_filter",
    )(fr_lanes, w1p, b1.reshape(1, HY_FFN).astype(F32), freq.astype(F32), w2.astype(F32),
      b2.reshape(1, HY_FFN).astype(F32), w3.astype(F32), decay.reshape(1, 2 * B_WIDTH).astype(F32))


def _bdot(a, b):
    return jnp.dot(a, b, preferred_element_type=F32)


def dft_tables(n1):
    n = n1 * DFT_N2
    rows = _half_rows(n1)
    kk = jnp.arange(rows, dtype=jnp.int32)
    k1 = kk[None, :, None]
    m1 = jnp.arange(n1, dtype=jnp.int32)[None, None, :]
    n2 = jnp.arange(DFT_N2, dtype=jnp.int32)[:, None, None]
    ang = ((k1 * (DFT_N2 * m1 + n2)) % n).astype(F32) * (2.0 * math.pi / n)
    live = (kk <= n1 // 2).astype(F32)[None, :, None]
    weight = jnp.where((kk == 0) | (kk == n1 // 2), 1.0, 2.0)[None, :, None] * live
    cos, sin = jnp.cos(ang), jnp.sin(ang)
    g = jnp.concatenate([cos * live, -sin * live], axis=1).astype(BF16)
    hc = jnp.swapaxes(cos * weight, 1, 2)[:, :n1 // 2].astype(BF16)
    hsn = jnp.swapaxes(-sin * weight, 1, 2)[:, :n1 // 2].astype(BF16)
    a = jnp.arange(DFT_N2, dtype=jnp.int32)
    ang2 = ((a[:, None] * a[None, :]) % DFT_N2).astype(F32) * (2.0 * math.pi / DFT_N2)
    ff = jnp.concatenate([jnp.cos(ang2), jnp.sin(ang2)], axis=0).astype(BF16)
    return g, hc, hsn, ff


def _dft1_kernel(x_ref, g_ref, ar_ref, ai_ref):
    rows = ar_ref.shape[0]
    for j in range(SUB):
        a = _bdot(g_ref[j], x_ref[j].astype(BF16))
        ar_ref[:, j, :] = a[:rows]
        ai_ref[:, j, :] = a[rows:]


def dft_stage1(x, g, *, tc=512):
    batch, _, k1rows, c = x.shape
    rows = g.shape[1] // 2
    tc = _tile(c, tc)
    out = jax.ShapeDtypeStruct((batch, rows, DFT_N2, c), F32)
    g_spec = pl.BlockSpec((SUB, 2 * rows, k1rows), lambda g, b, ci: (g, 0, 0))
    o_spec = pl.BlockSpec((None, rows, SUB, tc), lambda g, b, ci: (b, 0, g, ci))
    return pl.pallas_call(
        _dft1_kernel,
        grid=(DFT_N2 // SUB, batch, c // tc),
        in_specs=[pl.BlockSpec((None, SUB, k1rows, tc), lambda g, b, ci: (b, g, 0, ci)), g_spec],
        out_specs=[o_spec, o_spec],
        out_shape=[out, out],
        compiler_params=_params(("parallel", "parallel", "parallel")),
        name="dft_stage1",
    )(x, g)


def _stage2(ff, ar, ai):
    p = _bdot(ff, ar)
    q = _bdot(ff, ai)
    return p[:DFT_N2] + q[DFT_N2:], q[:DFT_N2] - p[DFT_N2:]


def _filter_spectrum_kernel(ar_ref, ai_ref, s_ref, ff_ref, kr_ref, ki_ref, *, n):
    ff = ff_ref[...]
    scale = 1.0 / (s_ref[...] * n)
    for j in range(SUB):
        xr, xi = _stage2(ff, ar_ref[j].astype(BF16), ai_ref[j].astype(BF16))
        kr_ref[j] = xr * scale
        ki_ref[j] = xi * scale


def filter_spectrum(ar, ai, asum, ff, n, *, tc=512):
    rows, _, c = ar.shape
    tc = _tile(c, tc)
    a_spec = pl.BlockSpec((SUB, DFT_N2, tc), lambda k, ci: (k, 0, ci))
    out = jax.ShapeDtypeStruct((rows, DFT_N2, c), F32)
    return pl.pallas_call(
        functools.partial(_filter_spectrum_kernel, n=n),
        grid=(rows // SUB, c // tc),
        in_specs=[a_spec, a_spec, pl.BlockSpec((1, tc), lambda k, ci: (0, ci)),
                  pl.BlockSpec((2 * DFT_N2, DFT_N2), lambda k, ci: (0, 0))],
        out_specs=[a_spec, a_spec],
        out_shape=[out, out],
        compiler_params=_params(("parallel", "parallel")),
        name="filter_spectrum",
    )(ar, ai, asum, ff)


def _spectral_kernel(ar_ref, ai_ref, kr_ref, ki_ref, ff_ref, br_ref, bi_ref):
    ff = ff_ref[...]
    for j in range(SUB):
        xr, xi = _stage2(ff, ar_ref[j].astype(BF16), ai_ref[j].astype(BF16))
        kr, ki = kr_ref[j], ki_ref[j]
        yr = (xr * kr - xi * ki).astype(BF16)
        yi = (xr * ki + xi * kr).astype(BF16)
        r = _bdot(ff, yr)
        s = _bdot(ff, yi)
        br_ref[:, j, :] = r[:DFT_N2] - s[DFT_N2:]
        bi_ref[:, j, :] = s[:DFT_N2] + r[DFT_N2:]


def spectral_multiply(ar, ai, kr, ki, ff, *, tc=512):
    batch, rows, _, c = ar.shape
    tc = _tile(c, tc)
    a_spec = pl.BlockSpec((None, SUB, DFT_N2, tc), lambda k, ci, b: (b, k, 0, ci))
    k_spec = pl.BlockSpec((SUB, DFT_N2, tc), lambda k, ci, b: (k, 0, ci))
    o_spec = pl.BlockSpec((None, DFT_N2, SUB, tc), lambda k, ci, b: (b, 0, k, ci))
    out = jax.ShapeDtypeStruct((batch, DFT_N2, rows, c), F32)
    return pl.pallas_call(
        _spectral_kernel,
        grid=(rows // SUB, c // tc, batch),
        in_specs=[a_spec, a_spec, k_spec, k_spec, pl.BlockSpec((2 * DFT_N2, DFT_N2), lambda k, ci, b: (0, 0))],
        out_specs=[o_spec, o_spec],
        out_shape=[out, out],
        compiler_params=_params(("parallel", "parallel", "parallel")),
        name="spectral_multiply",
    )(ar, ai, kr, ki, ff)


def _idft_out_kernel(br_ref, bi_ref, hc_ref, hs_ref, zz_ref, x0_ref, bias_ref, o_ref):
    bias = bias_ref[...]
    for j in range(SUB):
        y = _bdot(hc_ref[j], br_ref[j].astype(BF16)) + _bdot(hs_ref[j], bi_ref[j].astype(BF16))
        o_ref[:, j, :] = (y + zz_ref[j] * bias) * x0_ref[j]


def idft_output(br, bi, hc, hsn, zz, x0, bias, *, tc=512):
    batch, _, rows, c = br.shape
    half = hc.shape[1]
    tc = _tile(c, tc)
    b_spec = pl.BlockSpec((None, SUB, rows, tc), lambda g, b, ci: (b, g, 0, ci))
    h_spec = pl.BlockSpec((SUB, half, rows), lambda g, b, ci: (g, 0, 0))
    z_spec = pl.BlockSpec((None, SUB, half, tc), lambda g, b, ci: (b, g, 0, ci))
    return pl.pallas_call(
        _idft_out_kernel,
        grid=(DFT_N2 // SUB, batch, c // tc),
        in_specs=[b_spec, b_spec, h_spec, h_spec, z_spec, z_spec, pl.BlockSpec((1, tc), lambda g, b, ci: (0, ci))],
        out_specs=pl.BlockSpec((half, SUB, tc), lambda g, b, ci: (b, g, ci)),
        out_shape=jax.ShapeDtypeStruct((batch * half, DFT_N2, c), F32),
        compiler_params=_params(("parallel", "parallel", "parallel")),
        name="idft_output",
    )(br, bi, hc, hsn, zz, x0, bias)


def hyena_mixer(p, u_col, g_col, prm, e, batch, seq):
    n1 = 2 * seq // DFT_N2
    g, hc, hsn, ff = dft_tables(n1)
    kern, asum = hyena_filter(seq, prm["hy_w1"][e], prm["hy_b1"][e], prm["hy_freq"][e], prm["hy_w2"][e],
                              prm["hy_b2"][e], prm["hy_w3"][e], prm["hy_decay"][e])
    far, fai = dft_stage1(kern[None], g)
    kr, ki = filter_spectrum(far[0], fai[0], asum, ff, n1 * DFT_N2)
    zz, x0 = hyena_gate(p, u_col, g_col, prm["hy_short_w"][e], prm["hy_short_b"][e], batch, seq)
    ar, ai = dft_stage1(zz, g[:, :, :n1 // 2])
    br, bi = spectral_multiply(ar, ai, kr, ki, ff)
    yb = idft_output(br, bi, hc, hsn, zz, x0, prm["hy_bias"][e].reshape(1, B_WIDTH).astype(F32))
    return yb.reshape(batch * seq, B_WIDTH)


def _outproj_kernel(y1_ref, y2_ref, y3_ref, x_ref, w_ref, o_ref):
    acc = x_ref[...]
    off = 0
    for y_ref in (y1_ref, y2_ref, y3_ref):
        wd = y_ref.shape[1]
        acc = acc + jnp.dot(y_ref[...].astype(BF16), w_ref[off:off + wd, :], preferred_element_type=F32)
        off += wd
    o_ref[...] = acc


def out_projection(y1, y2, y3, x, w, *, tm=512):
    rows, d = x.shape
    tm = _tile(rows, tm)
    row = lambda i: (i, 0)
    return pl.pallas_call(
        _outproj_kernel,
        grid=(rows // tm,),
        in_specs=[pl.BlockSpec((tm, y1.shape[1]), row),
                  pl.BlockSpec((tm, y2.shape[1]), row),
                  pl.BlockSpec((tm, y3.shape[1]), row),
                  pl.BlockSpec((tm, d), row),
                  pl.BlockSpec((MIX_WIDTH, d), lambda i: (0, 0), pipeline_mode=pl.Buffered(1))],
        out_specs=pl.BlockSpec((tm, d), row),
        out_shape=jax.ShapeDtypeStruct((rows, d), F32),
        compiler_params=_params(("parallel",)),
        name="out_projection",
    )(y1, y2, y3, x, w)


def _split_cols(w, sizes):
    out, off = [], 0
    for s in sizes:
        out.append(w[:, off:off + s])
        off += s
    return out


ROT_HALF = ROT_DIM // 2
HEAD_PERM = (tuple(range(ROT_HALF)) + tuple(range(ROT_DIM, LANES // 2 + ROT_HALF))
             + tuple(range(ROT_HALF, ROT_DIM)) + tuple(range(LANES // 2 + ROT_HALF, LANES)))


def _permute_heads(w, nheads):
    k = w.shape[0]
    return w.reshape(k, nheads, HEAD_DIM)[:, :, jnp.array(HEAD_PERM)].reshape(k, nheads * HEAD_DIM)


def _rope_slot(w):
    half = D_ROPE // 2
    z = jnp.zeros(w.shape[:-1] + (LANES // 2 - half,), w.dtype)
    return jnp.concatenate([w[..., :half], z, w[..., half:], z], axis=-1)


def _rope_tables(seq, dim, passthrough):
    half = dim // 2
    inv = ROPE_THETA ** (-jnp.arange(0, dim, 2, dtype=F32) / dim)
    ang = jnp.arange(seq, dtype=F32)[:, None] * inv[None, :]
    cos, sin = jnp.cos(ang), jnp.sin(ang)
    rest = jnp.full((seq, LANES // 2 - half), 1.0 if passthrough else 0.0, F32)
    zeros = jnp.zeros((seq, LANES // 2 - half), F32)
    cf = jnp.concatenate([cos, rest, cos, rest], axis=1)
    sn = jnp.concatenate([-sin, zeros, sin, zeros], axis=1)
    return cf, sn


def _memory_kv(mem, prm, l):
    batch = mem.shape[0]
    return normed_matmul(mem.reshape(batch * N_MEM, D_MODEL), prm["mem_norm_gain"][l],
                         prm["w_mem_kv"][l].astype(BF16), seq=N_MEM)


def _head_gains(*groups):
    rows = [jnp.broadcast_to(g[jnp.array(HEAD_PERM)][None, :], (n, HEAD_DIM)) for g, n in groups]
    return jnp.concatenate(rows, axis=0)


def _even_layer(x, mem, prm, l, e, batch, seq, rope_a):
    aq, ak, av, ag, bu, bg, mq, mg = _split_cols(prm["w_in_even"][e], EVEN_SPLITS)
    w_in = jnp.concatenate([ag, bg, mg, mq, bu, _permute_heads(aq, A_HEADS), _permute_heads(ak, A_KV_HEADS), av],
                           axis=1).astype(BF16)
    tn = _col_tile(w_in.shape[1], 1536)
    assert EV_AQ % tn == 0 and w_in.shape[1] - EV_AQ == tn
    slots = [HEAD_DIM ** -0.5] * A_HEADS + [1.0] * A_KV_HEADS + [None] * A_KV_HEADS
    p = normed_matmul(x, prm["norm_gain"][l], w_in, seq=seq, tn=tn, preps={EV_AQ // tn: (0, slots)},
                      head_gains=_head_gains((prm["a_q_norm"][e], A_HEADS), (prm["a_k_norm"][e], A_KV_HEADS)),
                      tables=rope_a)
    ya = window_attention(p, EV_AQ, EV_AK, EV_AV, EV_GATE, prm["a_sink"][e], batch, seq)
    yb = hyena_mixer(p, EV_BU, EV_GATE + A_WIDTH, prm, e, batch, seq)
    kvm = _memory_kv(mem, prm, l)
    ym = memory_attention(p, EV_MQ, EV_GATE + A_WIDTH + B_WIDTH, kvm, prm["mem_q_norm"][l], prm["mem_k_norm"][l],
                          batch, seq)
    return out_projection(ya, yb, ym, x, prm["w_out_even"][e].astype(BF16))


def _odd_layer(x, mem, prm, l, o, batch, seq, rope_a, rope_d):
    cq, ck, cv, cg, dqa, dkva, dg, mq, mg = _split_cols(prm["w_in_odd"][o], ODD_SPLITS)
    pad = jnp.zeros((D_MODEL, OD_WIDTH - (OD_KR + LANES)), F32)
    w_in = jnp.concatenate([cg, dg, mg, mq, _permute_heads(cq, 2 * C_HEADS), _permute_heads(ck, 2 * C_HEADS), cv, dqa,
                            dkva[:, :D_KV_RANK], _rope_slot(dkva[:, D_KV_RANK:]), pad], axis=1).astype(BF16)
    tn = _col_tile(w_in.shape[1], 1024)
    assert OD_CQ % tn == 0 and OD_CK - OD_CQ == tn and OD_CV - OD_CK == tn
    nslot = 2 * C_HEADS
    p = normed_matmul(x, prm["norm_gain"][l], w_in, seq=seq, tn=tn,
                      preps={OD_CQ // tn: (0, [C_QK_DIM ** -0.5 * LOG2E] * nslot), OD_CK // tn: (nslot, [1.0] * nslot)},
                      head_gains=_head_gains((prm["c_q_norm"][o], nslot), (prm["c_k_norm"][o], nslot)),
                      tables=rope_a)
    lam_init = 0.8 - 0.6 * math.exp(-0.3 * l)
    yc = diff_attention(p, OD_CQ, OD_CK, OD_CV, OD_GATE, prm["c_lambda"][o], prm["c_out_norm"][o], lam_init, batch, seq)
    wq = prm["w_q_b"][o].reshape(D_Q_RANK, D_HEADS, D_NOPE + D_ROPE)
    wq2 = jnp.concatenate([wq[:, :, :D_NOPE].reshape(D_Q_RANK, -1), _rope_slot(wq[:, :, D_NOPE:]).reshape(D_Q_RANK, -1)],
                          axis=1)
    wkv = prm["w_kv_b"][o].reshape(D_KV_RANK, D_HEADS, D_NOPE + D_V)
    wkv2 = jnp.concatenate([wkv[:, :, :D_NOPE].reshape(D_KV_RANK, -1), wkv[:, :, D_NOPE:].reshape(D_KV_RANK, -1)],
                           axis=1)
    dscale = (D_NOPE + D_ROPE) ** -0.5
    qd, = mla_projection(p, OD_DQA, D_Q_RANK, prm["d_q_a_norm"][o], wq2.astype(BF16), prm["d_q_norm"][o], rope_d,
                         dscale * LOG2E, seq)
    kd, vd = mla_projection(p, OD_CKV, D_KV_RANK, prm["d_kv_a_norm"][o], wkv2.astype(BF16), prm["d_k_norm"][o], rope_d,
                            1.0, seq, rope_col=OD_KR)
    yd = mla_attention(qd, kd, vd, p, OD_GATE + C_WIDTH, batch, seq)
    kvm = _memory_kv(mem, prm, l)
    ym = memory_attention(p, OD_MQ, OD_GATE + C_WIDTH + D_WIDTH, kvm, prm["mem_q_norm"][l], prm["mem_k_norm"][l],
                          batch, seq)
    return out_projection(yc, yd, ym, x, prm["w_out_odd"][o].astype(BF16))


def _trunk(x, mem, prm):
    batch, seq, d = x.shape
    depth = prm["norm_gain"].shape[0]
    rope_a = _rope_tables(seq, ROT_DIM, True)
    rope_d = _rope_tables(seq, D_ROPE, False)
    h = x.reshape(batch * seq, d)
    for l in range(depth):
        if l % 2 == 0:
            h = _even_layer(h, mem, prm, l, l // 2, batch, seq, rope_a)
        else:
            h = _odd_layer(h, mem, prm, l, l // 2, batch, seq, rope_a, rope_d)
    return h.reshape(batch, seq, d)


def kernel(x_prompt, x_sample, mem_prompt, mem_sample, norm_gain, mem_norm_gain, w_mem_kv, mem_q_norm, mem_k_norm, w_in_even, w_out_even, a_q_norm, a_k_norm, a_sink, hy_short_w, hy_short_b, hy_w1, hy_b1, hy_freq, hy_w2, hy_b2, hy_w3, hy_decay, hy_bias, w_in_odd, w_out_odd, c_q_norm, c_k_norm, c_lambda, c_out_norm, d_q_a_norm, w_q_b, d_kv_a_norm, w_kv_b, d_q_norm, d_k_norm):
    prm = dict(norm_gain=norm_gain, mem_norm_gain=mem_norm_gain, w_mem_kv=w_mem_kv,
               mem_q_norm=mem_q_norm, mem_k_norm=mem_k_norm, w_in_even=w_in_even,
               w_out_even=w_out_even, a_q_norm=a_q_norm, a_k_norm=a_k_norm, a_sink=a_sink,
               hy_short_w=hy_short_w, hy_short_b=hy_short_b, hy_w1=hy_w1, hy_b1=hy_b1,
               hy_freq=hy_freq, hy_w2=hy_w2, hy_b2=hy_b2, hy_w3=hy_w3, hy_decay=hy_decay,
               hy_bias=hy_bias, w_in_odd=w_in_odd, w_out_odd=w_out_odd, c_q_norm=c_q_norm,
               c_k_norm=c_k_norm, c_lambda=c_lambda, c_out_norm=c_out_norm, d_q_a_norm=d_q_a_norm,
               w_q_b=w_q_b, d_kv_a_norm=d_kv_a_norm, w_kv_b=w_kv_b, d_q_norm=d_q_norm,
               d_k_norm=d_k_norm)
    return (_trunk(x_prompt, mem_prompt, prm), _trunk(x_sample, mem_sample, prm))
```

```python
import functools
import math

import jax
import jax.numpy as jnp
from jax import lax
from jax.experimental import pallas as pl
from jax.experimental.pallas import tpu as pltpu

F32 = jnp.float32
BF16 = jnp.bfloat16

D_MODEL = 2048
N_MEM = 256
HEAD_DIM = 128
ROPE_THETA = 500000.0
ROT_DIM = HEAD_DIM // 4
EPS = 1e-6
BLOCK = 128
WINDOW = 128
NEG_INF = -1e30
A_HEADS = 8
A_KV_HEADS = 2
A_WIDTH = A_HEADS * HEAD_DIM
B_WIDTH = 1024
HY_EMB = 33
HY_FFN = 64
M_HEADS = 4
M_WIDTH = M_HEADS * HEAD_DIM
C_HEADS = 4
C_QK_DIM = 128
C_V_DIM = 256
C_WIDTH = C_HEADS * C_V_DIM
D_HEADS = 8
D_NOPE = 128
D_ROPE = 64
D_V = 128
D_Q_RANK = 512
D_KV_RANK = 256
D_WIDTH = D_HEADS * D_V
MIX_WIDTH = 2560
EVEN_SPLITS = (A_WIDTH, A_KV_HEADS * HEAD_DIM, A_KV_HEADS * HEAD_DIM, A_WIDTH,
               3 * B_WIDTH, B_WIDTH, M_WIDTH, M_WIDTH)
ODD_SPLITS = (C_HEADS * 2 * C_QK_DIM, C_HEADS * 2 * C_QK_DIM, C_WIDTH, C_WIDTH,
              D_Q_RANK, D_KV_RANK + D_ROPE, D_WIDTH, M_WIDTH, M_WIDTH)

LANES = 128
SUBLANES = 8
MXU_COLS = 256
DFT_N2 = 128
VMEM_LIMIT = 56 * 1024 * 1024

EV_GATE, EV_MQ, EV_BU, EV_AQ, EV_AK, EV_AV = 0, 2560, 3072, 6144, 7168, 7424
OD_GATE, OD_MQ, OD_CQ, OD_CK, OD_CV, OD_DQA, OD_CKV, OD_KR = 0, 2560, 3072, 4096, 5120, 6144, 6656, 6912
OD_WIDTH = 7168


def _tile(n, pref):
    t = min(n, pref)
    assert n % t == 0, (n, t)
    return t


def _col_tile(n, cap):
    best = None
    for t in range(MXU_COLS, min(n, cap) + 1, MXU_COLS):
        if n % t == 0:
            best = t
    assert best is not None, (n, cap)
    return best


def _params(sem):
    return pltpu.CompilerParams(dimension_semantics=sem, vmem_limit_bytes=VMEM_LIMIT)


def _rms(x, gain):
    ms = jnp.mean(x * x, axis=-1, keepdims=True)
    return x * lax.rsqrt(ms + EPS) * gain


def _silu(g):
    g = g.astype(F32)
    return g * jax.nn.sigmoid(g)


def _lanes(x, k):
    return x if k == 1 else jnp.concatenate([x] * k, axis=1)


def _rope(y, cf, sn):
    return y * cf + pltpu.roll(y, LANES // 2, 1) * sn


def _nmm_kernel(x_ref, g_ref, w_ref, hg_ref, cf_ref, sn_ref, o_ref, h_ref, acc_ref, *, preps):
    j = pl.program_id(1)

    def product():
        return jnp.dot(h_ref[...], w_ref[...], preferred_element_type=F32)

    assert 0 not in preps

    @pl.when(j == 0)
    def _():
        h = _rms(x_ref[...].astype(F32), g_ref[...]).astype(BF16)
        h_ref[...] = h
        o_ref[...] = jnp.dot(h, w_ref[...], preferred_element_type=F32).astype(o_ref.dtype)

    plain = j != 0
    for blk in preps:
        plain = jnp.logical_and(plain, j != blk)

    @pl.when(plain)
    def _():
        o_ref[...] = product().astype(o_ref.dtype)

    for blk, (gain_row0, scales) in preps.items():
        @pl.when(j == blk)
        def _(gain_row0=gain_row0, scales=scales):
            acc_ref[...] = product()

            @pl.when(pl.program_id(0) >= 0)
            def _():
                cf, sn = cf_ref[...], sn_ref[...]
                for slot, scale in enumerate(scales):
                    cols = slice(slot * LANES, (slot + 1) * LANES)
                    y = acc_ref[:, cols]
                    if scale is not None:
                        y = _rope(_rms(y, hg_ref[gain_row0 + slot:gain_row0 + slot + 1, :]), cf, sn) * scale
                    o_ref[:, cols] = y.astype(o_ref.dtype)


def normed_matmul(x, gain, w, *, seq, col_block=0, tm=1024, tn=1536, preps=None, head_gains=None, tables=None):
    rows = x.shape[0]
    k, n = w.shape
    tm = _tile(seq, tm)
    tn = _col_tile(n, tn)
    if preps is None:
        preps, head_gains = {}, jnp.ones((8, LANES), F32)
        tables = (jnp.ones((tm, LANES), F32), jnp.zeros((tm, LANES), F32))
    tab_spec = pl.BlockSpec((tm, LANES), lambda i, j: (i % (tables[0].shape[0] // tm), 0))
    return pl.pallas_call(
        functools.partial(_nmm_kernel, preps=preps),
        grid=(rows // tm, n // tn),
        in_specs=[pl.BlockSpec((tm, k), lambda i, j: (i, col_block)),
                  pl.BlockSpec((1, k), lambda i, j: (0, 0)),
                  pl.BlockSpec((k, tn), lambda i, j: (0, j)),
                  pl.BlockSpec(head_gains.shape, lambda i, j: (0, 0)),
                  tab_spec, tab_spec],
        out_specs=pl.BlockSpec((tm, tn), lambda i, j: (i, j)),
        out_shape=jax.ShapeDtypeStruct((rows, n), BF16),
        scratch_shapes=[pltpu.VMEM((tm, k), BF16), pltpu.VMEM((tm, tn) if preps else (SUBLANES, LANES), F32)],
        compiler_params=_params(("parallel", "arbitrary")),
        name="normed_matmul",
    )(x, gain.reshape(1, k).astype(F32), w, head_gains.astype(F32), *tables)


WIN_QB = 8


def _window_kernel(sink_ref, q_ref, kp_ref, kc_ref, kn_ref, vp_ref, vc_ref, vn_ref, g_ref, o_ref, s_ref, p_ref, *,
                   seq, qb):
    n = pl.program_id(1)
    group = A_HEADS // A_KV_HEADS
    rows = group * BLOCK
    qi = lax.broadcasted_iota(jnp.int32, (rows, 3 * BLOCK), 0) & (BLOCK - 1)
    kj = lax.broadcasted_iota(jnp.int32, (rows, 3 * BLOCK), 1)
    in_win = jnp.abs(kj - BLOCK - qi) <= WINDOW
    pairs = [(kv, i) for kv in range(A_KV_HEADS) for i in range(qb)]

    def kv_rows(refs, kv):
        cols = slice(kv * HEAD_DIM, (kv + 1) * HEAD_DIM)
        return jnp.concatenate([r[:, cols] for r in refs], axis=0).astype(BF16)

    kfull = [kv_rows((kp_ref, kc_ref, kn_ref), kv) for kv in range(A_KV_HEADS)]
    for idx, (kv, i) in enumerate(pairs):
        qrows = slice(i * BLOCK, (i + 1) * BLOCK)
        q = jnp.concatenate([q_ref[qrows, h * HEAD_DIM:(h + 1) * HEAD_DIM]
                             for h in range(kv * group, (kv + 1) * group)], axis=0)
        s_ref[idx] = lax.dot_general(q, kfull[kv][i * BLOCK:(i + 3) * BLOCK], (((1,), (1,)), ((), ())),
                                     preferred_element_type=F32)

    for idx, (kv, i) in enumerate(pairs):
        sk = jnp.concatenate([jnp.full((BLOCK, 1), sink_ref[h], F32)
                              for h in range(kv * group, (kv + 1) * group)], axis=0)
        kpos = (n * qb + i - 1) * BLOCK + kj
        s = jnp.where(in_win & (kpos >= 0) & (kpos < seq), s_ref[idx], NEG_INF)
        m = jnp.maximum(jnp.max(s, axis=-1, keepdims=True), sk)
        pr = jnp.exp(s - m)
        denom = jnp.sum(pr, axis=-1, keepdims=True) + jnp.exp(sk - m)
        p_ref[idx] = (pr / denom).astype(BF16)

    vfull = [kv_rows((vp_ref, vc_ref, vn_ref), kv) for kv in range(A_KV_HEADS)]
    for idx, (kv, i) in enumerate(pairs):
        qrows = slice(i * BLOCK, (i + 1) * BLOCK)
        o = jnp.dot(p_ref[idx], vfull[kv][i * BLOCK:(i + 3) * BLOCK], preferred_element_type=F32)
        for g, h in enumerate(range(kv * group, (kv + 1) * group)):
            hc = slice(h * HEAD_DIM, (h + 1) * HEAD_DIM)
            o_ref[qrows, hc] = (o[g * BLOCK:(g + 1) * BLOCK] * _silu(g_ref[qrows, hc])).astype(o_ref.dtype)


def window_attention(p, q_col, k_col, v_col, g_col, sink, batch, seq):
    nblk = seq // BLOCK
    qb = _tile(nblk, WIN_QB)
    nstep = nblk // qb
    kvw = A_KV_HEADS * HEAD_DIM
    kb, vb = k_col // kvw, v_col // kvw

    def prev(b, n):
        return b * nblk + jnp.maximum(n * qb - 1, 0)

    def cur(b, n):
        return b * nstep + n

    def nxt(b, n):
        return b * nblk + jnp.minimum((n + 1) * qb, nblk - 1)

    return pl.pallas_call(
        functools.partial(_window_kernel, seq=seq, qb=qb),
        grid=(batch, nstep),
        in_specs=[pl.BlockSpec(memory_space=pltpu.SMEM),
                  pl.BlockSpec((qb * BLOCK, A_WIDTH), lambda b, n: (cur(b, n), q_col // A_WIDTH)),
                  pl.BlockSpec((BLOCK, kvw), lambda b, n: (prev(b, n), kb)),
                  pl.BlockSpec((qb * BLOCK, kvw), lambda b, n: (cur(b, n), kb)),
                  pl.BlockSpec((BLOCK, kvw), lambda b, n: (nxt(b, n), kb)),
                  pl.BlockSpec((BLOCK, kvw), lambda b, n: (prev(b, n), vb)),
                  pl.BlockSpec((qb * BLOCK, kvw), lambda b, n: (cur(b, n), vb)),
                  pl.BlockSpec((BLOCK, kvw), lambda b, n: (nxt(b, n), vb)),
                  pl.BlockSpec((qb * BLOCK, A_WIDTH), lambda b, n: (cur(b, n), g_col // A_WIDTH))],
        out_specs=pl.BlockSpec((qb * BLOCK, A_WIDTH), lambda b, n: (cur(b, n), 0)),
        out_shape=jax.ShapeDtypeStruct((batch * seq, A_WIDTH), BF16),
        scratch_shapes=[pltpu.VMEM((A_KV_HEADS * qb, A_HEADS // A_KV_HEADS * BLOCK, 3 * BLOCK), F32),
                        pltpu.VMEM((A_KV_HEADS * qb, A_HEADS // A_KV_HEADS * BLOCK, 3 * BLOCK), BF16)],
        compiler_params=_params(("parallel", "parallel")),
        name="window_attention",
    )(sink.astype(F32), p, p, p, p, p, p, p, p)


def _mem_kernel(q_ref, kv_ref, qg_ref, kg_ref, g_ref, o_ref):
    qg, kg = qg_ref[...], kg_ref[...]
    scale = HEAD_DIM ** -0.5
    for h in range(M_HEADS):
        cols = slice(h * HEAD_DIM, (h + 1) * HEAD_DIM)
        q = (_rms(q_ref[:, cols].astype(F32), qg) * scale).astype(BF16)
        k = _rms(kv_ref[:, cols].astype(F32), kg).astype(BF16)
        v = kv_ref[:, M_WIDTH + h * HEAD_DIM:M_WIDTH + (h + 1) * HEAD_DIM].astype(BF16)
        s = lax.dot_general(q, k, (((1,), (1,)), ((), ())), preferred_element_type=F32)
        m = jnp.max(s, axis=-1, keepdims=True)
        pr = jnp.exp(s - m)
        pr = pr / jnp.sum(pr, axis=-1, keepdims=True)
        o = jnp.dot(pr.astype(BF16), v, preferred_element_type=F32)
        o_ref[:, cols] = (o * _silu(g_ref[:, cols])).astype(o_ref.dtype)


def memory_attention(p, q_col, g_col, kvm, q_gain, k_gain, batch, seq, *, tq=512):
    tq = _tile(seq, tq)
    nq = seq // tq
    return pl.pallas_call(
        _mem_kernel,
        grid=(batch, nq),
        in_specs=[pl.BlockSpec((tq, M_WIDTH), lambda b, i: (b * nq + i, q_col // M_WIDTH)),
                  pl.BlockSpec((N_MEM, 2 * M_WIDTH), lambda b, i: (b, 0)),
                  pl.BlockSpec((1, HEAD_DIM), lambda b, i: (0, 0)),
                  pl.BlockSpec((1, HEAD_DIM), lambda b, i: (0, 0)),
                  pl.BlockSpec((tq, M_WIDTH), lambda b, i: (b * nq + i, g_col // M_WIDTH))],
        out_specs=pl.BlockSpec((tq, M_WIDTH), lambda b, i: (b * nq + i, 0)),
        out_shape=jax.ShapeDtypeStruct((batch * seq, M_WIDTH), BF16),
        compiler_params=_params(("parallel", "parallel")),
        name="memory_attention",
    )(p, kvm, q_gain.reshape(1, HEAD_DIM).astype(F32), k_gain.reshape(1, HEAD_DIM).astype(F32), p)


LOG2E = math.log2(math.e)


def _chunk_rows(chunk, tk):
    return pl.ds(pl.multiple_of(chunk * tk, tk), tk)


MLA_TRIP_CHUNKS = 4
DIFF_TRIP_CHUNKS = 2


def _pipelined_chunks(scores, consume, nk, per_trip):
    scores(0, 0)

    def body(i, carry):
        for j in range(per_trip):
            chunk = per_trip * i + j
            scores(jnp.minimum(chunk + 1, nk - 1), (j + 1) % 2)
            consume(chunk, j % 2)
        return carry

    lax.fori_loop(0, nk // per_trip, body, 0)


def _diff_flash_kernel(q_ref, k_ref, v_ref, lam_ref, g_ref, gate_ref, o_ref, m_ref, l_ref, acc_ref, s_ref, *,
                       tk, nk, lam_init):
    m_ref[...] = jnp.full(m_ref.shape, NEG_INF, F32)
    l_ref[...] = jnp.zeros(l_ref.shape, F32)
    acc_ref[...] = jnp.zeros(acc_ref.shape, F32)

    def scores(chunk, c):
        cols = slice(c * C_QK_DIM, (c + 1) * C_QK_DIM)
        s_ref[c] = lax.dot_general(q_ref[:, cols], k_ref[_chunk_rows(chunk, tk), cols], (((1,), (1,)), ((), ())),
                                   preferred_element_type=F32)

    def consume(chunk, c):
        s = s_ref[c]
        m_prev = m_ref[c]
        m_new = jnp.maximum(m_prev, jnp.max(s, axis=-1, keepdims=True))
        alpha = jnp.exp2(m_prev - m_new)
        pr = jnp.exp2(s - _lanes(m_new, tk // LANES))
        l_ref[c] = alpha * l_ref[c] + jnp.sum(pr, axis=-1, keepdims=True)
        acc_ref[c] = (_lanes(alpha, C_V_DIM // LANES) * acc_ref[c]
                      + jnp.dot(pr.astype(BF16), v_ref[_chunk_rows(chunk, tk), :], preferred_element_type=F32))
        m_ref[c] = m_new

    scores(0, 0)

    def body(i, carry):
        for step in range(DIFF_TRIP_CHUNKS):
            j = DIFF_TRIP_CHUNKS * i + step
            scores(j, 1)
            consume(j, 0)
            scores(jnp.minimum(j + 1, nk - 1), 0)
            consume(j, 1)
        return carry

    lax.fori_loop(0, nk // DIFF_TRIP_CHUNKS, body, 0)
    lv = lam_ref[...]
    lam = (jnp.exp(jnp.sum(lv[0:1] * lv[1:2], axis=-1, keepdims=True))
           - jnp.exp(jnp.sum(lv[2:3] * lv[3:4], axis=-1, keepdims=True)) + lam_init)
    o1 = acc_ref[0] / _lanes(l_ref[0], C_V_DIM // LANES)
    o2 = acc_ref[1] / _lanes(l_ref[1], C_V_DIM // LANES)
    o_ref[...] = (_rms(o1 - lam * o2, g_ref[...]) * (1.0 - lam_init) * _silu(gate_ref[...])).astype(o_ref.dtype)


def _mla_flash_kernel(q_ref, k_ref, v_ref, gate_ref, o_ref, m_ref, acc_ref, s_ref, *, tk, nk):
    m_ref[...] = jnp.full(m_ref.shape, NEG_INF, F32)
    acc_ref[...] = jnp.zeros(acc_ref.shape, F32)
    ones = jnp.ones((tk, LANES), BF16)

    def scores(chunk, slot):
        s_ref[slot] = lax.dot_general(q_ref[...], k_ref[_chunk_rows(chunk, tk), :], (((1,), (1,)), ((), ())),
                                      preferred_element_type=F32)

    def consume(chunk, slot):
        s = s_ref[slot]
        m_prev = m_ref[...]
        m_new = jnp.maximum(m_prev, jnp.max(s, axis=-1, keepdims=True))
        alpha = jnp.exp2(m_prev - m_new)
        pr = jnp.exp2(s - _lanes(m_new, tk // LANES))
        vv = jnp.concatenate([v_ref[_chunk_rows(chunk, tk), :], ones], axis=1)
        acc_ref[...] = (_lanes(alpha, 2) * acc_ref[...]
                        + jnp.dot(pr.astype(BF16), vv, preferred_element_type=F32))
        m_ref[...] = m_new

    _pipelined_chunks(scores, consume, nk, MLA_TRIP_CHUNKS)
    o_ref[...] = (acc_ref[:, :D_V] / acc_ref[:, D_V:] * _silu(gate_ref[...])).astype(o_ref.dtype)


def _flash_call(kernel, q, q_block0, k, k_block0, v, v_block0, gate, gate_block0, extra, extra_specs, scratch, *,
                heads, dv, batch, seq, tq, tk, kv_buffers, name):
    nq, nk = seq // tq, seq // tk
    qk_w = 2 * LANES
    resident = pl.Buffered(kv_buffers)
    return pl.pallas_call(
        functools.partial(kernel, tk=tk, nk=nk),
        grid=(batch, heads, nq),
        in_specs=[pl.BlockSpec((tq, qk_w), lambda b, h, i: (b * nq + i, q_block0 + h)),
                  pl.BlockSpec((seq, qk_w), lambda b, h, i: (b, k_block0 + h), pipeline_mode=resident),
                  pl.BlockSpec((seq, dv), lambda b, h, i: (b, v_block0 + h), pipeline_mode=resident)]
        + extra_specs + [pl.BlockSpec((tq, dv), lambda b, h, i: (b * nq + i, gate_block0 + h))],
        out_specs=pl.BlockSpec((tq, dv), lambda b, h, i: (b * nq + i, h)),
        out_shape=jax.ShapeDtypeStruct((batch * seq, heads * dv), BF16),
        scratch_shapes=scratch,
        compiler_params=_params(("parallel", "parallel", "arbitrary")),
        name=name,
    )(q, k, v, *extra, gate)


def diff_attention(p, q_col, k_col, v_col, g_col, c_lambda, out_gain, lam_init, batch, seq, *, tq=1024, tk=1024):
    tq, tk = _tile(seq, tq), _tile(seq // DIFF_TRIP_CHUNKS, tk)
    const = lambda b, h, i: (0, 0)
    return _flash_call(
        functools.partial(_diff_flash_kernel, lam_init=lam_init), p, q_col // C_V_DIM, p, k_col // C_V_DIM,
        p, v_col // C_V_DIM, p, g_col // C_V_DIM,
        [c_lambda.astype(F32), out_gain.reshape(1, C_V_DIM).astype(F32)],
        [pl.BlockSpec((4, C_QK_DIM), const), pl.BlockSpec((1, C_V_DIM), const)],
        [pltpu.VMEM((2, tq, LANES), F32), pltpu.VMEM((2, tq, LANES), F32), pltpu.VMEM((2, tq, C_V_DIM), F32),
         pltpu.VMEM((2, tq, tk), F32)],
        heads=C_HEADS, dv=C_V_DIM, batch=batch, seq=seq, tq=tq, tk=tk, kv_buffers=1, name="diff_attention")


def mla_attention(q, k, v, p, g_col, batch, seq, *, tq=2048, tk=512):
    tq, tk = _tile(seq, tq), _tile(seq // MLA_TRIP_CHUNKS, tk)
    return _flash_call(_mla_flash_kernel, q, 0, k, 0, v, 0, p, g_col // D_V, [], [],
                       [pltpu.VMEM((tq, LANES), F32), pltpu.VMEM((tq, D_V + LANES), F32),
                        pltpu.VMEM((2, tq, tk), F32)],
                       heads=D_HEADS, dv=D_V, batch=batch, seq=seq, tq=tq, tk=tk, kv_buffers=2,
                       name="mla_attention")


def _mla_proj_kernel(*refs, shared_rope, scale):
    if shared_rope:
        x_ref, g_ref, w_ref, kr_ref, gn_ref, gr_ref, cf_ref, sn_ref, o_ref, v_ref = refs
    else:
        x_ref, g_ref, w_ref, gn_ref, gr_ref, cf_ref, sn_ref, o_ref = refs
    h = _rms(x_ref[...].astype(F32), g_ref[...]).astype(BF16)
    acc = jnp.dot(h, w_ref[...], preferred_element_type=F32)
    gn, gr = gn_ref[...], gr_ref[...]
    cf, sn = cf_ref[...], sn_ref[...]
    width = D_HEADS * LANES
    for hd in range(D_HEADS):
        cols = slice(hd * LANES, (hd + 1) * LANES)
        xn = acc[:, cols]
        xr = kr_ref[...].astype(F32) if shared_rope else acc[:, width + hd * LANES:width + (hd + 1) * LANES]
        ms = (jnp.sum(xn * xn, axis=-1, keepdims=True)
              + jnp.sum(xr * xr, axis=-1, keepdims=True)) * (1.0 / (D_NOPE + D_ROPE))
        inv = lax.rsqrt(ms + EPS)
        o_ref[:, 2 * hd * LANES:(2 * hd + 1) * LANES] = (xn * inv * gn * scale).astype(o_ref.dtype)
        o_ref[:, (2 * hd + 1) * LANES:(2 * hd + 2) * LANES] = (_rope(xr * inv * gr, cf, sn) * scale).astype(o_ref.dtype)
    if shared_rope:
        v_ref[...] = acc[:, width:].astype(v_ref.dtype)


def mla_projection(p, col, rank, gain, w, head_gain, tables, scale, seq, *, rope_col=None, tm=512):
    rows = p.shape[0]
    width = D_HEADS * LANES
    tm = _tile(seq, tm)
    nblk = seq // tm
    shared_rope = rope_col is not None
    gn = head_gain[:D_NOPE].reshape(1, LANES).astype(F32)
    gr = _rope_slot(head_gain[D_NOPE:]).reshape(1, LANES).astype(F32)
    one = pl.BlockSpec((1, LANES), lambda i: (0, 0))
    tab_spec = pl.BlockSpec((tm, LANES), lambda i: (i % nblk, 0))
    in_specs = [pl.BlockSpec((tm, rank), lambda i: (i, col // rank)),
                pl.BlockSpec((1, rank), lambda i: (0, 0)),
                pl.BlockSpec((rank, 2 * width), lambda i: (0, 0))]
    args = [p, gain.reshape(1, rank).astype(F32), w]
    out_specs = [pl.BlockSpec((tm, 2 * width), lambda i: (i, 0))]
    out_shape = [jax.ShapeDtypeStruct((rows, 2 * width), BF16)]
    if shared_rope:
        in_specs.append(pl.BlockSpec((tm, LANES), lambda i: (i, rope_col // LANES)))
        args.append(p)
        out_specs.append(pl.BlockSpec((tm, width), lambda i: (i, 0)))
        out_shape.append(jax.ShapeDtypeStruct((rows, width), BF16))
    return pl.pallas_call(
        functools.partial(_mla_proj_kernel, shared_rope=shared_rope, scale=scale),
        grid=(rows // tm,),
        in_specs=in_specs + [one, one, tab_spec, tab_spec],
        out_specs=out_specs,
        out_shape=out_shape,
        compiler_params=_params(("parallel",)),
        name="mla_projection",
    )(*args, gn, gr, *tables)


SUB = SUBLANES
FILT_ROWS = SUB * DFT_N2
HALO = 16


def _half_rows(n1):
    return n1 // 2 + SUB


def _hyena_gate_kernel(*refs, nblk, tm):
    u_refs, up_refs, un_refs = refs[0:3], refs[3:6], refs[6:9]
    w_ref, b_ref, g_ref, zz_ref, x0_ref = refs[9:]
    li = pl.program_id(1)
    tc = zz_ref.shape[-1]
    row = lax.broadcasted_iota(jnp.int32, (tm, tc), 0)
    has_prev = (li > 0).astype(F32)
    has_next = (li < nblk - 1).astype(F32)

    def conv(part):
        u = u_refs[part][...].astype(F32)
        prev_row = up_refs[part][HALO - 1:HALO, :].astype(F32) * has_prev
        next_row = un_refs[part][0:1, :].astype(F32) * has_next
        above = jnp.where(row == 0, prev_row, pltpu.roll(u, 1, 0))
        below = jnp.where(row == tm - 1, next_row, pltpu.roll(u, tm - 1, 0))
        w = w_ref[part]
        return above * w[0:1] + u * w[1:2] + below * w[2:3] + b_ref[part]

    x0 = conv(0) * _silu(g_ref[...])
    zz = conv(2) * conv(1)
    for a in range(tm // DFT_N2):
        rows = slice(a * DFT_N2, (a + 1) * DFT_N2)
        zz_ref[:, a, :] = zz[rows]
        x0_ref[:, a, :] = x0[rows]


def hyena_gate(p, u_col, g_col, short_w, short_b, batch, seq, *, tc=512):
    tm = FILT_ROWS
    nblk = seq // tm
    half = seq // DFT_N2
    tc = _tile(B_WIDTH, tc)
    nc = B_WIDTH // tc
    hb = tm // HALO
    last_halo = batch * nblk * hb - 1

    def part_specs(part):
        cb = (u_col + part * B_WIDTH) // tc
        return (pl.BlockSpec((tm, tc), lambda b, i, c: (b * nblk + i, cb + c)),
                pl.BlockSpec((HALO, tc), lambda b, i, c: (jnp.maximum((b * nblk + i) * hb - 1, 0), cb + c)),
                pl.BlockSpec((HALO, tc), lambda b, i, c: (jnp.minimum((b * nblk + i + 1) * hb, last_halo), cb + c)))

    specs = [part_specs(part) for part in range(3)]
    out = jax.ShapeDtypeStruct((batch, DFT_N2, half, B_WIDTH), F32)
    o_spec = pl.BlockSpec((None, DFT_N2, tm // DFT_N2, tc), lambda b, i, c: (b, 0, i, c))
    w3 = short_w.astype(F32).reshape(3, 3, B_WIDTH).transpose(1, 0, 2)
    b3 = short_b.astype(F32).reshape(3, 1, B_WIDTH)
    return pl.pallas_call(
        functools.partial(_hyena_gate_kernel, nblk=nblk, tm=tm),
        grid=(batch, nblk, nc),
        in_specs=[sp[0] for sp in specs] + [sp[1] for sp in specs] + [sp[2] for sp in specs]
        + [pl.BlockSpec((3, 3, tc), lambda b, i, c: (0, 0, c)),
           pl.BlockSpec((3, 1, tc), lambda b, i, c: (0, 0, c)),
           pl.BlockSpec((tm, tc), lambda b, i, c: (b * nblk + i, g_col // tc + c))],
        out_specs=[o_spec, o_spec],
        out_shape=[out, out],
        compiler_params=_params(("parallel", "parallel", "parallel")),
        name="hyena_gate",
    )(*([p] * 9), w3, b3, p)


def _hdot(a, b):
    return jnp.dot(a, b, preferred_element_type=F32, precision=lax.Precision.HIGHEST)


def _filter_kernel(fr_ref, w1_ref, b1_ref, fq_ref, w2_ref, b2_ref, w3_ref, dec_ref, k_ref, s_ref, t_ref, h_ref, *,
                   seq, tr):
    i, c = pl.program_id(0), pl.program_id(1)
    r = i * tr + lax.broadcasted_iota(jnp.int32, (tr, 1), 0)

    @pl.when(c == 0)
    def _():
        pos = jnp.where(r < seq, r, 2 * seq - 1 - r).astype(F32)
        t = pos * (1.0 / (seq - 1))
        w = pos * (2.0 * math.pi / seq)
        lane = lax.broadcasted_iota(jnp.int32, (tr, LANES), 1)
        bands = (HY_EMB - 1) // 2
        z = jnp.where(lane == 0, t, jnp.where(lane <= 2 * bands, jnp.cos(w * fr_ref[0:1] + fr_ref[1:2]), 0.0))
        h = jnp.sin(fq_ref[0:1] * (_hdot(z, w1_ref[...]) + b1_ref[...]))
        h_ref[...] = jnp.sin(fq_ref[1:2] * (_hdot(h, w2_ref[...]) + b2_ref[...]))
        t_ref[...] = jnp.broadcast_to(t, t_ref.shape)

    h = _hdot(h_ref[...], w3_ref[...]) * jnp.exp(-t_ref[:, 0:1] * jnp.abs(dec_ref[...]))
    h = jnp.where(r == seq, 0.0, h)
    for a in range(tr // DFT_N2):
        k_ref[:, a, :] = h[a * DFT_N2:(a + 1) * DFT_N2]

    @pl.when(jnp.logical_and(i == 0, c == 0))
    def _():
        s_ref[...] = jnp.zeros(s_ref.shape, F32)

    tc = h.shape[1]
    col_sum = jnp.sum(jnp.abs(h), axis=0, keepdims=True)
    for cc in range(s_ref.shape[1] // tc):
        @pl.when(c == cc)
        def _(cc=cc):
            s_ref[:, cc * tc:(cc + 1) * tc] += col_sum


def hyena_filter(seq, w1, b1, freq, w2, b2, w3, decay, *, tc=512):
    tr = FILT_ROWS
    assert seq % tr == 0
    nhalf = seq // tr
    n1 = 2 * seq // DFT_N2
    tc = _tile(B_WIDTH, tc)
    nc = B_WIDTH // tc
    bands = (HY_EMB - 1) // 2
    fr = jnp.linspace(1e-4, bands - 1, bands, dtype=F32)
    pad = jnp.zeros((LANES - HY_EMB,), F32)
    fr_lanes = jnp.stack([jnp.concatenate([jnp.zeros((1,), F32), fr, fr, pad]),
                          jnp.concatenate([jnp.zeros((1 + bands,), F32), jnp.full((bands,), 0.5 * math.pi, F32), pad])])
    w1p = jnp.pad(w1.astype(F32), ((0, LANES - HY_EMB), (0, 0)))
    const = lambda i, c: (0, 0)
    half = lambda i, c: (0, (i // nhalf) * nc + c)
    return pl.pallas_call(
        functools.partial(_filter_kernel, seq=seq, tr=tr),
        grid=(2 * nhalf, nc),
        in_specs=[pl.BlockSpec((2, LANES), const),
                  pl.BlockSpec((LANES, HY_FFN), const),
                  pl.BlockSpec((1, HY_FFN), const),
                  pl.BlockSpec((2, HY_FFN), const),
                  pl.BlockSpec((HY_FFN, HY_FFN), const),
                  pl.BlockSpec((1, HY_FFN), const),
                  pl.BlockSpec((HY_FFN, tc), half),
                  pl.BlockSpec((1, tc), half)],
        out_specs=[pl.BlockSpec((DFT_N2, tr // DFT_N2, tc), lambda i, c: (0, i, c)),
                   pl.BlockSpec((1, B_WIDTH), const)],
        out_shape=[jax.ShapeDtypeStruct((DFT_N2, n1, B_WIDTH), F32),
                   jax.ShapeDtypeStruct((1, B_WIDTH), F32)],
        scratch_shapes=[pltpu.VMEM((tr, LANES), F32), pltpu.VMEM((tr, HY_FFN), F32)],
        compiler_params=_params(("arbitrary", "arbitrary")),
        name="hyena_filter",
    )(fr_lanes, w1p, b1.reshape(1, HY_FFN).astype(F32), freq.astype(F32), w2.astype(F32),
      b2.reshape(1, HY_FFN).astype(F32), w3.astype(F32), decay.reshape(1, 2 * B_WIDTH).astype(F32))


def _bdot(a, b):
    return jnp.dot(a, b, preferred_element_type=F32)


def dft_tables(n1):
    n = n1 * DFT_N2
    rows = _half_rows(n1)
    kk = jnp.arange(rows, dtype=jnp.int32)
    k1 = kk[None, :, None]
    m1 = jnp.arange(n1, dtype=jnp.int32)[None, None, :]
    n2 = jnp.arange(DFT_N2, dtype=jnp.int32)[:, None, None]
    ang = ((k1 * (DFT_N2 * m1 + n2)) % n).astype(F32) * (2.0 * math.pi / n)
    live = (kk <= n1 // 2).astype(F32)[None, :, None]
    weight = jnp.where((kk == 0) | (kk == n1 // 2), 1.0, 2.0)[None, :, None] * live
    cos, sin = jnp.cos(ang), jnp.sin(ang)
    g = jnp.concatenate([cos * live, -sin * live], axis=1).astype(BF16)
    hc = jnp.swapaxes(cos * weight, 1, 2)[:, :n1 // 2].astype(BF16)
    hsn = jnp.swapaxes(-sin * weight, 1, 2)[:, :n1 // 2].astype(BF16)
    a = jnp.arange(DFT_N2, dtype=jnp.int32)
    ang2 = ((a[:, None] * a[None, :]) % DFT_N2).astype(F32) * (2.0 * math.pi / DFT_N2)
    ff = jnp.concatenate([jnp.cos(ang2), jnp.sin(ang2)], axis=0).astype(BF16)
    return g, hc, hsn, ff


def _dft1_kernel(x_ref, g_ref, ar_ref, ai_ref):
    rows = ar_ref.shape[0]
    for j in range(SUB):
        a = _bdot(g_ref[j], x_ref[j].astype(BF16))
        ar_ref[:, j, :] = a[:rows]
        ai_ref[:, j, :] = a[rows:]


def dft_stage1(x, g, *, tc=512):
    batch, _, k1rows, c = x.shape
    rows = g.shape[1] // 2
    tc = _tile(c, tc)
    out = jax.ShapeDtypeStruct((batch, rows, DFT_N2, c), F32)
    g_spec = pl.BlockSpec((SUB, 2 * rows, k1rows), lambda g, b, ci: (g, 0, 0))
    o_spec = pl.BlockSpec((None, rows, SUB, tc), lambda g, b, ci: (b, 0, g, ci))
    return pl.pallas_call(
        _dft1_kernel,
        grid=(DFT_N2 // SUB, batch, c // tc),
        in_specs=[pl.BlockSpec((None, SUB, k1rows, tc), lambda g, b, ci: (b, g, 0, ci)), g_spec],
        out_specs=[o_spec, o_spec],
        out_shape=[out, out],
        compiler_params=_params(("parallel", "parallel", "parallel")),
        name="dft_stage1",
    )(x, g)


def _stage2(ff, ar, ai):
    p = _bdot(ff, ar)
    q = _bdot(ff, ai)
    return p[:DFT_N2] + q[DFT_N2:], q[:DFT_N2] - p[DFT_N2:]


def _filter_spectrum_kernel(ar_ref, ai_ref, s_ref, ff_ref, kr_ref, ki_ref, *, n):
    ff = ff_ref[...]
    scale = 1.0 / (s_ref[...] * n)
    for j in range(SUB):
        xr, xi = _stage2(ff, ar_ref[j].astype(BF16), ai_ref[j].astype(BF16))
        kr_ref[j] = xr * scale
        ki_ref[j] = xi * scale


def filter_spectrum(ar, ai, asum, ff, n, *, tc=512):
    rows, _, c = ar.shape
    tc = _tile(c, tc)
    a_spec = pl.BlockSpec((SUB, DFT_N2, tc), lambda k, ci: (k, 0, ci))
    out = jax.ShapeDtypeStruct((rows, DFT_N2, c), F32)
    return pl.pallas_call(
        functools.partial(_filter_spectrum_kernel, n=n),
        grid=(rows // SUB, c // tc),
        in_specs=[a_spec, a_spec, pl.BlockSpec((1, tc), lambda k, ci: (0, ci)),
                  pl.BlockSpec((2 * DFT_N2, DFT_N2), lambda k, ci: (0, 0))],
        out_specs=[a_spec, a_spec],
        out_shape=[out, out],
        compiler_params=_params(("parallel", "parallel")),
        name="filter_spectrum",
    )(ar, ai, asum, ff)


def _spectral_kernel(ar_ref, ai_ref, kr_ref, ki_ref, ff_ref, br_ref, bi_ref):
    ff = ff_ref[...]
    for j in range(SUB):
        xr, xi = _stage2(ff, ar_ref[j].astype(BF16), ai_ref[j].astype(BF16))
        kr, ki = kr_ref[j], ki_ref[j]
        yr = (xr * kr - xi * ki).astype(BF16)
        yi = (xr * ki + xi * kr).astype(BF16)
        r = _bdot(ff, yr)
        s = _bdot(ff, yi)
        br_ref[:, j, :] = r[:DFT_N2] - s[DFT_N2:]
        bi_ref[:, j, :] = s[:DFT_N2] + r[DFT_N2:]


def spectral_multiply(ar, ai, kr, ki, ff, *, tc=512):
    batch, rows, _, c = ar.shape
    tc = _tile(c, tc)
    a_spec = pl.BlockSpec((None, SUB, DFT_N2, tc), lambda k, ci, b: (b, k, 0, ci))
    k_spec = pl.BlockSpec((SUB, DFT_N2, tc), lambda k, ci, b: (k, 0, ci))
    o_spec = pl.BlockSpec((None, DFT_N2, SUB, tc), lambda k, ci, b: (b, 0, k, ci))
    out = jax.ShapeDtypeStruct((batch, DFT_N2, rows, c), F32)
    return pl.pallas_call(
        _spectral_kernel,
        grid=(rows // SUB, c // tc, batch),
        in_specs=[a_spec, a_spec, k_spec, k_spec, pl.BlockSpec((2 * DFT_N2, DFT_N2), lambda k, ci, b: (0, 0))],
        out_specs=[o_spec, o_spec],
        out_shape=[out, out],
        compiler_params=_params(("parallel", "parallel", "parallel")),
        name="spectral_multiply",
    )(ar, ai, kr, ki, ff)


def _idft_out_kernel(br_ref, bi_ref, hc_ref, hs_ref, zz_ref, x0_ref, bias_ref, o_ref):
    bias = bias_ref[...]
    for j in range(SUB):
        y = _bdot(hc_ref[j], br_ref[j].astype(BF16)) + _bdot(hs_ref[j], bi_ref[j].astype(BF16))
        o_ref[:, j, :] = (y + zz_ref[j] * bias) * x0_ref[j]


def idft_output(br, bi, hc, hsn, zz, x0, bias, *, tc=512):
    batch, _, rows, c = br.shape
    half = hc.shape[1]
    tc = _tile(c, tc)
    b_spec = pl.BlockSpec((None, SUB, rows, tc), lambda g, b, ci: (b, g, 0, ci))
    h_spec = pl.BlockSpec((SUB, half, rows), lambda g, b, ci: (g, 0, 0))
    z_spec = pl.BlockSpec((None, SUB, half, tc), lambda g, b, ci: (b, g, 0, ci))
    return pl.pallas_call(
        _idft_out_kernel,
        grid=(DFT_N2 // SUB, batch, c // tc),
        in_specs=[b_spec, b_spec, h_spec, h_spec, z_spec, z_spec, pl.BlockSpec((1, tc), lambda g, b, ci: (0, ci))],
        out_specs=pl.BlockSpec((half, SUB, tc), lambda g, b, ci: (b, g, ci)),
        out_shape=jax.ShapeDtypeStruct((batch * half, DFT_N2, c), F32),
        compiler_params=_params(("parallel", "parallel", "parallel")),
        name="idft_output",
    )(br, bi, hc, hsn, zz, x0, bias)


def hyena_mixer(p, u_col, g_col, prm, e, batch, seq):
    n1 = 2 * seq // DFT_N2
    g, hc, hsn, ff = dft_tables(n1)
    kern, asum = hyena_filter(seq, prm["hy_w1"][e], prm["hy_b1"][e], prm["hy_freq"][e], prm["hy_w2"][e],
                              prm["hy_b2"][e], prm["hy_w3"][e], prm["hy_decay"][e])
    far, fai = dft_stage1(kern[None], g)
    kr, ki = filter_spectrum(far[0], fai[0], asum, ff, n1 * DFT_N2)
    zz, x0 = hyena_gate(p, u_col, g_col, prm["hy_short_w"][e], prm["hy_short_b"][e], batch, seq)
    ar, ai = dft_stage1(zz, g[:, :, :n1 // 2])
    br, bi = spectral_multiply(ar, ai, kr, ki, ff)
    yb = idft_output(br, bi, hc, hsn, zz, x0, prm["hy_bias"][e].reshape(1, B_WIDTH).astype(F32))
    return yb.reshape(batch * seq, B_WIDTH)


def _outproj_kernel(y1_ref, y2_ref, y3_ref, x_ref, w_ref, o_ref):
    acc = x_ref[...]
    off = 0
    for y_ref in (y1_ref, y2_ref, y3_ref):
        wd = y_ref.shape[1]
        acc = acc + jnp.dot(y_ref[...].astype(BF16), w_ref[off:off + wd, :], preferred_element_type=F32)
        off += wd
    o_ref[...] = acc


def out_projection(y1, y2, y3, x, w, *, tm=512):
    rows, d = x.shape
    tm = _tile(rows, tm)
    row = lambda i: (i, 0)
    return pl.pallas_call(
        _outproj_kernel,
        grid=(rows // tm,),
        in_specs=[pl.BlockSpec((tm, y1.shape[1]), row),
                  pl.BlockSpec((tm, y2.shape[1]), row),
                  pl.BlockSpec((tm, y3.shape[1]), row),
                  pl.BlockSpec((tm, d), row),
                  pl.BlockSpec((MIX_WIDTH, d), lambda i: (0, 0), pipeline_mode=pl.Buffered(1))],
        out_specs=pl.BlockSpec((tm, d), row),
        out_shape=jax.ShapeDtypeStruct((rows, d), F32),
        compiler_params=_params(("parallel",)),
        name="out_projection",
    )(y1, y2, y3, x, w)


def _split_cols(w, sizes):
    out, off = [], 0
    for s in sizes:
        out.append(w[:, off:off + s])
        off += s
    return out


ROT_HALF = ROT_DIM // 2
HEAD_PERM = (tuple(range(ROT_HALF)) + tuple(range(ROT_DIM, LANES // 2 + ROT_HALF))
             + tuple(range(ROT_HALF, ROT_DIM)) + tuple(range(LANES // 2 + ROT_HALF, LANES)))


def _permute_heads(w, nheads):
    k = w.shape[0]
    return w.reshape(k, nheads, HEAD_DIM)[:, :, jnp.array(HEAD_PERM)].reshape(k, nheads * HEAD_DIM)


def _rope_slot(w):
    half = D_ROPE // 2
    z = jnp.zeros(w.shape[:-1] + (LANES // 2 - half,), w.dtype)
    return jnp.concatenate([w[..., :half], z, w[..., half:], z], axis=-1)


def _rope_tables(seq, dim, passthrough):
    half = dim // 2
    inv = ROPE_THETA ** (-jnp.arange(0, dim, 2, dtype=F32) / dim)
    ang = jnp.arange(seq, dtype=F32)[:, None] * inv[None, :]
    cos, sin = jnp.cos(ang), jnp.sin(ang)
    rest = jnp.full((seq, LANES // 2 - half), 1.0 if passthrough else 0.0, F32)
    zeros = jnp.zeros((seq, LANES // 2 - half), F32)
    cf = jnp.concatenate([cos, rest, cos, rest], axis=1)
    sn = jnp.concatenate([-sin, zeros, sin, zeros], axis=1)
    return cf, sn


def _memory_kv(mem, prm, l):
    batch = mem.shape[0]
    return normed_matmul(mem.reshape(batch * N_MEM, D_MODEL), prm["mem_norm_gain"][l],
                         prm["w_mem_kv"][l].astype(BF16), seq=N_MEM)


def _head_gains(*groups):
    rows = [jnp.broadcast_to(g[jnp.array(HEAD_PERM)][None, :], (n, HEAD_DIM)) for g, n in groups]
    return jnp.concatenate(rows, axis=0)


def _even_layer(x, mem, prm, l, e, batch, seq, rope_a):
    aq, ak, av, ag, bu, bg, mq, mg = _split_cols(prm["w_in_even"][e], EVEN_SPLITS)
    w_in = jnp.concatenate([ag, bg, mg, mq, bu, _permute_heads(aq, A_HEADS), _permute_heads(ak, A_KV_HEADS), av],
                           axis=1).astype(BF16)
    tn = _col_tile(w_in.shape[1], 1536)
    assert EV_AQ % tn == 0 and w_in.shape[1] - EV_AQ == tn
    slots = [HEAD_DIM ** -0.5] * A_HEADS + [1.0] * A_KV_HEADS + [None] * A_KV_HEADS
    p = normed_matmul(x, prm["norm_gain"][l], w_in, seq=seq, tn=tn, preps={EV_AQ // tn: (0, slots)},
                      head_gains=_head_gains((prm["a_q_norm"][e], A_HEADS), (prm["a_k_norm"][e], A_KV_HEADS)),
                      tables=rope_a)
    ya = window_attention(p, EV_AQ, EV_AK, EV_AV, EV_GATE, prm["a_sink"][e], batch, seq)
    yb = hyena_mixer(p, EV_BU, EV_GATE + A_WIDTH, prm, e, batch, seq)
    kvm = _memory_kv(mem, prm, l)
    ym = memory_attention(p, EV_MQ, EV_GATE + A_WIDTH + B_WIDTH, kvm, prm["mem_q_norm"][l], prm["mem_k_norm"][l],
                          batch, seq)
    return out_projection(ya, yb, ym, x, prm["w_out_even"][e].astype(BF16))


def _odd_layer(x, mem, prm, l, o, batch, seq, rope_a, rope_d):
    cq, ck, cv, cg, dqa, dkva, dg, mq, mg = _split_cols(prm["w_in_odd"][o], ODD_SPLITS)
    pad = jnp.zeros((D_MODEL, OD_WIDTH - (OD_KR + LANES)), F32)
    w_in = jnp.concatenate([cg, dg, mg, mq, _permute_heads(cq, 2 * C_HEADS), _permute_heads(ck, 2 * C_HEADS), cv, dqa,
                            dkva[:, :D_KV_RANK], _rope_slot(dkva[:, D_KV_RANK:]), pad], axis=1).astype(BF16)
    tn = _col_tile(w_in.shape[1], 1024)
    assert OD_CQ % tn == 0 and OD_CK - OD_CQ == tn and OD_CV - OD_CK == tn
    nslot = 2 * C_HEADS
    p = normed_matmul(x, prm["norm_gain"][l], w_in, seq=seq, tn=tn,
                      preps={OD_CQ // tn: (0, [C_QK_DIM ** -0.5 * LOG2E] * nslot), OD_CK // tn: (nslot, [1.0] * nslot)},
                      head_gains=_head_gains((prm["c_q_norm"][o], nslot), (prm["c_k_norm"][o], nslot)),
                      tables=rope_a)
    lam_init = 0.8 - 0.6 * math.exp(-0.3 * l)
    yc = diff_attention(p, OD_CQ, OD_CK, OD_CV, OD_GATE, prm["c_lambda"][o], prm["c_out_norm"][o], lam_init, batch, seq)
    wq = prm["w_q_b"][o].reshape(D_Q_RANK, D_HEADS, D_NOPE + D_ROPE)
    wq2 = jnp.concatenate([wq[:, :, :D_NOPE].reshape(D_Q_RANK, -1), _rope_slot(wq[:, :, D_NOPE:]).reshape(D_Q_RANK, -1)],
                          axis=1)
    wkv = prm["w_kv_b"][o].reshape(D_KV_RANK, D_HEADS, D_NOPE + D_V)
    wkv2 = jnp.concatenate([wkv[:, :, :D_NOPE].reshape(D_KV_RANK, -1), wkv[:, :, D_NOPE:].reshape(D_KV_RANK, -1)],
                           axis=1)
    dscale = (D_NOPE + D_ROPE) ** -0.5
    qd, = mla_projection(p, OD_DQA, D_Q_RANK, prm["d_q_a_norm"][o], wq2.astype(BF16), prm["d_q_norm"][o], rope_d,
                         dscale * LOG2E, seq)
    kd, vd = mla_projection(p, OD_CKV, D_KV_RANK, prm["d_kv_a_norm"][o], wkv2.astype(BF16), prm["d_k_norm"][o], rope_d,
                            1.0, seq, rope_col=OD_KR, tm=1024)
    yd = mla_attention(qd, kd, vd, p, OD_GATE + C_WIDTH, batch, seq)
    kvm = _memory_kv(mem, prm, l)
    ym = memory_attention(p, OD_MQ, OD_GATE + C_WIDTH + D_WIDTH, kvm, prm["mem_q_norm"][l], prm["mem_k_norm"][l],
                          batch, seq)
    return out_projection(yc, yd, ym, x, prm["w_out_odd"][o].astype(BF16))


def _trunk(x, mem, prm):
    batch, seq, d = x.shape
    depth = prm["norm_gain"].shape[0]
    rope_a = _rope_tables(seq, ROT_DIM, True)
    rope_d = _rope_tables(seq, D_ROPE, False)
    h = x.reshape(batch * seq, d)
    for l in range(depth):
        if l % 2 == 0:
            h = _even_layer(h, mem, prm, l, l // 2, batch, seq, rope_a)
        else:
            h = _odd_layer(h, mem, prm, l, l // 2, batch, seq, rope_a, rope_d)
    return h.reshape(batch, seq, d)


def kernel(x_prompt, x_sample, mem_prompt, mem_sample, norm_gain, mem_norm_gain, w_mem_kv, mem_q_norm, mem_k_norm, w_in_even, w_out_even, a_q_norm, a_k_norm, a_sink, hy_short_w, hy_short_b, hy_w1, hy_b1, hy_freq, hy_w2, hy_b2, hy_w3, hy_decay, hy_bias, w_in_odd, w_out_odd, c_q_norm, c_k_norm, c_lambda, c_out_norm, d_q_a_norm, w_q_b, d_kv_a_norm, w_kv_b, d_q_norm, d_k_norm):
    prm = dict(norm_gain=norm_gain, mem_norm_gain=mem_norm_gain, w_mem_kv=w_mem_kv,
               mem_q_norm=mem_q_norm, mem_k_norm=mem_k_norm, w_in_even=w_in_even,
               w_out_even=w_out_even, a_q_norm=a_q_norm, a_k_norm=a_k_norm, a_sink=a_sink,
               hy_short_w=hy_short_w, hy_short_b=hy_short_b, hy_w1=hy_w1, hy_b1=hy_b1,
               hy_freq=hy_freq, hy_w2=hy_w2, hy_b2=hy_b2, hy_w3=hy_w3, hy_decay=hy_decay,
               hy_bias=hy_bias, w_in_odd=w_in_odd, w_out_odd=w_out_odd, c_q_norm=c_q_norm,
               c_k_norm=c_k_norm, c_lambda=c_lambda, c_out_norm=c_out_norm, d_q_a_norm=d_q_a_norm,
               w_q_b=w_q_b, d_kv_a_norm=d_kv_a_norm, w_kv_b=w_kv_b, d_q_norm=d_q_norm,
               d_k_norm=d_k_norm)
    return (_trunk(x_prompt, mem_prompt, prm), _trunk(x_sample, mem_sample, prm))
```

```python
import functools
import math

import jax
import jax.numpy as jnp
from jax import lax
from jax.experimental import pallas as pl
from jax.experimental.pallas import tpu as pltpu

F32 = jnp.float32
BF16 = jnp.bfloat16

D_MODEL = 2048
N_MEM = 256
HEAD_DIM = 128
ROPE_THETA = 500000.0
ROT_DIM = HEAD_DIM // 4
EPS = 1e-6
BLOCK = 128
WINDOW = 128
NEG_INF = -1e30
A_HEADS = 8
A_KV_HEADS = 2
A_WIDTH = A_HEADS * HEAD_DIM
B_WIDTH = 1024
HY_EMB = 33
HY_FFN = 64
M_HEADS = 4
M_WIDTH = M_HEADS * HEAD_DIM
C_HEADS = 4
C_QK_DIM = 128
C_V_DIM = 256
C_WIDTH = C_HEADS * C_V_DIM
D_HEADS = 8
D_NOPE = 128
D_ROPE = 64
D_V = 128
D_Q_RANK = 512
D_KV_RANK = 256
D_WIDTH = D_HEADS * D_V
MIX_WIDTH = 2560
EVEN_SPLITS = (A_WIDTH, A_KV_HEADS * HEAD_DIM, A_KV_HEADS * HEAD_DIM, A_WIDTH,
               3 * B_WIDTH, B_WIDTH, M_WIDTH, M_WIDTH)
ODD_SPLITS = (C_HEADS * 2 * C_QK_DIM, C_HEADS * 2 * C_QK_DIM, C_WIDTH, C_WIDTH,
              D_Q_RANK, D_KV_RANK + D_ROPE, D_WIDTH, M_WIDTH, M_WIDTH)

LANES = 128
SUBLANES = 8
MXU_COLS = 256
DFT_N2 = 128
VMEM_LIMIT = 56 * 1024 * 1024

EV_GATE, EV_MQ, EV_BU, EV_AQ, EV_AK, EV_AV = 0, 2560, 3072, 6144, 7168, 7424
OD_GATE, OD_MQ, OD_CQ, OD_CK, OD_CV, OD_DQA, OD_CKV, OD_KR = 0, 2560, 3072, 4096, 5120, 6144, 6656, 6912
OD_WIDTH = 7168


def _tile(n, pref):
    t = min(n, pref)
    assert n % t == 0, (n, t)
    return t


def _col_tile(n, cap):
    best = None
    for t in range(MXU_COLS, min(n, cap) + 1, MXU_COLS):
        if n % t == 0:
            best = t
    assert best is not None, (n, cap)
    return best


def _params(sem):
    return pltpu.CompilerParams(dimension_semantics=sem, vmem_limit_bytes=VMEM_LIMIT)


def _rms(x, gain):
    ms = jnp.mean(x * x, axis=-1, keepdims=True)
    return x * lax.rsqrt(ms + EPS) * gain


def _silu(g):
    g = g.astype(F32)
    return g * jax.nn.sigmoid(g)


def _lanes(x, k):
    return x if k == 1 else jnp.concatenate([x] * k, axis=1)


def _rope(y, cf, sn):
    return y * cf + pltpu.roll(y, LANES // 2, 1) * sn


def _nmm_kernel(x_ref, g_ref, w_ref, hg_ref, cf_ref, sn_ref, o_ref, h_ref, acc_ref, *, preps):
    j = pl.program_id(1)

    def product():
        return jnp.dot(h_ref[...], w_ref[...], preferred_element_type=F32)

    assert 0 not in preps

    @pl.when(j == 0)
    def _():
        h = _rms(x_ref[...].astype(F32), g_ref[...]).astype(BF16)
        h_ref[...] = h
        o_ref[...] = jnp.dot(h, w_ref[...], preferred_element_type=F32).astype(o_ref.dtype)

    plain = j != 0
    for blk in preps:
        plain = jnp.logical_and(plain, j != blk)

    @pl.when(plain)
    def _():
        o_ref[...] = product().astype(o_ref.dtype)

    for blk, (gain_row0, scales) in preps.items():
        @pl.when(j == blk)
        def _(gain_row0=gain_row0, scales=scales):
            acc_ref[...] = product()

            @pl.when(pl.program_id(0) >= 0)
            def _():
                cf, sn = cf_ref[...], sn_ref[...]
                for slot, scale in enumerate(scales):
                    cols = slice(slot * LANES, (slot + 1) * LANES)
                    y = acc_ref[:, cols]
                    if scale is not None:
                        y = _rope(_rms(y, hg_ref[gain_row0 + slot:gain_row0 + slot + 1, :]), cf, sn) * scale
                    o_ref[:, cols] = y.astype(o_ref.dtype)


def normed_matmul(x, gain, w, *, seq, col_block=0, tm=1024, tn=1536, preps=None, head_gains=None, tables=None):
    rows = x.shape[0]
    k, n = w.shape
    tm = _tile(seq, tm)
    tn = _col_tile(n, tn)
    if preps is None:
        preps, head_gains = {}, jnp.ones((8, LANES), F32)
        tables = (jnp.ones((tm, LANES), F32), jnp.zeros((tm, LANES), F32))
    tab_spec = pl.BlockSpec((tm, LANES), lambda i, j: (i % (tables[0].shape[0] // tm), 0))
    return pl.pallas_call(
        functools.partial(_nmm_kernel, preps=preps),
        grid=(rows // tm, n // tn),
        in_specs=[pl.BlockSpec((tm, k), lambda i, j: (i, col_block)),
                  pl.BlockSpec((1, k), lambda i, j: (0, 0)),
                  pl.BlockSpec((k, tn), lambda i, j: (0, j)),
                  pl.BlockSpec(head_gains.shape, lambda i, j: (0, 0)),
                  tab_spec, tab_spec],
        out_specs=pl.BlockSpec((tm, tn), lambda i, j: (i, j)),
        out_shape=jax.ShapeDtypeStruct((rows, n), BF16),
        scratch_shapes=[pltpu.VMEM((tm, k), BF16), pltpu.VMEM((tm, tn) if preps else (SUBLANES, LANES), F32)],
        compiler_params=_params(("parallel", "arbitrary")),
        name="normed_matmul",
    )(x, gain.reshape(1, k).astype(F32), w, head_gains.astype(F32), *tables)


WIN_QB = 8


def _window_kernel(sink_ref, q_ref, kp_ref, kc_ref, kn_ref, vp_ref, vc_ref, vn_ref, g_ref, o_ref, s_ref, p_ref, *,
                   seq, qb):
    n = pl.program_id(1)
    group = A_HEADS // A_KV_HEADS
    rows = group * BLOCK
    qi = lax.broadcasted_iota(jnp.int32, (rows, 3 * BLOCK), 0) & (BLOCK - 1)
    kj = lax.broadcasted_iota(jnp.int32, (rows, 3 * BLOCK), 1)
    in_win = jnp.abs(kj - BLOCK - qi) <= WINDOW
    pairs = [(kv, i) for kv in range(A_KV_HEADS) for i in range(qb)]

    def kv_rows(refs, kv):
        cols = slice(kv * HEAD_DIM, (kv + 1) * HEAD_DIM)
        return jnp.concatenate([r[:, cols] for r in refs], axis=0).astype(BF16)

    kfull = [kv_rows((kp_ref, kc_ref, kn_ref), kv) for kv in range(A_KV_HEADS)]
    for idx, (kv, i) in enumerate(pairs):
        qrows = slice(i * BLOCK, (i + 1) * BLOCK)
        q = jnp.concatenate([q_ref[qrows, h * HEAD_DIM:(h + 1) * HEAD_DIM]
                             for h in range(kv * group, (kv + 1) * group)], axis=0)
        s_ref[idx] = lax.dot_general(q, kfull[kv][i * BLOCK:(i + 3) * BLOCK], (((1,), (1,)), ((), ())),
                                     preferred_element_type=F32)

    for idx, (kv, i) in enumerate(pairs):
        sk = jnp.concatenate([jnp.full((BLOCK, 1), sink_ref[h], F32)
                              for h in range(kv * group, (kv + 1) * group)], axis=0)
        kpos = (n * qb + i - 1) * BLOCK + kj
        s = jnp.where(in_win & (kpos >= 0) & (kpos < seq), s_ref[idx], NEG_INF)
        m = jnp.maximum(jnp.max(s, axis=-1, keepdims=True), sk)
        pr = jnp.exp(s - m)
        denom = jnp.sum(pr, axis=-1, keepdims=True) + jnp.exp(sk - m)
        p_ref[idx] = (pr / denom).astype(BF16)

    vfull = [kv_rows((vp_ref, vc_ref, vn_ref), kv) for kv in range(A_KV_HEADS)]
    for idx, (kv, i) in enumerate(pairs):
        qrows = slice(i * BLOCK, (i + 1) * BLOCK)
        o = jnp.dot(p_ref[idx], vfull[kv][i * BLOCK:(i + 3) * BLOCK], preferred_element_type=F32)
        for g, h in enumerate(range(kv * group, (kv + 1) * group)):
            hc = slice(h * HEAD_DIM, (h + 1) * HEAD_DIM)
            o_ref[qrows, hc] = (o[g * BLOCK:(g + 1) * BLOCK] * _silu(g_ref[qrows, hc])).astype(o_ref.dtype)


def window_attention(p, q_col, k_col, v_col, g_col, sink, batch, seq):
    nblk = seq // BLOCK
    qb = _tile(nblk, WIN_QB)
    nstep = nblk // qb
    kvw = A_KV_HEADS * HEAD_DIM
    kb, vb = k_col // kvw, v_col // kvw

    def prev(b, n):
        return b * nblk + jnp.maximum(n * qb - 1, 0)

    def cur(b, n):
        return b * nstep + n

    def nxt(b, n):
        return b * nblk + jnp.minimum((n + 1) * qb, nblk - 1)

    return pl.pallas_call(
        functools.partial(_window_kernel, seq=seq, qb=qb),
        grid=(batch, nstep),
        in_specs=[pl.BlockSpec(memory_space=pltpu.SMEM),
                  pl.BlockSpec((qb * BLOCK, A_WIDTH), lambda b, n: (cur(b, n), q_col // A_WIDTH)),
                  pl.BlockSpec((BLOCK, kvw), lambda b, n: (prev(b, n), kb)),
                  pl.BlockSpec((qb * BLOCK, kvw), lambda b, n: (cur(b, n), kb)),
                  pl.BlockSpec((BLOCK, kvw), lambda b, n: (nxt(b, n), kb)),
                  pl.BlockSpec((BLOCK, kvw), lambda b, n: (prev(b, n), vb)),
                  pl.BlockSpec((qb * BLOCK, kvw), lambda b, n: (cur(b, n), vb)),
                  pl.BlockSpec((BLOCK, kvw), lambda b, n: (nxt(b, n), vb)),
                  pl.BlockSpec((qb * BLOCK, A_WIDTH), lambda b, n: (cur(b, n), g_col // A_WIDTH))],
        out_specs=pl.BlockSpec((qb * BLOCK, A_WIDTH), lambda b, n: (cur(b, n), 0)),
        out_shape=jax.ShapeDtypeStruct((batch * seq, A_WIDTH), BF16),
        scratch_shapes=[pltpu.VMEM((A_KV_HEADS * qb, A_HEADS // A_KV_HEADS * BLOCK, 3 * BLOCK), F32),
                        pltpu.VMEM((A_KV_HEADS * qb, A_HEADS // A_KV_HEADS * BLOCK, 3 * BLOCK), BF16)],
        compiler_params=_params(("parallel", "parallel")),
        name="window_attention",
    )(sink.astype(F32), p, p, p, p, p, p, p, p)


def _mem_kernel(q_ref, kv_ref, qg_ref, kg_ref, g_ref, o_ref):
    qg, kg = qg_ref[...], kg_ref[...]
    scale = HEAD_DIM ** -0.5
    for h in range(M_HEADS):
        cols = slice(h * HEAD_DIM, (h + 1) * HEAD_DIM)
        q = (_rms(q_ref[:, cols].astype(F32), qg) * scale).astype(BF16)
        k = _rms(kv_ref[:, cols].astype(F32), kg).astype(BF16)
        v = kv_ref[:, M_WIDTH + h * HEAD_DIM:M_WIDTH + (h + 1) * HEAD_DIM].astype(BF16)
        s = lax.dot_general(q, k, (((1,), (1,)), ((), ())), preferred_element_type=F32)
        m = jnp.max(s, axis=-1, keepdims=True)
        pr = jnp.exp(s - m)
        pr = pr / jnp.sum(pr, axis=-1, keepdims=True)
        o = jnp.dot(pr.astype(BF16), v, preferred_element_type=F32)
        o_ref[:, cols] = (o * _silu(g_ref[:, cols])).astype(o_ref.dtype)


def memory_attention(p, q_col, g_col, kvm, q_gain, k_gain, batch, seq, *, tq=1024):
    tq = _tile(seq, tq)
    nq = seq // tq
    return pl.pallas_call(
        _mem_kernel,
        grid=(batch, nq),
        in_specs=[pl.BlockSpec((tq, M_WIDTH), lambda b, i: (b * nq + i, q_col // M_WIDTH)),
                  pl.BlockSpec((N_MEM, 2 * M_WIDTH), lambda b, i: (b, 0)),
                  pl.BlockSpec((1, HEAD_DIM), lambda b, i: (0, 0)),
                  pl.BlockSpec((1, HEAD_DIM), lambda b, i: (0, 0)),
                  pl.BlockSpec((tq, M_WIDTH), lambda b, i: (b * nq + i, g_col // M_WIDTH))],
        out_specs=pl.BlockSpec((tq, M_WIDTH), lambda b, i: (b * nq + i, 0)),
        out_shape=jax.ShapeDtypeStruct((batch * seq, M_WIDTH), BF16),
        compiler_params=_params(("parallel", "parallel")),
        name="memory_attention",
    )(p, kvm, q_gain.reshape(1, HEAD_DIM).astype(F32), k_gain.reshape(1, HEAD_DIM).astype(F32), p)


LOG2E = math.log2(math.e)


def _chunk_rows(chunk, tk):
    return pl.ds(pl.multiple_of(chunk * tk, tk), tk)


MLA_TRIP_CHUNKS = 8
DIFF_TRIP_CHUNKS = 2


def _pipelined_chunks(scores, consume, nk, per_trip):
    scores(0, 0)

    def body(i, carry):
        for j in range(per_trip):
            chunk = per_trip * i + j
            scores(jnp.minimum(chunk + 1, nk - 1), (j + 1) % 2)
            consume(chunk, j % 2)
        return carry

    lax.fori_loop(0, nk // per_trip, body, 0)


def _diff_flash_kernel(q_ref, k_ref, v_ref, lam_ref, g_ref, gate_ref, o_ref, m_ref, l_ref, acc_ref, s_ref, *,
                       tk, nk, lam_init):
    m_ref[...] = jnp.full(m_ref.shape, NEG_INF, F32)
    l_ref[...] = jnp.zeros(l_ref.shape, F32)
    acc_ref[...] = jnp.zeros(acc_ref.shape, F32)

    def scores(chunk, c):
        cols = slice(c * C_QK_DIM, (c + 1) * C_QK_DIM)
        s_ref[c] = lax.dot_general(q_ref[:, cols], k_ref[_chunk_rows(chunk, tk), cols], (((1,), (1,)), ((), ())),
                                   preferred_element_type=F32)

    def consume(chunk, c):
        s = s_ref[c]
        m_prev = m_ref[c]
        m_new = jnp.maximum(m_prev, jnp.max(s, axis=-1, keepdims=True))
        alpha = jnp.exp2(m_prev - m_new)
        pr = jnp.exp2(s - _lanes(m_new, tk // LANES))
        l_ref[c] = alpha * l_ref[c] + jnp.sum(pr, axis=-1, keepdims=True)
        acc_ref[c] = (_lanes(alpha, C_V_DIM // LANES) * acc_ref[c]
                      + jnp.dot(pr.astype(BF16), v_ref[_chunk_rows(chunk, tk), :], preferred_element_type=F32))
        m_ref[c] = m_new

    scores(0, 0)

    def body(i, carry):
        for step in range(DIFF_TRIP_CHUNKS):
            j = DIFF_TRIP_CHUNKS * i + step
            scores(j, 1)
            consume(j, 0)
            scores(jnp.minimum(j + 1, nk - 1), 0)
            consume(j, 1)
        return carry

    lax.fori_loop(0, nk // DIFF_TRIP_CHUNKS, body, 0)
    lv = lam_ref[...]
    lam = (jnp.exp(jnp.sum(lv[0:1] * lv[1:2], axis=-1, keepdims=True))
           - jnp.exp(jnp.sum(lv[2:3] * lv[3:4], axis=-1, keepdims=True)) + lam_init)
    o1 = acc_ref[0] / _lanes(l_ref[0], C_V_DIM // LANES)
    o2 = acc_ref[1] / _lanes(l_ref[1], C_V_DIM // LANES)
    o_ref[...] = (_rms(o1 - lam * o2, g_ref[...]) * (1.0 - lam_init) * _silu(gate_ref[...])).astype(o_ref.dtype)


def _mla_flash_kernel(q_ref, k_ref, v_ref, gate_ref, o_ref, m_ref, acc_ref, s_ref, *, tk, nk):
    m_ref[...] = jnp.full(m_ref.shape, NEG_INF, F32)
    acc_ref[...] = jnp.zeros(acc_ref.shape, F32)
    ones = jnp.ones((tk, LANES), BF16)

    def scores(chunk, slot):
        s_ref[slot] = lax.dot_general(q_ref[...], k_ref[_chunk_rows(chunk, tk), :], (((1,), (1,)), ((), ())),
                                      preferred_element_type=F32)

    def consume(chunk, slot):
        s = s_ref[slot]
        m_prev = m_ref[...]
        m_new = jnp.maximum(m_prev, jnp.max(s, axis=-1, keepdims=True))
        alpha = jnp.exp2(m_prev - m_new)
        pr = jnp.exp2(s - _lanes(m_new, tk // LANES))
        vv = jnp.concatenate([v_ref[_chunk_rows(chunk, tk), :], ones], axis=1)
        acc_ref[...] = (_lanes(alpha, 2) * acc_ref[...]
                        + jnp.dot(pr.astype(BF16), vv, preferred_element_type=F32))
        m_ref[...] = m_new

    _pipelined_chunks(scores, consume, nk, MLA_TRIP_CHUNKS)
    o_ref[...] = (acc_ref[:, :D_V] / acc_ref[:, D_V:] * _silu(gate_ref[...])).astype(o_ref.dtype)


def _flash_call(kernel, q, q_block0, k, k_block0, v, v_block0, gate, gate_block0, extra, extra_specs, scratch, *,
                heads, dv, batch, seq, tq, tk, kv_buffers, name):
    nq, nk = seq // tq, seq // tk
    qk_w = 2 * LANES
    resident = pl.Buffered(kv_buffers)
    return pl.pallas_call(
        functools.partial(kernel, tk=tk, nk=nk),
        grid=(batch, heads, nq),
        in_specs=[pl.BlockSpec((tq, qk_w), lambda b, h, i: (b * nq + i, q_block0 + h)),
                  pl.BlockSpec((seq, qk_w), lambda b, h, i: (b, k_block0 + h), pipeline_mode=resident),
                  pl.BlockSpec((seq, dv), lambda b, h, i: (b, v_block0 + h), pipeline_mode=resident)]
        + extra_specs + [pl.BlockSpec((tq, dv), lambda b, h, i: (b * nq + i, gate_block0 + h))],
        out_specs=pl.BlockSpec((tq, dv), lambda b, h, i: (b * nq + i, h)),
        out_shape=jax.ShapeDtypeStruct((batch * seq, heads * dv), BF16),
        scratch_shapes=scratch,
        compiler_params=_params(("parallel", "parallel", "arbitrary")),
        name=name,
    )(q, k, v, *extra, gate)


def diff_attention(p, q_col, k_col, v_col, g_col, c_lambda, out_gain, lam_init, batch, seq, *, tq=1024, tk=1024):
    tq, tk = _tile(seq, tq), _tile(seq // DIFF_TRIP_CHUNKS, tk)
    const = lambda b, h, i: (0, 0)
    return _flash_call(
        functools.partial(_diff_flash_kernel, lam_init=lam_init), p, q_col // C_V_DIM, p, k_col // C_V_DIM,
        p, v_col // C_V_DIM, p, g_col // C_V_DIM,
        [c_lambda.astype(F32), out_gain.reshape(1, C_V_DIM).astype(F32)],
        [pl.BlockSpec((4, C_QK_DIM), const), pl.BlockSpec((1, C_V_DIM), const)],
        [pltpu.VMEM((2, tq, LANES), F32), pltpu.VMEM((2, tq, LANES), F32), pltpu.VMEM((2, tq, C_V_DIM), F32),
         pltpu.VMEM((2, tq, tk), F32)],
        heads=C_HEADS, dv=C_V_DIM, batch=batch, seq=seq, tq=tq, tk=tk, kv_buffers=1, name="diff_attention")


def mla_attention(q, k, v, p, g_col, batch, seq, *, tq=2048, tk=512):
    tq, tk = _tile(seq, tq), _tile(seq // MLA_TRIP_CHUNKS, tk)
    return _flash_call(_mla_flash_kernel, q, 0, k, 0, v, 0, p, g_col // D_V, [], [],
                       [pltpu.VMEM((tq, LANES), F32), pltpu.VMEM((tq, D_V + LANES), F32),
                        pltpu.VMEM((2, tq, tk), F32)],
                       heads=D_HEADS, dv=D_V, batch=batch, seq=seq, tq=tq, tk=tk, kv_buffers=2,
                       name="mla_attention")


def _mla_proj_kernel(*refs, shared_rope, scale):
    if shared_rope:
        x_ref, g_ref, w_ref, kr_ref, gn_ref, gr_ref, cf_ref, sn_ref, o_ref, v_ref = refs
    else:
        x_ref, g_ref, w_ref, gn_ref, gr_ref, cf_ref, sn_ref, o_ref = refs
    h = _rms(x_ref[...].astype(F32), g_ref[...]).astype(BF16)
    acc = jnp.dot(h, w_ref[...], preferred_element_type=F32)
    gn, gr = gn_ref[...], gr_ref[...]
    cf, sn = cf_ref[...], sn_ref[...]
    width = D_HEADS * LANES
    for hd in range(D_HEADS):
        cols = slice(hd * LANES, (hd + 1) * LANES)
        xn = acc[:, cols]
        xr = kr_ref[...].astype(F32) if shared_rope else acc[:, width + hd * LANES:width + (hd + 1) * LANES]
        ms = (jnp.sum(xn * xn, axis=-1, keepdims=True)
              + jnp.sum(xr * xr, axis=-1, keepdims=True)) * (1.0 / (D_NOPE + D_ROPE))
        inv = lax.rsqrt(ms + EPS)
        o_ref[:, 2 * hd * LANES:(2 * hd + 1) * LANES] = (xn * inv * gn * scale).astype(o_ref.dtype)
        o_ref[:, (2 * hd + 1) * LANES:(2 * hd + 2) * LANES] = (_rope(xr * inv * gr, cf, sn) * scale).astype(o_ref.dtype)
    if shared_rope:
        v_ref[...] = acc[:, width:].astype(v_ref.dtype)


def mla_projection(p, col, rank, gain, w, head_gain, tables, scale, seq, *, rope_col=None, tm=512):
    rows = p.shape[0]
    width = D_HEADS * LANES
    tm = _tile(seq, tm)
    nblk = seq // tm
    shared_rope = rope_col is not None
    gn = head_gain[:D_NOPE].reshape(1, LANES).astype(F32)
    gr = _rope_slot(head_gain[D_NOPE:]).reshape(1, LANES).astype(F32)
    one = pl.BlockSpec((1, LANES), lambda i: (0, 0))
    tab_spec = pl.BlockSpec((tm, LANES), lambda i: (i % nblk, 0))
    in_specs = [pl.BlockSpec((tm, rank), lambda i: (i, col // rank)),
                pl.BlockSpec((1, rank), lambda i: (0, 0)),
                pl.BlockSpec((rank, 2 * width), lambda i: (0, 0))]
    args = [p, gain.reshape(1, rank).astype(F32), w]
    out_specs = [pl.BlockSpec((tm, 2 * width), lambda i: (i, 0))]
    out_shape = [jax.ShapeDtypeStruct((rows, 2 * width), BF16)]
    if shared_rope:
        in_specs.append(pl.BlockSpec((tm, LANES), lambda i: (i, rope_col // LANES)))
        args.append(p)
        out_specs.append(pl.BlockSpec((tm, width), lambda i: (i, 0)))
        out_shape.append(jax.ShapeDtypeStruct((rows, width), BF16))
    return pl.pallas_call(
        functools.partial(_mla_proj_kernel, shared_rope=shared_rope, scale=scale),
        grid=(rows // tm,),
        in_specs=in_specs + [one, one, tab_spec, tab_spec],
        out_specs=out_specs,
        out_shape=out_shape,
        compiler_params=_params(("parallel",)),
        name="mla_projection",
    )(*args, gn, gr, *tables)


SUB = SUBLANES
FILT_ROWS = SUB * DFT_N2
HALO = 16


def _half_rows(n1):
    return n1 // 2 + SUB


def _hyena_gate_kernel(*refs, nblk, tm):
    u_refs, up_refs, un_refs = refs[0:3], refs[3:6], refs[6:9]
    w_ref, b_ref, g_ref, zz_ref, x0_ref = refs[9:]
    li = pl.program_id(1)
    tc = zz_ref.shape[-1]
    row = lax.broadcasted_iota(jnp.int32, (tm, tc), 0)
    has_prev = (li > 0).astype(F32)
    has_next = (li < nblk - 1).astype(F32)

    def conv(part):
        u = u_refs[part][...].astype(F32)
        prev_row = up_refs[part][HALO - 1:HALO, :].astype(F32) * has_prev
        next_row = un_refs[part][0:1, :].astype(F32) * has_next
        above = jnp.where(row == 0, prev_row, pltpu.roll(u, 1, 0))
        below = jnp.where(row == tm - 1, next_row, pltpu.roll(u, tm - 1, 0))
        w = w_ref[part]
        return above * w[0:1] + u * w[1:2] + below * w[2:3] + b_ref[part]

    x0 = conv(0) * _silu(g_ref[...])
    zz = conv(2) * conv(1)
    for a in range(tm // DFT_N2):
        rows = slice(a * DFT_N2, (a + 1) * DFT_N2)
        zz_ref[:, a, :] = zz[rows]
        x0_ref[:, a, :] = x0[rows]


def hyena_gate(p, u_col, g_col, short_w, short_b, batch, seq, *, tc=512):
    tm = FILT_ROWS
    nblk = seq // tm
    half = seq // DFT_N2
    tc = _tile(B_WIDTH, tc)
    nc = B_WIDTH // tc
    hb = tm // HALO
    last_halo = batch * nblk * hb - 1

    def part_specs(part):
        cb = (u_col + part * B_WIDTH) // tc
        return (pl.BlockSpec((tm, tc), lambda b, i, c: (b * nblk + i, cb + c)),
                pl.BlockSpec((HALO, tc), lambda b, i, c: (jnp.maximum((b * nblk + i) * hb - 1, 0), cb + c)),
                pl.BlockSpec((HALO, tc), lambda b, i, c: (jnp.minimum((b * nblk + i + 1) * hb, last_halo), cb + c)))

    specs = [part_specs(part) for part in range(3)]
    out = jax.ShapeDtypeStruct((batch, DFT_N2, half, B_WIDTH), F32)
    o_spec = pl.BlockSpec((None, DFT_N2, tm // DFT_N2, tc), lambda b, i, c: (b, 0, i, c))
    w3 = short_w.astype(F32).reshape(3, 3, B_WIDTH).transpose(1, 0, 2)
    b3 = short_b.astype(F32).reshape(3, 1, B_WIDTH)
    return pl.pallas_call(
        functools.partial(_hyena_gate_kernel, nblk=nblk, tm=tm),
        grid=(batch, nblk, nc),
        in_specs=[sp[0] for sp in specs] + [sp[1] for sp in specs] + [sp[2] for sp in specs]
        + [pl.BlockSpec((3, 3, tc), lambda b, i, c: (0, 0, c)),
           pl.BlockSpec((3, 1, tc), lambda b, i, c: (0, 0, c)),
           pl.BlockSpec((tm, tc), lambda b, i, c: (b * nblk + i, g_col // tc + c))],
        out_specs=[o_spec, o_spec],
        out_shape=[out, out],
        compiler_params=_params(("parallel", "parallel", "parallel")),
        name="hyena_gate",
    )(*([p] * 9), w3, b3, p)


def _hdot(a, b):
    return jnp.dot(a, b, preferred_element_type=F32, precision=lax.Precision.HIGHEST)


def _filter_kernel(fr_ref, w1_ref, b1_ref, fq_ref, w2_ref, b2_ref, w3_ref, dec_ref, k_ref, s_ref, t_ref, h_ref, *,
                   seq, tr):
    i, c = pl.program_id(0), pl.program_id(1)
    r = i * tr + lax.broadcasted_iota(jnp.int32, (tr, 1), 0)

    @pl.when(c == 0)
    def _():
        pos = jnp.where(r < seq, r, 2 * seq - 1 - r).astype(F32)
        t = pos * (1.0 / (seq - 1))
        w = pos * (2.0 * math.pi / seq)
        lane = lax.broadcasted_iota(jnp.int32, (tr, LANES), 1)
        bands = (HY_EMB - 1) // 2
        z = jnp.where(lane == 0, t, jnp.where(lane <= 2 * bands, jnp.cos(w * fr_ref[0:1] + fr_ref[1:2]), 0.0))
        h = jnp.sin(fq_ref[0:1] * (_hdot(z, w1_ref[...]) + b1_ref[...]))
        h_ref[...] = jnp.sin(fq_ref[1:2] * (_hdot(h, w2_ref[...]) + b2_ref[...]))
        t_ref[...] = jnp.broadcast_to(t, t_ref.shape)

    h = _hdot(h_ref[...], w3_ref[...]) * jnp.exp(-t_ref[:, 0:1] * jnp.abs(dec_ref[...]))
    h = jnp.where(r == seq, 0.0, h)
    for a in range(tr // DFT_N2):
        k_ref[:, a, :] = h[a * DFT_N2:(a + 1) * DFT_N2]

    @pl.when(jnp.logical_and(i == 0, c == 0))
    def _():
        s_ref[...] = jnp.zeros(s_ref.shape, F32)

    tc = h.shape[1]
    col_sum = jnp.sum(jnp.abs(h), axis=0, keepdims=True)
    for cc in range(s_ref.shape[1] // tc):
        @pl.when(c == cc)
        def _(cc=cc):
            s_ref[:, cc * tc:(cc + 1) * tc] += col_sum


def hyena_filter(seq, w1, b1, freq, w2, b2, w3, decay, *, tc=512):
    tr = FILT_ROWS
    assert seq % tr == 0
    nhalf = seq // tr
    n1 = 2 * seq // DFT_N2
    tc = _tile(B_WIDTH, tc)
    nc = B_WIDTH // tc
    bands = (HY_EMB - 1) // 2
    fr = jnp.linspace(1e-4, bands - 1, bands, dtype=F32)
    pad = jnp.zeros((LANES - HY_EMB,), F32)
    fr_lanes = jnp.stack([jnp.concatenate([jnp.zeros((1,), F32), fr, fr, pad]),
                          jnp.concatenate([jnp.zeros((1 + bands,), F32), jnp.full((bands,), 0.5 * math.pi, F32), pad])])
    w1p = jnp.pad(w1.astype(F32), ((0, LANES - HY_EMB), (0, 0)))
    const = lambda i, c: (0, 0)
    half = lambda i, c: (0, (i // nhalf) * nc + c)
    return pl.pallas_call(
        functools.partial(_filter_kernel, seq=seq, tr=tr),
        grid=(2 * nhalf, nc),
        in_specs=[pl.BlockSpec((2, LANES), const),
                  pl.BlockSpec((LANES, HY_FFN), const),
                  pl.BlockSpec((1, HY_FFN), const),
                  pl.BlockSpec((2, HY_FFN), const),
                  pl.BlockSpec((HY_FFN, HY_FFN), const),
                  pl.BlockSpec((1, HY_FFN), const),
                  pl.BlockSpec((HY_FFN, tc), half),
                  pl.BlockSpec((1, tc), half)],
        out_specs=[pl.BlockSpec((DFT_N2, tr // DFT_N2, tc), lambda i, c: (0, i, c)),
                   pl.BlockSpec((1, B_WIDTH), const)],
        out_shape=[jax.ShapeDtypeStruct((DFT_N2, n1, B_WIDTH), F32),
                   jax.ShapeDtypeStruct((1, B_WIDTH), F32)],
        scratch_shapes=[pltpu.VMEM((tr, LANES), F32), pltpu.VMEM((tr, HY_FFN), F32)],
        compiler_params=_params(("arbitrary", "arbitrary")),
        name="hyena_filter",
    )(fr_lanes, w1p, b1.reshape(1, HY_FFN).astype(F32), freq.astype(F32), w2.astype(F32),
      b2.reshape(1, HY_FFN).astype(F32), w3.astype(F32), decay.reshape(1, 2 * B_WIDTH).astype(F32))


def _bdot(a, b):
    return jnp.dot(a, b, preferred_element_type=F32)


def dft_tables(n1):
    n = n1 * DFT_N2
    rows = _half_rows(n1)
    kk = jnp.arange(rows, dtype=jnp.int32)
    k1 = kk[None, :, None]
    m1 = jnp.arange(n1, dtype=jnp.int32)[None, None, :]
    n2 = jnp.arange(DFT_N2, dtype=jnp.int32)[:, None, None]
    ang = ((k1 * (DFT_N2 * m1 + n2)) % n).astype(F32) * (2.0 * math.pi / n)
    live = (kk <= n1 // 2).astype(F32)[None, :, None]
    weight = jnp.where((kk == 0) | (kk == n1 // 2), 1.0, 2.0)[None, :, None] * live
    cos, sin = jnp.cos(ang), jnp.sin(ang)
    g = jnp.concatenate([cos * live, -sin * live], axis=1).astype(BF16)
    hc = jnp.swapaxes(cos * weight, 1, 2)[:, :n1 // 2].astype(BF16)
    hsn = jnp.swapaxes(-sin * weight, 1, 2)[:, :n1 // 2].astype(BF16)
    a = jnp.arange(DFT_N2, dtype=jnp.int32)
    ang2 = ((a[:, None] * a[None, :]) % DFT_N2).astype(F32) * (2.0 * math.pi / DFT_N2)
    ff = jnp.concatenate([jnp.cos(ang2), jnp.sin(ang2)], axis=0).astype(BF16)
    return g, hc, hsn, ff


def _dft1_kernel(x_ref, g_ref, ar_ref, ai_ref):
    rows = ar_ref.shape[0]
    for j in range(SUB):
        a = _bdot(g_ref[j], x_ref[j].astype(BF16))
        ar_ref[:, j, :] = a[:rows]
        ai_ref[:, j, :] = a[rows:]


def dft_stage1(x, g, *, tc=512):
    batch, _, k1rows, c = x.shape
    rows = g.shape[1] // 2
    tc = _tile(c, tc)
    out = jax.ShapeDtypeStruct((batch, rows, DFT_N2, c), F32)
    g_spec = pl.BlockSpec((SUB, 2 * rows, k1rows), lambda g, b, ci: (g, 0, 0))
    o_spec = pl.BlockSpec((None, rows, SUB, tc), lambda g, b, ci: (b, 0, g, ci))
    return pl.pallas_call(
        _dft1_kernel,
        grid=(DFT_N2 // SUB, batch, c // tc),
        in_specs=[pl.BlockSpec((None, SUB, k1rows, tc), lambda g, b, ci: (b, g, 0, ci)), g_spec],
        out_specs=[o_spec, o_spec],
        out_shape=[out, out],
        compiler_params=_params(("parallel", "parallel", "parallel")),
        name="dft_stage1",
    )(x, g)


def _stage2(ff, ar, ai):
    p = _bdot(ff, ar)
    q = _bdot(ff, ai)
    return p[:DFT_N2] + q[DFT_N2:], q[:DFT_N2] - p[DFT_N2:]


def _filter_spectrum_kernel(ar_ref, ai_ref, s_ref, ff_ref, kr_ref, ki_ref, *, n):
    ff = ff_ref[...]
    scale = 1.0 / (s_ref[...] * n)
    for j in range(SUB):
        xr, xi = _stage2(ff, ar_ref[j].astype(BF16), ai_ref[j].astype(BF16))
        kr_ref[j] = xr * scale
        ki_ref[j] = xi * scale


def filter_spectrum(ar, ai, asum, ff, n, *, tc=512):
    rows, _, c = ar.shape
    tc = _tile(c, tc)
    a_spec = pl.BlockSpec((SUB, DFT_N2, tc), lambda k, ci: (k, 0, ci))
    out = jax.ShapeDtypeStruct((rows, DFT_N2, c), F32)
    return pl.pallas_call(
        functools.partial(_filter_spectrum_kernel, n=n),
        grid=(rows // SUB, c // tc),
        in_specs=[a_spec, a_spec, pl.BlockSpec((1, tc), lambda k, ci: (0, ci)),
                  pl.BlockSpec((2 * DFT_N2, DFT_N2), lambda k, ci: (0, 0))],
        out_specs=[a_spec, a_spec],
        out_shape=[out, out],
        compiler_params=_params(("parallel", "parallel")),
        name="filter_spectrum",
    )(ar, ai, asum, ff)


def _spectral_kernel(ar_ref, ai_ref, kr_ref, ki_ref, ff_ref, br_ref, bi_ref, yr_ref, yi_ref):
    ff = ff_ref[...]
    for j in range(SUB):
        xr, xi = _stage2(ff, ar_ref[j].astype(BF16), ai_ref[j].astype(BF16))
        kr, ki = kr_ref[j], ki_ref[j]
        yr_ref[j] = (xr * kr - xi * ki).astype(BF16)
        yi_ref[j] = (xr * ki + xi * kr).astype(BF16)
    for j in range(SUB):
        r = _bdot(ff, yr_ref[j])
        s = _bdot(ff, yi_ref[j])
        br_ref[:, j, :] = r[:DFT_N2] - s[DFT_N2:]
        bi_ref[:, j, :] = s[:DFT_N2] + r[DFT_N2:]


def spectral_multiply(ar, ai, kr, ki, ff, *, tc=512):
    batch, rows, _, c = ar.shape
    tc = _tile(c, tc)
    a_spec = pl.BlockSpec((None, SUB, DFT_N2, tc), lambda k, ci, b: (b, k, 0, ci))
    k_spec = pl.BlockSpec((SUB, DFT_N2, tc), lambda k, ci, b: (k, 0, ci))
    o_spec = pl.BlockSpec((None, DFT_N2, SUB, tc), lambda k, ci, b: (b, 0, k, ci))
    out = jax.ShapeDtypeStruct((batch, DFT_N2, rows, c), F32)
    return pl.pallas_call(
        _spectral_kernel,
        grid=(rows // SUB, c // tc, batch),
        in_specs=[a_spec, a_spec, k_spec, k_spec, pl.BlockSpec((2 * DFT_N2, DFT_N2), lambda k, ci, b: (0, 0))],
        out_specs=[o_spec, o_spec],
        out_shape=[out, out],
        scratch_shapes=[pltpu.VMEM((SUB, DFT_N2, tc), BF16), pltpu.VMEM((SUB, DFT_N2, tc), BF16)],
        compiler_params=_params(("parallel", "parallel", "parallel")),
        name="spectral_multiply",
    )(ar, ai, kr, ki, ff)


def _idft_out_kernel(br_ref, bi_ref, hc_ref, hs_ref, zz_ref, x0_ref, bias_ref, o_ref):
    bias = bias_ref[...]
    for j in range(SUB):
        y = _bdot(hc_ref[j], br_ref[j].astype(BF16)) + _bdot(hs_ref[j], bi_ref[j].astype(BF16))
        o_ref[:, j, :] = (y + zz_ref[j] * bias) * x0_ref[j]


def idft_output(br, bi, hc, hsn, zz, x0, bias, *, tc=512):
    batch, _, rows, c = br.shape
    half = hc.shape[1]
    tc = _tile(c, tc)
    b_spec = pl.BlockSpec((None, SUB, rows, tc), lambda g, b, ci: (b, g, 0, ci))
    h_spec = pl.BlockSpec((SUB, half, rows), lambda g, b, ci: (g, 0, 0))
    z_spec = pl.BlockSpec((None, SUB, half, tc), lambda g, b, ci: (b, g, 0, ci))
    return pl.pallas_call(
        _idft_out_kernel,
        grid=(DFT_N2 // SUB, batch, c // tc),
        in_specs=[b_spec, b_spec, h_spec, h_spec, z_spec, z_spec, pl.BlockSpec((1, tc), lambda g, b, ci: (0, ci))],
        out_specs=pl.BlockSpec((half, SUB, tc), lambda g, b, ci: (b, g, ci)),
        out_shape=jax.ShapeDtypeStruct((batch * half, DFT_N2, c), F32),
        compiler_params=_params(("parallel", "parallel", "parallel")),
        name="idft_output",
    )(br, bi, hc, hsn, zz, x0, bias)


def hyena_mixer(p, u_col, g_col, prm, e, batch, seq):
    n1 = 2 * seq // DFT_N2
    g, hc, hsn, ff = dft_tables(n1)
    kern, asum = hyena_filter(seq, prm["hy_w1"][e], prm["hy_b1"][e], prm["hy_freq"][e], prm["hy_w2"][e],
                              prm["hy_b2"][e], prm["hy_w3"][e], prm["hy_decay"][e])
    far, fai = dft_stage1(kern[None], g)
    kr, ki = filter_spectrum(far[0], fai[0], asum, ff, n1 * DFT_N2)
    zz, x0 = hyena_gate(p, u_col, g_col, prm["hy_short_w"][e], prm["hy_short_b"][e], batch, seq)
    ar, ai = dft_stage1(zz, g[:, :, :n1 // 2])
    br, bi = spectral_multiply(ar, ai, kr, ki, ff)
    yb = idft_output(br, bi, hc, hsn, zz, x0, prm["hy_bias"][e].reshape(1, B_WIDTH).astype(F32))
    return yb.reshape(batch * seq, B_WIDTH)


def _outproj_kernel(y1_ref, y2_ref, y3_ref, x_ref, w_ref, o_ref):
    acc = x_ref[...]
    off = 0
    for y_ref in (y1_ref, y2_ref, y3_ref):
        wd = y_ref.shape[1]
        acc = acc + jnp.dot(y_ref[...].astype(BF16), w_ref[off:off + wd, :], preferred_element_type=F32)
        off += wd
    o_ref[...] = acc


def out_projection(y1, y2, y3, x, w, *, tm=512):
    rows, d = x.shape
    tm = _tile(rows, tm)
    row = lambda i: (i, 0)
    return pl.pallas_call(
        _outproj_kernel,
        grid=(rows // tm,),
        in_specs=[pl.BlockSpec((tm, y1.shape[1]), row),
                  pl.BlockSpec((tm, y2.shape[1]), row),
                  pl.BlockSpec((tm, y3.shape[1]), row),
                  pl.BlockSpec((tm, d), row),
                  pl.BlockSpec((MIX_WIDTH, d), lambda i: (0, 0), pipeline_mode=pl.Buffered(1))],
        out_specs=pl.BlockSpec((tm, d), row),
        out_shape=jax.ShapeDtypeStruct((rows, d), F32),
        compiler_params=_params(("parallel",)),
        name="out_projection",
    )(y1, y2, y3, x, w)


def _split_cols(w, sizes):
    out, off = [], 0
    for s in sizes:
        out.append(w[:, off:off + s])
        off += s
    return out


ROT_HALF = ROT_DIM // 2
HEAD_PERM = (tuple(range(ROT_HALF)) + tuple(range(ROT_DIM, LANES // 2 + ROT_HALF))
             + tuple(range(ROT_HALF, ROT_DIM)) + tuple(range(LANES // 2 + ROT_HALF, LANES)))


def _permute_heads(w, nheads):
    k = w.shape[0]
    return w.reshape(k, nheads, HEAD_DIM)[:, :, jnp.array(HEAD_PERM)].reshape(k, nheads * HEAD_DIM)


def _rope_slot(w):
    half = D_ROPE // 2
    z = jnp.zeros(w.shape[:-1] + (LANES // 2 - half,), w.dtype)
    return jnp.concatenate([w[..., :half], z, w[..., half:], z], axis=-1)


def _rope_tables(seq, dim, passthrough):
    half = dim // 2
    inv = ROPE_THETA ** (-jnp.arange(0, dim, 2, dtype=F32) / dim)
    ang = jnp.arange(seq, dtype=F32)[:, None] * inv[None, :]
    cos, sin = jnp.cos(ang), jnp.sin(ang)
    rest = jnp.full((seq, LANES // 2 - half), 1.0 if passthrough else 0.0, F32)
    zeros = jnp.zeros((seq, LANES // 2 - half), F32)
    cf = jnp.concatenate([cos, rest, cos, rest], axis=1)
    sn = jnp.concatenate([-sin, zeros, sin, zeros], axis=1)
    return cf, sn


def _memory_kv(mem, prm, l):
    batch = mem.shape[0]
    return normed_matmul(mem.reshape(batch * N_MEM, D_MODEL), prm["mem_norm_gain"][l],
                         prm["w_mem_kv"][l].astype(BF16), seq=N_MEM)


def _head_gains(*groups):
    rows = [jnp.broadcast_to(g[jnp.array(HEAD_PERM)][None, :], (n, HEAD_DIM)) for g, n in groups]
    return jnp.concatenate(rows, axis=0)


def _even_layer(x, mem, prm, l, e, batch, seq, rope_a):
    aq, ak, av, ag, bu, bg, mq, mg = _split_cols(prm["w_in_even"][e], EVEN_SPLITS)
    w_in = jnp.concatenate([ag, bg, mg, mq, bu, _permute_heads(aq, A_HEADS), _permute_heads(ak, A_KV_HEADS), av],
                           axis=1).astype(BF16)
    tn = _col_tile(w_in.shape[1], 1536)
    assert EV_AQ % tn == 0 and w_in.shape[1] - EV_AQ == tn
    slots = [HEAD_DIM ** -0.5] * A_HEADS + [1.0] * A_KV_HEADS + [None] * A_KV_HEADS
    p = normed_matmul(x, prm["norm_gain"][l], w_in, seq=seq, tn=tn, preps={EV_AQ // tn: (0, slots)},
                      head_gains=_head_gains((prm["a_q_norm"][e], A_HEADS), (prm["a_k_norm"][e], A_KV_HEADS)),
                      tables=rope_a)
    ya = window_attention(p, EV_AQ, EV_AK, EV_AV, EV_GATE, prm["a_sink"][e], batch, seq)
    yb = hyena_mixer(p, EV_BU, EV_GATE + A_WIDTH, prm, e, batch, seq)
    kvm = _memory_kv(mem, prm, l)
    ym = memory_attention(p, EV_MQ, EV_GATE + A_WIDTH + B_WIDTH, kvm, prm["mem_q_norm"][l], prm["mem_k_norm"][l],
                          batch, seq)
    return out_projection(ya, yb, ym, x, prm["w_out_even"][e].astype(BF16))


def _odd_layer(x, mem, prm, l, o, batch, seq, rope_a, rope_d):
    cq, ck, cv, cg, dqa, dkva, dg, mq, mg = _split_cols(prm["w_in_odd"][o], ODD_SPLITS)
    pad = jnp.zeros((D_MODEL, OD_WIDTH - (OD_KR + LANES)), F32)
    w_in = jnp.concatenate([cg, dg, mg, mq, _permute_heads(cq, 2 * C_HEADS), _permute_heads(ck, 2 * C_HEADS), cv, dqa,
                            dkva[:, :D_KV_RANK], _rope_slot(dkva[:, D_KV_RANK:]), pad], axis=1).astype(BF16)
    tn = _col_tile(w_in.shape[1], 1024)
    assert OD_CQ % tn == 0 and OD_CK - OD_CQ == tn and OD_CV - OD_CK == tn
    nslot = 2 * C_HEADS
    p = normed_matmul(x, prm["norm_gain"][l], w_in, seq=seq, tn=tn,
                      preps={OD_CQ // tn: (0, [C_QK_DIM ** -0.5 * LOG2E] * nslot), OD_CK // tn: (nslot, [1.0] * nslot)},
                      head_gains=_head_gains((prm["c_q_norm"][o], nslot), (prm["c_k_norm"][o], nslot)),
                      tables=rope_a)
    lam_init = 0.8 - 0.6 * math.exp(-0.3 * l)
    yc = diff_attention(p, OD_CQ, OD_CK, OD_CV, OD_GATE, prm["c_lambda"][o], prm["c_out_norm"][o], lam_init, batch, seq)
    wq = prm["w_q_b"][o].reshape(D_Q_RANK, D_HEADS, D_NOPE + D_ROPE)
    wq2 = jnp.concatenate([wq[:, :, :D_NOPE].reshape(D_Q_RANK, -1), _rope_slot(wq[:, :, D_NOPE:]).reshape(D_Q_RANK, -1)],
                          axis=1)
    wkv = prm["w_kv_b"][o].reshape(D_KV_RANK, D_HEADS, D_NOPE + D_V)
    wkv2 = jnp.concatenate([wkv[:, :, :D_NOPE].reshape(D_KV_RANK, -1), wkv[:, :, D_NOPE:].reshape(D_KV_RANK, -1)],
                           axis=1)
    dscale = (D_NOPE + D_ROPE) ** -0.5
    qd, = mla_projection(p, OD_DQA, D_Q_RANK, prm["d_q_a_norm"][o], wq2.astype(BF16), prm["d_q_norm"][o], rope_d,
                         dscale * LOG2E, seq)
    kd, vd = mla_projection(p, OD_CKV, D_KV_RANK, prm["d_kv_a_norm"][o], wkv2.astype(BF16), prm["d_k_norm"][o], rope_d,
                            1.0, seq, rope_col=OD_KR, tm=1024)
    yd = mla_attention(qd, kd, vd, p, OD_GATE + C_WIDTH, batch, seq)
    kvm = _memory_kv(mem, prm, l)
    ym = memory_attention(p, OD_MQ, OD_GATE + C_WIDTH + D_WIDTH, kvm, prm["mem_q_norm"][l], prm["mem_k_norm"][l],
                          batch, seq)
    return out_projection(yc, yd, ym, x, prm["w_out_odd"][o].astype(BF16))


def _trunk(x, mem, prm):
    batch, seq, d = x.shape
    depth = prm["norm_gain"].shape[0]
    rope_a = _rope_tables(seq, ROT_DIM, True)
    rope_d = _rope_tables(seq, D_ROPE, False)
    h = x.reshape(batch * seq, d)
    for l in range(depth):
        if l % 2 == 0:
            h = _even_layer(h, mem, prm, l, l // 2, batch, seq, rope_a)
        else:
            h = _odd_layer(h, mem, prm, l, l // 2, batch, seq, rope_a, rope_d)
    return h.reshape(batch, seq, d)


def kernel(x_prompt, x_sample, mem_prompt, mem_sample, norm_gain, mem_norm_gain, w_mem_kv, mem_q_norm, mem_k_norm, w_in_even, w_out_even, a_q_norm, a_k_norm, a_sink, hy_short_w, hy_short_b, hy_w1, hy_b1, hy_freq, hy_w2, hy_b2, hy_w3, hy_decay, hy_bias, w_in_odd, w_out_odd, c_q_norm, c_k_norm, c_lambda, c_out_norm, d_q_a_norm, w_q_b, d_kv_a_norm, w_kv_b, d_q_norm, d_k_norm):
    prm = dict(norm_gain=norm_gain, mem_norm_gain=mem_norm_gain, w_mem_kv=w_mem_kv,
               mem_q_norm=mem_q_norm, mem_k_norm=mem_k_norm, w_in_even=w_in_even,
               w_out_even=w_out_even, a_q_norm=a_q_norm, a_k_norm=a_k_norm, a_sink=a_sink,
               hy_short_w=hy_short_w, hy_short_b=hy_short_b, hy_w1=hy_w1, hy_b1=hy_b1,
               hy_freq=hy_freq, hy_w2=hy_w2, hy_b2=hy_b2, hy_w3=hy_w3, hy_decay=hy_decay,
               hy_bias=hy_bias, w_in_odd=w_in_odd, w_out_odd=w_out_odd, c_q_norm=c_q_norm,
               c_k_norm=c_k_norm, c_lambda=c_lambda, c_out_norm=c_out_norm, d_q_a_norm=d_q_a_norm,
               w_q_b=w_q_b, d_kv_a_norm=d_kv_a_norm, w_kv_b=w_kv_b, d_q_norm=d_q_norm,
               d_k_norm=d_k_norm)
    return (_trunk(x_prompt, mem_prompt, prm), _trunk(x_sample, mem_sample, prm))
```

```python
import functools
import math

import jax
import jax.numpy as jnp
from jax import lax
from jax.experimental import pallas as pl
from jax.experimental.pallas import tpu as pltpu

F32 = jnp.float32
BF16 = jnp.bfloat16

D_MODEL = 2048
N_MEM = 256
HEAD_DIM = 128
ROPE_THETA = 500000.0
ROT_DIM = HEAD_DIM // 4
EPS = 1e-6
BLOCK = 128
WINDOW = 128
NEG_INF = -1e30
A_HEADS = 8
A_KV_HEADS = 2
A_WIDTH = A_HEADS * HEAD_DIM
B_WIDTH = 1024
HY_EMB = 33
HY_FFN = 64
M_HEADS = 4
M_WIDTH = M_HEADS * HEAD_DIM
C_HEADS = 4
C_QK_DIM = 128
C_V_DIM = 256
C_WIDTH = C_HEADS * C_V_DIM
D_HEADS = 8
D_NOPE = 128
D_ROPE = 64
D_V = 128
D_Q_RANK = 512
D_KV_RANK = 256
D_WIDTH = D_HEADS * D_V
MIX_WIDTH = 2560
EVEN_SPLITS = (A_WIDTH, A_KV_HEADS * HEAD_DIM, A_KV_HEADS * HEAD_DIM, A_WIDTH,
               3 * B_WIDTH, B_WIDTH, M_WIDTH, M_WIDTH)
ODD_SPLITS = (C_HEADS * 2 * C_QK_DIM, C_HEADS * 2 * C_QK_DIM, C_WIDTH, C_WIDTH,
              D_Q_RANK, D_KV_RANK + D_ROPE, D_WIDTH, M_WIDTH, M_WIDTH)

LANES = 128
SUBLANES = 8
MXU_COLS = 256
DFT_N2 = 128
VMEM_LIMIT = 56 * 1024 * 1024

EV_GATE, EV_MQ, EV_BU, EV_AQ, EV_AK, EV_AV = 0, 2560, 3072, 6144, 7168, 7424
OD_GATE, OD_MQ, OD_CQ, OD_CK, OD_CV, OD_DQA, OD_CKV, OD_KR = 0, 2560, 3072, 4096, 5120, 6144, 6656, 6912
OD_WIDTH = 7168


def _tile(n, pref):
    t = min(n, pref)
    assert n % t == 0, (n, t)
    return t


def _col_tile(n, cap):
    best = None
    for t in range(MXU_COLS, min(n, cap) + 1, MXU_COLS):
        if n % t == 0:
            best = t
    assert best is not None, (n, cap)
    return best


def _params(sem):
    return pltpu.CompilerParams(dimension_semantics=sem, vmem_limit_bytes=VMEM_LIMIT)


def _rms(x, gain):
    ms = jnp.mean(x * x, axis=-1, keepdims=True)
    return x * lax.rsqrt(ms + EPS) * gain


def _silu(g):
    g = g.astype(F32)
    return g * jax.nn.sigmoid(g)


def _lanes(x, k):
    return x if k == 1 else jnp.concatenate([x] * k, axis=1)


def _rope(y, cf, sn):
    return y * cf + pltpu.roll(y, LANES // 2, 1) * sn


def _nmm_kernel(x_ref, g_ref, w_ref, hg_ref, cf_ref, sn_ref, o_ref, h_ref, acc_ref, *, preps):
    j = pl.program_id(1)

    def product():
        return jnp.dot(h_ref[...], w_ref[...], preferred_element_type=F32)

    assert 0 not in preps

    @pl.when(j == 0)
    def _():
        h = _rms(x_ref[...].astype(F32), g_ref[...]).astype(BF16)
        h_ref[...] = h
        o_ref[...] = jnp.dot(h, w_ref[...], preferred_element_type=F32).astype(o_ref.dtype)

    plain = j != 0
    for blk in preps:
        plain = jnp.logical_and(plain, j != blk)

    @pl.when(plain)
    def _():
        o_ref[...] = product().astype(o_ref.dtype)

    for blk, (gain_row0, scales) in preps.items():
        @pl.when(j == blk)
        def _(gain_row0=gain_row0, scales=scales):
            acc_ref[...] = product()

            @pl.when(pl.program_id(0) >= 0)
            def _():
                cf, sn = cf_ref[...], sn_ref[...]
                for slot, scale in enumerate(scales):
                    cols = slice(slot * LANES, (slot + 1) * LANES)
                    y = acc_ref[:, cols]
                    if scale is not None:
                        y = _rope(_rms(y, hg_ref[gain_row0 + slot:gain_row0 + slot + 1, :]), cf, sn) * scale
                    o_ref[:, cols] = y.astype(o_ref.dtype)


def normed_matmul(x, gain, w, *, seq, col_block=0, tm=1024, tn=1536, preps=None, head_gains=None, tables=None):
    rows = x.shape[0]
    k, n = w.shape
    tm = _tile(seq, tm)
    tn = _col_tile(n, tn)
    if preps is None:
        preps, head_gains = {}, jnp.ones((8, LANES), F32)
        tables = (jnp.ones((tm, LANES), F32), jnp.zeros((tm, LANES), F32))
    tab_spec = pl.BlockSpec((tm, LANES), lambda i, j: (i % (tables[0].shape[0] // tm), 0))
    return pl.pallas_call(
        functools.partial(_nmm_kernel, preps=preps),
        grid=(rows // tm, n // tn),
        in_specs=[pl.BlockSpec((tm, k), lambda i, j: (i, col_block)),
                  pl.BlockSpec((1, k), lambda i, j: (0, 0)),
                  pl.BlockSpec((k, tn), lambda i, j: (0, j)),
                  pl.BlockSpec(head_gains.shape, lambda i, j: (0, 0)),
                  tab_spec, tab_spec],
        out_specs=pl.BlockSpec((tm, tn), lambda i, j: (i, j)),
        out_shape=jax.ShapeDtypeStruct((rows, n), BF16),
        scratch_shapes=[pltpu.VMEM((tm, k), BF16), pltpu.VMEM((tm, tn) if preps else (SUBLANES, LANES), F32)],
        compiler_params=_params(("parallel", "arbitrary")),
        name="normed_matmul",
    )(x, gain.reshape(1, k).astype(F32), w, head_gains.astype(F32), *tables)


WIN_QB = 8


def _window_kernel(sink_ref, q_ref, kp_ref, kc_ref, kn_ref, vp_ref, vc_ref, vn_ref, g_ref, o_ref, s_ref, p_ref, *,
                   seq, qb):
    n = pl.program_id(1)
    group = A_HEADS // A_KV_HEADS
    rows = group * BLOCK
    qi = lax.broadcasted_iota(jnp.int32, (rows, 3 * BLOCK), 0) & (BLOCK - 1)
    kj = lax.broadcasted_iota(jnp.int32, (rows, 3 * BLOCK), 1)
    in_win = jnp.abs(kj - BLOCK - qi) <= WINDOW
    pairs = [(kv, i) for kv in range(A_KV_HEADS) for i in range(qb)]

    def kv_rows(refs, kv):
        cols = slice(kv * HEAD_DIM, (kv + 1) * HEAD_DIM)
        return jnp.concatenate([r[:, cols] for r in refs], axis=0).astype(BF16)

    kfull = [kv_rows((kp_ref, kc_ref, kn_ref), kv) for kv in range(A_KV_HEADS)]
    for idx, (kv, i) in enumerate(pairs):
        qrows = slice(i * BLOCK, (i + 1) * BLOCK)
        q = jnp.concatenate([q_ref[qrows, h * HEAD_DIM:(h + 1) * HEAD_DIM]
                             for h in range(kv * group, (kv + 1) * group)], axis=0)
        s_ref[idx] = lax.dot_general(q, kfull[kv][i * BLOCK:(i + 3) * BLOCK], (((1,), (1,)), ((), ())),
                                     preferred_element_type=F32)

    for idx, (kv, i) in enumerate(pairs):
        sk = jnp.concatenate([jnp.full((BLOCK, 1), sink_ref[h], F32)
                              for h in range(kv * group, (kv + 1) * group)], axis=0)
        kpos = (n * qb + i - 1) * BLOCK + kj
        s = jnp.where(in_win & (kpos >= 0) & (kpos < seq), s_ref[idx], NEG_INF)
        m = jnp.maximum(jnp.max(s, axis=-1, keepdims=True), sk)
        pr = jnp.exp(s - m)
        denom = jnp.sum(pr, axis=-1, keepdims=True) + jnp.exp(sk - m)
        p_ref[idx] = (pr / denom).astype(BF16)

    vfull = [kv_rows((vp_ref, vc_ref, vn_ref), kv) for kv in range(A_KV_HEADS)]
    for idx, (kv, i) in enumerate(pairs):
        qrows = slice(i * BLOCK, (i + 1) * BLOCK)
        o = jnp.dot(p_ref[idx], vfull[kv][i * BLOCK:(i + 3) * BLOCK], preferred_element_type=F32)
        for g, h in enumerate(range(kv * group, (kv + 1) * group)):
            hc = slice(h * HEAD_DIM, (h + 1) * HEAD_DIM)
            o_ref[qrows, hc] = (o[g * BLOCK:(g + 1) * BLOCK] * _silu(g_ref[qrows, hc])).astype(o_ref.dtype)


def window_attention(p, q_col, k_col, v_col, g_col, sink, batch, seq):
    nblk = seq // BLOCK
    qb = _tile(nblk, WIN_QB)
    nstep = nblk // qb
    kvw = A_KV_HEADS * HEAD_DIM
    kb, vb = k_col // kvw, v_col // kvw

    def prev(b, n):
        return b * nblk + jnp.maximum(n * qb - 1, 0)

    def cur(b, n):
        return b * nstep + n

    def nxt(b, n):
        return b * nblk + jnp.minimum((n + 1) * qb, nblk - 1)

    return pl.pallas_call(
        functools.partial(_window_kernel, seq=seq, qb=qb),
        grid=(batch, nstep),
        in_specs=[pl.BlockSpec(memory_space=pltpu.SMEM),
                  pl.BlockSpec((qb * BLOCK, A_WIDTH), lambda b, n: (cur(b, n), q_col // A_WIDTH)),
                  pl.BlockSpec((BLOCK, kvw), lambda b, n: (prev(b, n), kb)),
                  pl.BlockSpec((qb * BLOCK, kvw), lambda b, n: (cur(b, n), kb)),
                  pl.BlockSpec((BLOCK, kvw), lambda b, n: (nxt(b, n), kb)),
                  pl.BlockSpec((BLOCK, kvw), lambda b, n: (prev(b, n), vb)),
                  pl.BlockSpec((qb * BLOCK, kvw), lambda b, n: (cur(b, n), vb)),
                  pl.BlockSpec((BLOCK, kvw), lambda b, n: (nxt(b, n), vb)),
                  pl.BlockSpec((qb * BLOCK, A_WIDTH), lambda b, n: (cur(b, n), g_col // A_WIDTH))],
        out_specs=pl.BlockSpec((qb * BLOCK, A_WIDTH), lambda b, n: (cur(b, n), 0)),
        out_shape=jax.ShapeDtypeStruct((batch * seq, A_WIDTH), BF16),
        scratch_shapes=[pltpu.VMEM((A_KV_HEADS * qb, A_HEADS // A_KV_HEADS * BLOCK, 3 * BLOCK), F32),
                        pltpu.VMEM((A_KV_HEADS * qb, A_HEADS // A_KV_HEADS * BLOCK, 3 * BLOCK), BF16)],
        compiler_params=_params(("parallel", "parallel")),
        name="window_attention",
    )(sink.astype(F32), p, p, p, p, p, p, p, p)


def _mem_kernel(q_ref, kv_ref, qg_ref, kg_ref, g_ref, o_ref):
    qg, kg = qg_ref[...], kg_ref[...]
    scale = HEAD_DIM ** -0.5
    for h in range(M_HEADS):
        cols = slice(h * HEAD_DIM, (h + 1) * HEAD_DIM)
        q = (_rms(q_ref[:, cols].astype(F32), qg) * scale).astype(BF16)
        k = _rms(kv_ref[:, cols].astype(F32), kg).astype(BF16)
        v = kv_ref[:, M_WIDTH + h * HEAD_DIM:M_WIDTH + (h + 1) * HEAD_DIM].astype(BF16)
        s = lax.dot_general(q, k, (((1,), (1,)), ((), ())), preferred_element_type=F32)
        m = jnp.max(s, axis=-1, keepdims=True)
        pr = jnp.exp(s - m)
        pr = pr / jnp.sum(pr, axis=-1, keepdims=True)
        o = jnp.dot(pr.astype(BF16), v, preferred_element_type=F32)
        o_ref[:, cols] = (o * _silu(g_ref[:, cols])).astype(o_ref.dtype)


def memory_attention(p, q_col, g_col, kvm, q_gain, k_gain, batch, seq, *, tq=1024):
    tq = _tile(seq, tq)
    nq = seq // tq
    return pl.pallas_call(
        _mem_kernel,
        grid=(batch, nq),
        in_specs=[pl.BlockSpec((tq, M_WIDTH), lambda b, i: (b * nq + i, q_col // M_WIDTH)),
                  pl.BlockSpec((N_MEM, 2 * M_WIDTH), lambda b, i: (b, 0)),
                  pl.BlockSpec((1, HEAD_DIM), lambda b, i: (0, 0)),
                  pl.BlockSpec((1, HEAD_DIM), lambda b, i: (0, 0)),
                  pl.BlockSpec((tq, M_WIDTH), lambda b, i: (b * nq + i, g_col // M_WIDTH))],
        out_specs=pl.BlockSpec((tq, M_WIDTH), lambda b, i: (b * nq + i, 0)),
        out_shape=jax.ShapeDtypeStruct((batch * seq, M_WIDTH), BF16),
        compiler_params=_params(("parallel", "parallel")),
        name="memory_attention",
    )(p, kvm, q_gain.reshape(1, HEAD_DIM).astype(F32), k_gain.reshape(1, HEAD_DIM).astype(F32), p)


LOG2E = math.log2(math.e)


def _chunk_rows(chunk, tk):
    return pl.ds(pl.multiple_of(chunk * tk, tk), tk)


MLA_TRIP_CHUNKS = 8
DIFF_TRIP_CHUNKS = 2


def _pipelined_chunks(scores, consume, nk, per_trip):
    scores(0, 0)

    def body(i, carry):
        for j in range(per_trip):
            chunk = per_trip * i + j
            scores(jnp.minimum(chunk + 1, nk - 1), (j + 1) % 2)
            consume(chunk, j % 2)
        return carry

    lax.fori_loop(0, nk // per_trip, body, 0)


def _diff_flash_kernel(q_ref, k_ref, v_ref, lam_ref, g_ref, gate_ref, o_ref, m_ref, l_ref, acc_ref, s_ref, *,
                       tk, nk, lam_init):
    m_ref[...] = jnp.full(m_ref.shape, NEG_INF, F32)
    l_ref[...] = jnp.zeros(l_ref.shape, F32)
    acc_ref[...] = jnp.zeros(acc_ref.shape, F32)

    def scores(chunk, c):
        cols = slice(c * C_QK_DIM, (c + 1) * C_QK_DIM)
        s_ref[c] = lax.dot_general(q_ref[:, cols], k_ref[_chunk_rows(chunk, tk), cols], (((1,), (1,)), ((), ())),
                                   preferred_element_type=F32)

    def consume(chunk, c):
        s = s_ref[c]
        m_prev = m_ref[c]
        m_new = jnp.maximum(m_prev, jnp.max(s, axis=-1, keepdims=True))
        alpha = jnp.exp2(m_prev - m_new)
        pr = jnp.exp2(s - _lanes(m_new, tk // LANES))
        l_ref[c] = alpha * l_ref[c] + jnp.sum(pr, axis=-1, keepdims=True)
        acc_ref[c] = (_lanes(alpha, C_V_DIM // LANES) * acc_ref[c]
                      + jnp.dot(pr.astype(BF16), v_ref[_chunk_rows(chunk, tk), :], preferred_element_type=F32))
        m_ref[c] = m_new

    scores(0, 0)

    def body(i, carry):
        for step in range(DIFF_TRIP_CHUNKS):
            j = DIFF_TRIP_CHUNKS * i + step
            scores(j, 1)
            consume(j, 0)
            scores(jnp.minimum(j + 1, nk - 1), 0)
            consume(j, 1)
        return carry

    lax.fori_loop(0, nk // DIFF_TRIP_CHUNKS, body, 0)
    lv = lam_ref[...]
    lam = (jnp.exp(jnp.sum(lv[0:1] * lv[1:2], axis=-1, keepdims=True))
           - jnp.exp(jnp.sum(lv[2:3] * lv[3:4], axis=-1, keepdims=True)) + lam_init)
    o1 = acc_ref[0] / _lanes(l_ref[0], C_V_DIM // LANES)
    o2 = acc_ref[1] / _lanes(l_ref[1], C_V_DIM // LANES)
    o_ref[...] = (_rms(o1 - lam * o2, g_ref[...]) * (1.0 - lam_init) * _silu(gate_ref[...])).astype(o_ref.dtype)


def _mla_flash_kernel(q_ref, k_ref, v_ref, gate_ref, o_ref, m_ref, acc_ref, s_ref, *, tk, nk):
    m_ref[...] = jnp.full(m_ref.shape, NEG_INF, F32)
    acc_ref[...] = jnp.zeros(acc_ref.shape, F32)
    ones = jnp.ones((tk, LANES), BF16)

    def scores(chunk, slot):
        s_ref[slot] = lax.dot_general(q_ref[...], k_ref[_chunk_rows(chunk, tk), :], (((1,), (1,)), ((), ())),
                                      preferred_element_type=F32)

    def consume(chunk, slot):
        s = s_ref[slot]
        m_prev = m_ref[...]
        m_new = jnp.maximum(m_prev, jnp.max(s, axis=-1, keepdims=True))
        alpha = jnp.exp2(m_prev - m_new)
        pr = jnp.exp2(s - _lanes(m_new, tk // LANES))
        vv = jnp.concatenate([v_ref[_chunk_rows(chunk, tk), :], ones], axis=1)
        acc_ref[...] = (_lanes(alpha, 2) * acc_ref[...]
                        + jnp.dot(pr.astype(BF16), vv, preferred_element_type=F32))
        m_ref[...] = m_new

    _pipelined_chunks(scores, consume, nk, MLA_TRIP_CHUNKS)
    o_ref[...] = (acc_ref[:, :D_V] / acc_ref[:, D_V:] * _silu(gate_ref[...])).astype(o_ref.dtype)


def _flash_call(kernel, q, q_block0, k, k_block0, v, v_block0, gate, gate_block0, extra, extra_specs, scratch, *,
                heads, dv, batch, seq, tq, tk, kv_buffers, name):
    nq, nk = seq // tq, seq // tk
    qk_w = 2 * LANES
    resident = pl.Buffered(kv_buffers)
    return pl.pallas_call(
        functools.partial(kernel, tk=tk, nk=nk),
        grid=(batch, heads, nq),
        in_specs=[pl.BlockSpec((tq, qk_w), lambda b, h, i: (b * nq + i, q_block0 + h)),
                  pl.BlockSpec((seq, qk_w), lambda b, h, i: (b, k_block0 + h), pipeline_mode=resident),
                  pl.BlockSpec((seq, dv), lambda b, h, i: (b, v_block0 + h), pipeline_mode=resident)]
        + extra_specs + [pl.BlockSpec((tq, dv), lambda b, h, i: (b * nq + i, gate_block0 + h))],
        out_specs=pl.BlockSpec((tq, dv), lambda b, h, i: (b * nq + i, h)),
        out_shape=jax.ShapeDtypeStruct((batch * seq, heads * dv), BF16),
        scratch_shapes=scratch,
        compiler_params=_params(("parallel", "parallel", "arbitrary")),
        name=name,
    )(q, k, v, *extra, gate)


def diff_attention(p, q_col, k_col, v_col, g_col, c_lambda, out_gain, lam_init, batch, seq, *, tq=1024, tk=1024):
    tq, tk = _tile(seq, tq), _tile(seq // DIFF_TRIP_CHUNKS, tk)
    const = lambda b, h, i: (0, 0)
    return _flash_call(
        functools.partial(_diff_flash_kernel, lam_init=lam_init), p, q_col // C_V_DIM, p, k_col // C_V_DIM,
        p, v_col // C_V_DIM, p, g_col // C_V_DIM,
        [c_lambda.astype(F32), out_gain.reshape(1, C_V_DIM).astype(F32)],
        [pl.BlockSpec((4, C_QK_DIM), const), pl.BlockSpec((1, C_V_DIM), const)],
        [pltpu.VMEM((2, tq, LANES), F32), pltpu.VMEM((2, tq, LANES), F32), pltpu.VMEM((2, tq, C_V_DIM), F32),
         pltpu.VMEM((2, tq, tk), F32)],
        heads=C_HEADS, dv=C_V_DIM, batch=batch, seq=seq, tq=tq, tk=tk, kv_buffers=1, name="diff_attention")


def mla_attention(q, k, v, p, g_col, batch, seq, *, tq=2048, tk=512):
    tq, tk = _tile(seq, tq), _tile(seq // MLA_TRIP_CHUNKS, tk)
    return _flash_call(_mla_flash_kernel, q, 0, k, 0, v, 0, p, g_col // D_V, [], [],
                       [pltpu.VMEM((tq, LANES), F32), pltpu.VMEM((tq, D_V + LANES), F32),
                        pltpu.VMEM((2, tq, tk), F32)],
                       heads=D_HEADS, dv=D_V, batch=batch, seq=seq, tq=tq, tk=tk, kv_buffers=2,
                       name="mla_attention")


def _mla_proj_kernel(*refs, shared_rope, scale):
    if shared_rope:
        x_ref, g_ref, w_ref, kr_ref, gn_ref, gr_ref, cf_ref, sn_ref, o_ref, v_ref = refs
    else:
        x_ref, g_ref, w_ref, gn_ref, gr_ref, cf_ref, sn_ref, o_ref = refs
    h = _rms(x_ref[...].astype(F32), g_ref[...]).astype(BF16)
    acc = jnp.dot(h, w_ref[...], preferred_element_type=F32)
    gn, gr = gn_ref[...], gr_ref[...]
    cf, sn = cf_ref[...], sn_ref[...]
    width = D_HEADS * LANES

    def parts(hd):
        xn = acc[:, hd * LANES:(hd + 1) * LANES]
        xr = kr_ref[...].astype(F32) if shared_rope else acc[:, width + hd * LANES:width + (hd + 1) * LANES]
        return xn, xr

    inv = []
    for hd in range(D_HEADS):
        xn, xr = parts(hd)
        ms = (jnp.sum(xn * xn, axis=-1, keepdims=True)
              + jnp.sum(xr * xr, axis=-1, keepdims=True)) * (1.0 / (D_NOPE + D_ROPE))
        inv.append(jnp.broadcast_to(lax.rsqrt(ms + EPS), xn.shape))
    for hd in range(D_HEADS):
        xn, xr = parts(hd)
        o_ref[:, 2 * hd * LANES:(2 * hd + 1) * LANES] = (xn * inv[hd] * gn * scale).astype(o_ref.dtype)
        o_ref[:, (2 * hd + 1) * LANES:(2 * hd + 2) * LANES] = (_rope(xr * inv[hd] * gr, cf, sn) * scale).astype(o_ref.dtype)
    if shared_rope:
        v_ref[...] = acc[:, width:].astype(v_ref.dtype)


def mla_projection(p, col, rank, gain, w, head_gain, tables, scale, seq, *, rope_col=None, tm=512):
    rows = p.shape[0]
    width = D_HEADS * LANES
    tm = _tile(seq, tm)
    nblk = seq // tm
    shared_rope = rope_col is not None
    gn = head_gain[:D_NOPE].reshape(1, LANES).astype(F32)
    gr = _rope_slot(head_gain[D_NOPE:]).reshape(1, LANES).astype(F32)
    one = pl.BlockSpec((1, LANES), lambda i: (0, 0))
    tab_spec = pl.BlockSpec((tm, LANES), lambda i: (i % nblk, 0))
    in_specs = [pl.BlockSpec((tm, rank), lambda i: (i, col // rank)),
                pl.BlockSpec((1, rank), lambda i: (0, 0)),
                pl.BlockSpec((rank, 2 * width), lambda i: (0, 0))]
    args = [p, gain.reshape(1, rank).astype(F32), w]
    out_specs = [pl.BlockSpec((tm, 2 * width), lambda i: (i, 0))]
    out_shape = [jax.ShapeDtypeStruct((rows, 2 * width), BF16)]
    if shared_rope:
        in_specs.append(pl.BlockSpec((tm, LANES), lambda i: (i, rope_col // LANES)))
        args.append(p)
        out_specs.append(pl.BlockSpec((tm, width), lambda i: (i, 0)))
        out_shape.append(jax.ShapeDtypeStruct((rows, width), BF16))
    return pl.pallas_call(
        functools.partial(_mla_proj_kernel, shared_rope=shared_rope, scale=scale),
        grid=(rows // tm,),
        in_specs=in_specs + [one, one, tab_spec, tab_spec],
        out_specs=out_specs,
        out_shape=out_shape,
        compiler_params=_params(("parallel",)),
        name="mla_projection",
    )(*args, gn, gr, *tables)


SUB = SUBLANES
FILT_ROWS = SUB * DFT_N2
HALO = 16


def _half_rows(n1):
    return n1 // 2 + SUB


def _hyena_gate_kernel(*refs, nblk, tm):
    u_refs, up_refs, un_refs = refs[0:3], refs[3:6], refs[6:9]
    w_ref, b_ref, g_ref, zz_ref, x0_ref = refs[9:]
    li = pl.program_id(1)
    tc = zz_ref.shape[-1]
    row = lax.broadcasted_iota(jnp.int32, (tm, tc), 0)
    has_prev = (li > 0).astype(F32)
    has_next = (li < nblk - 1).astype(F32)

    def conv(part):
        u = u_refs[part][...].astype(F32)
        prev_row = up_refs[part][HALO - 1:HALO, :].astype(F32) * has_prev
        next_row = un_refs[part][0:1, :].astype(F32) * has_next
        above = jnp.where(row == 0, prev_row, pltpu.roll(u, 1, 0))
        below = jnp.where(row == tm - 1, next_row, pltpu.roll(u, tm - 1, 0))
        w = w_ref[part]
        return above * w[0:1] + u * w[1:2] + below * w[2:3] + b_ref[part]

    x0 = conv(0) * _silu(g_ref[...])
    zz = conv(2) * conv(1)
    for a in range(tm // DFT_N2):
        rows = slice(a * DFT_N2, (a + 1) * DFT_N2)
        zz_ref[:, a, :] = zz[rows]
        x0_ref[:, a, :] = x0[rows]


def hyena_gate(p, u_col, g_col, short_w, short_b, batch, seq, *, tc=512):
    tm = FILT_ROWS
    nblk = seq // tm
    half = seq // DFT_N2
    tc = _tile(B_WIDTH, tc)
    nc = B_WIDTH // tc
    hb = tm // HALO
    last_halo = batch * nblk * hb - 1

    def part_specs(part):
        cb = (u_col + part * B_WIDTH) // tc
        return (pl.BlockSpec((tm, tc), lambda b, i, c: (b * nblk + i, cb + c)),
                pl.BlockSpec((HALO, tc), lambda b, i, c: (jnp.maximum((b * nblk + i) * hb - 1, 0), cb + c)),
                pl.BlockSpec((HALO, tc), lambda b, i, c: (jnp.minimum((b * nblk + i + 1) * hb, last_halo), cb + c)))

    specs = [part_specs(part) for part in range(3)]
    out = jax.ShapeDtypeStruct((batch, DFT_N2, half, B_WIDTH), F32)
    o_spec = pl.BlockSpec((None, DFT_N2, tm // DFT_N2, tc), lambda b, i, c: (b, 0, i, c))
    w3 = short_w.astype(F32).reshape(3, 3, B_WIDTH).transpose(1, 0, 2)
    b3 = short_b.astype(F32).reshape(3, 1, B_WIDTH)
    return pl.pallas_call(
        functools.partial(_hyena_gate_kernel, nblk=nblk, tm=tm),
        grid=(batch, nblk, nc),
        in_specs=[sp[0] for sp in specs] + [sp[1] for sp in specs] + [sp[2] for sp in specs]
        + [pl.BlockSpec((3, 3, tc), lambda b, i, c: (0, 0, c)),
           pl.BlockSpec((3, 1, tc), lambda b, i, c: (0, 0, c)),
           pl.BlockSpec((tm, tc), lambda b, i, c: (b * nblk + i, g_col // tc + c))],
        out_specs=[o_spec, o_spec],
        out_shape=[out, out],
        compiler_params=_params(("parallel", "parallel", "parallel")),
        name="hyena_gate",
    )(*([p] * 9), w3, b3, p)


def _hdot(a, b):
    return jnp.dot(a, b, preferred_element_type=F32, precision=lax.Precision.HIGHEST)


def _filter_kernel(fr_ref, w1_ref, b1_ref, fq_ref, w2_ref, b2_ref, w3_ref, dec_ref, k_ref, s_ref, t_ref, h_ref, *,
                   seq, tr):
    i, c = pl.program_id(0), pl.program_id(1)
    r = i * tr + lax.broadcasted_iota(jnp.int32, (tr, 1), 0)

    @pl.when(c == 0)
    def _():
        pos = jnp.where(r < seq, r, 2 * seq - 1 - r).astype(F32)
        t = pos * (1.0 / (seq - 1))
        w = pos * (2.0 * math.pi / seq)
        lane = lax.broadcasted_iota(jnp.int32, (tr, LANES), 1)
        bands = (HY_EMB - 1) // 2
        z = jnp.where(lane == 0, t, jnp.where(lane <= 2 * bands, jnp.cos(w * fr_ref[0:1] + fr_ref[1:2]), 0.0))
        h = jnp.sin(fq_ref[0:1] * (_hdot(z, w1_ref[...]) + b1_ref[...]))
        h_ref[...] = jnp.sin(fq_ref[1:2] * (_hdot(h, w2_ref[...]) + b2_ref[...]))
        t_ref[...] = jnp.broadcast_to(t, t_ref.shape)

    h = _hdot(h_ref[...], w3_ref[...]) * jnp.exp(-t_ref[:, 0:1] * jnp.abs(dec_ref[...]))
    h = jnp.where(r == seq, 0.0, h)
    for a in range(tr // DFT_N2):
        k_ref[:, a, :] = h[a * DFT_N2:(a + 1) * DFT_N2]

    @pl.when(jnp.logical_and(i == 0, c == 0))
    def _():
        s_ref[...] = jnp.zeros(s_ref.shape, F32)

    tc = h.shape[1]
    col_sum = jnp.sum(jnp.abs(h), axis=0, keepdims=True)
    for cc in range(s_ref.shape[1] // tc):
        @pl.when(c == cc)
        def _(cc=cc):
            s_ref[:, cc * tc:(cc + 1) * tc] += col_sum


def hyena_filter(seq, w1, b1, freq, w2, b2, w3, decay, *, tc=512):
    tr = FILT_ROWS
    assert seq % tr == 0
    nhalf = seq // tr
    n1 = 2 * seq // DFT_N2
    tc = _tile(B_WIDTH, tc)
    nc = B_WIDTH // tc
    bands = (HY_EMB - 1) // 2
    fr = jnp.linspace(1e-4, bands - 1, bands, dtype=F32)
    pad = jnp.zeros((LANES - HY_EMB,), F32)
    fr_lanes = jnp.stack([jnp.concatenate([jnp.zeros((1,), F32), fr, fr, pad]),
                          jnp.concatenate([jnp.zeros((1 + bands,), F32), jnp.full((bands,), 0.5 * math.pi, F32), pad])])
    w1p = jnp.pad(w1.astype(F32), ((0, LANES - HY_EMB), (0, 0)))
    const = lambda i, c: (0, 0)
    half = lambda i, c: (0, (i // nhalf) * nc + c)
    return pl.pallas_call(
        functools.partial(_filter_kernel, seq=seq, tr=tr),
        grid=(2 * nhalf, nc),
        in_specs=[pl.BlockSpec((2, LANES), const),
                  pl.BlockSpec((LANES, HY_FFN), const),
                  pl.BlockSpec((1, HY_FFN), const),
                  pl.BlockSpec((2, HY_FFN), const),
                  pl.BlockSpec((HY_FFN, HY_FFN), const),
                  pl.BlockSpec((1, HY_FFN), const),
                  pl.BlockSpec((HY_FFN, tc), half),
                  pl.BlockSpec((1, tc), half)],
        out_specs=[pl.BlockSpec((DFT_N2, tr // DFT_N2, tc), lambda i, c: (0, i, c)),
                   pl.BlockSpec((1, B_WIDTH), const)],
        out_shape=[jax.ShapeDtypeStruct((DFT_N2, n1, B_WIDTH), F32),
                   jax.ShapeDtypeStruct((1, B_WIDTH), F32)],
        scratch_shapes=[pltpu.VMEM((tr, LANES), F32), pltpu.VMEM((tr, HY_FFN), F32)],
        compiler_params=_params(("arbitrary", "arbitrary")),
        name="hyena_filter",
    )(fr_lanes, w1p, b1.reshape(1, HY_FFN).astype(F32), freq.astype(F32), w2.astype(F32),
      b2.reshape(1, HY_FFN).astype(F32), w3.astype(F32), decay.reshape(1, 2 * B_WIDTH).astype(F32))


def _bdot(a, b):
    return jnp.dot(a, b, preferred_element_type=F32)


def dft_tables(n1):
    n = n1 * DFT_N2
    rows = _half_rows(n1)
    kk = jnp.arange(rows, dtype=jnp.int32)
    k1 = kk[None, :, None]
    m1 = jnp.arange(n1, dtype=jnp.int32)[None, None, :]
    n2 = jnp.arange(DFT_N2, dtype=jnp.int32)[:, None, None]
    ang = ((k1 * (DFT_N2 * m1 + n2)) % n).astype(F32) * (2.0 * math.pi / n)
    live = (kk <= n1 // 2).astype(F32)[None, :, None]
    weight = jnp.where((kk == 0) | (kk == n1 // 2), 1.0, 2.0)[None, :, None] * live
    cos, sin = jnp.cos(ang), jnp.sin(ang)
    g = jnp.concatenate([cos * live, -sin * live], axis=1).astype(BF16)
    hc = jnp.swapaxes(cos * weight, 1, 2)[:, :n1 // 2].astype(BF16)
    hsn = jnp.swapaxes(-sin * weight, 1, 2)[:, :n1 // 2].astype(BF16)
    a = jnp.arange(DFT_N2, dtype=jnp.int32)
    ang2 = ((a[:, None] * a[None, :]) % DFT_N2).astype(F32) * (2.0 * math.pi / DFT_N2)
    ff = jnp.concatenate([jnp.cos(ang2), jnp.sin(ang2)], axis=0).astype(BF16)
    return g, hc, hsn, ff


def _dft1_kernel(x_ref, g_ref, ar_ref, ai_ref):
    rows = ar_ref.shape[0]
    for j in range(SUB):
        a = _bdot(g_ref[j], x_ref[j].astype(BF16))
        ar_ref[:, j, :] = a[:rows]
        ai_ref[:, j, :] = a[rows:]


def dft_stage1(x, g, *, tc=512):
    batch, _, k1rows, c = x.shape
    rows = g.shape[1] // 2
    tc = _tile(c, tc)
    out = jax.ShapeDtypeStruct((batch, rows, DFT_N2, c), F32)
    g_spec = pl.BlockSpec((SUB, 2 * rows, k1rows), lambda g, b, ci: (g, 0, 0))
    o_spec = pl.BlockSpec((None, rows, SUB, tc), lambda g, b, ci: (b, 0, g, ci))
    return pl.pallas_call(
        _dft1_kernel,
        grid=(DFT_N2 // SUB, batch, c // tc),
        in_specs=[pl.BlockSpec((None, SUB, k1rows, tc), lambda g, b, ci: (b, g, 0, ci)), g_spec],
        out_specs=[o_spec, o_spec],
        out_shape=[out, out],
        compiler_params=_params(("parallel", "parallel", "parallel")),
        name="dft_stage1",
    )(x, g)


def _stage2(ff, ar, ai):
    p = _bdot(ff, ar)
    q = _bdot(ff, ai)
    return p[:DFT_N2] + q[DFT_N2:], q[:DFT_N2] - p[DFT_N2:]


def _filter_spectrum_kernel(ar_ref, ai_ref, s_ref, ff_ref, kr_ref, ki_ref, *, n):
    ff = ff_ref[...]
    scale = 1.0 / (s_ref[...] * n)
    for j in range(SUB):
        xr, xi = _stage2(ff, ar_ref[j].astype(BF16), ai_ref[j].astype(BF16))
        kr_ref[j] = xr * scale
        ki_ref[j] = xi * scale


def filter_spectrum(ar, ai, asum, ff, n, *, tc=512):
    rows, _, c = ar.shape
    tc = _tile(c, tc)
    a_spec = pl.BlockSpec((SUB, DFT_N2, tc), lambda k, ci: (k, 0, ci))
    out = jax.ShapeDtypeStruct((rows, DFT_N2, c), F32)
    return pl.pallas_call(
        functools.partial(_filter_spectrum_kernel, n=n),
        grid=(rows // SUB, c // tc),
        in_specs=[a_spec, a_spec, pl.BlockSpec((1, tc), lambda k, ci: (0, ci)),
                  pl.BlockSpec((2 * DFT_N2, DFT_N2), lambda k, ci: (0, 0))],
        out_specs=[a_spec, a_spec],
        out_shape=[out, out],
        compiler_params=_params(("parallel", "parallel")),
        name="filter_spectrum",
    )(ar, ai, asum, ff)


def _spectral_kernel(ar_ref, ai_ref, kr_ref, ki_ref, ff_ref, br_ref, bi_ref, yr_ref, yi_ref):
    ff = ff_ref[...]
    for j in range(SUB):
        xr, xi = _stage2(ff, ar_ref[j].astype(BF16), ai_ref[j].astype(BF16))
        kr, ki = kr_ref[j], ki_ref[j]
        yr_ref[j] = (xr * kr - xi * ki).astype(BF16)
        yi_ref[j] = (xr * ki + xi * kr).astype(BF16)
    for j in range(SUB):
        r = _bdot(ff, yr_ref[j])
        s = _bdot(ff, yi_ref[j])
        br_ref[:, j, :] = r[:DFT_N2] - s[DFT_N2:]
        bi_ref[:, j, :] = s[:DFT_N2] + r[DFT_N2:]


def spectral_multiply(ar, ai, kr, ki, ff, *, tc=512):
    batch, rows, _, c = ar.shape
    tc = _tile(c, tc)
    a_spec = pl.BlockSpec((None, SUB, DFT_N2, tc), lambda k, ci, b: (b, k, 0, ci))
    k_spec = pl.BlockSpec((SUB, DFT_N2, tc), lambda k, ci, b: (k, 0, ci))
    o_spec = pl.BlockSpec((None, DFT_N2, SUB, tc), lambda k, ci, b: (b, 0, k, ci))
    out = jax.ShapeDtypeStruct((batch, DFT_N2, rows, c), F32)
    return pl.pallas_call(
        _spectral_kernel,
        grid=(rows // SUB, c // tc, batch),
        in_specs=[a_spec, a_spec, k_spec, k_spec, pl.BlockSpec((2 * DFT_N2, DFT_N2), lambda k, ci, b: (0, 0))],
        out_specs=[o_spec, o_spec],
        out_shape=[out, out],
        scratch_shapes=[pltpu.VMEM((SUB, DFT_N2, tc), BF16), pltpu.VMEM((SUB, DFT_N2, tc), BF16)],
        compiler_params=_params(("parallel", "parallel", "parallel")),
        name="spectral_multiply",
    )(ar, ai, kr, ki, ff)


def _idft_out_kernel(br_ref, bi_ref, hc_ref, hs_ref, zz_ref, x0_ref, bias_ref, o_ref):
    bias = bias_ref[...]
    for j in range(SUB):
        y = _bdot(hc_ref[j], br_ref[j].astype(BF16)) + _bdot(hs_ref[j], bi_ref[j].astype(BF16))
        o_ref[:, j, :] = (y + zz_ref[j] * bias) * x0_ref[j]


def idft_output(br, bi, hc, hsn, zz, x0, bias, *, tc=512):
    batch, _, rows, c = br.shape
    half = hc.shape[1]
    tc = _tile(c, tc)
    b_spec = pl.BlockSpec((None, SUB, rows, tc), lambda g, b, ci: (b, g, 0, ci))
    h_spec = pl.BlockSpec((SUB, half, rows), lambda g, b, ci: (g, 0, 0))
    z_spec = pl.BlockSpec((None, SUB, half, tc), lambda g, b, ci: (b, g, 0, ci))
    return pl.pallas_call(
        _idft_out_kernel,
        grid=(DFT_N2 // SUB, batch, c // tc),
        in_specs=[b_spec, b_spec, h_spec, h_spec, z_spec, z_spec, pl.BlockSpec((1, tc), lambda g, b, ci: (0, ci))],
        out_specs=pl.BlockSpec((half, SUB, tc), lambda g, b, ci: (b, g, ci)),
        out_shape=jax.ShapeDtypeStruct((batch * half, DFT_N2, c), F32),
        compiler_params=_params(("parallel", "parallel", "parallel")),
        name="idft_output",
    )(br, bi, hc, hsn, zz, x0, bias)


def hyena_mixer(p, u_col, g_col, prm, e, batch, seq):
    n1 = 2 * seq // DFT_N2
    g, hc, hsn, ff = dft_tables(n1)
    kern, asum = hyena_filter(seq, prm["hy_w1"][e], prm["hy_b1"][e], prm["hy_freq"][e], prm["hy_w2"][e],
                              prm["hy_b2"][e], prm["hy_w3"][e], prm["hy_decay"][e])
    far, fai = dft_stage1(kern[None], g)
    kr, ki = filter_spectrum(far[0], fai[0], asum, ff, n1 * DFT_N2)
    zz, x0 = hyena_gate(p, u_col, g_col, prm["hy_short_w"][e], prm["hy_short_b"][e], batch, seq)
    ar, ai = dft_stage1(zz, g[:, :, :n1 // 2])
    br, bi = spectral_multiply(ar, ai, kr, ki, ff)
    yb = idft_output(br, bi, hc, hsn, zz, x0, prm["hy_bias"][e].reshape(1, B_WIDTH).astype(F32))
    return yb.reshape(batch * seq, B_WIDTH)


def _outproj_kernel(y1_ref, y2_ref, y3_ref, x_ref, w_ref, o_ref):
    acc = x_ref[...]
    off = 0
    for y_ref in (y1_ref, y2_ref, y3_ref):
        wd = y_ref.shape[1]
        acc = acc + jnp.dot(y_ref[...].astype(BF16), w_ref[off:off + wd, :], preferred_element_type=F32)
        off += wd
    o_ref[...] = acc


def out_projection(y1, y2, y3, x, w, *, tm=512):
    rows, d = x.shape
    tm = _tile(rows, tm)
    row = lambda i: (i, 0)
    return pl.pallas_call(
        _outproj_kernel,
        grid=(rows // tm,),
        in_specs=[pl.BlockSpec((tm, y1.shape[1]), row),
                  pl.BlockSpec((tm, y2.shape[1]), row),
                  pl.BlockSpec((tm, y3.shape[1]), row),
                  pl.BlockSpec((tm, d), row),
                  pl.BlockSpec((MIX_WIDTH, d), lambda i: (0, 0), pipeline_mode=pl.Buffered(1))],
        out_specs=pl.BlockSpec((tm, d), row),
        out_shape=jax.ShapeDtypeStruct((rows, d), F32),
        compiler_params=_params(("parallel",)),
        name="out_projection",
    )(y1, y2, y3, x, w)


def _split_cols(w, sizes):
    out, off = [], 0
    for s in sizes:
        out.append(w[:, off:off + s])
        off += s
    return out


ROT_HALF = ROT_DIM // 2
HEAD_PERM = (tuple(range(ROT_HALF)) + tuple(range(ROT_DIM, LANES // 2 + ROT_HALF))
             + tuple(range(ROT_HALF, ROT_DIM)) + tuple(range(LANES // 2 + ROT_HALF, LANES)))


def _permute_heads(w, nheads):
    k = w.shape[0]
    return w.reshape(k, nheads, HEAD_DIM)[:, :, jnp.array(HEAD_PERM)].reshape(k, nheads * HEAD_DIM)


def _rope_slot(w):
    half = D_ROPE // 2
    z = jnp.zeros(w.shape[:-1] + (LANES // 2 - half,), w.dtype)
    return jnp.concatenate([w[..., :half], z, w[..., half:], z], axis=-1)


def _rope_tables(seq, dim, passthrough):
    half = dim // 2
    inv = ROPE_THETA ** (-jnp.arange(0, dim, 2, dtype=F32) / dim)
    ang = jnp.arange(seq, dtype=F32)[:, None] * inv[None, :]
    cos, sin = jnp.cos(ang), jnp.sin(ang)
    rest = jnp.full((seq, LANES // 2 - half), 1.0 if passthrough else 0.0, F32)
    zeros = jnp.zeros((seq, LANES // 2 - half), F32)
    cf = jnp.concatenate([cos, rest, cos, rest], axis=1)
    sn = jnp.concatenate([-sin, zeros, sin, zeros], axis=1)
    return cf, sn


def _memory_kv(mem, prm, l):
    batch = mem.shape[0]
    return normed_matmul(mem.reshape(batch * N_MEM, D_MODEL), prm["mem_norm_gain"][l],
                         prm["w_mem_kv"][l].astype(BF16), seq=N_MEM)


def _head_gains(*groups):
    rows = [jnp.broadcast_to(g[jnp.array(HEAD_PERM)][None, :], (n, HEAD_DIM)) for g, n in groups]
    return jnp.concatenate(rows, axis=0)


def _even_layer(x, mem, prm, l, e, batch, seq, rope_a):
    aq, ak, av, ag, bu, bg, mq, mg = _split_cols(prm["w_in_even"][e], EVEN_SPLITS)
    w_in = jnp.concatenate([ag, bg, mg, mq, bu, _permute_heads(aq, A_HEADS), _permute_heads(ak, A_KV_HEADS), av],
                           axis=1).astype(BF16)
    tn = _col_tile(w_in.shape[1], 1536)
    assert EV_AQ % tn == 0 and w_in.shape[1] - EV_AQ == tn
    slots = [HEAD_DIM ** -0.5] * A_HEADS + [1.0] * A_KV_HEADS + [None] * A_KV_HEADS
    p = normed_matmul(x, prm["norm_gain"][l], w_in, seq=seq, tn=tn, preps={EV_AQ // tn: (0, slots)},
                      head_gains=_head_gains((prm["a_q_norm"][e], A_HEADS), (prm["a_k_norm"][e], A_KV_HEADS)),
                      tables=rope_a)
    ya = window_attention(p, EV_AQ, EV_AK, EV_AV, EV_GATE, prm["a_sink"][e], batch, seq)
    yb = hyena_mixer(p, EV_BU, EV_GATE + A_WIDTH, prm, e, batch, seq)
    kvm = _memory_kv(mem, prm, l)
    ym = memory_attention(p, EV_MQ, EV_GATE + A_WIDTH + B_WIDTH, kvm, prm["mem_q_norm"][l], prm["mem_k_norm"][l],
                          batch, seq)
    return out_projection(ya, yb, ym, x, prm["w_out_even"][e].astype(BF16))


def _odd_layer(x, mem, prm, l, o, batch, seq, rope_a, rope_d):
    cq, ck, cv, cg, dqa, dkva, dg, mq, mg = _split_cols(prm["w_in_odd"][o], ODD_SPLITS)
    pad = jnp.zeros((D_MODEL, OD_WIDTH - (OD_KR + LANES)), F32)
    w_in = jnp.concatenate([cg, dg, mg, mq, _permute_heads(cq, 2 * C_HEADS), _permute_heads(ck, 2 * C_HEADS), cv, dqa,
                            dkva[:, :D_KV_RANK], _rope_slot(dkva[:, D_KV_RANK:]), pad], axis=1).astype(BF16)
    tn = _col_tile(w_in.shape[1], 1024)
    assert OD_CQ % tn == 0 and OD_CK - OD_CQ == tn and OD_CV - OD_CK == tn
    nslot = 2 * C_HEADS
    p = normed_matmul(x, prm["norm_gain"][l], w_in, seq=seq, tn=tn,
                      preps={OD_CQ // tn: (0, [C_QK_DIM ** -0.5 * LOG2E] * nslot), OD_CK // tn: (nslot, [1.0] * nslot)},
                      head_gains=_head_gains((prm["c_q_norm"][o], nslot), (prm["c_k_norm"][o], nslot)),
                      tables=rope_a)
    lam_init = 0.8 - 0.6 * math.exp(-0.3 * l)
    yc = diff_attention(p, OD_CQ, OD_CK, OD_CV, OD_GATE, prm["c_lambda"][o], prm["c_out_norm"][o], lam_init, batch, seq)
    wq = prm["w_q_b"][o].reshape(D_Q_RANK, D_HEADS, D_NOPE + D_ROPE)
    wq2 = jnp.concatenate([wq[:, :, :D_NOPE].reshape(D_Q_RANK, -1), _rope_slot(wq[:, :, D_NOPE:]).reshape(D_Q_RANK, -1)],
                          axis=1)
    wkv = prm["w_kv_b"][o].reshape(D_KV_RANK, D_HEADS, D_NOPE + D_V)
    wkv2 = jnp.concatenate([wkv[:, :, :D_NOPE].reshape(D_KV_RANK, -1), wkv[:, :, D_NOPE:].reshape(D_KV_RANK, -1)],
                           axis=1)
    dscale = (D_NOPE + D_ROPE) ** -0.5
    qd, = mla_projection(p, OD_DQA, D_Q_RANK, prm["d_q_a_norm"][o], wq2.astype(BF16), prm["d_q_norm"][o], rope_d,
                         dscale * LOG2E, seq)
    kd, vd = mla_projection(p, OD_CKV, D_KV_RANK, prm["d_kv_a_norm"][o], wkv2.astype(BF16), prm["d_k_norm"][o], rope_d,
                            1.0, seq, rope_col=OD_KR, tm=1024)
    yd = mla_attention(qd, kd, vd, p, OD_GATE + C_WIDTH, batch, seq)
    kvm = _memory_kv(mem, prm, l)
    ym = memory_attention(p, OD_MQ, OD_GATE + C_WIDTH + D_WIDTH, kvm, prm["mem_q_norm"][l], prm["mem_k_norm"][l],
                          batch, seq)
    return out_projection(yc, yd, ym, x, prm["w_out_odd"][o].astype(BF16))


def _trunk(x, mem, prm):
    batch, seq, d = x.shape
    depth = prm["norm_gain"].shape[0]
    rope_a = _rope_tables(seq, ROT_DIM, True)
    rope_d = _rope_tables(seq, D_ROPE, False)
    h = x.reshape(batch * seq, d)
    for l in range(depth):
        if l % 2 == 0:
            h = _even_layer(h, mem, prm, l, l // 2, batch, seq, rope_a)
        else:
            h = _odd_layer(h, mem, prm, l, l // 2, batch, seq, rope_a, rope_d)
    return h.reshape(batch, seq, d)


def kernel(x_prompt, x_sample, mem_prompt, mem_sample, norm_gain, mem_norm_gain, w_mem_kv, mem_q_norm, mem_k_norm, w_in_even, w_out_even, a_q_norm, a_k_norm, a_sink, hy_short_w, hy_short_b, hy_w1, hy_b1, hy_freq, hy_w2, hy_b2, hy_w3, hy_decay, hy_bias, w_in_odd, w_out_odd, c_q_norm, c_k_norm, c_lambda, c_out_norm, d_q_a_norm, w_q_b, d_kv_a_norm, w_kv_b, d_q_norm, d_k_norm):
    prm = dict(norm_gain=norm_gain, mem_norm_gain=mem_norm_gain, w_mem_kv=w_mem_kv,
               mem_q_norm=mem_q_norm, mem_k_norm=mem_k_norm, w_in_even=w_in_even,
               w_out_even=w_out_even, a_q_norm=a_q_norm, a_k_norm=a_k_norm, a_sink=a_sink,
               hy_short_w=hy_short_w, hy_short_b=hy_short_b, hy_w1=hy_w1, hy_b1=hy_b1,
               hy_freq=hy_freq, hy_w2=hy_w2, hy_b2=hy_b2, hy_w3=hy_w3, hy_decay=hy_decay,
               hy_bias=hy_bias, w_in_odd=w_in_odd, w_out_odd=w_out_odd, c_q_norm=c_q_norm,
               c_k_norm=c_k_norm, c_lambda=c_lambda, c_out_norm=c_out_norm, d_q_a_norm=d_q_a_norm,
               w_q_b=w_q_b, d_kv_a_norm=d_kv_a_norm, w_kv_b=w_kv_b, d_q_norm=d_q_norm,
               d_k_norm=d_k_norm)
    return (_trunk(x_prompt, mem_prompt, prm), _trunk(x_sample, mem_sample, prm))
```
